```python
import math
import jax, jax.numpy as jnp
from jax import lax
import numpy as np

D_MODEL = 1024
BATCH = 8
SEQ = 2048
DEPTH = 4

MEM_LEN = 256
EXPAND = 2
MIX_WIDTH = EXPAND * D_MODEL
A_WIDTH = MIX_WIDTH // 2
A_HEAD_DIM = 64
A_HEADS = A_WIDTH // A_HEAD_DIM
A_PATTERNS = ((128, 1), (512, 4), (2048, 16))
A_BLOCK = 128
B_WIDTH = MIX_WIDTH - A_WIDTH
POOL_WINDOWS = (2, 4, 8, 16)
B_GROUP = B_WIDTH // len(POOL_WINDOWS)
C_WIDTH = D_MODEL
C_CHUNK = 128
C_GROUPS = 4
C_GROUP_DIM = C_WIDTH // C_GROUPS
D_WIDTH = D_MODEL // 2
S5_GROUP_DIM = 16
S5_GROUPS = D_WIDTH // S5_GROUP_DIM
S5_STATE = 64
X_HEADS = 4
X_HEAD_DIM = D_MODEL // X_HEADS
N_EVEN = (DEPTH + 1) // 2
N_ODD = DEPTH // 2
EPS = 1e-6
NEG = -1e30

kernel_name = 'hybrid_dilated_pool_sgu_s5_trunk'


def rms_norm(x, g):
    xf = x.astype(jnp.float32)
    y = xf * lax.rsqrt(jnp.mean(xf * xf, axis=-1, keepdims=True) + EPS)
    return (y * g.astype(jnp.float32)).astype(x.dtype)


def _dilated_pattern(q, k, v, window, dilation):
    b, s, h, dh = q.shape
    d = dilation
    L = s // d
    w = window // d
    nb = -(-L // A_BLOCK)
    lp = nb * A_BLOCK
    n = b * d

    def to_dilated(t):
        t = t.reshape(b, L, d, h, dh).transpose(0, 2, 1, 3, 4).reshape(n, L, h, dh)
        return jnp.pad(t, ((0, 0), (0, lp - L), (0, 0), (0, 0)))

    def band(t):
        tp = jnp.pad(t, ((0, 0), (A_BLOCK, 0), (0, 0), (0, 0))).reshape(n, nb + 1, A_BLOCK, h, dh)
        return jnp.concatenate([tp[:, :-1], tp[:, 1:]], axis=2)

    qb = to_dilated(q).reshape(n, nb, A_BLOCK, h, dh)
    kb = band(to_dilated(k))
    vb = band(to_dilated(v))
    i = jnp.arange(A_BLOCK)[:, None]
    j = jnp.arange(2 * A_BLOCK)[None, :]
    dist = i + A_BLOCK - j
    blk = jnp.arange(nb)[:, None, None]
    valid = (dist >= 0) & (dist <= w) & ((j >= A_BLOCK) | (blk > 0))
    sc = jnp.einsum('nbihd,nbjhd->nbhij', qb, kb, preferred_element_type=jnp.float32)
    sc = jnp.where(valid[None, :, None], sc, NEG)
    m = jnp.max(sc, axis=-1, keepdims=True)
    p = jnp.exp(sc - m)
    den = jnp.sum(p, axis=-1, keepdims=True)
    o = jnp.einsum('nbhij,nbjhd->nbihd', (p / den).astype(v.dtype), vb)
    lse = (m + jnp.log(den))[..., 0].transpose(0, 1, 3, 2)

    def from_dilated(t):
        rest = t.shape[3:]
        t = t.reshape((n, lp) + rest)[:, :L]
        return t.reshape((b, d, L) + rest).swapaxes(1, 2).reshape((b, s) + rest)

    return from_dilated(o), from_dilated(lse)


def dilated_attention(q, k, v):
    outs, lses = zip(*[_dilated_pattern(q, k, v, w, d) for (w, d) in A_PATTERNS])
    wts = jax.nn.softmax(jnp.stack(lses, axis=0), axis=0)
    o = jnp.sum(jnp.stack(outs, axis=0).astype(jnp.float32) * wts[..., None], axis=0)
    return o.astype(q.dtype)


def multiscale_pool(v, pool_w, pool_scale):
    b, s, _ = v.shape
    vf = v.astype(jnp.float32)
    c0 = jnp.pad(jnp.cumsum(vf, axis=1), ((0, 0), (1, 0), (0, 0)))
    pos = jnp.arange(1, s + 1, dtype=jnp.float32)[None, :, None]
    groups = []
    for g, w in enumerate(POOL_WINDOWS):
        sl = slice(g * B_GROUP, (g + 1) * B_GROUP)
        cg = c0[..., sl]
        lower = jnp.pad(cg, ((0, 0), (w - 1, 0), (0, 0)))[:, :s]
        mean = (cg[:, 1:] - lower) / jnp.minimum(pos, float(w))
        groups.append(mean - vf[..., sl])
    pooled = jnp.stack(groups, axis=2).astype(v.dtype)
    mixed = jnp.einsum('bsgc,gcd->bsgd', pooled, pool_w).reshape(b, s, B_WIDTH)
    return mixed * pool_scale


def spatial_gating(u, v, ln_g, ln_b, w_s, b_s):
    b, s, _ = u.shape
    vf = v.astype(jnp.float32)
    mu = jnp.mean(vf, axis=-1, keepdims=True)
    var = jnp.mean(jnp.square(vf - mu), axis=-1, keepdims=True)
    vn = ((vf - mu) * lax.rsqrt(var + EPS) * ln_g.astype(jnp.float32) + ln_b.astype(jnp.float32)).astype(v.dtype)
    nc = s // C_CHUNK
    vc = vn.reshape(b, nc, C_CHUNK, C_GROUPS, C_GROUP_DIM)
    mask = jnp.tril(jnp.ones((C_CHUNK, C_CHUNK), dtype=bool))
    w = jnp.where(mask[None], w_s, jnp.zeros_like(w_s))
    mixed = jnp.einsum('gij,bnjgc->bnigc', w, vc) + b_s.T[None, None, :, :, None]
    return u * mixed.reshape(b, s, C_WIDTH)


def _ssm_combine(e1, e2):
    a1r, a1i, b1r, b1i = e1
    a2r, a2i, b2r, b2i = e2
    return (a2r * a1r - a2i * a1i,
            a2r * a1i + a2i * a1r,
            a2r * b1r - a2i * b1i + b2r,
            a2r * b1i + a2i * b1r + b2i)


def s5_ssm(u, a_re, a_im, log_dt, b_re, b_im, c_re, c_im, d_skip, w1, w2):
    bsz, s, _ = u.shape
    f32 = jnp.float32
    uf = u.astype(f32).reshape(bsz, s, S5_GROUPS, S5_GROUP_DIM)
    ar, ai = a_re.astype(f32), a_im.astype(f32)
    dt = jnp.exp(log_dt.astype(f32))[:, None]
    mag = jnp.exp(dt * ar)
    abar_re = mag * jnp.cos(dt * ai)
    abar_im = mag * jnp.sin(dt * ai)
    nr, ni = abar_re - 1.0, abar_im
    inv = 1.0 / (ar * ar + ai * ai)
    coef_re = (nr * ar + ni * ai) * inv
    coef_im = (ni * ar - nr * ai) * inv
    br, bi = b_re.astype(f32), b_im.astype(f32)
    bbar_re = coef_re[..., None] * br - coef_im[..., None] * bi
    bbar_im = coef_re[..., None] * bi + coef_im[..., None] * br
    bu_re = jnp.einsum('bsgh,gph->bsgp', uf, bbar_re)
    bu_im = jnp.einsum('bsgh,gph->bsgp', uf, bbar_im)
    shape_a = (1, s, S5_GROUPS, S5_STATE)
    a_seq_re = jnp.broadcast_to(abar_re[None, None], shape_a)
    a_seq_im = jnp.broadcast_to(abar_im[None, None], shape_a)
    _, _, h_re, h_im = lax.associative_scan(_ssm_combine, (a_seq_re, a_seq_im, bu_re, bu_im), axis=1)
    y = (jnp.einsum('bsgp,ghp->bsgh', h_re, c_re.astype(f32))
         - jnp.einsum('bsgp,ghp->bsgh', h_im, c_im.astype(f32))
         + d_skip.astype(f32).reshape(S5_GROUPS, S5_GROUP_DIM) * uf)
    y = jax.nn.gelu(y.reshape(bsz, s, D_WIDTH)).astype(u.dtype)
    return (y @ w1) * jax.nn.sigmoid(y @ w2)


def memory_cross_attention(h, mem_n, w_q, w_kv, w_o):
    b, s, _ = h.shape
    m = mem_n.shape[1]
    q = (h @ w_q).reshape(b, s, X_HEADS, X_HEAD_DIM)
    kv = (mem_n @ w_kv).reshape(b, m, 2, X_HEADS, X_HEAD_DIM)
    k, v = kv[:, :, 0], kv[:, :, 1]
    sc = jnp.einsum('bshd,bmhd->bhsm', q, k, preferred_element_type=jnp.float32) * (X_HEAD_DIM ** -0.5)
    p = jax.nn.softmax(sc, axis=-1).astype(v.dtype)
    o = jnp.einsum('bhsm,bmhd->bshd', p, v).reshape(b, s, D_MODEL)
    return o @ w_o


def setup_inputs(seed: int = 0) -> dict:
    key = jax.random.key(seed)
    ks = iter(jax.random.split(key, 40))

    def nrm(shape, scale):
        return scale * jax.random.normal(next(ks), shape, jnp.float32)

    def gain(shape):
        return 1.0 + nrm(shape, 0.05)

    return {
        'x': nrm((BATCH, SEQ, D_MODEL), 1.0),
        'mem': nrm((BATCH, MEM_LEN, D_MODEL), 1.0),
        'norm_ab': gain((N_EVEN, D_MODEL)),
        'w_in_ab': nrm((N_EVEN, D_MODEL, 4 * A_WIDTH + 2 * B_WIDTH), D_MODEL ** -0.5),
        'pool_w': nrm((N_EVEN, len(POOL_WINDOWS), B_GROUP, B_GROUP), B_GROUP ** -0.5),
        'pool_scale': gain((N_EVEN, B_WIDTH)),
        'w_out_ab': nrm((N_EVEN, A_WIDTH + B_WIDTH, D_MODEL), (A_WIDTH + B_WIDTH) ** -0.5),
        'norm_cd': gain((N_ODD, D_MODEL)),
        'w_in_cd': nrm((N_ODD, D_MODEL, 3 * C_WIDTH + 2 * D_WIDTH), D_MODEL ** -0.5),
        'sgu_ln_g': gain((N_ODD, C_WIDTH)),
        'sgu_ln_b': nrm((N_ODD, C_WIDTH), 0.02),
        'sgu_w': nrm((N_ODD, C_GROUPS, C_CHUNK, C_CHUNK), C_CHUNK ** -0.5),
        'sgu_b': gain((N_ODD, C_GROUPS, C_CHUNK)),
        's5_a_re': -0.5 + nrm((N_ODD, S5_GROUPS, S5_STATE), 0.01),
        's5_a_im': jnp.pi * jnp.arange(S5_STATE, dtype=jnp.float32)[None, None, :] + nrm((N_ODD, S5_GROUPS, S5_STATE), 0.01),
        's5_log_dt': jax.random.uniform(next(ks), (N_ODD, S5_GROUPS), jnp.float32, math.log(1e-3), math.log(1e-1)),
        's5_b_re': nrm((N_ODD, S5_GROUPS, S5_STATE, S5_GROUP_DIM), (2 * S5_GROUP_DIM) ** -0.5),
        's5_b_im': nrm((N_ODD, S5_GROUPS, S5_STATE, S5_GROUP_DIM), (2 * S5_GROUP_DIM) ** -0.5),
        's5_c_re': nrm((N_ODD, S5_GROUPS, S5_GROUP_DIM, S5_STATE), S5_STATE ** -0.5),
        's5_c_im': nrm((N_ODD, S5_GROUPS, S5_GROUP_DIM, S5_STATE), S5_STATE ** -0.5),
        's5_d': nrm((N_ODD, D_WIDTH), 1.0),
        'glu_w1': nrm((N_ODD, D_WIDTH, D_WIDTH), D_WIDTH ** -0.5),
        'glu_w2': nrm((N_ODD, D_WIDTH, D_WIDTH), D_WIDTH ** -0.5),
        'w_out_cd': nrm((N_ODD, C_WIDTH + D_WIDTH, D_MODEL), (C_WIDTH + D_WIDTH) ** -0.5),
        'norm_x': gain((DEPTH, D_MODEL)),
        'w_xq': nrm((DEPTH, D_MODEL, D_MODEL), D_MODEL ** -0.5),
        'w_xkv': nrm((DEPTH, D_MODEL, 2 * D_MODEL), D_MODEL ** -0.5),
        'w_xo': nrm((DEPTH, D_MODEL, D_MODEL), D_MODEL ** -0.5),
        'mem_norm': gain((D_MODEL,)),
        'final_norm': gain((D_MODEL,)),
    }


def reference(x, mem, norm_ab, w_in_ab, pool_w, pool_scale, w_out_ab,
              norm_cd, w_in_cd, sgu_ln_g, sgu_ln_b, sgu_w, sgu_b,
              s5_a_re, s5_a_im, s5_log_dt, s5_b_re, s5_b_im, s5_c_re, s5_c_im, s5_d,
              glu_w1, glu_w2, w_out_cd, norm_x, w_xq, w_xkv, w_xo, mem_norm, final_norm):
    b, s, _ = x.shape
    mem_n = rms_norm(mem, mem_norm)
    for layer in range(DEPTH):
        i = layer // 2
        if layer % 2 == 0:
            hn = rms_norm(x, norm_ab[i])
            z = hn @ w_in_ab[i]
            q, k, v, g_a, v_b, g_b = jnp.split(
                z, [A_WIDTH, 2 * A_WIDTH, 3 * A_WIDTH, 4 * A_WIDTH, 4 * A_WIDTH + B_WIDTH], axis=-1)
            q = q.reshape(b, s, A_HEADS, A_HEAD_DIM) * (A_HEAD_DIM ** -0.5)
            k = k.reshape(b, s, A_HEADS, A_HEAD_DIM)
            v = v.reshape(b, s, A_HEADS, A_HEAD_DIM)
            a_out = dilated_attention(q, k, v).reshape(b, s, A_WIDTH) * jax.nn.silu(g_a)
            b_out = multiscale_pool(v_b, pool_w[i], pool_scale[i]) * jax.nn.silu(g_b)
            y = jnp.concatenate([a_out, b_out], axis=-1) @ w_out_ab[i]
        else:
            hn = rms_norm(x, norm_cd[i])
            z = hn @ w_in_cd[i]
            u_c, v_c, g_c, x_d, g_d = jnp.split(
                z, [C_WIDTH, 2 * C_WIDTH, 3 * C_WIDTH, 3 * C_WIDTH + D_WIDTH], axis=-1)
            c_out = spatial_gating(u_c, v_c, sgu_ln_g[i], sgu_ln_b[i], sgu_w[i], sgu_b[i]) * jax.nn.silu(g_c)
            d_out = s5_ssm(x_d, s5_a_re[i], s5_a_im[i], s5_log_dt[i], s5_b_re[i], s5_b_im[i],
                           s5_c_re[i], s5_c_im[i], s5_d[i], glu_w1[i], glu_w2[i]) * jax.nn.silu(g_d)
            y = jnp.concatenate([c_out, d_out], axis=-1) @ w_out_cd[i]
        x = x + y
        x = x + memory_cross_attention(rms_norm(x, norm_x[layer]), mem_n, w_xq[layer], w_xkv[layer], w_xo[layer])
    return rms_norm(x, final_norm)
```

```python
import functools

import jax
import jax.numpy as jnp
from jax import lax
from jax.experimental import pallas as pl
from jax.experimental.pallas import tpu as pltpu

F32 = jnp.float32
BF16 = jnp.bfloat16

D_MODEL = 1024
A_WIDTH = 1024
A_HEAD_DIM = 64
A_BLOCK = 128
A_DILATIONS = (1, 4, 16)
B_WIDTH = 1024
POOL_WINDOWS = (2, 4, 8, 16)
B_GROUP = 256
POOL_HALO = 16
C_WIDTH = 1024
C_CHUNK = 128
C_GROUPS = 4
C_GROUP_DIM = 256
D_WIDTH = 512
S5_GROUPS = 32
S5_GROUP_DIM = 16
S5_STATE = 64
S5_LANES = S5_GROUPS * S5_STATE
X_HEADS = 4
X_HEAD_DIM = 256
EPS = 1e-6
NEG = -1e30

VMEM_LIMIT = 48 * 1024 * 1024


def _params(*sem):
    return pltpu.CompilerParams(dimension_semantics=sem, vmem_limit_bytes=VMEM_LIMIT)


def _rms(x, g):
    ms = jnp.mean(x * x, axis=-1, keepdims=True)
    return x * lax.rsqrt(ms + EPS) * g


def _silu(x):
    return x * jax.nn.sigmoid(x)


def _dot(a, b):
    return jnp.dot(a, b, preferred_element_type=F32)


def _dot_nt(a, b):
    return lax.dot_general(a, b, (((1,), (1,)), ((), ())), preferred_element_type=F32)


def _norm_matmul_kernel(x_ref, g_ref, w_ref, o_ref, xn_ref):
    @pl.when(pl.program_id(1) == 0)
    def _():
        xn_ref[...] = _rms(x_ref[...], g_ref[...]).astype(BF16)

    o_ref[...] = _dot(xn_ref[...], w_ref[...]).astype(o_ref.dtype)


def norm_matmul(x, g, w, *, tm, tn, out_dtype=F32, out_shape=None, out_map=None):
    m, k = x.shape
    n = w.shape[1]
    if out_shape is None:
        out_shape = (m, n)
        out_map = lambda i, j: (i, j)
    return pl.pallas_call(
        _norm_matmul_kernel,
        out_shape=jax.ShapeDtypeStruct(out_shape, out_dtype),
        grid=(m // tm, n // tn),
        in_specs=[
            pl.BlockSpec((tm, k), lambda i, j: (i, 0)),
            pl.BlockSpec((1, k), lambda i, j: (0, 0)),
            pl.BlockSpec((k, tn), lambda i, j: (0, j)),
        ],
        out_specs=pl.BlockSpec((tm, tn), out_map),
        scratch_shapes=[pltpu.VMEM((tm, k), BF16)],
        compiler_params=_params("parallel", "arbitrary"),
        name="norm_matmul",
    )(x, g.reshape(1, k), w)


def _dilated_kernel(q_ref, k_ref, v_ref, g_ref, o_ref, acc_ref, m_ref, l_ref, bias_ref, *, seq):
    qi = lax.broadcasted_iota(jnp.int32, (A_BLOCK, 2 * A_BLOCK), 0)
    kj = lax.broadcasted_iota(jnp.int32, (A_BLOCK, 2 * A_BLOCK), 1)
    dist = qi + A_BLOCK - kj
    band = (dist >= 0) & (dist <= A_BLOCK)
    bias_ref[1] = jnp.where(band, 0.0, NEG)
    bias_ref[0] = jnp.where(band & (kj >= A_BLOCK), 0.0, NEG)

    scale = A_HEAD_DIM ** -0.5
    n_pat = len(A_DILATIONS)
    for pi, d in enumerate(A_DILATIONS):
        n_blk = seq // d // A_BLOCK

        def rows(start, d=d):
            if d == 1:
                return pl.ds(pl.multiple_of(start, A_BLOCK), A_BLOCK)
            return pl.ds(start, A_BLOCK, stride=d)

        def body(idx, carry, d=d, n_blk=n_blk, pi=pi, rows=rows):
            r = idx % d
            blk = idx // d
            start = r + d * A_BLOCK * blk
            cur = rows(start)
            q = q_ref[cur, :] * scale
            if n_blk > 1:
                prev = rows(jnp.maximum(start - d * A_BLOCK, r))
                kk = jnp.concatenate([k_ref[prev, :], k_ref[cur, :]], axis=0)
                vv = jnp.concatenate([v_ref[prev, :], v_ref[cur, :]], axis=0)
                bias = bias_ref[jnp.minimum(blk, 1)]
            else:
                kk = k_ref[cur, :]
                vv = v_ref[cur, :]
                bias = bias_ref[0][:, A_BLOCK:]
            num, mx, den = [], [], []
            for h in range(2):
                sl = slice(h * A_HEAD_DIM, (h + 1) * A_HEAD_DIM)
                s = _dot_nt(q[:, sl].astype(BF16), kk[:, sl].astype(BF16)) + bias
                m = jnp.max(s, axis=-1, keepdims=True)
                p = jnp.exp(s - m)
                den.append(jnp.broadcast_to(jnp.sum(p, axis=-1, keepdims=True), (A_BLOCK, A_HEAD_DIM)))
                mx.append(jnp.broadcast_to(m, (A_BLOCK, A_HEAD_DIM)))
                num.append(_dot(p.astype(BF16), vv[:, sl].astype(BF16)))
            num = jnp.concatenate(num, axis=1)
            mx = jnp.concatenate(mx, axis=1)
            den = jnp.concatenate(den, axis=1)
            if pi > 0:
                m_old = m_ref[cur, :]
                m_new = jnp.maximum(m_old, mx)
                w_old = jnp.exp(m_old - m_new)
                w_cur = jnp.exp(mx - m_new)
                num = acc_ref[cur, :] * w_old + num * w_cur
                den = l_ref[cur, :] * w_old + den * w_cur
                mx = m_new
            if pi < n_pat - 1:
                acc_ref[cur, :] = num
                m_ref[cur, :] = mx
                l_ref[cur, :] = den
            else:
                o_ref[cur, :] = num / den * _silu(g_ref[cur, :])
            return carry

        lax.fori_loop(0, d * n_blk, body, 0)


def dilated_attention(z, *, batch, seq):
    n_pair = A_WIDTH // 128
    blk = (None, seq, 128)
    return pl.pallas_call(
        functools.partial(_dilated_kernel, seq=seq),
        out_shape=jax.ShapeDtypeStruct((batch, seq, A_WIDTH), F32),
        grid=(batch, n_pair),
        in_specs=[
            pl.BlockSpec(blk, lambda b, h: (b, 0, h)),
            pl.BlockSpec(blk, lambda b, h: (b, 0, n_pair + h)),
            pl.BlockSpec(blk, lambda b, h: (b, 0, 2 * n_pair + h)),
            pl.BlockSpec(blk, lambda b, h: (b, 0, 3 * n_pair + h)),
        ],
        out_specs=pl.BlockSpec(blk, lambda b, h: (b, 0, h)),
        scratch_shapes=[
            pltpu.VMEM((seq, 128), F32),
            pltpu.VMEM((seq, 128), F32),
            pltpu.VMEM((seq, 128), F32),
            pltpu.VMEM((2, A_BLOCK, 2 * A_BLOCK), F32),
        ],
        compiler_params=_params("parallel", "parallel"),
        name="dilated_attention",
    )(z, z, z, z)


def _even_tail_kernel(vb_ref, halo_ref, gb_ref, a_ref, x_ref, pw_ref, ps_ref, wo_ref, o_ref, *, tm):
    ti = pl.program_id(1)
    v = vb_ref[...]
    halo = jnp.where(ti > 0, halo_ref[...], 0.0)
    xc = jnp.concatenate([halo, v], axis=0)
    s2 = xc + pltpu.roll(xc, 1, 0)
    t4 = s2[:, B_GROUP:]
    s4 = t4 + pltpu.roll(t4, 2, 0)
    t8 = s4[:, B_GROUP:]
    s8 = t8 + pltpu.roll(t8, 4, 0)
    t16 = s8[:, B_GROUP:]
    s16 = t16 + pltpu.roll(t16, 8, 0)
    sums = (s2[:, :B_GROUP], s4[:, :B_GROUP], s8[:, :B_GROUP], s16)
    pos = (ti * tm + 1 + lax.broadcasted_iota(jnp.int32, (tm, 1), 0)).astype(F32)
    mixed = []
    for g, w in enumerate(POOL_WINDOWS):
        mean = sums[g][POOL_HALO:, :] / jnp.minimum(pos, float(w))
        pooled = mean - v[:, g * B_GROUP:(g + 1) * B_GROUP]
        mixed.append(_dot(pooled.astype(BF16), pw_ref[g]))
    b_out = jnp.concatenate(mixed, axis=1) * ps_ref[...] * _silu(gb_ref[...])
    y = _dot(a_ref[...].astype(BF16), wo_ref[:A_WIDTH, :]) + _dot(b_out.astype(BF16), wo_ref[A_WIDTH:, :])
    o_ref[...] = x_ref[...] + y


def even_tail(z, a_out, x, pool_w, pool_scale, w_out, *, batch, seq, tm):
    vb_col = 4 * A_WIDTH // B_WIDTH
    halo_per_tile = tm // POOL_HALO
    return pl.pallas_call(
        functools.partial(_even_tail_kernel, tm=tm),
        out_shape=jax.ShapeDtypeStruct((batch, seq, D_MODEL), F32),
        grid=(batch, seq // tm),
        in_specs=[
            pl.BlockSpec((None, tm, B_WIDTH), lambda b, t: (b, t, vb_col)),
            pl.BlockSpec((None, POOL_HALO, B_WIDTH),
                         lambda b, t: (b, jnp.maximum(t * halo_per_tile - 1, 0), vb_col)),
            pl.BlockSpec((None, tm, B_WIDTH), lambda b, t: (b, t, vb_col + 1)),
            pl.BlockSpec((None, tm, A_WIDTH), lambda b, t: (b, t, 0)),
            pl.BlockSpec((None, tm, D_MODEL), lambda b, t: (b, t, 0)),
            pl.BlockSpec((len(POOL_WINDOWS), B_GROUP, B_GROUP), lambda b, t: (0, 0, 0)),
            pl.BlockSpec((1, B_WIDTH), lambda b, t: (0, 0)),
            pl.BlockSpec((A_WIDTH + B_WIDTH, D_MODEL), lambda b, t: (0, 0)),
        ],
        out_specs=pl.BlockSpec((None, tm, D_MODEL), lambda b, t: (b, t, 0)),
        compiler_params=_params("parallel", "parallel"),
        name="even_tail",
    )(z, z, z, a_out, x, pool_w, pool_scale.reshape(1, B_WIDTH), w_out)


def _s5_prep_kernel(ar_ref, ai_ref, ldt_ref, br_ref, bi_ref, abr_ref, abi_ref, bbr_ref, bbi_ref):
    ar, ai = ar_ref[...], ai_ref[...]
    dt = jnp.exp(ldt_ref[...])
    mag = jnp.exp(dt * ar)
    abar_re = mag * jnp.cos(dt * ai)
    abar_im = mag * jnp.sin(dt * ai)
    nr, ni = abar_re - 1.0, abar_im
    inv = 1.0 / (ar * ar + ai * ai)
    coef_re = (nr * ar + ni * ai) * inv
    coef_im = (ni * ar - nr * ai) * inv
    br, bi = br_ref[...], bi_ref[...]
    abr_ref[...] = abar_re
    abi_ref[...] = abar_im
    bbr_ref[...] = coef_re * br - coef_im * bi
    bbi_ref[...] = coef_re * bi + coef_im * br


def s5_prep(a_re, a_im, log_dt, b_re, b_im):
    rep = lambda t: jnp.repeat(t, S5_GROUP_DIM, axis=0)
    to_rows = lambda t: t.transpose(0, 2, 1).reshape(D_WIDTH, S5_STATE)
    shp = jax.ShapeDtypeStruct((D_WIDTH, S5_STATE), F32)
    abr, abi, bbr, bbi = pl.pallas_call(
        _s5_prep_kernel, out_shape=(shp, shp, shp, shp), name="s5_prep",
    )(rep(a_re), rep(a_im), rep(jnp.broadcast_to(log_dt[:, None], (S5_GROUPS, S5_STATE))),
      to_rows(b_re), to_rows(b_im))
    eye = jnp.eye(S5_GROUPS, dtype=F32)

    def block_diag_in(t):
        t = t.reshape(S5_GROUPS, S5_GROUP_DIM, 1, S5_STATE) * eye[:, None, :, None]
        return t.reshape(D_WIDTH, S5_LANES)

    w_b = jnp.concatenate([block_diag_in(bbr), block_diag_in(bbi)], axis=1).astype(BF16)
    abar_re = abr[::S5_GROUP_DIM].reshape(1, S5_LANES)
    abar_im = abi[::S5_GROUP_DIM].reshape(1, S5_LANES)
    return w_b, abar_re, abar_im


def s5_out_weights(c_re, c_im):
    eye = jnp.eye(S5_GROUPS, dtype=F32)

    def block_diag_out(t):
        t = t.transpose(0, 2, 1)[:, :, None, :] * eye[:, None, :, None]
        return t.reshape(S5_LANES, D_WIDTH).astype(BF16)

    return block_diag_out(c_re), block_diag_out(c_im)


def _s5_kernel(xd_ref, gd_ref, wb_ref, are_ref, aim_ref, wcr_ref, wci_ref, dsk_ref, w1_ref, w2_ref,
               o_ref, bu_ref, h_ref, *, lc, nb, lane_chunk):
    @pl.when(pl.program_id(0) == 0)
    def _():
        h_ref[...] = jnp.zeros_like(h_ref)

    u = xd_ref[...]
    bu_ref[...] = _dot(u.astype(BF16), wb_ref[...])
    for c in range(S5_LANES // lane_chunk):
        re = slice(c * lane_chunk, (c + 1) * lane_chunk)
        im = slice(S5_LANES + c * lane_chunk, S5_LANES + (c + 1) * lane_chunk)
        a_r = jnp.broadcast_to(are_ref[:, re], (nb, lane_chunk))
        a_i = jnp.broadcast_to(aim_ref[:, re], (nb, lane_chunk))

        def step(t, carry, re=re, im=im, a_r=a_r, a_i=a_i):
            h_r, h_i = carry
            row = pl.ds(pl.multiple_of(t * nb, nb), nb)
            n_r = a_r * h_r - a_i * h_i + bu_ref[row, re]
            n_i = a_r * h_i + a_i * h_r + bu_ref[row, im]
            bu_ref[row, re] = n_r
            bu_ref[row, im] = n_i
            return n_r, n_i

        h_r, h_i = lax.fori_loop(0, lc, step, (h_ref[:, re], h_ref[:, im]), unroll=4)
        h_ref[:, re] = h_r
        h_ref[:, im] = h_i

    y = (_dot(bu_ref[:, :S5_LANES].astype(BF16), wcr_ref[...])
         - _dot(bu_ref[:, S5_LANES:].astype(BF16), wci_ref[...])
         + dsk_ref[...] * u)
    y = jax.nn.gelu(y).astype(BF16)
    o_ref[...] = _dot(y, w1_ref[...]) * jax.nn.sigmoid(_dot(y, w2_ref[...])) * _silu(gd_ref[...])


def s5_layer(zd, w_b, abar_re, abar_im, w_cr, w_ci, d_skip, w1, w2, *, batch, seq, lc):
    rows = lc * batch
    full = lambda shape: pl.BlockSpec(shape, lambda c: (0, 0))
    return pl.pallas_call(
        functools.partial(_s5_kernel, lc=lc, nb=batch, lane_chunk=512),
        out_shape=jax.ShapeDtypeStruct((seq * batch, D_WIDTH), F32),
        grid=(seq // lc,),
        in_specs=[
            pl.BlockSpec((rows, D_WIDTH), lambda c: (c, 0)),
            pl.BlockSpec((rows, D_WIDTH), lambda c: (c, 1)),
            full((D_WIDTH, 2 * S5_LANES)),
            full((1, S5_LANES)),
            full((1, S5_LANES)),
            full((S5_LANES, D_WIDTH)),
            full((S5_LANES, D_WIDTH)),
            full((1, D_WIDTH)),
            full((D_WIDTH, D_WIDTH)),
            full((D_WIDTH, D_WIDTH)),
        ],
        out_specs=pl.BlockSpec((rows, D_WIDTH), lambda c: (c, 0)),
        scratch_shapes=[pltpu.VMEM((rows, 2 * S5_LANES), F32), pltpu.VMEM((batch, 2 * S5_LANES), F32)],
        compiler_params=_params("arbitrary"),
        name="s5_layer",
    )(zd, zd, w_b, abar_re, abar_im, w_cr, w_ci, d_skip.reshape(1, D_WIDTH), w1, w2)


def _odd_tail_kernel(u_ref, v_ref, gc_ref, d_ref, x_ref, lng_ref, lnb_ref, ws_ref, bs_ref, wo_ref,
                     o_ref, c_ref, *, tm):
    v = v_ref[...]
    mu = jnp.mean(v, axis=-1, keepdims=True)
    vc = v - mu
    var = jnp.mean(vc * vc, axis=-1, keepdims=True)
    vn = (vc * lax.rsqrt(var + EPS) * lng_ref[...] + lnb_ref[...]).astype(BF16)
    ri = lax.broadcasted_iota(jnp.int32, (C_CHUNK, C_CHUNK), 0)
    ci = lax.broadcasted_iota(jnp.int32, (C_CHUNK, C_CHUNK), 1)
    causal = ri >= ci
    for g in range(C_GROUPS):
        cols = slice(g * C_GROUP_DIM, (g + 1) * C_GROUP_DIM)
        w_g = jnp.where(causal, ws_ref[g], 0.0).astype(BF16)
        b_g = bs_ref[:, g:g + 1]
        for c in range(tm // C_CHUNK):
            rows = slice(c * C_CHUNK, (c + 1) * C_CHUNK)
            mixed = _dot(w_g, vn[rows, cols]) + b_g
            c_ref[rows, cols] = (u_ref[rows, cols] * mixed * _silu(gc_ref[rows, cols])).astype(BF16)
    y = _dot(c_ref[...], wo_ref[:C_WIDTH, :]) + _dot(d_ref[...].astype(BF16), wo_ref[C_WIDTH:, :])
    o_ref[...] = x_ref[...] + y


def odd_tail(zc, d_out, x, ln_g, ln_b, w_s, b_s, w_out, *, batch, seq, tm):
    return pl.pallas_call(
        functools.partial(_odd_tail_kernel, tm=tm),
        out_shape=jax.ShapeDtypeStruct((batch, seq, D_MODEL), F32),
        grid=(batch, seq // tm),
        in_specs=[
            pl.BlockSpec((None, tm, C_WIDTH), lambda b, t: (b, t, 0)),
            pl.BlockSpec((None, tm, C_WIDTH), lambda b, t: (b, t, 1)),
            pl.BlockSpec((None, tm, C_WIDTH), lambda b, t: (b, t, 2)),
            pl.BlockSpec((tm, D_WIDTH), lambda b, t: (t, b)),
            pl.BlockSpec((None, tm, D_MODEL), lambda b, t: (b, t, 0)),
            pl.BlockSpec((1, C_WIDTH), lambda b, t: (0, 0)),
            pl.BlockSpec((1, C_WIDTH), lambda b, t: (0, 0)),
            pl.BlockSpec((C_GROUPS, C_CHUNK, C_CHUNK), lambda b, t: (0, 0, 0)),
            pl.BlockSpec((C_CHUNK, C_GROUPS), lambda b, t: (0, 0)),
            pl.BlockSpec((C_WIDTH + D_WIDTH, D_MODEL), lambda b, t: (0, 0)),
        ],
        out_specs=pl.BlockSpec((None, tm, D_MODEL), lambda b, t: (b, t, 0)),
        scratch_shapes=[pltpu.VMEM((tm, C_WIDTH), BF16)],
        compiler_params=_params("parallel", "parallel"),
        name="odd_tail",
    )(zc, zc, zc, d_out, x, ln_g.reshape(1, C_WIDTH), ln_b.reshape(1, C_WIDTH), w_s, b_s.T, w_out)


def _xattn_kernel(x_ref, g_ref, wq_ref, kv_ref, wo_ref, fg_ref, o_ref, *, final):
    x = x_ref[...]
    q = _dot(_rms(x, g_ref[...]).astype(BF16), wq_ref[...])
    heads = []
    for h in range(X_HEADS):
        ks = slice(h * X_HEAD_DIM, (h + 1) * X_HEAD_DIM)
        vs = slice(D_MODEL + h * X_HEAD_DIM, D_MODEL + (h + 1) * X_HEAD_DIM)
        s = _dot_nt(q[:, ks].astype(BF16), kv_ref[:, ks]) * (X_HEAD_DIM ** -0.5)
        m = jnp.max(s, axis=-1, keepdims=True)
        p = jnp.exp(s - m)
        den = jnp.sum(p, axis=-1, keepdims=True)
        heads.append((_dot(p.astype(BF16), kv_ref[:, vs]) / den).astype(BF16))
    y = x + _dot(jnp.concatenate(heads, axis=1), wo_ref[...])
    if final:
        y = _rms(y, fg_ref[...])
    o_ref[...] = y


def cross_attention(x, g, w_q, kv, w_o, final_g, *, batch, seq, mem_len, tq, final):
    return pl.pallas_call(
        functools.partial(_xattn_kernel, final=final),
        out_shape=jax.ShapeDtypeStruct((batch, seq, D_MODEL), F32),
        grid=(batch, seq // tq),
        in_specs=[
            pl.BlockSpec((None, tq, D_MODEL), lambda b, t: (b, t, 0)),
            pl.BlockSpec((1, D_MODEL), lambda b, t: (0, 0)),
            pl.BlockSpec((D_MODEL, D_MODEL), lambda b, t: (0, 0)),
            pl.BlockSpec((None, mem_len, 2 * D_MODEL), lambda b, t: (b, 0, 0)),
            pl.BlockSpec((D_MODEL, D_MODEL), lambda b, t: (0, 0)),
            pl.BlockSpec((1, D_MODEL), lambda b, t: (0, 0)),
        ],
        out_specs=pl.BlockSpec((None, tq, D_MODEL), lambda b, t: (b, t, 0)),
        compiler_params=_params("parallel", "parallel"),
        name="cross_attention",
    )(x, g.reshape(1, D_MODEL), w_q, kv, w_o, final_g.reshape(1, D_MODEL))


def kernel(x, mem, norm_ab, w_in_ab, pool_w, pool_scale, w_out_ab, norm_cd, w_in_cd, sgu_ln_g, sgu_ln_b,
           sgu_w, sgu_b, s5_a_re, s5_a_im, s5_log_dt, s5_b_re, s5_b_im, s5_c_re, s5_c_im, s5_d,
           glu_w1, glu_w2, w_out_cd, norm_x, w_xq, w_xkv, w_xo, mem_norm, final_norm):
    batch, seq, _ = x.shape
    mem_len = mem.shape[1]
    depth = norm_x.shape[0]
    tokens = batch * seq
    tm_proj = 1024
    tiles_per_row = seq // tm_proj
    bf = lambda t: t.astype(BF16)

    mem2 = mem.reshape(batch * mem_len, D_MODEL)
    for layer in range(depth):
        i = layer // 2
        x2 = x.reshape(tokens, D_MODEL)
        if layer % 2 == 0:
            z = norm_matmul(x2, norm_ab[i], bf(w_in_ab[i]), tm=tm_proj, tn=1024)
            z = z.reshape(batch, seq, z.shape[1])
            a_out = dilated_attention(z, batch=batch, seq=seq)
            x = even_tail(z, a_out, x, bf(pool_w[i]), pool_scale[i], bf(w_out_ab[i]),
                          batch=batch, seq=seq, tm=512)
        else:
            n_c = 3 * C_WIDTH
            zc = norm_matmul(x2, norm_cd[i], bf(w_in_cd[i][:, :n_c]), tm=tm_proj, tn=1024)
            zd = norm_matmul(x2, norm_cd[i], bf(w_in_cd[i][:, n_c:]), tm=tm_proj, tn=2 * D_WIDTH,
                             out_shape=(seq, batch * 2 * D_WIDTH),
                             out_map=lambda r, j: (r % tiles_per_row, r // tiles_per_row))
            w_b, abar_re, abar_im = s5_prep(s5_a_re[i], s5_a_im[i], s5_log_dt[i], s5_b_re[i], s5_b_im[i])
            w_cr, w_ci = s5_out_weights(s5_c_re[i], s5_c_im[i])
            d_out = s5_layer(zd.reshape(seq * batch, 2 * D_WIDTH), w_b, abar_re, abar_im, w_cr, w_ci,
                             s5_d[i], bf(glu_w1[i]), bf(glu_w2[i]), batch=batch, seq=seq, lc=64)
            x = odd_tail(zc.reshape(batch, seq, n_c), d_out.reshape(seq, batch * D_WIDTH), x,
                         sgu_ln_g[i], sgu_ln_b[i], sgu_w[i], sgu_b[i], bf(w_out_cd[i]),
                         batch=batch, seq=seq, tm=512)
        kv = norm_matmul(mem2, mem_norm, bf(w_xkv[layer]), tm=batch * mem_len, tn=1024, out_dtype=BF16)
        x = cross_attention(x, norm_x[layer], bf(w_xq[layer]), kv.reshape(batch, mem_len, 2 * D_MODEL),
                            bf(w_xo[layer]), final_norm, batch=batch, seq=seq, mem_len=mem_len,
                            tq=512, final=(layer == depth - 1))
    return x
```

```python
import functools

import jax
import jax.numpy as jnp
from jax import lax
from jax.experimental import pallas as pl
from jax.experimental.pallas import tpu as pltpu

F32 = jnp.float32
BF16 = jnp.bfloat16

D_MODEL = 1024
A_WIDTH = 1024
A_HEAD_DIM = 64
A_BLOCK = 128
A_DILATIONS = (1, 4, 16)
DIL_STEP = 4
B_WIDTH = 1024
POOL_WINDOWS = (2, 4, 8, 16)
B_GROUP = 256
POOL_HALO = 16
C_WIDTH = 1024
C_CHUNK = 128
C_GROUPS = 4
C_GROUP_DIM = 256
D_WIDTH = 512
S5_GROUPS = 32
S5_GROUP_DIM = 16
S5_STATE = 64
S5_LANES = S5_GROUPS * S5_STATE
X_HEADS = 4
X_HEAD_DIM = 256
EPS = 1e-6
NEG = -1e30
LOG2E = 1.4426950408889634

VMEM_LIMIT = 48 * 1024 * 1024


def _params(*sem):
    return pltpu.CompilerParams(dimension_semantics=sem, vmem_limit_bytes=VMEM_LIMIT)


def _rms(x, g):
    ms = jnp.mean(x * x, axis=-1, keepdims=True)
    return x * lax.rsqrt(ms + EPS) * g


def _silu(x):
    return x * jax.nn.sigmoid(x)


def _dot(a, b):
    return jnp.dot(a, b, preferred_element_type=F32)


def _dot_nt(a, b):
    return lax.dot_general(a, b, (((1,), (1,)), ((), ())), preferred_element_type=F32)


def _norm_matmul_kernel(x_ref, g_ref, w_ref, o_ref, xn_ref):
    @pl.when(pl.program_id(1) == 0)
    def _():
        xn_ref[...] = _rms(x_ref[...], g_ref[...]).astype(BF16)

    o_ref[...] = _dot(xn_ref[...], w_ref[...]).astype(o_ref.dtype)


def norm_matmul(x, g, w, *, tm, tn, out_dtype=F32, out_shape=None, out_map=None):
    m, k = x.shape
    n = w.shape[1]
    if out_shape is None:
        out_shape = (m, n)
        out_map = lambda i, j: (i, j)
    return pl.pallas_call(
        _norm_matmul_kernel,
        out_shape=jax.ShapeDtypeStruct(out_shape, out_dtype),
        grid=(m // tm, n // tn),
        in_specs=[
            pl.BlockSpec((tm, k), lambda i, j: (i, 0)),
            pl.BlockSpec((1, k), lambda i, j: (0, 0)),
            pl.BlockSpec((k, tn), lambda i, j: (0, j)),
        ],
        out_specs=pl.BlockSpec((tm, tn), out_map),
        scratch_shapes=[pltpu.VMEM((tm, k), BF16)],
        compiler_params=_params("parallel", "arbitrary"),
        name="norm_matmul",
    )(x, g.reshape(1, k), w)


def _dilated_kernel(q_ref, k_ref, v_ref, g_ref, o_ref, qs_ref, ks_ref, vs_ref, gs_ref, num_ref, den_ref, m_ref,
                    bias_ref, *, seq, unroll):
    assert A_DILATIONS == (1, DIL_STEP, DIL_STEP * DIL_STEP)
    qi = lax.broadcasted_iota(jnp.int32, (A_BLOCK, 2 * A_BLOCK), 0)
    kj = lax.broadcasted_iota(jnp.int32, (A_BLOCK, 2 * A_BLOCK), 1)
    dist = qi + A_BLOCK - kj
    band = (dist >= 0) & (dist <= A_BLOCK)
    bias_ref[1] = jnp.where(band, 0.0, NEG)
    bias_ref[0] = jnp.where(band & (kj >= A_BLOCK), 0.0, NEG)

    len_s = seq // DIL_STEP
    len_w = len_s // DIL_STEP
    blocks_s = len_s // A_BLOCK
    for src, dst in ((q_ref, qs_ref), (k_ref, ks_ref), (v_ref, vs_ref)):
        for b in range(DIL_STEP):
            dst[0, b * len_s:(b + 1) * len_s, :] = src[pl.ds(b, len_s, stride=DIL_STEP), :]
        for c in range(DIL_STEP * DIL_STEP):
            b, a = divmod(c, DIL_STEP)
            dst[1, c * len_w:(c + 1) * len_w, :] = dst[0, pl.ds(b * len_s + a, len_w, stride=DIL_STEP), :]

    head0 = lax.broadcasted_iota(jnp.int32, (A_BLOCK, 128), 1) < A_HEAD_DIM
    q_scale = A_HEAD_DIM ** -0.5 * LOG2E

    def load(refs, cur, prev):
        qr, kr, vr = refs
        q = qr[cur, :] * q_scale
        q2 = jnp.concatenate([jnp.where(head0, q, 0.0), jnp.where(head0, 0.0, q)], axis=0).astype(BF16)
        if prev is None:
            return q2, kr[cur, :].astype(BF16), vr[cur, :].astype(BF16)
        kk = jnp.concatenate([kr[prev, :], kr[cur, :]], axis=0).astype(BF16)
        vv = jnp.concatenate([vr[prev, :], vr[cur, :]], axis=0).astype(BF16)
        return q2, kk, vv

    def attend(q2, kk, vv, bias):
        s = _dot_nt(q2, kk) + jnp.concatenate([bias, bias], axis=0)
        m = jnp.max(s, axis=-1, keepdims=True)
        p = jnp.exp2(s - m).astype(BF16)
        res = _dot(p, jnp.concatenate([vv, jnp.ones(vv.shape, BF16)], axis=1))
        top, bot = res[:A_BLOCK], res[A_BLOCK:]
        return (jnp.where(head0, top[:, :128], bot[:, :128]), jnp.where(head0, top[:, 128:], bot[:, 128:]),
                jnp.where(head0, m[:A_BLOCK], m[A_BLOCK:]))

    def store(pi, cur, num, den, m):
        num_ref[pi, cur, :] = num
        den_ref[pi, cur, :] = den
        m_ref[pi, cur, :] = m

    def aligned(start):
        return pl.ds(pl.multiple_of(start, A_BLOCK), A_BLOCK)

    def banded_rows(idx, blocks_per_seq):
        start = idx * A_BLOCK
        first = idx % blocks_per_seq == 0
        return aligned(start), aligned(jnp.where(first, start, start - A_BLOCK)), bias_ref[jnp.where(first, 0, 1)]

    def single_rows(idx):
        return aligned(idx * A_BLOCK), None, bias_ref[0][:, A_BLOCK:]

    def body(idx, carry, *, pi, refs, rows_of):
        cur, prev, bias = rows_of(idx)
        store(pi, cur, *attend(*load(refs, cur, prev), bias))
        return carry

    assert len_w == A_BLOCK
    n_blocks = seq // A_BLOCK
    for pi, refs, rows_of in (
            (0, (q_ref, k_ref, v_ref), functools.partial(banded_rows, blocks_per_seq=n_blocks)),
            (1, (qs_ref.at[0], ks_ref.at[0], vs_ref.at[0]), functools.partial(banded_rows, blocks_per_seq=blocks_s)),
            (2, (qs_ref.at[1], ks_ref.at[1], vs_ref.at[1]), single_rows)):
        lax.fori_loop(0, n_blocks, functools.partial(body, pi=pi, refs=refs, rows_of=rows_of), 0,
                      unroll=unroll)

    def weights(m_a, m_b):
        m = jnp.maximum(m_a, m_b)
        return m, jnp.exp2(m_a - m), jnp.exp2(m_b - m)

    def fold_step(c, carry):
        by_step = aligned(c * A_BLOCK)
        natural = pl.ds(c // blocks_s + DIL_STEP * A_BLOCK * (c % blocks_s), A_BLOCK, stride=DIL_STEP)
        m, w_a, w_b = weights(m_ref[0, natural, :], m_ref[1, by_step, :])
        num_ref[1, by_step, :] = w_a * num_ref[0, natural, :] + w_b * num_ref[1, by_step, :]
        den_ref[1, by_step, :] = w_a * den_ref[0, natural, :] + w_b * den_ref[1, by_step, :]
        m_ref[1, by_step, :] = m
        gs_ref[by_step, :] = g_ref[natural, :]
        return carry

    lax.fori_loop(0, n_blocks, fold_step, 0)

    def fold_wide(c, carry):
        wide = aligned(c * A_BLOCK)
        by_step = pl.ds((c // DIL_STEP) * len_s + c % DIL_STEP, A_BLOCK, stride=DIL_STEP)
        natural = pl.ds(DIL_STEP * (c % DIL_STEP) + c // DIL_STEP, A_BLOCK, stride=DIL_STEP * DIL_STEP)
        _, w_a, w_b = weights(m_ref[1, by_step, :], m_ref[2, wide, :])
        num = w_a * num_ref[1, by_step, :] + w_b * num_ref[2, wide, :]
        den = w_a * den_ref[1, by_step, :] + w_b * den_ref[2, wide, :]
        o_ref[natural, :] = num / den * _silu(gs_ref[by_step, :])
        return carry

    lax.fori_loop(0, n_blocks, fold_wide, 0)


def dilated_attention(z, *, batch, seq, unroll=8):
    n_pair = A_WIDTH // 128
    n_pat = len(A_DILATIONS)
    blk = (None, seq, 128)
    return pl.pallas_call(
        functools.partial(_dilated_kernel, seq=seq, unroll=unroll),
        out_shape=jax.ShapeDtypeStruct((batch, seq, A_WIDTH), F32),
        grid=(batch, n_pair),
        in_specs=[
            pl.BlockSpec(blk, lambda b, h: (b, 0, h)),
            pl.BlockSpec(blk, lambda b, h: (b, 0, n_pair + h)),
            pl.BlockSpec(blk, lambda b, h: (b, 0, 2 * n_pair + h)),
            pl.BlockSpec(blk, lambda b, h: (b, 0, 3 * n_pair + h)),
        ],
        out_specs=pl.BlockSpec(blk, lambda b, h: (b, 0, h)),
        scratch_shapes=[
            pltpu.VMEM((2, seq, 128), F32),
            pltpu.VMEM((2, seq, 128), F32),
            pltpu.VMEM((2, seq, 128), F32),
            pltpu.VMEM((seq, 128), F32),
            pltpu.VMEM((n_pat, seq, 128), F32),
            pltpu.VMEM((n_pat, seq, 128), F32),
            pltpu.VMEM((n_pat, seq, 128), F32),
            pltpu.VMEM((2, A_BLOCK, 2 * A_BLOCK), F32),
        ],
        compiler_params=_params("parallel", "parallel"),
        name="dilated_attention",
    )(z, z, z, z)


def _even_tail_kernel(vb_ref, halo_ref, gb_ref, a_ref, x_ref, pw_ref, ps_ref, wo_ref, o_ref, *, tm):
    ti = pl.program_id(1)
    v = vb_ref[...]
    halo = jnp.where(ti > 0, halo_ref[...], 0.0)
    xc = jnp.concatenate([halo, v], axis=0)
    s2 = xc + pltpu.roll(xc, 1, 0)
    t4 = s2[:, B_GROUP:]
    s4 = t4 + pltpu.roll(t4, 2, 0)
    t8 = s4[:, B_GROUP:]
    s8 = t8 + pltpu.roll(t8, 4, 0)
    t16 = s8[:, B_GROUP:]
    s16 = t16 + pltpu.roll(t16, 8, 0)
    sums = (s2[:, :B_GROUP], s4[:, :B_GROUP], s8[:, :B_GROUP], s16)
    pos = (ti * tm + 1 + lax.broadcasted_iota(jnp.int32, (tm, 1), 0)).astype(F32)
    mixed = []
    for g, w in enumerate(POOL_WINDOWS):
        mean = sums[g][POOL_HALO:, :] / jnp.minimum(pos, float(w))
        pooled = mean - v[:, g * B_GROUP:(g + 1) * B_GROUP]
        mixed.append(_dot(pooled.astype(BF16), pw_ref[g]))
    b_out = jnp.concatenate(mixed, axis=1) * ps_ref[...] * _silu(gb_ref[...])
    y = _dot(a_ref[...].astype(BF16), wo_ref[:A_WIDTH, :]) + _dot(b_out.astype(BF16), wo_ref[A_WIDTH:, :])
    o_ref[...] = x_ref[...] + y


def even_tail(z, a_out, x, pool_w, pool_scale, w_out, *, batch, seq, tm):
    vb_col = 4 * A_WIDTH // B_WIDTH
    halo_per_tile = tm // POOL_HALO
    return pl.pallas_call(
        functools.partial(_even_tail_kernel, tm=tm),
        out_shape=jax.ShapeDtypeStruct((batch, seq, D_MODEL), F32),
        grid=(batch, seq // tm),
        in_specs=[
            pl.BlockSpec((None, tm, B_WIDTH), lambda b, t: (b, t, vb_col)),
            pl.BlockSpec((None, POOL_HALO, B_WIDTH),
                         lambda b, t: (b, jnp.maximum(t * halo_per_tile - 1, 0), vb_col)),
            pl.BlockSpec((None, tm, B_WIDTH), lambda b, t: (b, t, vb_col + 1)),
            pl.BlockSpec((None, tm, A_WIDTH), lambda b, t: (b, t, 0)),
            pl.BlockSpec((None, tm, D_MODEL), lambda b, t: (b, t, 0)),
            pl.BlockSpec((len(POOL_WINDOWS), B_GROUP, B_GROUP), lambda b, t: (0, 0, 0)),
            pl.BlockSpec((1, B_WIDTH), lambda b, t: (0, 0)),
            pl.BlockSpec((A_WIDTH + B_WIDTH, D_MODEL), lambda b, t: (0, 0)),
        ],
        out_specs=pl.BlockSpec((None, tm, D_MODEL), lambda b, t: (b, t, 0)),
        compiler_params=_params("parallel", "parallel"),
        name="even_tail",
    )(z, z, z, a_out, x, pool_w, pool_scale.reshape(1, B_WIDTH), w_out)


def _s5_prep_kernel(ar_ref, ai_ref, ldt_ref, br_ref, bi_ref, abr_ref, abi_ref, bbr_ref, bbi_ref):
    ar, ai = ar_ref[...], ai_ref[...]
    dt = jnp.exp(ldt_ref[...])
    mag = jnp.exp(dt * ar)
    abar_re = mag * jnp.cos(dt * ai)
    abar_im = mag * jnp.sin(dt * ai)
    nr, ni = abar_re - 1.0, abar_im
    inv = 1.0 / (ar * ar + ai * ai)
    coef_re = (nr * ar + ni * ai) * inv
    coef_im = (ni * ar - nr * ai) * inv
    br, bi = br_ref[...], bi_ref[...]
    abr_ref[...] = abar_re
    abi_ref[...] = abar_im
    bbr_ref[...] = coef_re * br - coef_im * bi
    bbi_ref[...] = coef_re * bi + coef_im * br


def s5_prep(a_re, a_im, log_dt, b_re, b_im):
    rep = lambda t: jnp.repeat(t, S5_GROUP_DIM, axis=0)
    to_rows = lambda t: t.transpose(0, 2, 1).reshape(D_WIDTH, S5_STATE)
    shp = jax.ShapeDtypeStruct((D_WIDTH, S5_STATE), F32)
    abr, abi, bbr, bbi = pl.pallas_call(
        _s5_prep_kernel, out_shape=(shp, shp, shp, shp), name="s5_prep",
    )(rep(a_re), rep(a_im), rep(jnp.broadcast_to(log_dt[:, None], (S5_GROUPS, S5_STATE))),
      to_rows(b_re), to_rows(b_im))
    eye = jnp.eye(S5_GROUPS, dtype=F32)

    def block_diag_in(t):
        t = t.reshape(S5_GROUPS, S5_GROUP_DIM, 1, S5_STATE) * eye[:, None, :, None]
        return t.reshape(D_WIDTH, S5_LANES)

    w_b = jnp.concatenate([block_diag_in(bbr), block_diag_in(bbi)], axis=1).astype(BF16)
    abar_re = abr[::S5_GROUP_DIM].reshape(1, S5_LANES)
    abar_im = abi[::S5_GROUP_DIM].reshape(1, S5_LANES)
    return w_b, abar_re, abar_im


def s5_out_weights(c_re, c_im):
    eye = jnp.eye(S5_GROUPS, dtype=F32)

    def block_diag_out(t):
        t = t.transpose(0, 2, 1)[:, :, None, :] * eye[:, None, :, None]
        return t.reshape(S5_LANES, D_WIDTH).astype(BF16)

    return block_diag_out(c_re), block_diag_out(c_im)


def _s5_kernel(xd_ref, gd_ref, wb_ref, are_ref, aim_ref, wcr_ref, wci_ref, dsk_ref, w1_ref, w2_ref,
               o_ref, bu_ref, h_ref, *, lc, nb, lane_chunk):
    @pl.when(pl.program_id(0) == 0)
    def _():
        h_ref[...] = jnp.zeros_like(h_ref)

    u = xd_ref[...]
    bu_ref[...] = _dot(u.astype(BF16), wb_ref[...])
    for c in range(S5_LANES // lane_chunk):
        re = slice(c * lane_chunk, (c + 1) * lane_chunk)
        im = slice(S5_LANES + c * lane_chunk, S5_LANES + (c + 1) * lane_chunk)
        a_r = jnp.broadcast_to(are_ref[:, re], (nb, lane_chunk))
        a_i = jnp.broadcast_to(aim_ref[:, re], (nb, lane_chunk))

        def step(t, carry, re=re, im=im, a_r=a_r, a_i=a_i):
            h_r, h_i = carry
            row = pl.ds(pl.multiple_of(t * nb, nb), nb)
            n_r = a_r * h_r - a_i * h_i + bu_ref[row, re]
            n_i = a_r * h_i + a_i * h_r + bu_ref[row, im]
            bu_ref[row, re] = n_r
            bu_ref[row, im] = n_i
            return n_r, n_i

        h_r, h_i = lax.fori_loop(0, lc, step, (h_ref[:, re], h_ref[:, im]), unroll=4)
        h_ref[:, re] = h_r
        h_ref[:, im] = h_i

    y = (_dot(bu_ref[:, :S5_LANES].astype(BF16), wcr_ref[...])
         - _dot(bu_ref[:, S5_LANES:].astype(BF16), wci_ref[...])
         + dsk_ref[...] * u)
    y = jax.nn.gelu(y).astype(BF16)
    o_ref[...] = _dot(y, w1_ref[...]) * jax.nn.sigmoid(_dot(y, w2_ref[...])) * _silu(gd_ref[...])


def s5_layer(zd, w_b, abar_re, abar_im, w_cr, w_ci, d_skip, w1, w2, *, batch, seq, lc):
    rows = lc * batch
    full = lambda shape: pl.BlockSpec(shape, lambda c: (0, 0))
    return pl.pallas_call(
        functools.partial(_s5_kernel, lc=lc, nb=batch, lane_chunk=512),
        out_shape=jax.ShapeDtypeStruct((seq * batch, D_WIDTH), F32),
        grid=(seq // lc,),
        in_specs=[
            pl.BlockSpec((rows, D_WIDTH), lambda c: (c, 0)),
            pl.BlockSpec((rows, D_WIDTH), lambda c: (c, 1)),
            full((D_WIDTH, 2 * S5_LANES)),
            full((1, S5_LANES)),
            full((1, S5_LANES)),
            full((S5_LANES, D_WIDTH)),
            full((S5_LANES, D_WIDTH)),
            full((1, D_WIDTH)),
            full((D_WIDTH, D_WIDTH)),
            full((D_WIDTH, D_WIDTH)),
        ],
        out_specs=pl.BlockSpec((rows, D_WIDTH), lambda c: (c, 0)),
        scratch_shapes=[pltpu.VMEM((rows, 2 * S5_LANES), F32), pltpu.VMEM((batch, 2 * S5_LANES), F32)],
        compiler_params=_params("arbitrary"),
        name="s5_layer",
    )(zd, zd, w_b, abar_re, abar_im, w_cr, w_ci, d_skip.reshape(1, D_WIDTH), w1, w2)


def _odd_tail_kernel(u_ref, v_ref, gc_ref, d_ref, x_ref, lng_ref, lnb_ref, ws_ref, bs_ref, wo_ref,
                     o_ref, c_ref, *, tm):
    v = v_ref[...]
    mu = jnp.mean(v, axis=-1, keepdims=True)
    vc = v - mu
    var = jnp.mean(vc * vc, axis=-1, keepdims=True)
    vn = (vc * lax.rsqrt(var + EPS) * lng_ref[...] + lnb_ref[...]).astype(BF16)
    ri = lax.broadcasted_iota(jnp.int32, (C_CHUNK, C_CHUNK), 0)
    ci = lax.broadcasted_iota(jnp.int32, (C_CHUNK, C_CHUNK), 1)
    causal = ri >= ci
    for g in range(C_GROUPS):
        cols = slice(g * C_GROUP_DIM, (g + 1) * C_GROUP_DIM)
        w_g = jnp.where(causal, ws_ref[g], 0.0).astype(BF16)
        b_g = bs_ref[:, g:g + 1]
        for c in range(tm // C_CHUNK):
            rows = slice(c * C_CHUNK, (c + 1) * C_CHUNK)
            mixed = _dot(w_g, vn[rows, cols]) + b_g
            c_ref[rows, cols] = (u_ref[rows, cols] * mixed * _silu(gc_ref[rows, cols])).astype(BF16)
    y = _dot(c_ref[...], wo_ref[:C_WIDTH, :]) + _dot(d_ref[...].astype(BF16), wo_ref[C_WIDTH:, :])
    o_ref[...] = x_ref[...] + y


def odd_tail(zc, d_out, x, ln_g, ln_b, w_s, b_s, w_out, *, batch, seq, tm):
    return pl.pallas_call(
        functools.partial(_odd_tail_kernel, tm=tm),
        out_shape=jax.ShapeDtypeStruct((batch, seq, D_MODEL), F32),
        grid=(batch, seq // tm),
        in_specs=[
            pl.BlockSpec((None, tm, C_WIDTH), lambda b, t: (b, t, 0)),
            pl.BlockSpec((None, tm, C_WIDTH), lambda b, t: (b, t, 1)),
            pl.BlockSpec((None, tm, C_WIDTH), lambda b, t: (b, t, 2)),
            pl.BlockSpec((tm, D_WIDTH), lambda b, t: (t, b)),
            pl.BlockSpec((None, tm, D_MODEL), lambda b, t: (b, t, 0)),
            pl.BlockSpec((1, C_WIDTH), lambda b, t: (0, 0)),
            pl.BlockSpec((1, C_WIDTH), lambda b, t: (0, 0)),
            pl.BlockSpec((C_GROUPS, C_CHUNK, C_CHUNK), lambda b, t: (0, 0, 0)),
            pl.BlockSpec((C_CHUNK, C_GROUPS), lambda b, t: (0, 0)),
            pl.BlockSpec((C_WIDTH + D_WIDTH, D_MODEL), lambda b, t: (0, 0)),
        ],
        out_specs=pl.BlockSpec((None, tm, D_MODEL), lambda b, t: (b, t, 0)),
        scratch_shapes=[pltpu.VMEM((tm, C_WIDTH), BF16)],
        compiler_params=_params("parallel", "parallel"),
        name="odd_tail",
    )(zc, zc, zc, d_out, x, ln_g.reshape(1, C_WIDTH), ln_b.reshape(1, C_WIDTH), w_s, b_s.T, w_out)


def _xattn_kernel(x_ref, g_ref, wq_ref, kv_ref, wo_ref, fg_ref, o_ref, *, final):
    x = x_ref[...]
    q = _dot(_rms(x, g_ref[...]).astype(BF16), wq_ref[...])
    heads = []
    for h in range(X_HEADS):
        ks = slice(h * X_HEAD_DIM, (h + 1) * X_HEAD_DIM)
        vs = slice(D_MODEL + h * X_HEAD_DIM, D_MODEL + (h + 1) * X_HEAD_DIM)
        s = _dot_nt(q[:, ks].astype(BF16), kv_ref[:, ks]) * (X_HEAD_DIM ** -0.5)
        m = jnp.max(s, axis=-1, keepdims=True)
        p = jnp.exp(s - m)
        den = jnp.sum(p, axis=-1, keepdims=True)
        heads.append((_dot(p.astype(BF16), kv_ref[:, vs]) / den).astype(BF16))
    y = x + _dot(jnp.concatenate(heads, axis=1), wo_ref[...])
    if final:
        y = _rms(y, fg_ref[...])
    o_ref[...] = y


def cross_attention(x, g, w_q, kv, w_o, final_g, *, batch, seq, mem_len, tq, final):
    return pl.pallas_call(
        functools.partial(_xattn_kernel, final=final),
        out_shape=jax.ShapeDtypeStruct((batch, seq, D_MODEL), F32),
        grid=(batch, seq // tq),
        in_specs=[
            pl.BlockSpec((None, tq, D_MODEL), lambda b, t: (b, t, 0)),
            pl.BlockSpec((1, D_MODEL), lambda b, t: (0, 0)),
            pl.BlockSpec((D_MODEL, D_MODEL), lambda b, t: (0, 0)),
            pl.BlockSpec((None, mem_len, 2 * D_MODEL), lambda b, t: (b, 0, 0)),
            pl.BlockSpec((D_MODEL, D_MODEL), lambda b, t: (0, 0)),
            pl.BlockSpec((1, D_MODEL), lambda b, t: (0, 0)),
        ],
        out_specs=pl.BlockSpec((None, tq, D_MODEL), lambda b, t: (b, t, 0)),
        compiler_params=_params("parallel", "parallel"),
        name="cross_attention",
    )(x, g.reshape(1, D_MODEL), w_q, kv, w_o, final_g.reshape(1, D_MODEL))


def kernel(x, mem, norm_ab, w_in_ab, pool_w, pool_scale, w_out_ab, norm_cd, w_in_cd, sgu_ln_g, sgu_ln_b,
           sgu_w, sgu_b, s5_a_re, s5_a_im, s5_log_dt, s5_b_re, s5_b_im, s5_c_re, s5_c_im, s5_d,
           glu_w1, glu_w2, w_out_cd, norm_x, w_xq, w_xkv, w_xo, mem_norm, final_norm):
    batch, seq, _ = x.shape
    mem_len = mem.shape[1]
    depth = norm_x.shape[0]
    tokens = batch * seq
    tm_proj = 1024
    tiles_per_row = seq // tm_proj
    bf = lambda t: t.astype(BF16)

    mem2 = mem.reshape(batch * mem_len, D_MODEL)
    for layer in range(depth):
        i = layer // 2
        x2 = x.reshape(tokens, D_MODEL)
        if layer % 2 == 0:
            z = norm_matmul(x2, norm_ab[i], bf(w_in_ab[i]), tm=tm_proj, tn=1024)
            z = z.reshape(batch, seq, z.shape[1])
            a_out = dilated_attention(z, batch=batch, seq=seq)
            x = even_tail(z, a_out, x, bf(pool_w[i]), pool_scale[i], bf(w_out_ab[i]),
                          batch=batch, seq=seq, tm=512)
        else:
            n_c = 3 * C_WIDTH
            zc = norm_matmul(x2, norm_cd[i], bf(w_in_cd[i][:, :n_c]), tm=tm_proj, tn=1024)
            zd = norm_matmul(x2, norm_cd[i], bf(w_in_cd[i][:, n_c:]), tm=tm_proj, tn=2 * D_WIDTH,
                             out_shape=(seq, batch * 2 * D_WIDTH),
                             out_map=lambda r, j: (r % tiles_per_row, r // tiles_per_row))
            w_b, abar_re, abar_im = s5_prep(s5_a_re[i], s5_a_im[i], s5_log_dt[i], s5_b_re[i], s5_b_im[i])
            w_cr, w_ci = s5_out_weights(s5_c_re[i], s5_c_im[i])
            d_out = s5_layer(zd.reshape(seq * batch, 2 * D_WIDTH), w_b, abar_re, abar_im, w_cr, w_ci,
                             s5_d[i], bf(glu_w1[i]), bf(glu_w2[i]), batch=batch, seq=seq, lc=64)
            x = odd_tail(zc.reshape(batch, seq, n_c), d_out.reshape(seq, batch * D_WIDTH), x,
                         sgu_ln_g[i], sgu_ln_b[i], sgu_w[i], sgu_b[i], bf(w_out_cd[i]),
                         batch=batch, seq=seq, tm=512)
        kv = norm_matmul(mem2, mem_norm, bf(w_xkv[layer]), tm=batch * mem_len, tn=1024, out_dtype=BF16)
        x = cross_attention(x, norm_x[layer], bf(w_xq[layer]), kv.reshape(batch, mem_len, 2 * D_MODEL),
                            bf(w_xo[layer]), final_norm, batch=batch, seq=seq, mem_len=mem_len,
                            tq=512, final=(layer == depth - 1))
    return x
```

```python
import functools

import jax
import jax.numpy as jnp
from jax import lax
from jax.experimental import pallas as pl
from jax.experimental.pallas import tpu as pltpu

F32 = jnp.float32
BF16 = jnp.bfloat16

LANES = 128
D_MODEL = 1024
A_WIDTH = 1024
A_HEAD_DIM = 64
A_BLOCK = 128
A_DILATIONS = (1, 4, 16)
DIL_STEP = 4
B_WIDTH = 1024
POOL_WINDOWS = (2, 4, 8, 16)
B_GROUP = 256
POOL_HALO = 16
C_WIDTH = 1024
C_CHUNK = 128
C_GROUPS = 4
C_GROUP_DIM = 256
D_WIDTH = 512
S5_GROUPS = 32
S5_GROUP_DIM = 16
S5_STATE = 64
S5_LANES = S5_GROUPS * S5_STATE
S5_PARTS = 2
S5_PART_IN = D_WIDTH // S5_PARTS
S5_PART_STATE = S5_LANES // S5_PARTS
X_HEADS = 4
X_HEAD_DIM = 256
EPS = 1e-6
NEG = -1e30
LOG2E = 1.4426950408889634

VMEM_LIMIT = 48 * 1024 * 1024


def _params(*sem):
    return pltpu.CompilerParams(dimension_semantics=sem, vmem_limit_bytes=VMEM_LIMIT)


def _rms(x, g):
    ms = jnp.mean(x * x, axis=-1, keepdims=True)
    return x * lax.rsqrt(ms + EPS) * g


def _silu(x):
    return x * jax.nn.sigmoid(x)


def _dot(a, b):
    return jnp.dot(a, b, preferred_element_type=F32)


def _dot_nt(a, b):
    return lax.dot_general(a, b, (((1,), (1,)), ((), ())), preferred_element_type=F32)


def _norm_matmul_kernel(x_ref, g_ref, w_ref, o_ref, xn_ref):
    @pl.when(pl.program_id(1) == 0)
    def _():
        xn_ref[...] = _rms(x_ref[...], g_ref[...]).astype(BF16)

    o_ref[...] = _dot(xn_ref[...], w_ref[...]).astype(o_ref.dtype)


def norm_matmul(x, g, w, *, tm, tn, out_shape=None, out_map=None):
    m, k = x.shape
    n = w.shape[1]
    if out_shape is None:
        out_shape = (m, n)
        out_map = lambda i, j: (i, j)
    return pl.pallas_call(
        _norm_matmul_kernel,
        out_shape=jax.ShapeDtypeStruct(out_shape, BF16),
        grid=(m // tm, n // tn),
        in_specs=[
            pl.BlockSpec((tm, k), lambda i, j: (i, 0)),
            pl.BlockSpec((1, k), lambda i, j: (0, 0)),
            pl.BlockSpec((k, tn), lambda i, j: (0, j)),
        ],
        out_specs=pl.BlockSpec((tm, tn), out_map),
        scratch_shapes=[pltpu.VMEM((tm, k), BF16)],
        compiler_params=_params("parallel", "arbitrary"),
        name="norm_matmul",
    )(x, g.reshape(1, k), w)


def _dilated_kernel(q_ref, k_ref, v_ref, g_ref, o_ref, qs_ref, ks_ref, vs_ref, gs_ref, os_ref, num_ref, den_ref,
                    m_ref, bias_ref, *, seq, unroll):
    assert A_DILATIONS == (1, DIL_STEP, DIL_STEP * DIL_STEP)
    qi = lax.broadcasted_iota(jnp.int32, (A_BLOCK, 2 * A_BLOCK), 0)
    kj = lax.broadcasted_iota(jnp.int32, (A_BLOCK, 2 * A_BLOCK), 1)
    dist = qi + A_BLOCK - kj
    band = (dist >= 0) & (dist <= A_BLOCK)
    bias_ref[1] = jnp.where(band, 0.0, NEG)
    bias_ref[0] = jnp.where(band & (kj >= A_BLOCK), 0.0, NEG)

    len_s = seq // DIL_STEP
    len_w = len_s // DIL_STEP
    blocks_s = len_s // A_BLOCK
    q_scale = A_HEAD_DIM ** -0.5 * LOG2E
    for src, dst, scale in ((q_ref, qs_ref, q_scale), (k_ref, ks_ref, None), (v_ref, vs_ref, None)):
        nat = src[...].astype(F32)
        dst[0] = nat if scale is None else nat * scale
        for b in range(DIL_STEP):
            dst[1, b * len_s:(b + 1) * len_s, :] = dst[0, pl.ds(b, len_s, stride=DIL_STEP), :]
        for c in range(DIL_STEP * DIL_STEP):
            b, a = divmod(c, DIL_STEP)
            dst[2, c * len_w:(c + 1) * len_w, :] = dst[1, pl.ds(b * len_s + a, len_w, stride=DIL_STEP), :]
    gs_ref[0] = g_ref[...].astype(F32)

    head0 = lax.broadcasted_iota(jnp.int32, (A_BLOCK, LANES), 1) < A_HEAD_DIM

    def load(slab, cur, prev):
        q = qs_ref[slab, cur, :]
        q2 = jnp.concatenate([jnp.where(head0, q, 0.0), jnp.where(head0, 0.0, q)], axis=0).astype(BF16)
        if prev is None:
            return q2, ks_ref[slab, cur, :].astype(BF16), vs_ref[slab, cur, :].astype(BF16)
        kk = jnp.concatenate([ks_ref[slab, prev, :], ks_ref[slab, cur, :]], axis=0).astype(BF16)
        vv = jnp.concatenate([vs_ref[slab, prev, :], vs_ref[slab, cur, :]], axis=0).astype(BF16)
        return q2, kk, vv

    def attend(q2, kk, vv, bias):
        s = _dot_nt(q2, kk) + jnp.concatenate([bias, bias], axis=0)
        m = jnp.max(s, axis=-1, keepdims=True)
        p = jnp.exp2(s - m).astype(BF16)
        res = _dot(p, jnp.concatenate([vv, jnp.ones(vv.shape, BF16)], axis=1))
        top, bot = res[:A_BLOCK], res[A_BLOCK:]
        return (jnp.where(head0, top[:, :LANES], bot[:, :LANES]), jnp.where(head0, top[:, LANES:], bot[:, LANES:]),
                jnp.where(head0, m[:A_BLOCK], m[A_BLOCK:]))

    def aligned(start):
        return pl.ds(pl.multiple_of(start, A_BLOCK), A_BLOCK)

    def banded_rows(idx, blocks_per_seq):
        start = idx * A_BLOCK
        first = idx % blocks_per_seq == 0
        return aligned(start), aligned(jnp.where(first, start, start - A_BLOCK)), bias_ref[jnp.where(first, 0, 1)]

    def single_rows(idx):
        return aligned(idx * A_BLOCK), None, bias_ref[0][:, A_BLOCK:]

    def body(idx, carry, *, slab, rows_of):
        cur, prev, bias = rows_of(idx)
        num, den, m = attend(*load(slab, cur, prev), bias)
        num_ref[slab, cur, :] = num
        den_ref[slab, cur, :] = den
        m_ref[slab, cur, :] = m
        return carry

    assert len_w == A_BLOCK
    n_blocks = seq // A_BLOCK
    for slab, rows_of in ((0, functools.partial(banded_rows, blocks_per_seq=n_blocks)),
                          (1, functools.partial(banded_rows, blocks_per_seq=blocks_s)),
                          (2, single_rows)):
        lax.fori_loop(0, n_blocks, functools.partial(body, slab=slab, rows_of=rows_of), 0, unroll=unroll)

    def weights(m_a, m_b):
        m = jnp.maximum(m_a, m_b)
        return m, jnp.exp2(m_a - m), jnp.exp2(m_b - m)

    def fold_step(c, carry):
        by_step = aligned(c * A_BLOCK)
        natural = pl.ds(c // blocks_s + DIL_STEP * A_BLOCK * (c % blocks_s), A_BLOCK, stride=DIL_STEP)
        m, w_a, w_b = weights(m_ref[0, natural, :], m_ref[1, by_step, :])
        num_ref[1, by_step, :] = w_a * num_ref[0, natural, :] + w_b * num_ref[1, by_step, :]
        den_ref[1, by_step, :] = w_a * den_ref[0, natural, :] + w_b * den_ref[1, by_step, :]
        m_ref[1, by_step, :] = m
        gs_ref[1, by_step, :] = gs_ref[0, natural, :]
        return carry

    lax.fori_loop(0, n_blocks, fold_step, 0)

    def fold_wide(c, carry):
        wide = aligned(c * A_BLOCK)
        by_step = pl.ds((c // DIL_STEP) * len_s + c % DIL_STEP, A_BLOCK, stride=DIL_STEP)
        natural = pl.ds(DIL_STEP * (c % DIL_STEP) + c // DIL_STEP, A_BLOCK, stride=DIL_STEP * DIL_STEP)
        _, w_a, w_b = weights(m_ref[1, by_step, :], m_ref[2, wide, :])
        num = w_a * num_ref[1, by_step, :] + w_b * num_ref[2, wide, :]
        den = w_a * den_ref[1, by_step, :] + w_b * den_ref[2, wide, :]
        os_ref[natural, :] = num / den * _silu(gs_ref[1, by_step, :])
        return carry

    lax.fori_loop(0, n_blocks, fold_wide, 0)
    o_ref[...] = os_ref[...].astype(o_ref.dtype)


def dilated_attention(z, *, batch, seq, unroll=8):
    n_pair = A_WIDTH // LANES
    n_pat = len(A_DILATIONS)
    blk = (None, seq, LANES)
    slabs = pltpu.VMEM((n_pat, seq, LANES), F32)
    return pl.pallas_call(
        functools.partial(_dilated_kernel, seq=seq, unroll=unroll),
        out_shape=jax.ShapeDtypeStruct((batch, seq, A_WIDTH), BF16),
        grid=(batch, n_pair),
        in_specs=[
            pl.BlockSpec(blk, lambda b, h: (b, 0, h)),
            pl.BlockSpec(blk, lambda b, h: (b, 0, n_pair + h)),
            pl.BlockSpec(blk, lambda b, h: (b, 0, 2 * n_pair + h)),
            pl.BlockSpec(blk, lambda b, h: (b, 0, 3 * n_pair + h)),
        ],
        out_specs=pl.BlockSpec(blk, lambda b, h: (b, 0, h)),
        scratch_shapes=[
            slabs, slabs, slabs,
            pltpu.VMEM((2, seq, LANES), F32),
            pltpu.VMEM((seq, LANES), F32),
            slabs, slabs, slabs,
            pltpu.VMEM((2, A_BLOCK, 2 * A_BLOCK), F32),
        ],
        compiler_params=_params("parallel", "parallel"),
        name="dilated_attention",
    )(z, z, z, z)


def _even_tail_kernel(vb_ref, halo_ref, gb_ref, a_ref, x_ref, pw_ref, ps_ref, wo_ref, o_ref, *, tm):
    ti = pl.program_id(1)
    v = vb_ref[...].astype(F32)
    halo = jnp.where(ti > 0, halo_ref[...].astype(F32), 0.0)
    xc = jnp.concatenate([halo, v], axis=0)
    s2 = xc + pltpu.roll(xc, 1, 0)
    t4 = s2[:, B_GROUP:]
    s4 = t4 + pltpu.roll(t4, 2, 0)
    t8 = s4[:, B_GROUP:]
    s8 = t8 + pltpu.roll(t8, 4, 0)
    t16 = s8[:, B_GROUP:]
    s16 = t16 + pltpu.roll(t16, 8, 0)
    sums = (s2[:, :B_GROUP], s4[:, :B_GROUP], s8[:, :B_GROUP], s16)
    pos = (ti * tm + 1 + lax.broadcasted_iota(jnp.int32, (tm, 1), 0)).astype(F32)
    mixed = []
    for g, w in enumerate(POOL_WINDOWS):
        mean = sums[g][POOL_HALO:, :] / jnp.minimum(pos, float(w))
        pooled = mean - v[:, g * B_GROUP:(g + 1) * B_GROUP]
        mixed.append(_dot(pooled.astype(BF16), pw_ref[g]))
    b_out = jnp.concatenate(mixed, axis=1) * ps_ref[...] * _silu(gb_ref[...].astype(F32))
    y = _dot(a_ref[...], wo_ref[:A_WIDTH, :]) + _dot(b_out.astype(BF16), wo_ref[A_WIDTH:, :])
    o_ref[...] = x_ref[...] + y


def even_tail(z, a_out, x, pool_w, pool_scale, w_out, *, batch, seq, tm):
    vb_col = 4 * A_WIDTH // B_WIDTH
    halo_per_tile = tm // POOL_HALO
    return pl.pallas_call(
        functools.partial(_even_tail_kernel, tm=tm),
        out_shape=jax.ShapeDtypeStruct((batch, seq, D_MODEL), F32),
        grid=(batch, seq // tm),
        in_specs=[
            pl.BlockSpec((None, tm, B_WIDTH), lambda b, t: (b, t, vb_col)),
            pl.BlockSpec((None, POOL_HALO, B_WIDTH),
                         lambda b, t: (b, jnp.maximum(t * halo_per_tile - 1, 0), vb_col)),
            pl.BlockSpec((None, tm, B_WIDTH), lambda b, t: (b, t, vb_col + 1)),
            pl.BlockSpec((None, tm, A_WIDTH), lambda b, t: (b, t, 0)),
            pl.BlockSpec((None, tm, D_MODEL), lambda b, t: (b, t, 0)),
            pl.BlockSpec((len(POOL_WINDOWS), B_GROUP, B_GROUP), lambda b, t: (0, 0, 0)),
            pl.BlockSpec((1, B_WIDTH), lambda b, t: (0, 0)),
            pl.BlockSpec((A_WIDTH + B_WIDTH, D_MODEL), lambda b, t: (0, 0)),
        ],
        out_specs=pl.BlockSpec((None, tm, D_MODEL), lambda b, t: (b, t, 0)),
        compiler_params=_params("parallel", "parallel"),
        name="even_tail",
    )(z, z, z, a_out, x, pool_w, pool_scale.reshape(1, B_WIDTH), w_out)


def _s5_prep_kernel(ar_ref, ai_ref, ldt_ref, br_ref, bi_ref, abr_ref, abi_ref, bbr_ref, bbi_ref):
    ar, ai = ar_ref[...], ai_ref[...]
    dt = jnp.exp(ldt_ref[...])
    mag = jnp.exp(dt * ar)
    abar_re = mag * jnp.cos(dt * ai)
    abar_im = mag * jnp.sin(dt * ai)
    nr, ni = abar_re - 1.0, abar_im
    inv = 1.0 / (ar * ar + ai * ai)
    coef_re = (nr * ar + ni * ai) * inv
    coef_im = (ni * ar - nr * ai) * inv
    br, bi = br_ref[...], bi_ref[...]
    abr_ref[...] = abar_re
    abi_ref[...] = abar_im
    bbr_ref[...] = coef_re * br - coef_im * bi
    bbi_ref[...] = coef_re * bi + coef_im * br


def _part_eye():
    return jnp.eye(S5_GROUPS // S5_PARTS, dtype=F32)


def s5_prep(a_re, a_im, log_dt, b_re, b_im):
    rep = lambda t: jnp.repeat(t, S5_GROUP_DIM, axis=0)
    to_rows = lambda t: t.transpose(0, 2, 1).reshape(D_WIDTH, S5_STATE)
    shp = jax.ShapeDtypeStruct((D_WIDTH, S5_STATE), F32)
    abr, abi, bbr, bbi = pl.pallas_call(
        _s5_prep_kernel, out_shape=(shp, shp, shp, shp), name="s5_prep",
    )(rep(a_re), rep(a_im), rep(jnp.broadcast_to(log_dt[:, None], (S5_GROUPS, S5_STATE))),
      to_rows(b_re), to_rows(b_im))
    gpp = S5_GROUPS // S5_PARTS

    def block_diag_in(t):
        t = t.reshape(S5_PARTS, gpp, S5_GROUP_DIM, 1, S5_STATE) * _part_eye()[None, :, None, :, None]
        return t.reshape(S5_PARTS, S5_PART_IN, S5_PART_STATE)

    w_b = jnp.concatenate([block_diag_in(bbr), block_diag_in(bbi)], axis=2).astype(BF16)
    abar_re = abr[::S5_GROUP_DIM].reshape(1, S5_LANES)
    abar_im = abi[::S5_GROUP_DIM].reshape(1, S5_LANES)
    return w_b, abar_re, abar_im


def s5_out_weights(c_re, c_im):
    gpp = S5_GROUPS // S5_PARTS

    def block_diag_out(t):
        t = t.reshape(S5_PARTS, gpp, S5_GROUP_DIM, S5_STATE).transpose(0, 1, 3, 2)
        t = t[:, :, :, None, :] * _part_eye()[None, :, None, :, None]
        return t.reshape(S5_PARTS, S5_PART_STATE, S5_PART_IN).astype(BF16)

    return block_diag_out(c_re), block_diag_out(c_im)


def _s5_kernel(xd_ref, wb_ref, are_ref, aim_ref, wcr_ref, wci_ref, dsk_ref, w1_ref, w2_ref,
               o_ref, u_ref, y_ref, bu_ref, h_ref, *, lc, nb, lane_chunk):
    @pl.when(pl.program_id(0) == 0)
    def _():
        h_ref[...] = jnp.zeros_like(h_ref)

    n_slab = D_WIDTH // LANES
    xd = xd_ref[...].astype(F32)
    for b in range(nb):
        for s in range(n_slab):
            col = b * D_WIDTH + s * LANES
            u_ref[s, pl.ds(b, lc, stride=nb), :] = xd[:, col:col + LANES]
    u = jnp.concatenate([u_ref[s] for s in range(n_slab)], axis=1)
    ub = u.astype(BF16)
    width = 2 * S5_PART_STATE
    for part in range(S5_PARTS):
        bu_ref[:, part * width:(part + 1) * width] = _dot(
            ub[:, part * S5_PART_IN:(part + 1) * S5_PART_IN], wb_ref[part])

    for part in range(S5_PARTS):
        for c in range(S5_PART_STATE // lane_chunk):
            re = slice(part * width + c * lane_chunk, part * width + (c + 1) * lane_chunk)
            im = slice(re.start + S5_PART_STATE, re.stop + S5_PART_STATE)
            ab = slice(part * S5_PART_STATE + c * lane_chunk, part * S5_PART_STATE + (c + 1) * lane_chunk)
            a_r = jnp.broadcast_to(are_ref[:, ab], (nb, lane_chunk))
            a_i = jnp.broadcast_to(aim_ref[:, ab], (nb, lane_chunk))

            def step(t, carry, re=re, im=im, a_r=a_r, a_i=a_i):
                h_r, h_i = carry
                row = pl.ds(pl.multiple_of(t * nb, nb), nb)
                n_r = a_r * h_r - a_i * h_i + bu_ref[row, re]
                n_i = a_r * h_i + a_i * h_r + bu_ref[row, im]
                bu_ref[row, re] = n_r
                bu_ref[row, im] = n_i
                return n_r, n_i

            h_r, h_i = lax.fori_loop(0, lc, step, (h_ref[:, re], h_ref[:, im]), unroll=4)
            h_ref[:, re] = h_r
            h_ref[:, im] = h_i

    ys = []
    for part in range(S5_PARTS):
        re = slice(part * width, part * width + S5_PART_STATE)
        im = slice(re.stop, re.stop + S5_PART_STATE)
        ys.append(_dot(bu_ref[:, re].astype(BF16), wcr_ref[part]) - _dot(bu_ref[:, im].astype(BF16), wci_ref[part]))
    y = jnp.concatenate(ys, axis=1) + dsk_ref[...] * u
    y = jax.nn.gelu(y).astype(BF16)
    out = _dot(y, w1_ref[...]) * jax.nn.sigmoid(_dot(y, w2_ref[...]))
    for s in range(n_slab):
        y_ref[s] = out[:, s * LANES:(s + 1) * LANES]
    o_ref[...] = jnp.concatenate(
        [y_ref[s, pl.ds(b, lc, stride=nb), :] for b in range(nb) for s in range(n_slab)], axis=1).astype(o_ref.dtype)


def s5_layer(zd, w_b, abar_re, abar_im, w_cr, w_ci, d_skip, w1, w2, *, batch, seq, lc):
    rows = lc * batch
    n_slab = D_WIDTH // LANES
    full2 = lambda shape: pl.BlockSpec(shape, lambda c: (0, 0))
    full3 = lambda shape: pl.BlockSpec(shape, lambda c: (0, 0, 0))
    return pl.pallas_call(
        functools.partial(_s5_kernel, lc=lc, nb=batch, lane_chunk=512),
        out_shape=jax.ShapeDtypeStruct((seq, batch * D_WIDTH), BF16),
        grid=(seq // lc,),
        in_specs=[
            pl.BlockSpec((lc, batch * D_WIDTH), lambda c: (c, 0)),
            full3((S5_PARTS, S5_PART_IN, 2 * S5_PART_STATE)),
            full2((1, S5_LANES)),
            full2((1, S5_LANES)),
            full3((S5_PARTS, S5_PART_STATE, S5_PART_IN)),
            full3((S5_PARTS, S5_PART_STATE, S5_PART_IN)),
            full2((1, D_WIDTH)),
            full2((D_WIDTH, D_WIDTH)),
            full2((D_WIDTH, D_WIDTH)),
        ],
        out_specs=pl.BlockSpec((lc, batch * D_WIDTH), lambda c: (c, 0)),
        scratch_shapes=[
            pltpu.VMEM((n_slab, rows, LANES), F32),
            pltpu.VMEM((n_slab, rows, LANES), F32),
            pltpu.VMEM((rows, 2 * S5_LANES), F32),
            pltpu.VMEM((batch, 2 * S5_LANES), F32),
        ],
        compiler_params=_params("arbitrary"),
        name="s5_layer",
    )(zd, w_b, abar_re, abar_im, w_cr, w_ci, d_skip.reshape(1, D_WIDTH), w1, w2)


def _odd_tail_kernel(u_ref, v_ref, gc_ref, d_ref, gd_ref, x_ref, lng_ref, lnb_ref, ws_ref, bs_ref, wo_ref,
                     o_ref, c_ref, *, tm):
    v = v_ref[...].astype(F32)
    mu = jnp.mean(v, axis=-1, keepdims=True)
    vc = v - mu
    var = jnp.mean(vc * vc, axis=-1, keepdims=True)
    vn = (vc * lax.rsqrt(var + EPS) * lng_ref[...] + lnb_ref[...]).astype(BF16)
    ri = lax.broadcasted_iota(jnp.int32, (C_CHUNK, C_CHUNK), 0)
    ci = lax.broadcasted_iota(jnp.int32, (C_CHUNK, C_CHUNK), 1)
    causal = ri >= ci
    for g in range(C_GROUPS):
        cols = slice(g * C_GROUP_DIM, (g + 1) * C_GROUP_DIM)
        w_g = jnp.where(causal, ws_ref[g], 0.0).astype(BF16)
        b_g = bs_ref[:, g:g + 1]
        for c in range(tm // C_CHUNK):
            rows = slice(c * C_CHUNK, (c + 1) * C_CHUNK)
            mixed = _dot(w_g, vn[rows, cols]) + b_g
            gate = _silu(gc_ref[rows, cols].astype(F32))
            c_ref[rows, cols] = (u_ref[rows, cols].astype(F32) * mixed * gate).astype(BF16)
    d = (d_ref[...].astype(F32) * _silu(gd_ref[...].astype(F32))).astype(BF16)
    y = _dot(c_ref[...], wo_ref[:C_WIDTH, :]) + _dot(d, wo_ref[C_WIDTH:, :])
    o_ref[...] = x_ref[...] + y


def odd_tail(zc, zd, d_out, x, ln_g, ln_b, w_s, b_s, w_out, *, batch, seq, tm):
    return pl.pallas_call(
        functools.partial(_odd_tail_kernel, tm=tm),
        out_shape=jax.ShapeDtypeStruct((batch, seq, D_MODEL), F32),
        grid=(batch, seq // tm),
        in_specs=[
            pl.BlockSpec((None, tm, C_WIDTH), lambda b, t: (b, t, 0)),
            pl.BlockSpec((None, tm, C_WIDTH), lambda b, t: (b, t, 1)),
            pl.BlockSpec((None, tm, C_WIDTH), lambda b, t: (b, t, 2)),
            pl.BlockSpec((tm, D_WIDTH), lambda b, t: (t, b)),
            pl.BlockSpec((tm, D_WIDTH), lambda b, t: (t, batch + b)),
            pl.BlockSpec((None, tm, D_MODEL), lambda b, t: (b, t, 0)),
            pl.BlockSpec((1, C_WIDTH), lambda b, t: (0, 0)),
            pl.BlockSpec((1, C_WIDTH), lambda b, t: (0, 0)),
            pl.BlockSpec((C_GROUPS, C_CHUNK, C_CHUNK), lambda b, t: (0, 0, 0)),
            pl.BlockSpec((C_CHUNK, C_GROUPS), lambda b, t: (0, 0)),
            pl.BlockSpec((C_WIDTH + D_WIDTH, D_MODEL), lambda b, t: (0, 0)),
        ],
        out_specs=pl.BlockSpec((None, tm, D_MODEL), lambda b, t: (b, t, 0)),
        scratch_shapes=[pltpu.VMEM((tm, C_WIDTH), BF16)],
        compiler_params=_params("parallel", "parallel"),
        name="odd_tail",
    )(zc, zc, zc, d_out, zd, x, ln_g.reshape(1, C_WIDTH), ln_b.reshape(1, C_WIDTH), w_s, b_s.T, w_out)


def _xattn_kernel(x_ref, g_ref, wq_ref, kv_ref, wo_ref, fg_ref, o_ref, *, final):
    x = x_ref[...]
    q = _dot(_rms(x, g_ref[...]).astype(BF16), wq_ref[...])
    heads = []
    for h in range(X_HEADS):
        ks = slice(h * X_HEAD_DIM, (h + 1) * X_HEAD_DIM)
        vs = slice(D_MODEL + h * X_HEAD_DIM, D_MODEL + (h + 1) * X_HEAD_DIM)
        s = _dot_nt(q[:, ks].astype(BF16), kv_ref[:, ks]) * (X_HEAD_DIM ** -0.5)
        m = jnp.max(s, axis=-1, keepdims=True)
        p = jnp.exp(s - m)
        den = jnp.sum(p, axis=-1, keepdims=True)
        heads.append((_dot(p.astype(BF16), kv_ref[:, vs]) / den).astype(BF16))
    y = x + _dot(jnp.concatenate(heads, axis=1), wo_ref[...])
    if final:
        y = _rms(y, fg_ref[...])
    o_ref[...] = y


def cross_attention(x, g, w_q, kv, w_o, final_g, *, batch, seq, mem_len, tq, final):
    return pl.pallas_call(
        functools.partial(_xattn_kernel, final=final),
        out_shape=jax.ShapeDtypeStruct((batch, seq, D_MODEL), F32),
        grid=(batch, seq // tq),
        in_specs=[
            pl.BlockSpec((None, tq, D_MODEL), lambda b, t: (b, t, 0)),
            pl.BlockSpec((1, D_MODEL), lambda b, t: (0, 0)),
            pl.BlockSpec((D_MODEL, D_MODEL), lambda b, t: (0, 0)),
            pl.BlockSpec((None, mem_len, 2 * D_MODEL), lambda b, t: (b, 0, 0)),
            pl.BlockSpec((D_MODEL, D_MODEL), lambda b, t: (0, 0)),
            pl.BlockSpec((1, D_MODEL), lambda b, t: (0, 0)),
        ],
        out_specs=pl.BlockSpec((None, tq, D_MODEL), lambda b, t: (b, t, 0)),
        compiler_params=_params("parallel", "parallel"),
        name="cross_attention",
    )(x, g.reshape(1, D_MODEL), w_q, kv, w_o, final_g.reshape(1, D_MODEL))


def kernel(x, mem, norm_ab, w_in_ab, pool_w, pool_scale, w_out_ab, norm_cd, w_in_cd, sgu_ln_g, sgu_ln_b,
           sgu_w, sgu_b, s5_a_re, s5_a_im, s5_log_dt, s5_b_re, s5_b_im, s5_c_re, s5_c_im, s5_d,
           glu_w1, glu_w2, w_out_cd, norm_x, w_xq, w_xkv, w_xo, mem_norm, final_norm):
    batch, seq, _ = x.shape
    mem_len = mem.shape[1]
    depth = norm_x.shape[0]
    tokens = batch * seq
    tm_proj = 1024
    tiles_per_row = seq // tm_proj
    bf = lambda t: t.astype(BF16)

    mem2 = mem.reshape(batch * mem_len, D_MODEL)
    for layer in range(depth):
        i = layer // 2
        x2 = x.reshape(tokens, D_MODEL)
        if layer % 2 == 0:
            z = norm_matmul(x2, norm_ab[i], bf(w_in_ab[i]), tm=tm_proj, tn=1024)
            z = z.reshape(batch, seq, z.shape[1])
            a_out = dilated_attention(z, batch=batch, seq=seq)
            x = even_tail(z, a_out, x, bf(pool_w[i]), pool_scale[i], bf(w_out_ab[i]),
                          batch=batch, seq=seq, tm=512)
        else:
            n_c = 3 * C_WIDTH
            zc = norm_matmul(x2, norm_cd[i], bf(w_in_cd[i][:, :n_c]), tm=tm_proj, tn=1024)
            zd = norm_matmul(x2, norm_cd[i], bf(w_in_cd[i][:, n_c:]), tm=tm_proj, tn=D_WIDTH,
                             out_shape=(seq, 2 * batch * D_WIDTH),
                             out_map=lambda r, j: (r % tiles_per_row, j * batch + r // tiles_per_row))
            w_b, abar_re, abar_im = s5_prep(s5_a_re[i], s5_a_im[i], s5_log_dt[i], s5_b_re[i], s5_b_im[i])
            w_cr, w_ci = s5_out_weights(s5_c_re[i], s5_c_im[i])
            d_out = s5_layer(zd, w_b, abar_re, abar_im, w_cr, w_ci, s5_d[i], bf(glu_w1[i]), bf(glu_w2[i]),
                             batch=batch, seq=seq, lc=64)
            x = odd_tail(zc.reshape(batch, seq, n_c), zd, d_out, x, sgu_ln_g[i], sgu_ln_b[i], sgu_w[i],
                         sgu_b[i], bf(w_out_cd[i]), batch=batch, seq=seq, tm=512)
        kv = norm_matmul(mem2, mem_norm, bf(w_xkv[layer]), tm=batch * mem_len, tn=1024)
        x = cross_attention(x, norm_x[layer], bf(w_xq[layer]), kv.reshape(batch, mem_len, 2 * D_MODEL),
                            bf(w_xo[layer]), final_norm, batch=batch, seq=seq, mem_len=mem_len,
                            tq=512, final=(layer == depth - 1))
    return x
```

```python
import functools

import jax
import jax.numpy as jnp
from jax import lax
from jax.experimental import pallas as pl
from jax.experimental.pallas import tpu as pltpu

F32 = jnp.float32
BF16 = jnp.bfloat16

LANES = 128
D_MODEL = 1024
A_WIDTH = 1024
A_HEAD_DIM = 64
A_BLOCK = 128
A_DILATIONS = (1, 4, 16)
DIL_STEP = 4
B_WIDTH = 1024
POOL_WINDOWS = (2, 4, 8, 16)
B_GROUP = 256
POOL_HALO = 16
C_WIDTH = 1024
C_CHUNK = 128
C_GROUPS = 4
C_GROUP_DIM = 256
D_WIDTH = 512
S5_GROUPS = 32
S5_GROUP_DIM = 16
S5_STATE = 64
S5_LANES = S5_GROUPS * S5_STATE
S5_PARTS = 2
S5_PART_IN = D_WIDTH // S5_PARTS
S5_PART_STATE = S5_LANES // S5_PARTS
X_HEADS = 4
X_HEAD_DIM = 256
EPS = 1e-6
NEG = -1e30
LOG2E = 1.4426950408889634

VMEM_LIMIT = 48 * 1024 * 1024


def _params(*sem):
    return pltpu.CompilerParams(dimension_semantics=sem, vmem_limit_bytes=VMEM_LIMIT)


def _rms(x, g):
    ms = jnp.mean(x * x, axis=-1, keepdims=True)
    return x * lax.rsqrt(ms + EPS) * g


def _silu(x):
    return x * jax.nn.sigmoid(x)


def _dot(a, b):
    return jnp.dot(a, b, preferred_element_type=F32)


def _dot_nt(a, b):
    return lax.dot_general(a, b, (((1,), (1,)), ((), ())), preferred_element_type=F32)


def _norm_matmul_kernel(x_ref, g_ref, w_ref, o_ref, xn_ref):
    @pl.when(pl.program_id(1) == 0)
    def _():
        xn_ref[...] = _rms(x_ref[...], g_ref[...]).astype(BF16)

    o_ref[...] = _dot(xn_ref[...], w_ref[...]).astype(o_ref.dtype)


def norm_matmul(x, g, w, *, tm, tn, out_shape=None, out_map=None):
    m, k = x.shape
    n = w.shape[1]
    if out_shape is None:
        out_shape = (m, n)
        out_map = lambda i, j: (i, j)
    return pl.pallas_call(
        _norm_matmul_kernel,
        out_shape=jax.ShapeDtypeStruct(out_shape, BF16),
        grid=(m // tm, n // tn),
        in_specs=[
            pl.BlockSpec((tm, k), lambda i, j: (i, 0)),
            pl.BlockSpec((1, k), lambda i, j: (0, 0)),
            pl.BlockSpec((k, tn), lambda i, j: (0, j)),
        ],
        out_specs=pl.BlockSpec((tm, tn), out_map),
        scratch_shapes=[pltpu.VMEM((tm, k), BF16)],
        compiler_params=_params("parallel", "arbitrary"),
        name="norm_matmul",
    )(x, g.reshape(1, k), w)


def _dilated_kernel(q_ref, k_ref, v_ref, g_ref, o_ref, qs_ref, ks_ref, vs_ref, gs_ref, os_ref, num_ref, den_ref,
                    m_ref, bias_ref, *, seq, unroll):
    assert A_DILATIONS == (1, DIL_STEP, DIL_STEP * DIL_STEP)
    qi = lax.broadcasted_iota(jnp.int32, (A_BLOCK, 2 * A_BLOCK), 0)
    kj = lax.broadcasted_iota(jnp.int32, (A_BLOCK, 2 * A_BLOCK), 1)
    dist = qi + A_BLOCK - kj
    band = (dist >= 0) & (dist <= A_BLOCK)
    bias_ref[1] = jnp.where(band, 0.0, NEG)
    bias_ref[0] = jnp.where(band & (kj >= A_BLOCK), 0.0, NEG)

    len_s = seq // DIL_STEP
    len_w = len_s // DIL_STEP
    blocks_s = len_s // A_BLOCK
    q_scale = A_HEAD_DIM ** -0.5 * LOG2E
    for src, dst, scale in ((q_ref, qs_ref, q_scale), (k_ref, ks_ref, None), (v_ref, vs_ref, None)):
        nat = src[...].astype(F32)
        dst[0] = nat if scale is None else nat * scale
        for b in range(DIL_STEP):
            dst[1, b * len_s:(b + 1) * len_s, :] = dst[0, pl.ds(b, len_s, stride=DIL_STEP), :]
        for c in range(DIL_STEP * DIL_STEP):
            b, a = divmod(c, DIL_STEP)
            dst[2, c * len_w:(c + 1) * len_w, :] = dst[1, pl.ds(b * len_s + a, len_w, stride=DIL_STEP), :]
    gs_ref[0] = g_ref[...].astype(F32)

    head0 = lax.broadcasted_iota(jnp.int32, (A_BLOCK, LANES), 1) < A_HEAD_DIM

    def load(slab, cur, prev):
        q = qs_ref[slab, cur, :]
        q2 = jnp.concatenate([jnp.where(head0, q, 0.0), jnp.where(head0, 0.0, q)], axis=0).astype(BF16)
        if prev is None:
            return q2, ks_ref[slab, cur, :].astype(BF16), vs_ref[slab, cur, :].astype(BF16)
        kk = jnp.concatenate([ks_ref[slab, prev, :], ks_ref[slab, cur, :]], axis=0).astype(BF16)
        vv = jnp.concatenate([vs_ref[slab, prev, :], vs_ref[slab, cur, :]], axis=0).astype(BF16)
        return q2, kk, vv

    def attend(q2, kk, vv, bias):
        s = _dot_nt(q2, kk) + jnp.concatenate([bias, bias], axis=0)
        m = jnp.max(s, axis=-1, keepdims=True)
        p = jnp.exp2(s - m).astype(BF16)
        res = _dot(p, jnp.concatenate([vv, jnp.ones(vv.shape, BF16)], axis=1))
        top, bot = res[:A_BLOCK], res[A_BLOCK:]
        return (jnp.where(head0, top[:, :LANES], bot[:, :LANES]), jnp.where(head0, top[:, LANES:], bot[:, LANES:]),
                jnp.where(head0, m[:A_BLOCK], m[A_BLOCK:]))

    def aligned(start):
        return pl.ds(pl.multiple_of(start, A_BLOCK), A_BLOCK)

    def banded_rows(idx, blocks_per_seq):
        start = idx * A_BLOCK
        first = idx % blocks_per_seq == 0
        return aligned(start), aligned(jnp.where(first, start, start - A_BLOCK)), bias_ref[jnp.where(first, 0, 1)]

    def merge(num_a, den_a, m_a, num_b, den_b, m_b):
        gap = m_a - m_b
        shrink = jnp.exp2(-jnp.abs(gap))
        w_a = jnp.where(gap >= 0, 1.0, shrink)
        w_b = jnp.where(gap >= 0, shrink, 1.0)
        return w_a * num_a + w_b * num_b, w_a * den_a + w_b * den_b, jnp.maximum(m_a, m_b)

    def narrow_body(idx, carry):
        cur, prev, bias = banded_rows(idx, n_blocks)
        num, den, m = attend(*load(0, cur, prev), bias)
        num_ref[0, cur, :] = num
        den_ref[0, cur, :] = den
        m_ref[0, cur, :] = m
        return carry

    def step_body(idx, carry):
        cur, prev, bias = banded_rows(idx, blocks_s)
        natural = pl.ds(idx // blocks_s + DIL_STEP * A_BLOCK * (idx % blocks_s), A_BLOCK, stride=DIL_STEP)
        num, den, m = merge(*attend(*load(1, cur, prev), bias),
                            num_ref[0, natural, :], den_ref[0, natural, :], m_ref[0, natural, :])
        num_ref[1, cur, :] = num
        den_ref[1, cur, :] = den
        m_ref[1, cur, :] = m
        gs_ref[1, cur, :] = gs_ref[0, natural, :]
        return carry

    def wide_body(idx, carry):
        cur = aligned(idx * A_BLOCK)
        by_step = pl.ds((idx // DIL_STEP) * len_s + idx % DIL_STEP, A_BLOCK, stride=DIL_STEP)
        natural = pl.ds(DIL_STEP * (idx % DIL_STEP) + idx // DIL_STEP, A_BLOCK, stride=DIL_STEP * DIL_STEP)
        num, den, _ = merge(*attend(*load(2, cur, None), bias_ref[0][:, A_BLOCK:]),
                            num_ref[1, by_step, :], den_ref[1, by_step, :], m_ref[1, by_step, :])
        os_ref[natural, :] = num / den * _silu(gs_ref[1, by_step, :])
        return carry

    assert len_w == A_BLOCK
    n_blocks = seq // A_BLOCK
    for body in (narrow_body, step_body, wide_body):
        lax.fori_loop(0, n_blocks, body, 0, unroll=unroll)
    o_ref[...] = os_ref[...].astype(o_ref.dtype)


def dilated_attention(z, *, batch, seq, unroll=16):
    n_pair = A_WIDTH // LANES
    n_pat = len(A_DILATIONS)
    blk = (None, seq, LANES)
    slabs = pltpu.VMEM((n_pat, seq, LANES), F32)
    partial = pltpu.VMEM((2, seq, LANES), F32)
    return pl.pallas_call(
        functools.partial(_dilated_kernel, seq=seq, unroll=unroll),
        out_shape=jax.ShapeDtypeStruct((batch, seq, A_WIDTH), BF16),
        grid=(batch, n_pair),
        in_specs=[
            pl.BlockSpec(blk, lambda b, h: (b, 0, h)),
            pl.BlockSpec(blk, lambda b, h: (b, 0, n_pair + h)),
            pl.BlockSpec(blk, lambda b, h: (b, 0, 2 * n_pair + h)),
            pl.BlockSpec(blk, lambda b, h: (b, 0, 3 * n_pair + h)),
        ],
        out_specs=pl.BlockSpec(blk, lambda b, h: (b, 0, h)),
        scratch_shapes=[
            slabs, slabs, slabs,
            pltpu.VMEM((2, seq, LANES), F32),
            pltpu.VMEM((seq, LANES), F32),
            partial, partial, partial,
            pltpu.VMEM((2, A_BLOCK, 2 * A_BLOCK), F32),
        ],
        compiler_params=_params("parallel", "parallel"),
        name="dilated_attention",
    )(z, z, z, z)


def _even_tail_kernel(vb_ref, halo_ref, gb_ref, a_ref, x_ref, pw_ref, ps_ref, wo_ref, o_ref, *, tm):
    ti = pl.program_id(1)
    v = vb_ref[...].astype(F32)
    halo = jnp.where(ti > 0, halo_ref[...].astype(F32), 0.0)
    xc = jnp.concatenate([halo, v], axis=0)
    s2 = xc + pltpu.roll(xc, 1, 0)
    t4 = s2[:, B_GROUP:]
    s4 = t4 + pltpu.roll(t4, 2, 0)
    t8 = s4[:, B_GROUP:]
    s8 = t8 + pltpu.roll(t8, 4, 0)
    t16 = s8[:, B_GROUP:]
    s16 = t16 + pltpu.roll(t16, 8, 0)
    sums = (s2[:, :B_GROUP], s4[:, :B_GROUP], s8[:, :B_GROUP], s16)
    pos = (ti * tm + 1 + lax.broadcasted_iota(jnp.int32, (tm, 1), 0)).astype(F32)
    mixed = []
    for g, w in enumerate(POOL_WINDOWS):
        mean = sums[g][POOL_HALO:, :] / jnp.minimum(pos, float(w))
        pooled = mean - v[:, g * B_GROUP:(g + 1) * B_GROUP]
        mixed.append(_dot(pooled.astype(BF16), pw_ref[g]))
    b_out = jnp.concatenate(mixed, axis=1) * ps_ref[...] * _silu(gb_ref[...].astype(F32))
    y = _dot(a_ref[...], wo_ref[:A_WIDTH, :]) + _dot(b_out.astype(BF16), wo_ref[A_WIDTH:, :])
    o_ref[...] = x_ref[...] + y


def even_tail(z, a_out, x, pool_w, pool_scale, w_out, *, batch, seq, tm):
    vb_col = 4 * A_WIDTH // B_WIDTH
    halo_per_tile = tm // POOL_HALO
    return pl.pallas_call(
        functools.partial(_even_tail_kernel, tm=tm),
        out_shape=jax.ShapeDtypeStruct((batch, seq, D_MODEL), F32),
        grid=(batch, seq // tm),
        in_specs=[
            pl.BlockSpec((None, tm, B_WIDTH), lambda b, t: (b, t, vb_col)),
            pl.BlockSpec((None, POOL_HALO, B_WIDTH),
                         lambda b, t: (b, jnp.maximum(t * halo_per_tile - 1, 0), vb_col)),
            pl.BlockSpec((None, tm, B_WIDTH), lambda b, t: (b, t, vb_col + 1)),
            pl.BlockSpec((None, tm, A_WIDTH), lambda b, t: (b, t, 0)),
            pl.BlockSpec((None, tm, D_MODEL), lambda b, t: (b, t, 0)),
            pl.BlockSpec((len(POOL_WINDOWS), B_GROUP, B_GROUP), lambda b, t: (0, 0, 0)),
            pl.BlockSpec((1, B_WIDTH), lambda b, t: (0, 0)),
            pl.BlockSpec((A_WIDTH + B_WIDTH, D_MODEL), lambda b, t: (0, 0)),
        ],
        out_specs=pl.BlockSpec((None, tm, D_MODEL), lambda b, t: (b, t, 0)),
        compiler_params=_params("parallel", "parallel"),
        name="even_tail",
    )(z, z, z, a_out, x, pool_w, pool_scale.reshape(1, B_WIDTH), w_out)


def _s5_prep_kernel(ar_ref, ai_ref, ldt_ref, br_ref, bi_ref, abr_ref, abi_ref, bbr_ref, bbi_ref):
    ar, ai = ar_ref[...], ai_ref[...]
    dt = jnp.exp(ldt_ref[...])
    mag = jnp.exp(dt * ar)
    abar_re = mag * jnp.cos(dt * ai)
    abar_im = mag * jnp.sin(dt * ai)
    nr, ni = abar_re - 1.0, abar_im
    inv = 1.0 / (ar * ar + ai * ai)
    coef_re = (nr * ar + ni * ai) * inv
    coef_im = (ni * ar - nr * ai) * inv
    br, bi = br_ref[...], bi_ref[...]
    abr_ref[...] = abar_re
    abi_ref[...] = abar_im
    bbr_ref[...] = coef_re * br - coef_im * bi
    bbi_ref[...] = coef_re * bi + coef_im * br


def _part_eye():
    return jnp.eye(S5_GROUPS // S5_PARTS, dtype=F32)


def s5_prep(a_re, a_im, log_dt, b_re, b_im):
    rep = lambda t: jnp.repeat(t, S5_GROUP_DIM, axis=0)
    to_rows = lambda t: t.transpose(0, 2, 1).reshape(D_WIDTH, S5_STATE)
    shp = jax.ShapeDtypeStruct((D_WIDTH, S5_STATE), F32)
    abr, abi, bbr, bbi = pl.pallas_call(
        _s5_prep_kernel, out_shape=(shp, shp, shp, shp), name="s5_prep",
    )(rep(a_re), rep(a_im), rep(jnp.broadcast_to(log_dt[:, None], (S5_GROUPS, S5_STATE))),
      to_rows(b_re), to_rows(b_im))
    gpp = S5_GROUPS // S5_PARTS

    def block_diag_in(t):
        t = t.reshape(S5_PARTS, gpp, S5_GROUP_DIM, 1, S5_STATE) * _part_eye()[None, :, None, :, None]
        return t.reshape(S5_PARTS, S5_PART_IN, S5_PART_STATE)

    w_b = jnp.concatenate([block_diag_in(bbr), block_diag_in(bbi)], axis=2).astype(BF16)
    abar_re = abr[::S5_GROUP_DIM].reshape(1, S5_LANES)
    abar_im = abi[::S5_GROUP_DIM].reshape(1, S5_LANES)
    return w_b, abar_re, abar_im


def s5_out_weights(c_re, c_im):
    gpp = S5_GROUPS // S5_PARTS

    def block_diag_out(t):
        t = t.reshape(S5_PARTS, gpp, S5_GROUP_DIM, S5_STATE).transpose(0, 1, 3, 2)
        t = t[:, :, :, None, :] * _part_eye()[None, :, None, :, None]
        return t.reshape(S5_PARTS, S5_PART_STATE, S5_PART_IN).astype(BF16)

    return block_diag_out(c_re), block_diag_out(c_im)


def _s5_kernel(xd_ref, wb_ref, are_ref, aim_ref, wcr_ref, wci_ref, dsk_ref, w1_ref, w2_ref,
               o_ref, u_ref, y_ref, bu_ref, h_ref, *, lc, nb, lane_chunk):
    @pl.when(pl.program_id(0) == 0)
    def _():
        h_ref[...] = jnp.zeros_like(h_ref)

    n_slab = D_WIDTH // LANES
    xd = xd_ref[...].astype(F32)
    for b in range(nb):
        for s in range(n_slab):
            col = b * D_WIDTH + s * LANES
            u_ref[s, pl.ds(b, lc, stride=nb), :] = xd[:, col:col + LANES]
    u = jnp.concatenate([u_ref[s] for s in range(n_slab)], axis=1)
    ub = u.astype(BF16)
    width = 2 * S5_PART_STATE
    for part in range(S5_PARTS):
        bu_ref[:, part * width:(part + 1) * width] = _dot(
            ub[:, part * S5_PART_IN:(part + 1) * S5_PART_IN], wb_ref[part])

    for part in range(S5_PARTS):
        for c in range(S5_PART_STATE // lane_chunk):
            re = slice(part * width + c * lane_chunk, part * width + (c + 1) * lane_chunk)
            im = slice(re.start + S5_PART_STATE, re.stop + S5_PART_STATE)
            ab = slice(part * S5_PART_STATE + c * lane_chunk, part * S5_PART_STATE + (c + 1) * lane_chunk)
            a_r = jnp.broadcast_to(are_ref[:, ab], (nb, lane_chunk))
            a_i = jnp.broadcast_to(aim_ref[:, ab], (nb, lane_chunk))

            def step(t, carry, re=re, im=im, a_r=a_r, a_i=a_i):
                h_r, h_i = carry
                row = pl.ds(pl.multiple_of(t * nb, nb), nb)
                n_r = a_r * h_r - a_i * h_i + bu_ref[row, re]
                n_i = a_r * h_i + a_i * h_r + bu_ref[row, im]
                bu_ref[row, re] = n_r
                bu_ref[row, im] = n_i
                return n_r, n_i

            h_r, h_i = lax.fori_loop(0, lc, step, (h_ref[:, re], h_ref[:, im]), unroll=4)
            h_ref[:, re] = h_r
            h_ref[:, im] = h_i

    ys = []
    for part in range(S5_PARTS):
        re = slice(part * width, part * width + S5_PART_STATE)
        im = slice(re.stop, re.stop + S5_PART_STATE)
        ys.append(_dot(bu_ref[:, re].astype(BF16), wcr_ref[part]) - _dot(bu_ref[:, im].astype(BF16), wci_ref[part]))
    y = jnp.concatenate(ys, axis=1) + dsk_ref[...] * u
    y = jax.nn.gelu(y).astype(BF16)
    out = _dot(y, w1_ref[...]) * jax.nn.sigmoid(_dot(y, w2_ref[...]))
    for s in range(n_slab):
        y_ref[s] = out[:, s * LANES:(s + 1) * LANES]
    o_ref[...] = jnp.concatenate(
        [y_ref[s, pl.ds(b, lc, stride=nb), :] for b in range(nb) for s in range(n_slab)], axis=1).astype(o_ref.dtype)


def s5_layer(zd, w_b, abar_re, abar_im, w_cr, w_ci, d_skip, w1, w2, *, batch, seq, lc):
    rows = lc * batch
    n_slab = D_WIDTH // LANES
    full2 = lambda shape: pl.BlockSpec(shape, lambda c: (0, 0))
    full3 = lambda shape: pl.BlockSpec(shape, lambda c: (0, 0, 0))
    return pl.pallas_call(
        functools.partial(_s5_kernel, lc=lc, nb=batch, lane_chunk=512),
        out_shape=jax.ShapeDtypeStruct((seq, batch * D_WIDTH), BF16),
        grid=(seq // lc,),
        in_specs=[
            pl.BlockSpec((lc, batch * D_WIDTH), lambda c: (c, 0)),
            full3((S5_PARTS, S5_PART_IN, 2 * S5_PART_STATE)),
            full2((1, S5_LANES)),
            full2((1, S5_LANES)),
            full3((S5_PARTS, S5_PART_STATE, S5_PART_IN)),
            full3((S5_PARTS, S5_PART_STATE, S5_PART_IN)),
            full2((1, D_WIDTH)),
            full2((D_WIDTH, D_WIDTH)),
            full2((D_WIDTH, D_WIDTH)),
        ],
        out_specs=pl.BlockSpec((lc, batch * D_WIDTH), lambda c: (c, 0)),
        scratch_shapes=[
            pltpu.VMEM((n_slab, rows, LANES), F32),
            pltpu.VMEM((n_slab, rows, LANES), F32),
            pltpu.VMEM((rows, 2 * S5_LANES), F32),
            pltpu.VMEM((batch, 2 * S5_LANES), F32),
        ],
        compiler_params=_params("arbitrary"),
        name="s5_layer",
    )(zd, w_b, abar_re, abar_im, w_cr, w_ci, d_skip.reshape(1, D_WIDTH), w1, w2)


def _odd_tail_kernel(u_ref, v_ref, gc_ref, d_ref, gd_ref, x_ref, lng_ref, lnb_ref, ws_ref, bs_ref, wo_ref,
                     o_ref, c_ref, *, tm):
    v = v_ref[...].astype(F32)
    mu = jnp.mean(v, axis=-1, keepdims=True)
    vc = v - mu
    var = jnp.mean(vc * vc, axis=-1, keepdims=True)
    vn = (vc * lax.rsqrt(var + EPS) * lng_ref[...] + lnb_ref[...]).astype(BF16)
    ri = lax.broadcasted_iota(jnp.int32, (C_CHUNK, C_CHUNK), 0)
    ci = lax.broadcasted_iota(jnp.int32, (C_CHUNK, C_CHUNK), 1)
    causal = ri >= ci
    for g in range(C_GROUPS):
        cols = slice(g * C_GROUP_DIM, (g + 1) * C_GROUP_DIM)
        w_g = jnp.where(causal, ws_ref[g], 0.0).astype(BF16)
        b_g = bs_ref[:, g:g + 1]
        for c in range(tm // C_CHUNK):
            rows = slice(c * C_CHUNK, (c + 1) * C_CHUNK)
            mixed = _dot(w_g, vn[rows, cols]) + b_g
            gate = _silu(gc_ref[rows, cols].astype(F32))
            c_ref[rows, cols] = (u_ref[rows, cols].astype(F32) * mixed * gate).astype(BF16)
    d = (d_ref[...].astype(F32) * _silu(gd_ref[...].astype(F32))).astype(BF16)
    y = _dot(c_ref[...], wo_ref[:C_WIDTH, :]) + _dot(d, wo_ref[C_WIDTH:, :])
    o_ref[...] = x_ref[...] + y


def odd_tail(zc, zd, d_out, x, ln_g, ln_b, w_s, b_s, w_out, *, batch, seq, tm):
    return pl.pallas_call(
        functools.partial(_odd_tail_kernel, tm=tm),
        out_shape=jax.ShapeDtypeStruct((batch, seq, D_MODEL), F32),
        grid=(batch, seq // tm),
        in_specs=[
            pl.BlockSpec((None, tm, C_WIDTH), lambda b, t: (b, t, 0)),
            pl.BlockSpec((None, tm, C_WIDTH), lambda b, t: (b, t, 1)),
            pl.BlockSpec((None, tm, C_WIDTH), lambda b, t: (b, t, 2)),
            pl.BlockSpec((tm, D_WIDTH), lambda b, t: (t, b)),
            pl.BlockSpec((tm, D_WIDTH), lambda b, t: (t, batch + b)),
            pl.BlockSpec((None, tm, D_MODEL), lambda b, t: (b, t, 0)),
            pl.BlockSpec((1, C_WIDTH), lambda b, t: (0, 0)),
            pl.BlockSpec((1, C_WIDTH), lambda b, t: (0, 0)),
            pl.BlockSpec((C_GROUPS, C_CHUNK, C_CHUNK), lambda b, t: (0, 0, 0)),
            pl.BlockSpec((C_CHUNK, C_GROUPS), lambda b, t: (0, 0)),
            pl.BlockSpec((C_WIDTH + D_WIDTH, D_MODEL), lambda b, t: (0, 0)),
        ],
        out_specs=pl.BlockSpec((None, tm, D_MODEL), lambda b, t: (b, t, 0)),
        scratch_shapes=[pltpu.VMEM((tm, C_WIDTH), BF16)],
        compiler_params=_params("parallel", "parallel"),
        name="odd_tail",
    )(zc, zc, zc, d_out, zd, x, ln_g.reshape(1, C_WIDTH), ln_b.reshape(1, C_WIDTH), w_s, b_s.T, w_out)


def _xattn_kernel(x_ref, g_ref, wq_ref, kv_ref, wo_ref, fg_ref, o_ref, *, final):
    x = x_ref[...]
    q = _dot(_rms(x, g_ref[...]).astype(BF16), wq_ref[...])
    heads = []
    for h in range(X_HEADS):
        ks = slice(h * X_HEAD_DIM, (h + 1) * X_HEAD_DIM)
        vs = slice(D_MODEL + h * X_HEAD_DIM, D_MODEL + (h + 1) * X_HEAD_DIM)
        s = _dot_nt(q[:, ks].astype(BF16), kv_ref[:, ks]) * (X_HEAD_DIM ** -0.5)
        m = jnp.max(s, axis=-1, keepdims=True)
        p = jnp.exp(s - m)
        den = jnp.sum(p, axis=-1, keepdims=True)
        heads.append((_dot(p.astype(BF16), kv_ref[:, vs]) / den).astype(BF16))
    y = x + _dot(jnp.concatenate(heads, axis=1), wo_ref[...])
    if final:
        y = _rms(y, fg_ref[...])
    o_ref[...] = y


def cross_attention(x, g, w_q, kv, w_o, final_g, *, batch, seq, mem_len, tq, final):
    return pl.pallas_call(
        functools.partial(_xattn_kernel, final=final),
        out_shape=jax.ShapeDtypeStruct((batch, seq, D_MODEL), F32),
        grid=(batch, seq // tq),
        in_specs=[
            pl.BlockSpec((None, tq, D_MODEL), lambda b, t: (b, t, 0)),
            pl.BlockSpec((1, D_MODEL), lambda b, t: (0, 0)),
            pl.BlockSpec((D_MODEL, D_MODEL), lambda b, t: (0, 0)),
            pl.BlockSpec((None, mem_len, 2 * D_MODEL), lambda b, t: (b, 0, 0)),
            pl.BlockSpec((D_MODEL, D_MODEL), lambda b, t: (0, 0)),
            pl.BlockSpec((1, D_MODEL), lambda b, t: (0, 0)),
        ],
        out_specs=pl.BlockSpec((None, tq, D_MODEL), lambda b, t: (b, t, 0)),
        compiler_params=_params("parallel", "parallel"),
        name="cross_attention",
    )(x, g.reshape(1, D_MODEL), w_q, kv, w_o, final_g.reshape(1, D_MODEL))


def kernel(x, mem, norm_ab, w_in_ab, pool_w, pool_scale, w_out_ab, norm_cd, w_in_cd, sgu_ln_g, sgu_ln_b,
           sgu_w, sgu_b, s5_a_re, s5_a_im, s5_log_dt, s5_b_re, s5_b_im, s5_c_re, s5_c_im, s5_d,
           glu_w1, glu_w2, w_out_cd, norm_x, w_xq, w_xkv, w_xo, mem_norm, final_norm):
    batch, seq, _ = x.shape
    mem_len = mem.shape[1]
    depth = norm_x.shape[0]
    tokens = batch * seq
    tm_proj = 1024
    tiles_per_row = seq // tm_proj
    bf = lambda t: t.astype(BF16)

    mem2 = mem.reshape(batch * mem_len, D_MODEL)
    for layer in range(depth):
        i = layer // 2
        x2 = x.reshape(tokens, D_MODEL)
        if layer % 2 == 0:
            z = norm_matmul(x2, norm_ab[i], bf(w_in_ab[i]), tm=tm_proj, tn=1024)
            z = z.reshape(batch, seq, z.shape[1])
            a_out = dilated_attention(z, batch=batch, seq=seq)
            x = even_tail(z, a_out, x, bf(pool_w[i]), pool_scale[i], bf(w_out_ab[i]),
                          batch=batch, seq=seq, tm=512)
        else:
            n_c = 3 * C_WIDTH
            zc = norm_matmul(x2, norm_cd[i], bf(w_in_cd[i][:, :n_c]), tm=tm_proj, tn=1024)
            zd = norm_matmul(x2, norm_cd[i], bf(w_in_cd[i][:, n_c:]), tm=tm_proj, tn=D_WIDTH,
                             out_shape=(seq, 2 * batch * D_WIDTH),
                             out_map=lambda r, j: (r % tiles_per_row, j * batch + r // tiles_per_row))
            w_b, abar_re, abar_im = s5_prep(s5_a_re[i], s5_a_im[i], s5_log_dt[i], s5_b_re[i], s5_b_im[i])
            w_cr, w_ci = s5_out_weights(s5_c_re[i], s5_c_im[i])
            d_out = s5_layer(zd, w_b, abar_re, abar_im, w_cr, w_ci, s5_d[i], bf(glu_w1[i]), bf(glu_w2[i]),
                             batch=batch, seq=seq, lc=64)
            x = odd_tail(zc.reshape(batch, seq, n_c), zd, d_out, x, sgu_ln_g[i], sgu_ln_b[i], sgu_w[i],
                         sgu_b[i], bf(w_out_cd[i]), batch=batch, seq=seq, tm=512)
        kv = norm_matmul(mem2, mem_norm, bf(w_xkv[layer]), tm=batch * mem_len, tn=1024)
        x = cross_attention(x, norm_x[layer], bf(w_xq[layer]), kv.reshape(batch, mem_len, 2 * D_MODEL),
                            bf(w_xo[layer]), final_norm, batch=batch, seq=seq, mem_len=mem_len,
                            tq=512, final=(layer == depth - 1))
    return x
```

```python
import functools

import jax
import jax.numpy as jnp
from jax import lax
from jax.experimental import pallas as pl
from jax.experimental.pallas import tpu as pltpu

F32 = jnp.float32
BF16 = jnp.bfloat16

LANES = 128
D_MODEL = 1024
A_WIDTH = 1024
A_HEAD_DIM = 64
A_BLOCK = 128
A_DILATIONS = (1, 4, 16)
DIL_STEP = 4
B_WIDTH = 1024
POOL_WINDOWS = (2, 4, 8, 16)
B_GROUP = 256
POOL_HALO = 16
C_WIDTH = 1024
C_CHUNK = 128
C_GROUPS = 4
C_GROUP_DIM = 256
D_WIDTH = 512
S5_GROUPS = 32
S5_GROUP_DIM = 16
S5_STATE = 64
S5_LANES = S5_GROUPS * S5_STATE
S5_PARTS = 2
S5_PART_IN = D_WIDTH // S5_PARTS
S5_PART_STATE = S5_LANES // S5_PARTS
X_HEADS = 4
X_HEAD_DIM = 256
EPS = 1e-6
NEG = -1e30
LOG2E = 1.4426950408889634

VMEM_LIMIT = 48 * 1024 * 1024


def _params(*sem):
    return pltpu.CompilerParams(dimension_semantics=sem, vmem_limit_bytes=VMEM_LIMIT)


def _rms(x, g):
    ms = jnp.mean(x * x, axis=-1, keepdims=True)
    return x * lax.rsqrt(ms + EPS) * g


def _silu(x):
    return x * jax.nn.sigmoid(x)


def _dot(a, b):
    return jnp.dot(a, b, preferred_element_type=F32)


def _layer_block(shape, layer):
    return pl.BlockSpec((None,) + tuple(shape), lambda *_: (layer,) + (0,) * len(shape))


def _dot_nt(a, b):
    return lax.dot_general(a, b, (((1,), (1,)), ((), ())), preferred_element_type=F32)


def _norm_matmul_kernel(x_ref, g_ref, w_ref, o_ref, xn_ref):
    @pl.when(pl.program_id(1) == 0)
    def _():
        xn_ref[...] = _rms(x_ref[...], g_ref[...]).astype(BF16)

    o_ref[...] = _dot(xn_ref[...], w_ref[...]).astype(o_ref.dtype)


def norm_matmul(x, g, w, layer, *, n, tm, tn, col0=0, out_shape=None, out_map=None):
    m, k = x.shape
    col_block0 = col0 // tn
    if out_shape is None:
        out_shape = (m, n)
        out_map = lambda i, j: (i, j)
    return pl.pallas_call(
        _norm_matmul_kernel,
        out_shape=jax.ShapeDtypeStruct(out_shape, BF16),
        grid=(m // tm, n // tn),
        in_specs=[
            pl.BlockSpec((tm, k), lambda i, j: (i, 0)),
            pl.BlockSpec((1, k), lambda i, j: (0, 0)),
            pl.BlockSpec((None, k, tn), lambda i, j: (layer, 0, col_block0 + j)),
        ],
        out_specs=pl.BlockSpec((tm, tn), out_map),
        scratch_shapes=[pltpu.VMEM((tm, k), BF16)],
        compiler_params=_params("parallel", "arbitrary"),
        name="norm_matmul",
    )(x, g.reshape(1, k), w)


def _dilated_kernel(q_ref, k_ref, v_ref, g_ref, o_ref, qs_ref, ks_ref, vs_ref, gs_ref, os_ref, num_ref, den_ref,
                    m_ref, bias_ref, *, seq, unroll):
    assert A_DILATIONS == (1, DIL_STEP, DIL_STEP * DIL_STEP)
    qi = lax.broadcasted_iota(jnp.int32, (A_BLOCK, 2 * A_BLOCK), 0)
    kj = lax.broadcasted_iota(jnp.int32, (A_BLOCK, 2 * A_BLOCK), 1)
    dist = qi + A_BLOCK - kj
    band = (dist >= 0) & (dist <= A_BLOCK)
    bias_ref[1] = jnp.where(band, 0.0, NEG)
    bias_ref[0] = jnp.where(band & (kj >= A_BLOCK), 0.0, NEG)

    len_s = seq // DIL_STEP
    len_w = len_s // DIL_STEP
    blocks_s = len_s // A_BLOCK
    q_scale = A_HEAD_DIM ** -0.5 * LOG2E
    for src, dst, scale in ((q_ref, qs_ref, q_scale), (k_ref, ks_ref, None), (v_ref, vs_ref, None)):
        nat = src[...].astype(F32)
        dst[0] = nat if scale is None else nat * scale
        for b in range(DIL_STEP):
            dst[1, b * len_s:(b + 1) * len_s, :] = dst[0, pl.ds(b, len_s, stride=DIL_STEP), :]
        for c in range(DIL_STEP * DIL_STEP):
            b, a = divmod(c, DIL_STEP)
            dst[2, c * len_w:(c + 1) * len_w, :] = dst[1, pl.ds(b * len_s + a, len_w, stride=DIL_STEP), :]
    gs_ref[0] = g_ref[...].astype(F32)

    head0 = lax.broadcasted_iota(jnp.int32, (A_BLOCK, LANES), 1) < A_HEAD_DIM

    def load(slab, cur, prev):
        q = qs_ref[slab, cur, :]
        q2 = jnp.concatenate([jnp.where(head0, q, 0.0), jnp.where(head0, 0.0, q)], axis=0).astype(BF16)
        if prev is None:
            return q2, ks_ref[slab, cur, :].astype(BF16), vs_ref[slab, cur, :].astype(BF16)
        if slab == 0:
            kk = jnp.concatenate([k_ref[prev, :], k_ref[cur, :]], axis=0)
            vv = jnp.concatenate([v_ref[prev, :], v_ref[cur, :]], axis=0)
            return q2, kk, vv
        kk = jnp.concatenate([ks_ref[slab, prev, :], ks_ref[slab, cur, :]], axis=0).astype(BF16)
        vv = jnp.concatenate([vs_ref[slab, prev, :], vs_ref[slab, cur, :]], axis=0).astype(BF16)
        return q2, kk, vv

    def attend(q2, kk, vv, bias):
        s = _dot_nt(q2, kk) + jnp.concatenate([bias, bias], axis=0)
        m = jnp.max(s, axis=-1, keepdims=True)
        p = jnp.exp2(s - m).astype(BF16)
        res = _dot(p, jnp.concatenate([vv, jnp.ones(vv.shape, BF16)], axis=1))
        top, bot = res[:A_BLOCK], res[A_BLOCK:]
        return (jnp.where(head0, top[:, :LANES], bot[:, :LANES]), jnp.where(head0, top[:, LANES:], bot[:, LANES:]),
                jnp.where(head0, m[:A_BLOCK], m[A_BLOCK:]))

    def aligned(start):
        return pl.ds(pl.multiple_of(start, A_BLOCK), A_BLOCK)

    def banded_rows(idx, blocks_per_seq):
        start = idx * A_BLOCK
        first = idx % blocks_per_seq == 0
        return aligned(start), aligned(jnp.where(first, start, start - A_BLOCK)), bias_ref[jnp.where(first, 0, 1)]

    def merge(num_a, den_a, m_a, num_b, den_b, m_b):
        gap = m_a - m_b
        shrink = jnp.exp2(-jnp.abs(gap))
        w_a = jnp.where(gap >= 0, 1.0, shrink)
        w_b = jnp.where(gap >= 0, shrink, 1.0)
        return w_a * num_a + w_b * num_b, w_a * den_a + w_b * den_b, jnp.maximum(m_a, m_b)

    def narrow_body(idx, carry):
        cur, prev, bias = banded_rows(idx, n_blocks)
        num, den, m = attend(*load(0, cur, prev), bias)
        num_ref[0, cur, :] = num
        den_ref[0, cur, :] = den
        m_ref[0, cur, :] = m
        return carry

    def step_body(idx, carry):
        cur, prev, bias = banded_rows(idx, blocks_s)
        natural = pl.ds(idx // blocks_s + DIL_STEP * A_BLOCK * (idx % blocks_s), A_BLOCK, stride=DIL_STEP)
        num, den, m = merge(*attend(*load(1, cur, prev), bias),
                            num_ref[0, natural, :], den_ref[0, natural, :], m_ref[0, natural, :])
        num_ref[1, cur, :] = num
        den_ref[1, cur, :] = den
        m_ref[1, cur, :] = m
        gs_ref[1, cur, :] = gs_ref[0, natural, :]
        return carry

    def wide_body(idx, carry):
        cur = aligned(idx * A_BLOCK)
        by_step = pl.ds((idx // DIL_STEP) * len_s + idx % DIL_STEP, A_BLOCK, stride=DIL_STEP)
        natural = pl.ds(DIL_STEP * (idx % DIL_STEP) + idx // DIL_STEP, A_BLOCK, stride=DIL_STEP * DIL_STEP)
        num, den, _ = merge(*attend(*load(2, cur, None), bias_ref[0][:, A_BLOCK:]),
                            num_ref[1, by_step, :], den_ref[1, by_step, :], m_ref[1, by_step, :])
        os_ref[natural, :] = num / den * _silu(gs_ref[1, by_step, :])
        return carry

    assert len_w == A_BLOCK
    n_blocks = seq // A_BLOCK
    for body in (narrow_body, step_body, wide_body):
        lax.fori_loop(0, n_blocks, body, 0, unroll=unroll)
    o_ref[...] = os_ref[...].astype(o_ref.dtype)


def dilated_attention(z, *, batch, seq, unroll=16):
    n_pair = A_WIDTH // LANES
    n_pat = len(A_DILATIONS)
    blk = (None, seq, LANES)
    slabs = pltpu.VMEM((n_pat, seq, LANES), F32)
    partial = pltpu.VMEM((2, seq, LANES), F32)
    return pl.pallas_call(
        functools.partial(_dilated_kernel, seq=seq, unroll=unroll),
        out_shape=jax.ShapeDtypeStruct((batch, seq, A_WIDTH), BF16),
        grid=(batch, n_pair),
        in_specs=[
            pl.BlockSpec(blk, lambda b, h: (b, 0, h)),
            pl.BlockSpec(blk, lambda b, h: (b, 0, n_pair + h)),
            pl.BlockSpec(blk, lambda b, h: (b, 0, 2 * n_pair + h)),
            pl.BlockSpec(blk, lambda b, h: (b, 0, 3 * n_pair + h)),
        ],
        out_specs=pl.BlockSpec(blk, lambda b, h: (b, 0, h)),
        scratch_shapes=[
            slabs, slabs, slabs,
            pltpu.VMEM((2, seq, LANES), F32),
            pltpu.VMEM((seq, LANES), F32),
            partial, partial, partial,
            pltpu.VMEM((2, A_BLOCK, 2 * A_BLOCK), F32),
        ],
        compiler_params=_params("parallel", "parallel"),
        name="dilated_attention",
    )(z, z, z, z)


def _even_tail_kernel(vb_ref, halo_ref, gb_ref, a_ref, x_ref, pw_ref, ps_ref, wo_ref, o_ref, *, tm):
    ti = pl.program_id(1)
    v = vb_ref[...].astype(F32)
    halo = jnp.where(ti > 0, halo_ref[...].astype(F32), 0.0)
    xc = jnp.concatenate([halo, v], axis=0)
    s2 = xc + pltpu.roll(xc, 1, 0)
    t4 = s2[:, B_GROUP:]
    s4 = t4 + pltpu.roll(t4, 2, 0)
    t8 = s4[:, B_GROUP:]
    s8 = t8 + pltpu.roll(t8, 4, 0)
    t16 = s8[:, B_GROUP:]
    s16 = t16 + pltpu.roll(t16, 8, 0)
    sums = (s2[:, :B_GROUP], s4[:, :B_GROUP], s8[:, :B_GROUP], s16)
    pos = (ti * tm + 1 + lax.broadcasted_iota(jnp.int32, (tm, 1), 0)).astype(F32)
    mixed = []
    for g, w in enumerate(POOL_WINDOWS):
        mean = sums[g][POOL_HALO:, :] / jnp.minimum(pos, float(w))
        pooled = mean - v[:, g * B_GROUP:(g + 1) * B_GROUP]
        mixed.append(_dot(pooled.astype(BF16), pw_ref[g]))
    b_out = jnp.concatenate(mixed, axis=1) * ps_ref[...] * _silu(gb_ref[...].astype(F32))
    y = _dot(a_ref[...], wo_ref[:A_WIDTH, :]) + _dot(b_out.astype(BF16), wo_ref[A_WIDTH:, :])
    o_ref[...] = x_ref[...] + y


def even_tail(z, a_out, x, pool_w, pool_scale, w_out, layer, *, batch, seq, tm):
    vb_col = 4 * A_WIDTH // B_WIDTH
    halo_per_tile = tm // POOL_HALO
    return pl.pallas_call(
        functools.partial(_even_tail_kernel, tm=tm),
        out_shape=jax.ShapeDtypeStruct((batch, seq, D_MODEL), F32),
        grid=(batch, seq // tm),
        in_specs=[
            pl.BlockSpec((None, tm, B_WIDTH), lambda b, t: (b, t, vb_col)),
            pl.BlockSpec((None, POOL_HALO, B_WIDTH),
                         lambda b, t: (b, jnp.maximum(t * halo_per_tile - 1, 0), vb_col)),
            pl.BlockSpec((None, tm, B_WIDTH), lambda b, t: (b, t, vb_col + 1)),
            pl.BlockSpec((None, tm, A_WIDTH), lambda b, t: (b, t, 0)),
            pl.BlockSpec((None, tm, D_MODEL), lambda b, t: (b, t, 0)),
            _layer_block((len(POOL_WINDOWS), B_GROUP, B_GROUP), layer),
            pl.BlockSpec((1, B_WIDTH), lambda b, t: (0, 0)),
            _layer_block((A_WIDTH + B_WIDTH, D_MODEL), layer),
        ],
        out_specs=pl.BlockSpec((None, tm, D_MODEL), lambda b, t: (b, t, 0)),
        compiler_params=_params("parallel", "parallel"),
        name="even_tail",
    )(z, z, z, a_out, x, pool_w, pool_scale.reshape(1, B_WIDTH), w_out)


def _s5_prep_kernel(ar_ref, ai_ref, ldt_ref, br_ref, bi_ref, abr_ref, abi_ref, bbr_ref, bbi_ref):
    ar, ai = ar_ref[...], ai_ref[...]
    dt = jnp.exp(ldt_ref[...])
    mag = jnp.exp(dt * ar)
    abar_re = mag * jnp.cos(dt * ai)
    abar_im = mag * jnp.sin(dt * ai)
    nr, ni = abar_re - 1.0, abar_im
    inv = 1.0 / (ar * ar + ai * ai)
    coef_re = (nr * ar + ni * ai) * inv
    coef_im = (ni * ar - nr * ai) * inv
    br, bi = br_ref[...], bi_ref[...]
    abr_ref[...] = abar_re
    abi_ref[...] = abar_im
    bbr_ref[...] = coef_re * br - coef_im * bi
    bbi_ref[...] = coef_re * bi + coef_im * br


def _part_eye():
    return jnp.eye(S5_GROUPS // S5_PARTS, dtype=F32)


def s5_prep(a_re, a_im, log_dt, b_re, b_im):
    rep = lambda t: jnp.repeat(t, S5_GROUP_DIM, axis=0)
    to_rows = lambda t: t.transpose(0, 2, 1).reshape(D_WIDTH, S5_STATE)
    shp = jax.ShapeDtypeStruct((D_WIDTH, S5_STATE), F32)
    abr, abi, bbr, bbi = pl.pallas_call(
        _s5_prep_kernel, out_shape=(shp, shp, shp, shp), name="s5_prep",
    )(rep(a_re), rep(a_im), rep(jnp.broadcast_to(log_dt[:, None], (S5_GROUPS, S5_STATE))),
      to_rows(b_re), to_rows(b_im))
    gpp = S5_GROUPS // S5_PARTS

    def block_diag_in(t):
        t = t.reshape(S5_PARTS, gpp, S5_GROUP_DIM, 1, S5_STATE) * _part_eye()[None, :, None, :, None]
        return t.reshape(S5_PARTS, S5_PART_IN, S5_PART_STATE)

    w_b = jnp.concatenate([block_diag_in(bbr), block_diag_in(bbi)], axis=2).astype(BF16)
    abar_re = abr[::S5_GROUP_DIM].reshape(1, S5_LANES)
    abar_im = abi[::S5_GROUP_DIM].reshape(1, S5_LANES)
    return w_b, abar_re, abar_im


def s5_out_weights(c_re, c_im):
    gpp = S5_GROUPS // S5_PARTS

    def block_diag_out(t):
        t = t.reshape(S5_PARTS, gpp, S5_GROUP_DIM, S5_STATE).transpose(0, 1, 3, 2)
        t = t[:, :, :, None, :] * _part_eye()[None, :, None, :, None]
        return t.reshape(S5_PARTS, S5_PART_STATE, S5_PART_IN).astype(BF16)

    return block_diag_out(c_re), block_diag_out(c_im)


def _s5_kernel(xd_ref, wb_ref, are_ref, aim_ref, wcr_ref, wci_ref, dsk_ref, w1_ref, w2_ref,
               o_ref, u_ref, y_ref, bu_ref, h_ref, *, lc, nb, lane_chunk):
    @pl.when(pl.program_id(0) == 0)
    def _():
        h_ref[...] = jnp.zeros_like(h_ref)

    n_slab = D_WIDTH // LANES
    xd = xd_ref[...].astype(F32)
    for b in range(nb):
        for s in range(n_slab):
            col = b * D_WIDTH + s * LANES
            u_ref[s, pl.ds(b, lc, stride=nb), :] = xd[:, col:col + LANES]
    u = jnp.concatenate([u_ref[s] for s in range(n_slab)], axis=1)
    ub = u.astype(BF16)
    width = 2 * S5_PART_STATE
    def project(part):
        bu_ref[:, part * width:(part + 1) * width] = _dot(
            ub[:, part * S5_PART_IN:(part + 1) * S5_PART_IN], wb_ref[part])

    def scan(part):
        for c in range(S5_PART_STATE // lane_chunk):
            re = slice(part * width + c * lane_chunk, part * width + (c + 1) * lane_chunk)
            im = slice(re.start + S5_PART_STATE, re.stop + S5_PART_STATE)
            ab = slice(part * S5_PART_STATE + c * lane_chunk, part * S5_PART_STATE + (c + 1) * lane_chunk)
            a_r = jnp.broadcast_to(are_ref[:, ab], (nb, lane_chunk))
            a_i = jnp.broadcast_to(aim_ref[:, ab], (nb, lane_chunk))
            h_r, h_i = h_ref[:, re], h_ref[:, im]
            for t in range(lc):
                row = slice(t * nb, (t + 1) * nb)
                h_r, h_i = (a_r * h_r - a_i * h_i + bu_ref[row, re], a_r * h_i + a_i * h_r + bu_ref[row, im])
                bu_ref[row, re] = h_r
                bu_ref[row, im] = h_i
            h_ref[:, re] = h_r
            h_ref[:, im] = h_i

    def readout(part):
        re = slice(part * width, part * width + S5_PART_STATE)
        im = slice(re.stop, re.stop + S5_PART_STATE)
        return _dot(bu_ref[:, re].astype(BF16), wcr_ref[part]) - _dot(bu_ref[:, im].astype(BF16), wci_ref[part])

    ys = []
    project(0)
    for part in range(S5_PARTS):
        if part + 1 < S5_PARTS:
            project(part + 1)
        scan(part)
        ys.append(readout(part))
    y = jnp.concatenate(ys, axis=1) + dsk_ref[...] * u
    y = jax.nn.gelu(y).astype(BF16)
    out = _dot(y, w1_ref[...]) * jax.nn.sigmoid(_dot(y, w2_ref[...]))
    for s in range(n_slab):
        y_ref[s] = out[:, s * LANES:(s + 1) * LANES]
    o_ref[...] = jnp.concatenate(
        [y_ref[s, pl.ds(b, lc, stride=nb), :] for b in range(nb) for s in range(n_slab)], axis=1).astype(o_ref.dtype)


def s5_layer(zd, w_b, abar_re, abar_im, w_cr, w_ci, d_skip, w1, w2, layer, *, batch, seq, lc):
    rows = lc * batch
    n_slab = D_WIDTH // LANES
    full2 = lambda shape: pl.BlockSpec(shape, lambda c: (0, 0))
    full3 = lambda shape: pl.BlockSpec(shape, lambda c: (0, 0, 0))
    return pl.pallas_call(
        functools.partial(_s5_kernel, lc=lc, nb=batch, lane_chunk=512),
        out_shape=jax.ShapeDtypeStruct((seq, batch * D_WIDTH), BF16),
        grid=(seq // lc,),
        in_specs=[
            pl.BlockSpec((lc, batch * D_WIDTH), lambda c: (c, 0)),
            full3((S5_PARTS, S5_PART_IN, 2 * S5_PART_STATE)),
            full2((1, S5_LANES)),
            full2((1, S5_LANES)),
            full3((S5_PARTS, S5_PART_STATE, S5_PART_IN)),
            full3((S5_PARTS, S5_PART_STATE, S5_PART_IN)),
            full2((1, D_WIDTH)),
            _layer_block((D_WIDTH, D_WIDTH), layer),
            _layer_block((D_WIDTH, D_WIDTH), layer),
        ],
        out_specs=pl.BlockSpec((lc, batch * D_WIDTH), lambda c: (c, 0)),
        scratch_shapes=[
            pltpu.VMEM((n_slab, rows, LANES), F32),
            pltpu.VMEM((n_slab, rows, LANES), F32),
            pltpu.VMEM((rows, 2 * S5_LANES), F32),
            pltpu.VMEM((batch, 2 * S5_LANES), F32),
        ],
        compiler_params=_params("arbitrary"),
        name="s5_layer",
    )(zd, w_b, abar_re, abar_im, w_cr, w_ci, d_skip.reshape(1, D_WIDTH), w1, w2)


def _odd_tail_kernel(u_ref, v_ref, gc_ref, d_ref, gd_ref, x_ref, lng_ref, lnb_ref, ws_ref, bs_ref, wo_ref,
                     o_ref, c_ref, *, tm):
    v = v_ref[...].astype(F32)
    mu = jnp.mean(v, axis=-1, keepdims=True)
    vc = v - mu
    var = jnp.mean(vc * vc, axis=-1, keepdims=True)
    vn = (vc * lax.rsqrt(var + EPS) * lng_ref[...] + lnb_ref[...]).astype(BF16)
    ri = lax.broadcasted_iota(jnp.int32, (C_CHUNK, C_CHUNK), 0)
    ci = lax.broadcasted_iota(jnp.int32, (C_CHUNK, C_CHUNK), 1)
    causal = ri >= ci
    for g in range(C_GROUPS):
        cols = slice(g * C_GROUP_DIM, (g + 1) * C_GROUP_DIM)
        w_g = jnp.where(causal, ws_ref[g], 0.0).astype(BF16)
        b_g = bs_ref[:, g:g + 1]
        for c in range(tm // C_CHUNK):
            rows = slice(c * C_CHUNK, (c + 1) * C_CHUNK)
            mixed = _dot(w_g, vn[rows, cols]) + b_g
            gate = _silu(gc_ref[rows, cols].astype(F32))
            c_ref[rows, cols] = (u_ref[rows, cols].astype(F32) * mixed * gate).astype(BF16)
    d = (d_ref[...].astype(F32) * _silu(gd_ref[...].astype(F32))).astype(BF16)
    y = _dot(c_ref[...], wo_ref[:C_WIDTH, :]) + _dot(d, wo_ref[C_WIDTH:, :])
    o_ref[...] = x_ref[...] + y


def odd_tail(zc, zd, d_out, x, ln_g, ln_b, w_s, b_s, w_out, layer, *, batch, seq, tm):
    return pl.pallas_call(
        functools.partial(_odd_tail_kernel, tm=tm),
        out_shape=jax.ShapeDtypeStruct((batch, seq, D_MODEL), F32),
        grid=(batch, seq // tm),
        in_specs=[
            pl.BlockSpec((None, tm, C_WIDTH), lambda b, t: (b, t, 0)),
            pl.BlockSpec((None, tm, C_WIDTH), lambda b, t: (b, t, 1)),
            pl.BlockSpec((None, tm, C_WIDTH), lambda b, t: (b, t, 2)),
            pl.BlockSpec((tm, D_WIDTH), lambda b, t: (t, b)),
            pl.BlockSpec((tm, D_WIDTH), lambda b, t: (t, batch + b)),
            pl.BlockSpec((None, tm, D_MODEL), lambda b, t: (b, t, 0)),
            pl.BlockSpec((1, C_WIDTH), lambda b, t: (0, 0)),
            pl.BlockSpec((1, C_WIDTH), lambda b, t: (0, 0)),
            _layer_block((C_GROUPS, C_CHUNK, C_CHUNK), layer),
            pl.BlockSpec((C_CHUNK, C_GROUPS), lambda b, t: (0, 0)),
            _layer_block((C_WIDTH + D_WIDTH, D_MODEL), layer),
        ],
        out_specs=pl.BlockSpec((None, tm, D_MODEL), lambda b, t: (b, t, 0)),
        scratch_shapes=[pltpu.VMEM((tm, C_WIDTH), BF16)],
        compiler_params=_params("parallel", "parallel"),
        name="odd_tail",
    )(zc, zc, zc, d_out, zd, x, ln_g.reshape(1, C_WIDTH), ln_b.reshape(1, C_WIDTH), w_s, b_s.T, w_out)


def _xattn_kernel(x_ref, g_ref, wq_ref, kv_ref, wo_ref, fg_ref, o_ref, *, final):
    x = x_ref[...]
    q = _dot(_rms(x, g_ref[...]).astype(BF16), wq_ref[...])
    heads = []
    for h in range(X_HEADS):
        ks = slice(h * X_HEAD_DIM, (h + 1) * X_HEAD_DIM)
        vs = slice(D_MODEL + h * X_HEAD_DIM, D_MODEL + (h + 1) * X_HEAD_DIM)
        s = _dot_nt(q[:, ks].astype(BF16), kv_ref[:, ks]) * (X_HEAD_DIM ** -0.5)
        m = jnp.max(s, axis=-1, keepdims=True)
        p = jnp.exp(s - m)
        den = jnp.sum(p, axis=-1, keepdims=True)
        heads.append((_dot(p.astype(BF16), kv_ref[:, vs]) / den).astype(BF16))
    y = x + _dot(jnp.concatenate(heads, axis=1), wo_ref[...])
    if final:
        y = _rms(y, fg_ref[...])
    o_ref[...] = y


def cross_attention(x, g, w_q, kv, w_o, final_g, layer, *, batch, seq, mem_len, tq, final):
    return pl.pallas_call(
        functools.partial(_xattn_kernel, final=final),
        out_shape=jax.ShapeDtypeStruct((batch, seq, D_MODEL), F32),
        grid=(batch, seq // tq),
        in_specs=[
            pl.BlockSpec((None, tq, D_MODEL), lambda b, t: (b, t, 0)),
            pl.BlockSpec((1, D_MODEL), lambda b, t: (0, 0)),
            _layer_block((D_MODEL, D_MODEL), layer),
            pl.BlockSpec((None, mem_len, 2 * D_MODEL), lambda b, t: (b, 0, 0)),
            _layer_block((D_MODEL, D_MODEL), layer),
            pl.BlockSpec((1, D_MODEL), lambda b, t: (0, 0)),
        ],
        out_specs=pl.BlockSpec((None, tq, D_MODEL), lambda b, t: (b, t, 0)),
        compiler_params=_params("parallel", "parallel"),
        name="cross_attention",
    )(x, g.reshape(1, D_MODEL), w_q, kv, w_o, final_g.reshape(1, D_MODEL))


def kernel(x, mem, norm_ab, w_in_ab, pool_w, pool_scale, w_out_ab, norm_cd, w_in_cd, sgu_ln_g, sgu_ln_b,
           sgu_w, sgu_b, s5_a_re, s5_a_im, s5_log_dt, s5_b_re, s5_b_im, s5_c_re, s5_c_im, s5_d,
           glu_w1, glu_w2, w_out_cd, norm_x, w_xq, w_xkv, w_xo, mem_norm, final_norm):
    batch, seq, _ = x.shape
    mem_len = mem.shape[1]
    depth = norm_x.shape[0]
    tokens = batch * seq
    tm_proj = 1024
    tiles_per_row = seq // tm_proj
    bf = lambda t: t.astype(BF16)

    w_in_ab, pool_w, w_out_ab, w_in_cd, glu_w1, glu_w2, w_out_cd, w_xq, w_xkv, w_xo = (
        bf(t) for t in (w_in_ab, pool_w, w_out_ab, w_in_cd, glu_w1, glu_w2, w_out_cd, w_xq, w_xkv, w_xo))
    mem2 = mem.reshape(batch * mem_len, D_MODEL)
    for layer in range(depth):
        i = layer // 2
        x2 = x.reshape(tokens, D_MODEL)
        if layer % 2 == 0:
            z = norm_matmul(x2, norm_ab[i], w_in_ab, i, n=w_in_ab.shape[2], tm=tm_proj, tn=2048)
            z = z.reshape(batch, seq, z.shape[1])
            a_out = dilated_attention(z, batch=batch, seq=seq)
            x = even_tail(z, a_out, x, pool_w, pool_scale[i], w_out_ab, i, batch=batch, seq=seq, tm=512)
        else:
            n_c = 3 * C_WIDTH
            zc = norm_matmul(x2, norm_cd[i], w_in_cd, i, n=n_c, tm=tm_proj, tn=1536)
            zd = norm_matmul(x2, norm_cd[i], w_in_cd, i, n=2 * D_WIDTH, col0=n_c, tm=tm_proj, tn=D_WIDTH,
                             out_shape=(seq, 2 * batch * D_WIDTH),
                             out_map=lambda r, j: (r % tiles_per_row, j * batch + r // tiles_per_row))
            w_b, abar_re, abar_im = s5_prep(s5_a_re[i], s5_a_im[i], s5_log_dt[i], s5_b_re[i], s5_b_im[i])
            w_cr, w_ci = s5_out_weights(s5_c_re[i], s5_c_im[i])
            d_out = s5_layer(zd, w_b, abar_re, abar_im, w_cr, w_ci, s5_d[i], glu_w1, glu_w2, i,
                             batch=batch, seq=seq, lc=64)
            x = odd_tail(zc.reshape(batch, seq, n_c), zd, d_out, x, sgu_ln_g[i], sgu_ln_b[i], sgu_w,
                         sgu_b[i], w_out_cd, i, batch=batch, seq=seq, tm=512)
        kv = norm_matmul(mem2, mem_norm, w_xkv, layer, n=2 * D_MODEL, tm=batch * mem_len, tn=1024)
        x = cross_attention(x, norm_x[layer], w_xq, kv.reshape(batch, mem_len, 2 * D_MODEL), w_xo, final_norm,
                            layer, batch=batch, seq=seq, mem_len=mem_len, tq=512, final=(layer == depth - 1))
    return x
```

```python
import functools

import jax
import jax.numpy as jnp
from jax import lax
from jax.experimental import pallas as pl
from jax.experimental.pallas import tpu as pltpu

F32 = jnp.float32
BF16 = jnp.bfloat16

LANES = 128
D_MODEL = 1024
A_WIDTH = 1024
A_HEAD_DIM = 64
A_BLOCK = 128
A_DILATIONS = (1, 4, 16)
DIL_STEP = 4
B_WIDTH = 1024
POOL_WINDOWS = (2, 4, 8, 16)
B_GROUP = 256
POOL_HALO = 16
C_WIDTH = 1024
C_CHUNK = 128
C_GROUPS = 4
C_GROUP_DIM = 256
D_WIDTH = 512
S5_GROUPS = 32
S5_GROUP_DIM = 16
S5_STATE = 64
S5_LANES = S5_GROUPS * S5_STATE
S5_PARTS = 2
S5_PART_IN = D_WIDTH // S5_PARTS
S5_PART_STATE = S5_LANES // S5_PARTS
X_HEADS = 4
X_HEAD_DIM = 256
EPS = 1e-6
NEG = -1e30
LOG2E = 1.4426950408889634

VMEM_LIMIT = 48 * 1024 * 1024


def _params(*sem):
    return pltpu.CompilerParams(dimension_semantics=sem, vmem_limit_bytes=VMEM_LIMIT)


def _rms(x, g):
    ms = jnp.mean(x * x, axis=-1, keepdims=True)
    return x * lax.rsqrt(ms + EPS) * g


def _silu(x):
    return x * jax.nn.sigmoid(x)


def _dot(a, b):
    return jnp.dot(a, b, preferred_element_type=F32)


def _layer_block(shape, layer):
    return pl.BlockSpec((None,) + tuple(shape), lambda *_: (layer,) + (0,) * len(shape))


def _dot_nt(a, b):
    return lax.dot_general(a, b, (((1,), (1,)), ((), ())), preferred_element_type=F32)


def _norm_matmul_kernel(x_ref, g_ref, w_ref, o_ref, xn_ref):
    @pl.when(pl.program_id(1) == 0)
    def _():
        xn_ref[...] = _rms(x_ref[...], g_ref[...]).astype(BF16)

    o_ref[...] = _dot(xn_ref[...], w_ref[...]).astype(o_ref.dtype)


def norm_matmul(x, g, w, layer, *, tm, tn):
    m, k = x.shape
    n = w.shape[2]
    return pl.pallas_call(
        _norm_matmul_kernel,
        out_shape=jax.ShapeDtypeStruct((m, n), BF16),
        grid=(m // tm, n // tn),
        in_specs=[
            pl.BlockSpec((tm, k), lambda i, j: (i, 0)),
            pl.BlockSpec((1, k), lambda i, j: (0, 0)),
            pl.BlockSpec((None, k, tn), lambda i, j: (layer, 0, j)),
        ],
        out_specs=pl.BlockSpec((tm, tn), lambda i, j: (i, j)),
        scratch_shapes=[pltpu.VMEM((tm, k), BF16)],
        compiler_params=_params("parallel", "arbitrary"),
        name="norm_matmul",
    )(x, g.reshape(1, k), w)


def _odd_proj_kernel(x_ref, g_ref, w_ref, zc_ref, zd_ref, xn_ref, *, c_blocks):
    j = pl.program_id(1)

    @pl.when(j == 0)
    def _():
        xn_ref[...] = _rms(x_ref[...], g_ref[...]).astype(BF16)

    y = _dot(xn_ref[...], w_ref[...]).astype(BF16)

    @pl.when(j < c_blocks)
    def _():
        zc_ref[...] = y

    @pl.when(j == c_blocks)
    def _():
        zd_ref[...] = y


def odd_in_proj(x, g, w, layer, *, batch, seq, tm, tn):
    m, k = x.shape
    c_blocks = 3 * C_WIDTH // tn
    assert tn == 2 * D_WIDTH and seq % tm == 0
    tiles_per_row = seq // tm
    return pl.pallas_call(
        functools.partial(_odd_proj_kernel, c_blocks=c_blocks),
        out_shape=(jax.ShapeDtypeStruct((m, 3 * C_WIDTH), BF16),
                   jax.ShapeDtypeStruct((seq, batch * tn), BF16)),
        grid=(m // tm, c_blocks + 1),
        in_specs=[
            pl.BlockSpec((tm, k), lambda i, j: (i, 0)),
            pl.BlockSpec((1, k), lambda i, j: (0, 0)),
            pl.BlockSpec((None, k, tn), lambda i, j: (layer, 0, j)),
        ],
        out_specs=(pl.BlockSpec((tm, tn), lambda i, j: (i, jnp.minimum(j, c_blocks - 1))),
                   pl.BlockSpec((tm, tn), lambda i, j: (i % tiles_per_row, i // tiles_per_row))),
        scratch_shapes=[pltpu.VMEM((tm, k), BF16)],
        compiler_params=_params("parallel", "arbitrary"),
        name="odd_in_proj",
    )(x, g.reshape(1, k), w)


def _dilated_kernel(q_ref, k_ref, v_ref, g_ref, o_ref, qs_ref, ks_ref, vs_ref, gs_ref, os_ref, num_ref, den_ref,
                    m_ref, bias_ref, *, seq, unroll):
    assert A_DILATIONS == (1, DIL_STEP, DIL_STEP * DIL_STEP)
    qi = lax.broadcasted_iota(jnp.int32, (A_BLOCK, 2 * A_BLOCK), 0)
    kj = lax.broadcasted_iota(jnp.int32, (A_BLOCK, 2 * A_BLOCK), 1)
    dist = qi + A_BLOCK - kj
    band = (dist >= 0) & (dist <= A_BLOCK)
    bias_ref[1] = jnp.where(band, 0.0, NEG).astype(BF16)
    bias_ref[0] = jnp.where(band & (kj >= A_BLOCK), 0.0, NEG).astype(BF16)

    len_s = seq // DIL_STEP
    len_w = len_s // DIL_STEP
    blocks_s = len_s // A_BLOCK
    q_scale = A_HEAD_DIM ** -0.5 * LOG2E
    for src, dst, scale in ((q_ref, qs_ref, q_scale), (k_ref, ks_ref, None), (v_ref, vs_ref, None)):
        nat = src[...].astype(F32)
        dst[0] = nat if scale is None else nat * scale
        for b in range(DIL_STEP):
            dst[1, b * len_s:(b + 1) * len_s, :] = dst[0, pl.ds(b, len_s, stride=DIL_STEP), :]
        for c in range(DIL_STEP * DIL_STEP):
            b, a = divmod(c, DIL_STEP)
            dst[2, c * len_w:(c + 1) * len_w, :] = dst[1, pl.ds(b * len_s + a, len_w, stride=DIL_STEP), :]
    gs_ref[0] = g_ref[...].astype(F32)

    head0 = lax.broadcasted_iota(jnp.int32, (A_BLOCK, LANES), 1) < A_HEAD_DIM

    def load(slab, cur, prev):
        q = qs_ref[slab, cur, :]
        q2 = jnp.concatenate([jnp.where(head0, q, 0.0), jnp.where(head0, 0.0, q)], axis=0).astype(BF16)
        if prev is None:
            return q2, ks_ref[slab, cur, :].astype(BF16), vs_ref[slab, cur, :].astype(BF16)
        if slab == 0:
            kk = jnp.concatenate([k_ref[prev, :], k_ref[cur, :]], axis=0)
            vv = jnp.concatenate([v_ref[prev, :], v_ref[cur, :]], axis=0)
            return q2, kk, vv
        kk = jnp.concatenate([ks_ref[slab, prev, :], ks_ref[slab, cur, :]], axis=0).astype(BF16)
        vv = jnp.concatenate([vs_ref[slab, prev, :], vs_ref[slab, cur, :]], axis=0).astype(BF16)
        return q2, kk, vv

    def attend(q2, kk, vv, bias):
        s = _dot_nt(q2, kk).astype(BF16) + jnp.concatenate([bias, bias], axis=0)
        m = jnp.max(s, axis=-1, keepdims=True)
        p = jnp.exp2(s - m)
        res = _dot(p, jnp.concatenate([vv, jnp.ones(vv.shape, BF16)], axis=1))
        top, bot = res[:A_BLOCK], res[A_BLOCK:]
        m_top = jnp.broadcast_to(m[:A_BLOCK], (A_BLOCK, LANES)).astype(F32)
        m_bot = jnp.broadcast_to(m[A_BLOCK:], (A_BLOCK, LANES)).astype(F32)
        return (jnp.where(head0, top[:, :LANES], bot[:, :LANES]), jnp.where(head0, top[:, LANES:], bot[:, LANES:]),
                jnp.where(head0, m_top, m_bot))

    def aligned(start):
        return pl.ds(pl.multiple_of(start, A_BLOCK), A_BLOCK)

    def banded_rows(idx, blocks_per_seq):
        start = idx * A_BLOCK
        first = idx % blocks_per_seq == 0
        return aligned(start), aligned(jnp.where(first, start, start - A_BLOCK)), bias_ref[jnp.where(first, 0, 1)]

    def merge(num_a, den_a, m_a, num_b, den_b, m_b):
        gap = m_a - m_b
        shrink = jnp.exp2(-jnp.abs(gap))
        w_a = jnp.where(gap >= 0, 1.0, shrink)
        w_b = jnp.where(gap >= 0, shrink, 1.0)
        return w_a * num_a + w_b * num_b, w_a * den_a + w_b * den_b, jnp.maximum(m_a, m_b)

    def narrow_body(idx, carry):
        cur, prev, bias = banded_rows(idx, n_blocks)
        num, den, m = attend(*load(0, cur, prev), bias)
        num_ref[0, cur, :] = num
        den_ref[0, cur, :] = den
        m_ref[0, cur, :] = m
        return carry

    def step_body(idx, carry):
        cur, prev, bias = banded_rows(idx, blocks_s)
        natural = pl.ds(idx // blocks_s + DIL_STEP * A_BLOCK * (idx % blocks_s), A_BLOCK, stride=DIL_STEP)
        num, den, m = merge(*attend(*load(1, cur, prev), bias),
                            num_ref[0, natural, :], den_ref[0, natural, :], m_ref[0, natural, :])
        num_ref[1, cur, :] = num
        den_ref[1, cur, :] = den
        m_ref[1, cur, :] = m
        gs_ref[1, cur, :] = gs_ref[0, natural, :]
        return carry

    def wide_body(idx, carry):
        cur = aligned(idx * A_BLOCK)
        by_step = pl.ds((idx // DIL_STEP) * len_s + idx % DIL_STEP, A_BLOCK, stride=DIL_STEP)
        natural = pl.ds(DIL_STEP * (idx % DIL_STEP) + idx // DIL_STEP, A_BLOCK, stride=DIL_STEP * DIL_STEP)
        num, den, _ = merge(*attend(*load(2, cur, None), bias_ref[0][:, A_BLOCK:]),
                            num_ref[1, by_step, :], den_ref[1, by_step, :], m_ref[1, by_step, :])
        os_ref[natural, :] = num / den * _silu(gs_ref[1, by_step, :])
        return carry

    assert len_w == A_BLOCK
    n_blocks = seq // A_BLOCK
    for body in (narrow_body, step_body, wide_body):
        lax.fori_loop(0, n_blocks, body, 0, unroll=unroll)
    o_ref[...] = os_ref[...].astype(o_ref.dtype)


def dilated_attention(z, *, batch, seq, unroll=16):
    n_pair = A_WIDTH // LANES
    n_pat = len(A_DILATIONS)
    blk = (None, seq, LANES)
    slabs = pltpu.VMEM((n_pat, seq, LANES), F32)
    partial = pltpu.VMEM((2, seq, LANES), F32)
    return pl.pallas_call(
        functools.partial(_dilated_kernel, seq=seq, unroll=unroll),
        out_shape=jax.ShapeDtypeStruct((batch, seq, A_WIDTH), BF16),
        grid=(batch, n_pair),
        in_specs=[
            pl.BlockSpec(blk, lambda b, h: (b, 0, h)),
            pl.BlockSpec(blk, lambda b, h: (b, 0, n_pair + h)),
            pl.BlockSpec(blk, lambda b, h: (b, 0, 2 * n_pair + h)),
            pl.BlockSpec(blk, lambda b, h: (b, 0, 3 * n_pair + h)),
        ],
        out_specs=pl.BlockSpec(blk, lambda b, h: (b, 0, h)),
        scratch_shapes=[
            slabs, slabs, slabs,
            pltpu.VMEM((2, seq, LANES), F32),
            pltpu.VMEM((seq, LANES), F32),
            partial, partial, partial,
            pltpu.VMEM((2, A_BLOCK, 2 * A_BLOCK), BF16),
        ],
        compiler_params=_params("parallel", "parallel"),
        name="dilated_attention",
    )(z, z, z, z)


def _even_tail_kernel(vb_ref, halo_ref, gb_ref, a_ref, x_ref, pw_ref, ps_ref, wo_ref, o_ref, *, tm):
    ti = pl.program_id(1)
    v = vb_ref[...].astype(F32)
    halo = jnp.where(ti > 0, halo_ref[...].astype(F32), 0.0)
    xc = jnp.concatenate([halo, v], axis=0)
    s2 = xc + pltpu.roll(xc, 1, 0)
    t4 = s2[:, B_GROUP:]
    s4 = t4 + pltpu.roll(t4, 2, 0)
    t8 = s4[:, B_GROUP:]
    s8 = t8 + pltpu.roll(t8, 4, 0)
    t16 = s8[:, B_GROUP:]
    s16 = t16 + pltpu.roll(t16, 8, 0)
    sums = (s2[:, :B_GROUP], s4[:, :B_GROUP], s8[:, :B_GROUP], s16)
    pos = (ti * tm + 1 + lax.broadcasted_iota(jnp.int32, (tm, 1), 0)).astype(F32)
    mixed = []
    for g, w in enumerate(POOL_WINDOWS):
        mean = sums[g][POOL_HALO:, :] / jnp.minimum(pos, float(w))
        pooled = mean - v[:, g * B_GROUP:(g + 1) * B_GROUP]
        mixed.append(_dot(pooled.astype(BF16), pw_ref[g]))
    b_out = jnp.concatenate(mixed, axis=1) * ps_ref[...] * _silu(gb_ref[...].astype(F32))
    y = _dot(a_ref[...], wo_ref[:A_WIDTH, :]) + _dot(b_out.astype(BF16), wo_ref[A_WIDTH:, :])
    o_ref[...] = x_ref[...] + y


def even_tail(z, a_out, x, pool_w, pool_scale, w_out, layer, *, batch, seq, tm):
    vb_col = 4 * A_WIDTH // B_WIDTH
    halo_per_tile = tm // POOL_HALO
    return pl.pallas_call(
        functools.partial(_even_tail_kernel, tm=tm),
        out_shape=jax.ShapeDtypeStruct((batch, seq, D_MODEL), F32),
        grid=(batch, seq // tm),
        in_specs=[
            pl.BlockSpec((None, tm, B_WIDTH), lambda b, t: (b, t, vb_col)),
            pl.BlockSpec((None, POOL_HALO, B_WIDTH),
                         lambda b, t: (b, jnp.maximum(t * halo_per_tile - 1, 0), vb_col)),
            pl.BlockSpec((None, tm, B_WIDTH), lambda b, t: (b, t, vb_col + 1)),
            pl.BlockSpec((None, tm, A_WIDTH), lambda b, t: (b, t, 0)),
            pl.BlockSpec((None, tm, D_MODEL), lambda b, t: (b, t, 0)),
            _layer_block((len(POOL_WINDOWS), B_GROUP, B_GROUP), layer),
            pl.BlockSpec((1, B_WIDTH), lambda b, t: (0, 0)),
            _layer_block((A_WIDTH + B_WIDTH, D_MODEL), layer),
        ],
        out_specs=pl.BlockSpec((None, tm, D_MODEL), lambda b, t: (b, t, 0)),
        compiler_params=_params("parallel", "parallel"),
        name="even_tail",
    )(z, z, z, a_out, x, pool_w, pool_scale.reshape(1, B_WIDTH), w_out)


def _s5_prep_kernel(ar_ref, ai_ref, ldt_ref, br_ref, bi_ref, abr_ref, abi_ref, bbr_ref, bbi_ref):
    ar, ai = ar_ref[...], ai_ref[...]
    dt = jnp.exp(ldt_ref[...])
    mag = jnp.exp(dt * ar)
    abar_re = mag * jnp.cos(dt * ai)
    abar_im = mag * jnp.sin(dt * ai)
    nr, ni = abar_re - 1.0, abar_im
    inv = 1.0 / (ar * ar + ai * ai)
    coef_re = (nr * ar + ni * ai) * inv
    coef_im = (ni * ar - nr * ai) * inv
    br, bi = br_ref[...], bi_ref[...]
    abr_ref[...] = abar_re
    abi_ref[...] = abar_im
    bbr_ref[...] = coef_re * br - coef_im * bi
    bbi_ref[...] = coef_re * bi + coef_im * br


def _part_eye():
    return jnp.eye(S5_GROUPS // S5_PARTS, dtype=F32)


def s5_prep(a_re, a_im, log_dt, b_re, b_im):
    rep = lambda t: jnp.repeat(t, S5_GROUP_DIM, axis=0)
    to_rows = lambda t: t.transpose(0, 2, 1).reshape(D_WIDTH, S5_STATE)
    shp = jax.ShapeDtypeStruct((D_WIDTH, S5_STATE), F32)
    abr, abi, bbr, bbi = pl.pallas_call(
        _s5_prep_kernel, out_shape=(shp, shp, shp, shp), name="s5_prep",
    )(rep(a_re), rep(a_im), rep(jnp.broadcast_to(log_dt[:, None], (S5_GROUPS, S5_STATE))),
      to_rows(b_re), to_rows(b_im))
    gpp = S5_GROUPS // S5_PARTS

    def block_diag_in(t):
        t = t.reshape(S5_PARTS, gpp, S5_GROUP_DIM, 1, S5_STATE) * _part_eye()[None, :, None, :, None]
        return t.reshape(S5_PARTS, S5_PART_IN, S5_PART_STATE)

    w_b = jnp.concatenate([block_diag_in(bbr), block_diag_in(bbi)], axis=2).astype(BF16)
    abar_re = abr[::S5_GROUP_DIM].reshape(1, S5_LANES)
    abar_im = abi[::S5_GROUP_DIM].reshape(1, S5_LANES)
    return w_b, abar_re, abar_im


def s5_out_weights(c_re, c_im):
    gpp = S5_GROUPS // S5_PARTS

    def block_diag_out(t):
        t = t.reshape(S5_PARTS, gpp, S5_GROUP_DIM, S5_STATE).transpose(0, 1, 3, 2)
        t = t[:, :, :, None, :] * _part_eye()[None, :, None, :, None]
        return t.reshape(S5_PARTS, S5_PART_STATE, S5_PART_IN).astype(BF16)

    return block_diag_out(c_re), block_diag_out(c_im)


def _s5_kernel(xd_ref, wb_ref, are_ref, aim_ref, wcr_ref, wci_ref, dsk_ref, w1_ref, w2_ref,
               o_ref, u_ref, y_ref, bu_ref, h_ref, *, lc, nb, lane_chunk):
    @pl.when(pl.program_id(0) == 0)
    def _():
        h_ref[...] = jnp.zeros_like(h_ref)

    n_slab = D_WIDTH // LANES
    for b in range(nb):
        for s in range(n_slab):
            col = b * 2 * D_WIDTH + s * LANES
            u_ref[s, pl.ds(b, lc, stride=nb), :] = xd_ref[:, col:col + LANES].astype(F32)
    u = jnp.concatenate([u_ref[s] for s in range(n_slab)], axis=1)
    ub = u.astype(BF16)
    width = 2 * S5_PART_STATE
    def project(part):
        bu_ref[:, part * width:(part + 1) * width] = _dot(
            ub[:, part * S5_PART_IN:(part + 1) * S5_PART_IN], wb_ref[part])

    def scan(part):
        for c in range(S5_PART_STATE // lane_chunk):
            re = slice(part * width + c * lane_chunk, part * width + (c + 1) * lane_chunk)
            im = slice(re.start + S5_PART_STATE, re.stop + S5_PART_STATE)
            ab = slice(part * S5_PART_STATE + c * lane_chunk, part * S5_PART_STATE + (c + 1) * lane_chunk)
            a_r = jnp.broadcast_to(are_ref[:, ab], (nb, lane_chunk))
            a_i = jnp.broadcast_to(aim_ref[:, ab], (nb, lane_chunk))
            h_r, h_i = h_ref[:, re], h_ref[:, im]
            for t in range(lc):
                row = slice(t * nb, (t + 1) * nb)
                h_r, h_i = (a_r * h_r - a_i * h_i + bu_ref[row, re], a_r * h_i + a_i * h_r + bu_ref[row, im])
                bu_ref[row, re] = h_r
                bu_ref[row, im] = h_i
            h_ref[:, re] = h_r
            h_ref[:, im] = h_i

    def readout(part):
        re = slice(part * width, part * width + S5_PART_STATE)
        im = slice(re.stop, re.stop + S5_PART_STATE)
        return _dot(bu_ref[:, re].astype(BF16), wcr_ref[part]) - _dot(bu_ref[:, im].astype(BF16), wci_ref[part])

    ys = []
    project(0)
    for part in range(S5_PARTS):
        if part + 1 < S5_PARTS:
            project(part + 1)
        scan(part)
        ys.append(readout(part))
    y = jnp.concatenate(ys, axis=1) + dsk_ref[...] * u
    y = jax.nn.gelu(y).astype(BF16)
    out = _dot(y, w1_ref[...]) * jax.nn.sigmoid(_dot(y, w2_ref[...]))
    for s in range(n_slab):
        y_ref[s] = out[:, s * LANES:(s + 1) * LANES]
    o_ref[...] = jnp.concatenate(
        [y_ref[s, pl.ds(b, lc, stride=nb), :] for b in range(nb) for s in range(n_slab)], axis=1).astype(o_ref.dtype)


def s5_layer(zd, w_b, abar_re, abar_im, w_cr, w_ci, d_skip, w1, w2, layer, *, batch, seq, lc):
    rows = lc * batch
    n_slab = D_WIDTH // LANES
    full2 = lambda shape: pl.BlockSpec(shape, lambda c: (0, 0))
    full3 = lambda shape: pl.BlockSpec(shape, lambda c: (0, 0, 0))
    return pl.pallas_call(
        functools.partial(_s5_kernel, lc=lc, nb=batch, lane_chunk=512),
        out_shape=jax.ShapeDtypeStruct((seq, batch * D_WIDTH), BF16),
        grid=(seq // lc,),
        in_specs=[
            pl.BlockSpec((lc, batch * 2 * D_WIDTH), lambda c: (c, 0)),
            full3((S5_PARTS, S5_PART_IN, 2 * S5_PART_STATE)),
            full2((1, S5_LANES)),
            full2((1, S5_LANES)),
            full3((S5_PARTS, S5_PART_STATE, S5_PART_IN)),
            full3((S5_PARTS, S5_PART_STATE, S5_PART_IN)),
            full2((1, D_WIDTH)),
            _layer_block((D_WIDTH, D_WIDTH), layer),
            _layer_block((D_WIDTH, D_WIDTH), layer),
        ],
        out_specs=pl.BlockSpec((lc, batch * D_WIDTH), lambda c: (c, 0)),
        scratch_shapes=[
            pltpu.VMEM((n_slab, rows, LANES), F32),
            pltpu.VMEM((n_slab, rows, LANES), F32),
            pltpu.VMEM((rows, 2 * S5_LANES), F32),
            pltpu.VMEM((batch, 2 * S5_LANES), F32),
        ],
        compiler_params=_params("arbitrary"),
        name="s5_layer",
    )(zd, w_b, abar_re, abar_im, w_cr, w_ci, d_skip.reshape(1, D_WIDTH), w1, w2)


def _odd_tail_kernel(u_ref, v_ref, gc_ref, d_ref, gd_ref, x_ref, lng_ref, lnb_ref, ws_ref, bs_ref, wo_ref,
                     o_ref, c_ref, *, tm):
    v = v_ref[...].astype(F32)
    mu = jnp.mean(v, axis=-1, keepdims=True)
    vc = v - mu
    var = jnp.mean(vc * vc, axis=-1, keepdims=True)
    vn = (vc * lax.rsqrt(var + EPS) * lng_ref[...] + lnb_ref[...]).astype(BF16)
    ri = lax.broadcasted_iota(jnp.int32, (C_CHUNK, C_CHUNK), 0)
    ci = lax.broadcasted_iota(jnp.int32, (C_CHUNK, C_CHUNK), 1)
    causal = ri >= ci
    for g in range(C_GROUPS):
        cols = slice(g * C_GROUP_DIM, (g + 1) * C_GROUP_DIM)
        w_g = jnp.where(causal, ws_ref[g], 0.0).astype(BF16)
        b_g = bs_ref[:, g:g + 1]
        for c in range(tm // C_CHUNK):
            rows = slice(c * C_CHUNK, (c + 1) * C_CHUNK)
            mixed = _dot(w_g, vn[rows, cols]) + b_g
            gate = _silu(gc_ref[rows, cols].astype(F32))
            c_ref[rows, cols] = (u_ref[rows, cols].astype(F32) * mixed * gate).astype(BF16)
    d = (d_ref[...].astype(F32) * _silu(gd_ref[...].astype(F32))).astype(BF16)
    y = _dot(c_ref[...], wo_ref[:C_WIDTH, :]) + _dot(d, wo_ref[C_WIDTH:, :])
    o_ref[...] = x_ref[...] + y


def odd_tail(zc, zd, d_out, x, ln_g, ln_b, w_s, b_s, w_out, layer, *, batch, seq, tm):
    return pl.pallas_call(
        functools.partial(_odd_tail_kernel, tm=tm),
        out_shape=jax.ShapeDtypeStruct((batch, seq, D_MODEL), F32),
        grid=(batch, seq // tm),
        in_specs=[
            pl.BlockSpec((None, tm, C_WIDTH), lambda b, t: (b, t, 0)),
            pl.BlockSpec((None, tm, C_WIDTH), lambda b, t: (b, t, 1)),
            pl.BlockSpec((None, tm, C_WIDTH), lambda b, t: (b, t, 2)),
            pl.BlockSpec((tm, D_WIDTH), lambda b, t: (t, b)),
            pl.BlockSpec((tm, D_WIDTH), lambda b, t: (t, 2 * b + 1)),
            pl.BlockSpec((None, tm, D_MODEL), lambda b, t: (b, t, 0)),
            pl.BlockSpec((1, C_WIDTH), lambda b, t: (0, 0)),
            pl.BlockSpec((1, C_WIDTH), lambda b, t: (0, 0)),
            _layer_block((C_GROUPS, C_CHUNK, C_CHUNK), layer),
            pl.BlockSpec((C_CHUNK, C_GROUPS), lambda b, t: (0, 0)),
            _layer_block((C_WIDTH + D_WIDTH, D_MODEL), layer),
        ],
        out_specs=pl.BlockSpec((None, tm, D_MODEL), lambda b, t: (b, t, 0)),
        scratch_shapes=[pltpu.VMEM((tm, C_WIDTH), BF16)],
        compiler_params=_params("parallel", "parallel"),
        name="odd_tail",
    )(zc, zc, zc, d_out, zd, x, ln_g.reshape(1, C_WIDTH), ln_b.reshape(1, C_WIDTH), w_s, b_s.T, w_out)


def _xattn_kernel(x_ref, g_ref, wq_ref, kv_ref, wo_ref, fg_ref, o_ref, *, final):
    x = x_ref[...]
    q = _dot(_rms(x, g_ref[...]).astype(BF16), wq_ref[...])
    heads = []
    for h in range(X_HEADS):
        ks = slice(h * X_HEAD_DIM, (h + 1) * X_HEAD_DIM)
        vs = slice(D_MODEL + h * X_HEAD_DIM, D_MODEL + (h + 1) * X_HEAD_DIM)
        s = _dot_nt(q[:, ks].astype(BF16), kv_ref[:, ks]) * (X_HEAD_DIM ** -0.5)
        m = jnp.max(s, axis=-1, keepdims=True)
        p = jnp.exp(s - m)
        den = jnp.sum(p, axis=-1, keepdims=True)
        heads.append((_dot(p.astype(BF16), kv_ref[:, vs]) / den).astype(BF16))
    y = x + _dot(jnp.concatenate(heads, axis=1), wo_ref[...])
    if final:
        y = _rms(y, fg_ref[...])
    o_ref[...] = y


def cross_attention(x, g, w_q, kv, w_o, final_g, layer, *, batch, seq, mem_len, tq, final):
    return pl.pallas_call(
        functools.partial(_xattn_kernel, final=final),
        out_shape=jax.ShapeDtypeStruct((batch, seq, D_MODEL), F32),
        grid=(batch, seq // tq),
        in_specs=[
            pl.BlockSpec((None, tq, D_MODEL), lambda b, t: (b, t, 0)),
            pl.BlockSpec((1, D_MODEL), lambda b, t: (0, 0)),
            _layer_block((D_MODEL, D_MODEL), layer),
            pl.BlockSpec((None, mem_len, 2 * D_MODEL), lambda b, t: (b, 0, 0)),
            _layer_block((D_MODEL, D_MODEL), layer),
            pl.BlockSpec((1, D_MODEL), lambda b, t: (0, 0)),
        ],
        out_specs=pl.BlockSpec((None, tq, D_MODEL), lambda b, t: (b, t, 0)),
        compiler_params=_params("parallel", "parallel"),
        name="cross_attention",
    )(x, g.reshape(1, D_MODEL), w_q, kv, w_o, final_g.reshape(1, D_MODEL))


def kernel(x, mem, norm_ab, w_in_ab, pool_w, pool_scale, w_out_ab, norm_cd, w_in_cd, sgu_ln_g, sgu_ln_b,
           sgu_w, sgu_b, s5_a_re, s5_a_im, s5_log_dt, s5_b_re, s5_b_im, s5_c_re, s5_c_im, s5_d,
           glu_w1, glu_w2, w_out_cd, norm_x, w_xq, w_xkv, w_xo, mem_norm, final_norm):
    batch, seq, _ = x.shape
    mem_len = mem.shape[1]
    depth = norm_x.shape[0]
    tokens = batch * seq
    tm_proj = 1024
    tiles_per_row = seq // tm_proj
    bf = lambda t: t.astype(BF16)

    w_in_ab, pool_w, w_out_ab, w_in_cd, glu_w1, glu_w2, w_out_cd, w_xq, w_xkv, w_xo = (
        bf(t) for t in (w_in_ab, pool_w, w_out_ab, w_in_cd, glu_w1, glu_w2, w_out_cd, w_xq, w_xkv, w_xo))
    mem2 = mem.reshape(batch * mem_len, D_MODEL)
    for layer in range(depth):
        i = layer // 2
        x2 = x.reshape(tokens, D_MODEL)
        if layer % 2 == 0:
            z = norm_matmul(x2, norm_ab[i], w_in_ab, i, tm=tm_proj, tn=2048)
            z = z.reshape(batch, seq, z.shape[1])
            a_out = dilated_attention(z, batch=batch, seq=seq)
            x = even_tail(z, a_out, x, pool_w, pool_scale[i], w_out_ab, i, batch=batch, seq=seq, tm=512)
        else:
            zc, zd = odd_in_proj(x2, norm_cd[i], w_in_cd, i, batch=batch, seq=seq, tm=tm_proj, tn=2 * D_WIDTH)
            w_b, abar_re, abar_im = s5_prep(s5_a_re[i], s5_a_im[i], s5_log_dt[i], s5_b_re[i], s5_b_im[i])
            w_cr, w_ci = s5_out_weights(s5_c_re[i], s5_c_im[i])
            d_out = s5_layer(zd, w_b, abar_re, abar_im, w_cr, w_ci, s5_d[i], glu_w1, glu_w2, i,
                             batch=batch, seq=seq, lc=64)
            x = odd_tail(zc.reshape(batch, seq, zc.shape[1]), zd, d_out, x, sgu_ln_g[i], sgu_ln_b[i], sgu_w,
                         sgu_b[i], w_out_cd, i, batch=batch, seq=seq, tm=512)
        kv = norm_matmul(mem2, mem_norm, w_xkv, layer, tm=batch * mem_len, tn=1024)
        x = cross_attention(x, norm_x[layer], w_xq, kv.reshape(batch, mem_len, 2 * D_MODEL), w_xo, final_norm,
                            layer, batch=batch, seq=seq, mem_len=mem_len, tq=512, final=(layer == depth - 1))
    return x
```

```python
import functools

import jax
import jax.numpy as jnp
from jax import lax
from jax.experimental import pallas as pl
from jax.experimental.pallas import tpu as pltpu

F32 = jnp.float32
BF16 = jnp.bfloat16

LANES = 128
D_MODEL = 1024
A_WIDTH = 1024
A_HEAD_DIM = 64
A_BLOCK = 128
A_DILATIONS = (1, 4, 16)
DIL_STEP = 4
B_WIDTH = 1024
POOL_WINDOWS = (2, 4, 8, 16)
B_GROUP = 256
POOL_HALO = 16
C_WIDTH = 1024
C_CHUNK = 128
C_GROUPS = 4
C_GROUP_DIM = 256
D_WIDTH = 512
S5_GROUPS = 32
S5_GROUP_DIM = 16
S5_STATE = 64
S5_LANES = S5_GROUPS * S5_STATE
S5_PARTS = 2
S5_PART_IN = D_WIDTH // S5_PARTS
S5_PART_STATE = S5_LANES // S5_PARTS
X_HEADS = 4
X_HEAD_DIM = 256
EPS = 1e-6
NEG = -1e30
LOG2E = 1.4426950408889634

VMEM_LIMIT = 48 * 1024 * 1024


def _params(*sem):
    return pltpu.CompilerParams(dimension_semantics=sem, vmem_limit_bytes=VMEM_LIMIT)


def _rms(x, g):
    ms = jnp.mean(x * x, axis=-1, keepdims=True)
    return x * lax.rsqrt(ms + EPS) * g


def _silu(x):
    return x * jax.nn.sigmoid(x)


def _dot(a, b):
    return jnp.dot(a, b, preferred_element_type=F32)


def _layer_block(shape, layer):
    return pl.BlockSpec((None,) + tuple(shape), lambda *_: (layer,) + (0,) * len(shape))


def _dot_nt(a, b):
    return lax.dot_general(a, b, (((1,), (1,)), ((), ())), preferred_element_type=F32)


def _norm_matmul_kernel(x_ref, g_ref, w_ref, o_ref, xn_ref):
    @pl.when(pl.program_id(1) == 0)
    def _():
        xn_ref[...] = _rms(x_ref[...], g_ref[...]).astype(BF16)

    o_ref[...] = _dot(xn_ref[...], w_ref[...]).astype(o_ref.dtype)


def norm_matmul(x, g, w, layer, *, tm, tn):
    m, k = x.shape
    n = w.shape[2]
    return pl.pallas_call(
        _norm_matmul_kernel,
        out_shape=jax.ShapeDtypeStruct((m, n), BF16),
        grid=(m // tm, n // tn),
        in_specs=[
            pl.BlockSpec((tm, k), lambda i, j: (i, 0)),
            pl.BlockSpec((1, k), lambda i, j: (0, 0)),
            pl.BlockSpec((None, k, tn), lambda i, j: (layer, 0, j)),
        ],
        out_specs=pl.BlockSpec((tm, tn), lambda i, j: (i, j)),
        scratch_shapes=[pltpu.VMEM((tm, k), BF16)],
        compiler_params=_params("parallel", "arbitrary"),
        name="norm_matmul",
    )(x, g.reshape(1, k), w)


def _mem_proj_kernel(x_ref, g_ref, w_ref, o_ref, xn_ref):
    @pl.when((pl.program_id(0) == 0) & (pl.program_id(1) == 0))
    def _():
        xn_ref[...] = _rms(x_ref[...], g_ref[...]).astype(BF16)

    o_ref[...] = _dot(xn_ref[...], w_ref[...]).astype(o_ref.dtype)


def mem_kv_proj(mem, g, w, *, tn):
    m, k = mem.shape
    n_layers, _, n = w.shape
    return pl.pallas_call(
        _mem_proj_kernel,
        out_shape=jax.ShapeDtypeStruct((n_layers, m, n), BF16),
        grid=(n_layers, n // tn),
        in_specs=[
            pl.BlockSpec((m, k), lambda l, j: (0, 0)),
            pl.BlockSpec((1, k), lambda l, j: (0, 0)),
            pl.BlockSpec((None, k, tn), lambda l, j: (l, 0, j)),
        ],
        out_specs=pl.BlockSpec((None, m, tn), lambda l, j: (l, 0, j)),
        scratch_shapes=[pltpu.VMEM((m, k), BF16)],
        compiler_params=_params("arbitrary", "arbitrary"),
        name="mem_kv_proj",
    )(mem, g.reshape(1, k), w)


def _odd_proj_kernel(x_ref, g_ref, w_ref, zc_ref, zd_ref, xn_ref, *, c_blocks):
    j = pl.program_id(1)

    @pl.when(j == 0)
    def _():
        xn_ref[...] = _rms(x_ref[...], g_ref[...]).astype(BF16)

    y = _dot(xn_ref[...], w_ref[...]).astype(BF16)

    @pl.when(j < c_blocks)
    def _():
        zc_ref[...] = y

    @pl.when(j == c_blocks)
    def _():
        zd_ref[...] = y


def odd_in_proj(x, g, w, layer, *, batch, seq, tm, tn):
    m, k = x.shape
    c_blocks = 3 * C_WIDTH // tn
    assert tn == 2 * D_WIDTH and seq % tm == 0
    tiles_per_row = seq // tm
    return pl.pallas_call(
        functools.partial(_odd_proj_kernel, c_blocks=c_blocks),
        out_shape=(jax.ShapeDtypeStruct((m, 3 * C_WIDTH), BF16),
                   jax.ShapeDtypeStruct((seq, batch * tn), BF16)),
        grid=(m // tm, c_blocks + 1),
        in_specs=[
            pl.BlockSpec((tm, k), lambda i, j: (i, 0)),
            pl.BlockSpec((1, k), lambda i, j: (0, 0)),
            pl.BlockSpec((None, k, tn), lambda i, j: (layer, 0, j)),
        ],
        out_specs=(pl.BlockSpec((tm, tn), lambda i, j: (i, jnp.minimum(j, c_blocks - 1))),
                   pl.BlockSpec((tm, tn), lambda i, j: (i % tiles_per_row, i // tiles_per_row))),
        scratch_shapes=[pltpu.VMEM((tm, k), BF16)],
        compiler_params=_params("parallel", "arbitrary"),
        name="odd_in_proj",
    )(x, g.reshape(1, k), w)


def _dilated_kernel(q_ref, k_ref, v_ref, g_ref, o_ref, qs_ref, ks_ref, vs_ref, gs_ref, os_ref, num_ref, den_ref,
                    m_ref, bias_ref, *, seq, unroll):
    assert A_DILATIONS == (1, DIL_STEP, DIL_STEP * DIL_STEP)
    qi = lax.broadcasted_iota(jnp.int32, (A_BLOCK, 2 * A_BLOCK), 0)
    kj = lax.broadcasted_iota(jnp.int32, (A_BLOCK, 2 * A_BLOCK), 1)
    dist = qi + A_BLOCK - kj
    band = (dist >= 0) & (dist <= A_BLOCK)
    bias_ref[1] = jnp.where(band, 0.0, NEG).astype(BF16)
    bias_ref[0] = jnp.where(band & (kj >= A_BLOCK), 0.0, NEG).astype(BF16)

    len_s = seq // DIL_STEP
    len_w = len_s // DIL_STEP
    blocks_s = len_s // A_BLOCK
    q_scale = A_HEAD_DIM ** -0.5 * LOG2E
    for src, dst, scale in ((q_ref, qs_ref, q_scale), (k_ref, ks_ref, None), (v_ref, vs_ref, None)):
        nat = src[...].astype(F32)
        dst[0] = nat if scale is None else nat * scale
        for b in range(DIL_STEP):
            dst[1, b * len_s:(b + 1) * len_s, :] = dst[0, pl.ds(b, len_s, stride=DIL_STEP), :]
        for c in range(DIL_STEP * DIL_STEP):
            b, a = divmod(c, DIL_STEP)
            dst[2, c * len_w:(c + 1) * len_w, :] = dst[1, pl.ds(b * len_s + a, len_w, stride=DIL_STEP), :]
    gs_ref[0] = g_ref[...].astype(F32)

    head0 = lax.broadcasted_iota(jnp.int32, (A_BLOCK, LANES), 1) < A_HEAD_DIM

    def load(slab, cur, prev):
        q = qs_ref[slab, cur, :]
        q2 = jnp.concatenate([jnp.where(head0, q, 0.0), jnp.where(head0, 0.0, q)], axis=0).astype(BF16)
        if prev is None:
            return q2, ks_ref[slab, cur, :].astype(BF16), vs_ref[slab, cur, :].astype(BF16)
        if slab == 0:
            kk = jnp.concatenate([k_ref[prev, :], k_ref[cur, :]], axis=0)
            vv = jnp.concatenate([v_ref[prev, :], v_ref[cur, :]], axis=0)
            return q2, kk, vv
        kk = jnp.concatenate([ks_ref[slab, prev, :], ks_ref[slab, cur, :]], axis=0).astype(BF16)
        vv = jnp.concatenate([vs_ref[slab, prev, :], vs_ref[slab, cur, :]], axis=0).astype(BF16)
        return q2, kk, vv

    def attend(q2, kk, vv, bias):
        s = _dot_nt(q2, kk).astype(BF16) + jnp.concatenate([bias, bias], axis=0)
        m = jnp.max(s, axis=-1, keepdims=True)
        p = jnp.exp2(s - m)
        res = _dot(p, jnp.concatenate([vv, jnp.ones(vv.shape, BF16)], axis=1))
        top, bot = res[:A_BLOCK], res[A_BLOCK:]
        m_top = jnp.broadcast_to(m[:A_BLOCK], (A_BLOCK, LANES)).astype(F32)
        m_bot = jnp.broadcast_to(m[A_BLOCK:], (A_BLOCK, LANES)).astype(F32)
        return (jnp.where(head0, top[:, :LANES], bot[:, :LANES]), jnp.where(head0, top[:, LANES:], bot[:, LANES:]),
                jnp.where(head0, m_top, m_bot))

    def aligned(start):
        return pl.ds(pl.multiple_of(start, A_BLOCK), A_BLOCK)

    def banded_rows(idx, blocks_per_seq):
        start = idx * A_BLOCK
        first = idx % blocks_per_seq == 0
        return aligned(start), aligned(jnp.where(first, start, start - A_BLOCK)), bias_ref[jnp.where(first, 0, 1)]

    def merge(num_a, den_a, m_a, num_b, den_b, m_b):
        gap = m_a - m_b
        shrink = jnp.exp2(-jnp.abs(gap))
        w_a = jnp.where(gap >= 0, 1.0, shrink)
        w_b = jnp.where(gap >= 0, shrink, 1.0)
        return w_a * num_a + w_b * num_b, w_a * den_a + w_b * den_b, jnp.maximum(m_a, m_b)

    def narrow_body(idx, carry):
        cur, prev, bias = banded_rows(idx, n_blocks)
        num, den, m = attend(*load(0, cur, prev), bias)
        num_ref[0, cur, :] = num
        den_ref[0, cur, :] = den
        m_ref[0, cur, :] = m
        return carry

    def step_body(idx, carry):
        cur, prev, bias = banded_rows(idx, blocks_s)
        natural = pl.ds(idx // blocks_s + DIL_STEP * A_BLOCK * (idx % blocks_s), A_BLOCK, stride=DIL_STEP)
        num, den, m = merge(*attend(*load(1, cur, prev), bias),
                            num_ref[0, natural, :], den_ref[0, natural, :], m_ref[0, natural, :])
        num_ref[1, cur, :] = num
        den_ref[1, cur, :] = den
        m_ref[1, cur, :] = m
        gs_ref[1, cur, :] = gs_ref[0, natural, :]
        return carry

    def wide_body(idx, carry):
        cur = aligned(idx * A_BLOCK)
        by_step = pl.ds((idx // DIL_STEP) * len_s + idx % DIL_STEP, A_BLOCK, stride=DIL_STEP)
        natural = pl.ds(DIL_STEP * (idx % DIL_STEP) + idx // DIL_STEP, A_BLOCK, stride=DIL_STEP * DIL_STEP)
        num, den, _ = merge(*attend(*load(2, cur, None), bias_ref[0][:, A_BLOCK:]),
                            num_ref[1, by_step, :], den_ref[1, by_step, :], m_ref[1, by_step, :])
        os_ref[natural, :] = num / den * _silu(gs_ref[1, by_step, :])
        return carry

    assert len_w == A_BLOCK
    n_blocks = seq // A_BLOCK
    for body in (narrow_body, step_body, wide_body):
        lax.fori_loop(0, n_blocks, body, 0, unroll=unroll)
    o_ref[...] = os_ref[...].astype(o_ref.dtype)


def dilated_attention(z, *, batch, seq, unroll=16):
    n_pair = A_WIDTH // LANES
    n_pat = len(A_DILATIONS)
    blk = (None, seq, LANES)
    slabs = pltpu.VMEM((n_pat, seq, LANES), F32)
    partial = pltpu.VMEM((2, seq, LANES), F32)
    return pl.pallas_call(
        functools.partial(_dilated_kernel, seq=seq, unroll=unroll),
        out_shape=jax.ShapeDtypeStruct((batch, seq, A_WIDTH), BF16),
        grid=(batch, n_pair),
        in_specs=[
            pl.BlockSpec(blk, lambda b, h: (b, 0, h)),
            pl.BlockSpec(blk, lambda b, h: (b, 0, n_pair + h)),
            pl.BlockSpec(blk, lambda b, h: (b, 0, 2 * n_pair + h)),
            pl.BlockSpec(blk, lambda b, h: (b, 0, 3 * n_pair + h)),
        ],
        out_specs=pl.BlockSpec(blk, lambda b, h: (b, 0, h)),
        scratch_shapes=[
            slabs, slabs, slabs,
            pltpu.VMEM((2, seq, LANES), F32),
            pltpu.VMEM((seq, LANES), F32),
            partial, partial, partial,
            pltpu.VMEM((2, A_BLOCK, 2 * A_BLOCK), BF16),
        ],
        compiler_params=_params("parallel", "parallel"),
        name="dilated_attention",
    )(z, z, z, z)


def _even_tail_kernel(vb_ref, halo_ref, gb_ref, a_ref, x_ref, pw_ref, ps_ref, wo_ref, o_ref, *, tm, sub):
    ti = pl.program_id(1)
    for r in range(tm // sub):
        rows = slice(r * sub, (r + 1) * sub)
        v = vb_ref[rows, :].astype(F32)
        if r == 0:
            halo = jnp.where(ti > 0, halo_ref[...].astype(F32), 0.0)
        else:
            halo = vb_ref[r * sub - POOL_HALO:r * sub, :].astype(F32)
        xc = jnp.concatenate([halo, v], axis=0)
        s2 = xc + pltpu.roll(xc, 1, 0)
        t4 = s2[:, B_GROUP:]
        s4 = t4 + pltpu.roll(t4, 2, 0)
        t8 = s4[:, B_GROUP:]
        s8 = t8 + pltpu.roll(t8, 4, 0)
        t16 = s8[:, B_GROUP:]
        s16 = t16 + pltpu.roll(t16, 8, 0)
        sums = (s2[:, :B_GROUP], s4[:, :B_GROUP], s8[:, :B_GROUP], s16)
        pos = (ti * tm + r * sub + 1 + lax.broadcasted_iota(jnp.int32, (sub, 1), 0)).astype(F32)
        mixed = []
        for g, w in enumerate(POOL_WINDOWS):
            mean = sums[g][POOL_HALO:, :] / jnp.minimum(pos, float(w))
            pooled = mean - v[:, g * B_GROUP:(g + 1) * B_GROUP]
            mixed.append(_dot(pooled.astype(BF16), pw_ref[g]))
        b_out = jnp.concatenate(mixed, axis=1) * ps_ref[...] * _silu(gb_ref[rows, :].astype(F32))
        y = _dot(a_ref[rows, :], wo_ref[:A_WIDTH, :]) + _dot(b_out.astype(BF16), wo_ref[A_WIDTH:, :])
        o_ref[rows, :] = x_ref[rows, :] + y


def even_tail(z, a_out, x, pool_w, pool_scale, w_out, layer, *, batch, seq, tm, sub):
    vb_col = 4 * A_WIDTH // B_WIDTH
    halo_per_tile = tm // POOL_HALO
    return pl.pallas_call(
        functools.partial(_even_tail_kernel, tm=tm, sub=sub),
        out_shape=jax.ShapeDtypeStruct((batch, seq, D_MODEL), F32),
        grid=(batch, seq // tm),
        in_specs=[
            pl.BlockSpec((None, tm, B_WIDTH), lambda b, t: (b, t, vb_col)),
            pl.BlockSpec((None, POOL_HALO, B_WIDTH),
                         lambda b, t: (b, jnp.maximum(t * halo_per_tile - 1, 0), vb_col)),
            pl.BlockSpec((None, tm, B_WIDTH), lambda b, t: (b, t, vb_col + 1)),
            pl.BlockSpec((None, tm, A_WIDTH), lambda b, t: (b, t, 0)),
            pl.BlockSpec((None, tm, D_MODEL), lambda b, t: (b, t, 0)),
            _layer_block((len(POOL_WINDOWS), B_GROUP, B_GROUP), layer),
            pl.BlockSpec((1, B_WIDTH), lambda b, t: (0, 0)),
            _layer_block((A_WIDTH + B_WIDTH, D_MODEL), layer),
        ],
        out_specs=pl.BlockSpec((None, tm, D_MODEL), lambda b, t: (b, t, 0)),
        compiler_params=_params("parallel", "parallel"),
        name="even_tail",
    )(z, z, z, a_out, x, pool_w, pool_scale.reshape(1, B_WIDTH), w_out)


def _s5_prep_kernel(ar_ref, ai_ref, ldt_ref, br_ref, bi_ref, abr_ref, abi_ref, bbr_ref, bbi_ref):
    ar, ai = ar_ref[...], ai_ref[...]
    dt = jnp.exp(ldt_ref[...])
    mag = jnp.exp(dt * ar)
    abar_re = mag * jnp.cos(dt * ai)
    abar_im = mag * jnp.sin(dt * ai)
    nr, ni = abar_re - 1.0, abar_im
    inv = 1.0 / (ar * ar + ai * ai)
    coef_re = (nr * ar + ni * ai) * inv
    coef_im = (ni * ar - nr * ai) * inv
    br, bi = br_ref[...], bi_ref[...]
    abr_ref[...] = abar_re
    abi_ref[...] = abar_im
    bbr_ref[...] = coef_re * br - coef_im * bi
    bbi_ref[...] = coef_re * bi + coef_im * br


def _part_eye():
    return jnp.eye(S5_GROUPS // S5_PARTS, dtype=F32)


def s5_prep(a_re, a_im, log_dt, b_re, b_im):
    rep = lambda t: jnp.repeat(t, S5_GROUP_DIM, axis=0)
    to_rows = lambda t: t.transpose(0, 2, 1).reshape(D_WIDTH, S5_STATE)
    shp = jax.ShapeDtypeStruct((D_WIDTH, S5_STATE), F32)
    abr, abi, bbr, bbi = pl.pallas_call(
        _s5_prep_kernel, out_shape=(shp, shp, shp, shp), name="s5_prep",
    )(rep(a_re), rep(a_im), rep(jnp.broadcast_to(log_dt[:, None], (S5_GROUPS, S5_STATE))),
      to_rows(b_re), to_rows(b_im))
    gpp = S5_GROUPS // S5_PARTS

    def block_diag_in(t):
        t = t.reshape(S5_PARTS, gpp, S5_GROUP_DIM, 1, S5_STATE) * _part_eye()[None, :, None, :, None]
        return t.reshape(S5_PARTS, S5_PART_IN, S5_PART_STATE)

    w_b = jnp.concatenate([block_diag_in(bbr), block_diag_in(bbi)], axis=2).astype(BF16)
    abar_re = abr[::S5_GROUP_DIM].reshape(1, S5_LANES)
    abar_im = abi[::S5_GROUP_DIM].reshape(1, S5_LANES)
    return w_b, abar_re, abar_im


def s5_out_weights(c_re, c_im):
    gpp = S5_GROUPS // S5_PARTS

    def block_diag_out(t):
        t = t.reshape(S5_PARTS, gpp, S5_GROUP_DIM, S5_STATE).transpose(0, 1, 3, 2)
        t = t[:, :, :, None, :] * _part_eye()[None, :, None, :, None]
        return t.reshape(S5_PARTS, S5_PART_STATE, S5_PART_IN).astype(BF16)

    return block_diag_out(c_re), block_diag_out(c_im)


def _s5_kernel(xd_ref, wb_ref, are_ref, aim_ref, wcr_ref, wci_ref, dsk_ref, w1_ref, w2_ref,
               o_ref, u_ref, y_ref, bu_ref, h_ref, *, lc, nb, lane_chunk):
    @pl.when(pl.program_id(0) == 0)
    def _():
        h_ref[...] = jnp.zeros_like(h_ref)

    n_slab = D_WIDTH // LANES
    for b in range(nb):
        for s in range(n_slab):
            col = b * 2 * D_WIDTH + s * LANES
            u_ref[s, pl.ds(b, lc, stride=nb), :] = xd_ref[:, col:col + LANES].astype(F32)
    u = jnp.concatenate([u_ref[s] for s in range(n_slab)], axis=1)
    ub = u.astype(BF16)
    width = 2 * S5_PART_STATE
    def project(part):
        bu_ref[:, part * width:(part + 1) * width] = _dot(
            ub[:, part * S5_PART_IN:(part + 1) * S5_PART_IN], wb_ref[part])

    def scan(part):
        for c in range(S5_PART_STATE // lane_chunk):
            re = slice(part * width + c * lane_chunk, part * width + (c + 1) * lane_chunk)
            im = slice(re.start + S5_PART_STATE, re.stop + S5_PART_STATE)
            ab = slice(part * S5_PART_STATE + c * lane_chunk, part * S5_PART_STATE + (c + 1) * lane_chunk)
            a_r = jnp.broadcast_to(are_ref[:, ab], (nb, lane_chunk))
            a_i = jnp.broadcast_to(aim_ref[:, ab], (nb, lane_chunk))
            h_r, h_i = h_ref[:, re], h_ref[:, im]
            for t in range(lc):
                row = slice(t * nb, (t + 1) * nb)
                h_r, h_i = (a_r * h_r - a_i * h_i + bu_ref[row, re], a_r * h_i + a_i * h_r + bu_ref[row, im])
                bu_ref[row, re] = h_r
                bu_ref[row, im] = h_i
            h_ref[:, re] = h_r
            h_ref[:, im] = h_i

    def readout(part):
        re = slice(part * width, part * width + S5_PART_STATE)
        im = slice(re.stop, re.stop + S5_PART_STATE)
        return _dot(bu_ref[:, re].astype(BF16), wcr_ref[part]) - _dot(bu_ref[:, im].astype(BF16), wci_ref[part])

    ys = []
    project(0)
    for part in range(S5_PARTS):
        if part + 1 < S5_PARTS:
            project(part + 1)
        scan(part)
        ys.append(readout(part))
    y = jnp.concatenate(ys, axis=1) + dsk_ref[...] * u
    y = jax.nn.gelu(y).astype(BF16)
    out = _dot(y, w1_ref[...]) * jax.nn.sigmoid(_dot(y, w2_ref[...]))
    for s in range(n_slab):
        y_ref[s] = out[:, s * LANES:(s + 1) * LANES]
    o_ref[...] = jnp.concatenate(
        [y_ref[s, pl.ds(b, lc, stride=nb), :] for b in range(nb) for s in range(n_slab)], axis=1).astype(o_ref.dtype)


def s5_layer(zd, w_b, abar_re, abar_im, w_cr, w_ci, d_skip, w1, w2, layer, *, batch, seq, lc):
    rows = lc * batch
    n_slab = D_WIDTH // LANES
    full2 = lambda shape: pl.BlockSpec(shape, lambda c: (0, 0))
    full3 = lambda shape: pl.BlockSpec(shape, lambda c: (0, 0, 0))
    return pl.pallas_call(
        functools.partial(_s5_kernel, lc=lc, nb=batch, lane_chunk=512),
        out_shape=jax.ShapeDtypeStruct((seq, batch * D_WIDTH), BF16),
        grid=(seq // lc,),
        in_specs=[
            pl.BlockSpec((lc, batch * 2 * D_WIDTH), lambda c: (c, 0)),
            full3((S5_PARTS, S5_PART_IN, 2 * S5_PART_STATE)),
            full2((1, S5_LANES)),
            full2((1, S5_LANES)),
            full3((S5_PARTS, S5_PART_STATE, S5_PART_IN)),
            full3((S5_PARTS, S5_PART_STATE, S5_PART_IN)),
            full2((1, D_WIDTH)),
            _layer_block((D_WIDTH, D_WIDTH), layer),
            _layer_block((D_WIDTH, D_WIDTH), layer),
        ],
        out_specs=pl.BlockSpec((lc, batch * D_WIDTH), lambda c: (c, 0)),
        scratch_shapes=[
            pltpu.VMEM((n_slab, rows, LANES), F32),
            pltpu.VMEM((n_slab, rows, LANES), F32),
            pltpu.VMEM((rows, 2 * S5_LANES), F32),
            pltpu.VMEM((batch, 2 * S5_LANES), F32),
        ],
        compiler_params=_params("arbitrary"),
        name="s5_layer",
    )(zd, w_b, abar_re, abar_im, w_cr, w_ci, d_skip.reshape(1, D_WIDTH), w1, w2)


def _odd_tail_kernel(u_ref, v_ref, gc_ref, d_ref, gd_ref, x_ref, lng_ref, lnb_ref, ws_ref, bs_ref, wo_ref,
                     o_ref, c_ref, *, tm, sub):
    ri = lax.broadcasted_iota(jnp.int32, (C_CHUNK, C_CHUNK), 0)
    ci = lax.broadcasted_iota(jnp.int32, (C_CHUNK, C_CHUNK), 1)
    causal = ri >= ci
    w_s = [jnp.where(causal, ws_ref[g], 0.0).astype(BF16) for g in range(C_GROUPS)]
    for r in range(tm // sub):
        tile = slice(r * sub, (r + 1) * sub)
        v = v_ref[tile, :].astype(F32)
        mu = jnp.mean(v, axis=-1, keepdims=True)
        vc = v - mu
        var = jnp.mean(vc * vc, axis=-1, keepdims=True)
        vn = (vc * lax.rsqrt(var + EPS) * lng_ref[...] + lnb_ref[...]).astype(BF16)
        for g in range(C_GROUPS):
            cols = slice(g * C_GROUP_DIM, (g + 1) * C_GROUP_DIM)
            b_g = bs_ref[:, g:g + 1]
            for c in range(sub // C_CHUNK):
                rows = slice(r * sub + c * C_CHUNK, r * sub + (c + 1) * C_CHUNK)
                mixed = _dot(w_s[g], vn[c * C_CHUNK:(c + 1) * C_CHUNK, cols]) + b_g
                gate = _silu(gc_ref[rows, cols].astype(F32))
                c_ref[rows, cols] = (u_ref[rows, cols].astype(F32) * mixed * gate).astype(BF16)
        d = (d_ref[tile, :].astype(F32) * _silu(gd_ref[tile, :].astype(F32))).astype(BF16)
        y = _dot(c_ref[tile, :], wo_ref[:C_WIDTH, :]) + _dot(d, wo_ref[C_WIDTH:, :])
        o_ref[tile, :] = x_ref[tile, :] + y


def odd_tail(zc, zd, d_out, x, ln_g, ln_b, w_s, b_s, w_out, layer, *, batch, seq, tm, sub):
    return pl.pallas_call(
        functools.partial(_odd_tail_kernel, tm=tm, sub=sub),
        out_shape=jax.ShapeDtypeStruct((batch, seq, D_MODEL), F32),
        grid=(batch, seq // tm),
        in_specs=[
            pl.BlockSpec((None, tm, C_WIDTH), lambda b, t: (b, t, 0)),
            pl.BlockSpec((None, tm, C_WIDTH), lambda b, t: (b, t, 1)),
            pl.BlockSpec((None, tm, C_WIDTH), lambda b, t: (b, t, 2)),
            pl.BlockSpec((tm, D_WIDTH), lambda b, t: (t, b)),
            pl.BlockSpec((tm, D_WIDTH), lambda b, t: (t, 2 * b + 1)),
            pl.BlockSpec((None, tm, D_MODEL), lambda b, t: (b, t, 0)),
            pl.BlockSpec((1, C_WIDTH), lambda b, t: (0, 0)),
            pl.BlockSpec((1, C_WIDTH), lambda b, t: (0, 0)),
            _layer_block((C_GROUPS, C_CHUNK, C_CHUNK), layer),
            pl.BlockSpec((C_CHUNK, C_GROUPS), lambda b, t: (0, 0)),
            _layer_block((C_WIDTH + D_WIDTH, D_MODEL), layer),
        ],
        out_specs=pl.BlockSpec((None, tm, D_MODEL), lambda b, t: (b, t, 0)),
        scratch_shapes=[pltpu.VMEM((tm, C_WIDTH), BF16)],
        compiler_params=_params("parallel", "parallel"),
        name="odd_tail",
    )(zc, zc, zc, d_out, zd, x, ln_g.reshape(1, C_WIDTH), ln_b.reshape(1, C_WIDTH), w_s, b_s.T, w_out)


def _xattn_kernel(x_ref, g_ref, wq_ref, kv_ref, wo_ref, fg_ref, o_ref, *, final, sub):
    for r in range(x_ref.shape[0] // sub):
        rows = slice(r * sub, (r + 1) * sub)
        x = x_ref[rows, :]
        q = _dot(_rms(x, g_ref[...]).astype(BF16), wq_ref[...])
        heads = []
        for h in range(X_HEADS):
            ks = slice(h * X_HEAD_DIM, (h + 1) * X_HEAD_DIM)
            vs = slice(D_MODEL + h * X_HEAD_DIM, D_MODEL + (h + 1) * X_HEAD_DIM)
            s = _dot_nt(q[:, ks].astype(BF16), kv_ref[:, ks]) * (X_HEAD_DIM ** -0.5)
            m = jnp.max(s, axis=-1, keepdims=True)
            p = jnp.exp(s - m)
            den = jnp.sum(p, axis=-1, keepdims=True)
            heads.append((_dot(p.astype(BF16), kv_ref[:, vs]) / den).astype(BF16))
        y = x + _dot(jnp.concatenate(heads, axis=1), wo_ref[...])
        if final:
            y = _rms(y, fg_ref[...])
        o_ref[rows, :] = y


def cross_attention(x, g, w_q, kv, w_o, final_g, layer, *, batch, seq, mem_len, tq, sub, final):
    return pl.pallas_call(
        functools.partial(_xattn_kernel, final=final, sub=sub),
        out_shape=jax.ShapeDtypeStruct((batch, seq, D_MODEL), F32),
        grid=(batch, seq // tq),
        in_specs=[
            pl.BlockSpec((None, tq, D_MODEL), lambda b, t: (b, t, 0)),
            pl.BlockSpec((1, D_MODEL), lambda b, t: (0, 0)),
            _layer_block((D_MODEL, D_MODEL), layer),
            pl.BlockSpec((None, None, mem_len, 2 * D_MODEL), lambda b, t: (layer, b, 0, 0)),
            _layer_block((D_MODEL, D_MODEL), layer),
            pl.BlockSpec((1, D_MODEL), lambda b, t: (0, 0)),
        ],
        out_specs=pl.BlockSpec((None, tq, D_MODEL), lambda b, t: (b, t, 0)),
        compiler_params=_params("parallel", "parallel"),
        name="cross_attention",
    )(x, g.reshape(1, D_MODEL), w_q, kv, w_o, final_g.reshape(1, D_MODEL))


def kernel(x, mem, norm_ab, w_in_ab, pool_w, pool_scale, w_out_ab, norm_cd, w_in_cd, sgu_ln_g, sgu_ln_b,
           sgu_w, sgu_b, s5_a_re, s5_a_im, s5_log_dt, s5_b_re, s5_b_im, s5_c_re, s5_c_im, s5_d,
           glu_w1, glu_w2, w_out_cd, norm_x, w_xq, w_xkv, w_xo, mem_norm, final_norm):
    batch, seq, _ = x.shape
    mem_len = mem.shape[1]
    depth = norm_x.shape[0]
    tokens = batch * seq
    tm_proj = 1024
    tiles_per_row = seq // tm_proj
    bf = lambda t: t.astype(BF16)

    w_in_ab, pool_w, w_out_ab, w_in_cd, glu_w1, glu_w2, w_out_cd, w_xq, w_xkv, w_xo = (
        bf(t) for t in (w_in_ab, pool_w, w_out_ab, w_in_cd, glu_w1, glu_w2, w_out_cd, w_xq, w_xkv, w_xo))
    kv = mem_kv_proj(mem.reshape(batch * mem_len, D_MODEL), mem_norm, w_xkv, tn=1024)
    kv = kv.reshape(depth, batch, mem_len, 2 * D_MODEL)
    for layer in range(depth):
        i = layer // 2
        x2 = x.reshape(tokens, D_MODEL)
        if layer % 2 == 0:
            z = norm_matmul(x2, norm_ab[i], w_in_ab, i, tm=tm_proj, tn=2048)
            z = z.reshape(batch, seq, z.shape[1])
            a_out = dilated_attention(z, batch=batch, seq=seq)
            x = even_tail(z, a_out, x, pool_w, pool_scale[i], w_out_ab, i, batch=batch, seq=seq,
                          tm=1024, sub=512)
        else:
            zc, zd = odd_in_proj(x2, norm_cd[i], w_in_cd, i, batch=batch, seq=seq, tm=tm_proj, tn=2 * D_WIDTH)
            w_b, abar_re, abar_im = s5_prep(s5_a_re[i], s5_a_im[i], s5_log_dt[i], s5_b_re[i], s5_b_im[i])
            w_cr, w_ci = s5_out_weights(s5_c_re[i], s5_c_im[i])
            d_out = s5_layer(zd, w_b, abar_re, abar_im, w_cr, w_ci, s5_d[i], glu_w1, glu_w2, i,
                             batch=batch, seq=seq, lc=64)
            x = odd_tail(zc.reshape(batch, seq, zc.shape[1]), zd, d_out, x, sgu_ln_g[i], sgu_ln_b[i], sgu_w,
                         sgu_b[i], w_out_cd, i, batch=batch, seq=seq, tm=1024, sub=512)
        x = cross_attention(x, norm_x[layer], w_xq, kv, w_xo, final_norm,
                            layer, batch=batch, seq=seq, mem_len=mem_len, tq=1024, sub=512,
                            final=(layer == depth - 1))
    return x
```

```python
import functools

import jax
import jax.numpy as jnp
from jax import lax
from jax.experimental import pallas as pl
from jax.experimental.pallas import tpu as pltpu

F32 = jnp.float32
BF16 = jnp.bfloat16

LANES = 128
D_MODEL = 1024
A_WIDTH = 1024
A_HEAD_DIM = 64
A_BLOCK = 128
A_DILATIONS = (1, 4, 16)
DIL_STEP = 4
B_WIDTH = 1024
POOL_WINDOWS = (2, 4, 8, 16)
B_GROUP = 256
POOL_HALO = 16
C_WIDTH = 1024
C_CHUNK = 128
C_GROUPS = 4
C_GROUP_DIM = 256
D_WIDTH = 512
S5_GROUPS = 32
S5_GROUP_DIM = 16
S5_STATE = 64
S5_LANES = S5_GROUPS * S5_STATE
S5_PARTS = 2
S5_PART_IN = D_WIDTH // S5_PARTS
S5_PART_STATE = S5_LANES // S5_PARTS
X_HEADS = 4
X_HEAD_DIM = 256
EPS = 1e-6
NEG = -1e30
LOG2E = 1.4426950408889634

VMEM_LIMIT = 48 * 1024 * 1024


def _params(*sem):
    return pltpu.CompilerParams(dimension_semantics=sem, vmem_limit_bytes=VMEM_LIMIT)


def _rms(x, g):
    ms = jnp.mean(x * x, axis=-1, keepdims=True)
    return x * lax.rsqrt(ms + EPS) * g


def _silu(x):
    return x * jax.nn.sigmoid(x)


def _dot(a, b):
    return jnp.dot(a, b, preferred_element_type=F32)


def _layer_block(shape, layer):
    return pl.BlockSpec((None,) + tuple(shape), lambda *_: (layer,) + (0,) * len(shape))


def _dot_nt(a, b):
    return lax.dot_general(a, b, (((1,), (1,)), ((), ())), preferred_element_type=F32)


def _norm_matmul_kernel(x_ref, g_ref, w_ref, o_ref, xn_ref):
    @pl.when(pl.program_id(1) == 0)
    def _():
        xn_ref[...] = _rms(x_ref[...], g_ref[...]).astype(BF16)

    o_ref[...] = _dot(xn_ref[...], w_ref[...]).astype(o_ref.dtype)


def norm_matmul(x, g, w, layer, *, tm, tn):
    m, k = x.shape
    n = w.shape[2]
    return pl.pallas_call(
        _norm_matmul_kernel,
        out_shape=jax.ShapeDtypeStruct((m, n), BF16),
        grid=(m // tm, n // tn),
        in_specs=[
            pl.BlockSpec((tm, k), lambda i, j: (i, 0)),
            pl.BlockSpec((1, k), lambda i, j: (0, 0)),
            pl.BlockSpec((None, k, tn), lambda i, j: (layer, 0, j)),
        ],
        out_specs=pl.BlockSpec((tm, tn), lambda i, j: (i, j)),
        scratch_shapes=[pltpu.VMEM((tm, k), BF16)],
        compiler_params=_params("parallel", "arbitrary"),
        name="norm_matmul",
    )(x, g.reshape(1, k), w)


def _mem_proj_kernel(x_ref, g_ref, w_ref, o_ref, xn_ref):
    @pl.when((pl.program_id(0) == 0) & (pl.program_id(1) == 0))
    def _():
        xn_ref[...] = _rms(x_ref[...], g_ref[...]).astype(BF16)

    o_ref[...] = _dot(xn_ref[...], w_ref[...]).astype(o_ref.dtype)


def mem_kv_proj(mem, g, w, *, tn):
    m, k = mem.shape
    n_layers, _, n = w.shape
    return pl.pallas_call(
        _mem_proj_kernel,
        out_shape=jax.ShapeDtypeStruct((n_layers, m, n), BF16),
        grid=(n_layers, n // tn),
        in_specs=[
            pl.BlockSpec((m, k), lambda l, j: (0, 0)),
            pl.BlockSpec((1, k), lambda l, j: (0, 0)),
            pl.BlockSpec((None, k, tn), lambda l, j: (l, 0, j)),
        ],
        out_specs=pl.BlockSpec((None, m, tn), lambda l, j: (l, 0, j)),
        scratch_shapes=[pltpu.VMEM((m, k), BF16)],
        compiler_params=_params("arbitrary", "arbitrary"),
        name="mem_kv_proj",
    )(mem, g.reshape(1, k), w)


def _odd_proj_kernel(x_ref, g_ref, w_ref, zc_ref, zd_ref, xn_ref, *, c_blocks):
    j = pl.program_id(1)

    @pl.when(j == 0)
    def _():
        xn_ref[...] = _rms(x_ref[...], g_ref[...]).astype(BF16)

    y = _dot(xn_ref[...], w_ref[...]).astype(BF16)

    @pl.when(j < c_blocks)
    def _():
        zc_ref[...] = y

    @pl.when(j == c_blocks)
    def _():
        zd_ref[...] = y


def odd_in_proj(x, g, w, layer, *, batch, seq, tm, tn):
    m, k = x.shape
    c_blocks = 3 * C_WIDTH // tn
    assert tn == 2 * D_WIDTH and seq % tm == 0
    tiles_per_row = seq // tm
    return pl.pallas_call(
        functools.partial(_odd_proj_kernel, c_blocks=c_blocks),
        out_shape=(jax.ShapeDtypeStruct((m, 3 * C_WIDTH), BF16),
                   jax.ShapeDtypeStruct((seq, batch * tn), BF16)),
        grid=(m // tm, c_blocks + 1),
        in_specs=[
            pl.BlockSpec((tm, k), lambda i, j: (i, 0)),
            pl.BlockSpec((1, k), lambda i, j: (0, 0)),
            pl.BlockSpec((None, k, tn), lambda i, j: (layer, 0, j)),
        ],
        out_specs=(pl.BlockSpec((tm, tn), lambda i, j: (i, jnp.minimum(j, c_blocks - 1))),
                   pl.BlockSpec((tm, tn), lambda i, j: (i % tiles_per_row, i // tiles_per_row))),
        scratch_shapes=[pltpu.VMEM((tm, k), BF16)],
        compiler_params=_params("parallel", "arbitrary"),
        name="odd_in_proj",
    )(x, g.reshape(1, k), w)


def _dilated_kernel(q_ref, k_ref, v_ref, g_ref, o_ref, qs_ref, ks_ref, vs_ref, gs_ref, os_ref, num_ref, den_ref,
                    m_ref, bias_ref, *, seq, unroll):
    assert A_DILATIONS == (1, DIL_STEP, DIL_STEP * DIL_STEP)
    qi = lax.broadcasted_iota(jnp.int32, (A_BLOCK, 2 * A_BLOCK), 0)
    kj = lax.broadcasted_iota(jnp.int32, (A_BLOCK, 2 * A_BLOCK), 1)
    dist = qi + A_BLOCK - kj
    band = (dist >= 0) & (dist <= A_BLOCK)
    bias_ref[1] = jnp.where(band, 0.0, NEG).astype(BF16)
    bias_ref[0] = jnp.where(band & (kj >= A_BLOCK), 0.0, NEG).astype(BF16)

    len_s = seq // DIL_STEP
    len_w = len_s // DIL_STEP
    blocks_s = len_s // A_BLOCK
    q_scale = A_HEAD_DIM ** -0.5 * LOG2E
    for src, dst, scale in ((q_ref, qs_ref, q_scale), (k_ref, ks_ref, None), (v_ref, vs_ref, None)):
        nat = src[...].astype(F32)
        dst[0] = nat if scale is None else nat * scale
        for b in range(DIL_STEP):
            dst[1, b * len_s:(b + 1) * len_s, :] = dst[0, pl.ds(b, len_s, stride=DIL_STEP), :]
        for c in range(DIL_STEP * DIL_STEP):
            b, a = divmod(c, DIL_STEP)
            dst[2, c * len_w:(c + 1) * len_w, :] = dst[1, pl.ds(b * len_s + a, len_w, stride=DIL_STEP), :]
    gs_ref[0] = g_ref[...].astype(F32)

    head0 = lax.broadcasted_iota(jnp.int32, (A_BLOCK, LANES), 1) < A_HEAD_DIM

    def load(slab, cur, prev):
        q = qs_ref[slab, cur, :]
        q2 = jnp.concatenate([jnp.where(head0, q, 0.0), jnp.where(head0, 0.0, q)], axis=0).astype(BF16)
        if slab == 0:
            kk = jnp.concatenate([k_ref[prev, :], k_ref[cur, :]], axis=0)
            vv = jnp.concatenate([v_ref[prev, :], v_ref[cur, :]], axis=0)
            return q2, kk, vv
        kk = jnp.concatenate([ks_ref[slab, prev, :], ks_ref[slab, cur, :]], axis=0).astype(BF16)
        vv = jnp.concatenate([vs_ref[slab, prev, :], vs_ref[slab, cur, :]], axis=0).astype(BF16)
        return q2, kk, vv

    def attend(q2, kk, vv, bias):
        s = _dot_nt(q2, kk).astype(BF16) + jnp.concatenate([bias, bias], axis=0)
        m = jnp.max(s, axis=-1, keepdims=True)
        p = jnp.exp2(s - m)
        res = _dot(p, jnp.concatenate([vv, jnp.ones(vv.shape, BF16)], axis=1))
        top, bot = res[:A_BLOCK], res[A_BLOCK:]
        m_top = jnp.broadcast_to(m[:A_BLOCK], (A_BLOCK, LANES)).astype(F32)
        m_bot = jnp.broadcast_to(m[A_BLOCK:], (A_BLOCK, LANES)).astype(F32)
        return (jnp.where(head0, top[:, :LANES], bot[:, :LANES]), jnp.where(head0, top[:, LANES:], bot[:, LANES:]),
                jnp.where(head0, m_top, m_bot))

    head0_b = lax.broadcasted_iota(jnp.int32, (A_BLOCK, LANES), 1).astype(BF16) < A_HEAD_DIM
    one_b, zero_b = jnp.ones((), BF16), jnp.zeros((), BF16)

    def attend_single(cur):
        q = qs_ref[2, cur, :].astype(BF16)
        k = ks_ref[2, cur, :].astype(BF16)
        v = vs_ref[2, cur, :].astype(BF16)
        k_bd = jnp.concatenate([jnp.where(head0_b, k, zero_b), jnp.where(head0_b, zero_b, k)], axis=0)
        causal = bias_ref[0][:, A_BLOCK:]
        s = _dot_nt(q, k_bd).astype(BF16) + jnp.concatenate([causal, causal], axis=1)
        s0, s1 = s[:, :A_BLOCK], s[:, A_BLOCK:]
        m0 = jnp.max(s0, axis=-1, keepdims=True)
        m1 = jnp.max(s1, axis=-1, keepdims=True)
        p = jnp.concatenate([jnp.exp2(s0 - m0), jnp.exp2(s1 - m1)], axis=1)
        v_bd = jnp.concatenate([
            jnp.concatenate([jnp.where(head0_b, v, zero_b), jnp.where(head0_b, one_b, zero_b)], axis=1),
            jnp.concatenate([jnp.where(head0_b, zero_b, v), jnp.where(head0_b, zero_b, one_b)], axis=1)], axis=0)
        res = _dot(p, v_bd)
        m = jnp.where(head0, jnp.broadcast_to(m0, (A_BLOCK, LANES)).astype(F32),
                      jnp.broadcast_to(m1, (A_BLOCK, LANES)).astype(F32))
        return res[:, :LANES], res[:, LANES:], m

    def aligned(start):
        return pl.ds(pl.multiple_of(start, A_BLOCK), A_BLOCK)

    def banded_rows(idx, blocks_per_seq):
        start = idx * A_BLOCK
        first = idx % blocks_per_seq == 0
        return aligned(start), aligned(jnp.where(first, start, start - A_BLOCK)), bias_ref[jnp.where(first, 0, 1)]

    def merge(num_a, den_a, m_a, num_b, den_b, m_b):
        gap = m_a - m_b
        shrink = jnp.exp2(-jnp.abs(gap))
        w_a = jnp.where(gap >= 0, 1.0, shrink)
        w_b = jnp.where(gap >= 0, shrink, 1.0)
        return w_a * num_a + w_b * num_b, w_a * den_a + w_b * den_b, jnp.maximum(m_a, m_b)

    def narrow_body(idx, carry):
        cur, prev, bias = banded_rows(idx, n_blocks)
        num, den, m = attend(*load(0, cur, prev), bias)
        num_ref[0, cur, :] = num
        den_ref[0, cur, :] = den
        m_ref[0, cur, :] = m
        return carry

    def step_body(idx, carry):
        cur, prev, bias = banded_rows(idx, blocks_s)
        natural = pl.ds(idx // blocks_s + DIL_STEP * A_BLOCK * (idx % blocks_s), A_BLOCK, stride=DIL_STEP)
        num, den, m = merge(*attend(*load(1, cur, prev), bias),
                            num_ref[0, natural, :], den_ref[0, natural, :], m_ref[0, natural, :])
        num_ref[1, cur, :] = num
        den_ref[1, cur, :] = den
        m_ref[1, cur, :] = m
        gs_ref[1, cur, :] = gs_ref[0, natural, :]
        return carry

    def wide_body(idx, carry):
        cur = aligned(idx * A_BLOCK)
        by_step = pl.ds((idx // DIL_STEP) * len_s + idx % DIL_STEP, A_BLOCK, stride=DIL_STEP)
        natural = pl.ds(DIL_STEP * (idx % DIL_STEP) + idx // DIL_STEP, A_BLOCK, stride=DIL_STEP * DIL_STEP)
        num, den, _ = merge(*attend_single(cur),
                            num_ref[1, by_step, :], den_ref[1, by_step, :], m_ref[1, by_step, :])
        os_ref[natural, :] = num / den * _silu(gs_ref[1, by_step, :])
        return carry

    assert len_w == A_BLOCK
    n_blocks = seq // A_BLOCK
    for body in (narrow_body, step_body, wide_body):
        lax.fori_loop(0, n_blocks, body, 0, unroll=unroll)
    o_ref[...] = os_ref[...].astype(o_ref.dtype)


def dilated_attention(z, *, batch, seq, unroll=16):
    n_pair = A_WIDTH // LANES
    n_pat = len(A_DILATIONS)
    blk = (None, seq, LANES)
    slabs = pltpu.VMEM((n_pat, seq, LANES), F32)
    partial = pltpu.VMEM((2, seq, LANES), F32)
    return pl.pallas_call(
        functools.partial(_dilated_kernel, seq=seq, unroll=unroll),
        out_shape=jax.ShapeDtypeStruct((batch, seq, A_WIDTH), BF16),
        grid=(batch, n_pair),
        in_specs=[
            pl.BlockSpec(blk, lambda b, h: (b, 0, h)),
            pl.BlockSpec(blk, lambda b, h: (b, 0, n_pair + h)),
            pl.BlockSpec(blk, lambda b, h: (b, 0, 2 * n_pair + h)),
            pl.BlockSpec(blk, lambda b, h: (b, 0, 3 * n_pair + h)),
        ],
        out_specs=pl.BlockSpec(blk, lambda b, h: (b, 0, h)),
        scratch_shapes=[
            slabs, slabs, slabs,
            pltpu.VMEM((2, seq, LANES), F32),
            pltpu.VMEM((seq, LANES), F32),
            partial, partial, partial,
            pltpu.VMEM((2, A_BLOCK, 2 * A_BLOCK), BF16),
        ],
        compiler_params=_params("parallel", "parallel"),
        name="dilated_attention",
    )(z, z, z, z)


def _even_tail_kernel(vb_ref, halo_ref, gb_ref, a_ref, x_ref, pw_ref, ps_ref, wo_ref, o_ref, *, tm, sub):
    ti = pl.program_id(1)
    for r in range(tm // sub):
        rows = slice(r * sub, (r + 1) * sub)
        v = vb_ref[rows, :].astype(F32)
        if r == 0:
            halo = jnp.where(ti > 0, halo_ref[...].astype(F32), 0.0)
        else:
            halo = vb_ref[r * sub - POOL_HALO:r * sub, :].astype(F32)
        xc = jnp.concatenate([halo, v], axis=0)
        s2 = xc + pltpu.roll(xc, 1, 0)
        t4 = s2[:, B_GROUP:]
        s4 = t4 + pltpu.roll(t4, 2, 0)
        t8 = s4[:, B_GROUP:]
        s8 = t8 + pltpu.roll(t8, 4, 0)
        t16 = s8[:, B_GROUP:]
        s16 = t16 + pltpu.roll(t16, 8, 0)
        sums = (s2[:, :B_GROUP], s4[:, :B_GROUP], s8[:, :B_GROUP], s16)
        pos = (ti * tm + r * sub + 1 + lax.broadcasted_iota(jnp.int32, (sub, 1), 0)).astype(F32)
        mixed = []
        for g, w in enumerate(POOL_WINDOWS):
            mean = sums[g][POOL_HALO:, :] / jnp.minimum(pos, float(w))
            pooled = mean - v[:, g * B_GROUP:(g + 1) * B_GROUP]
            mixed.append(_dot(pooled.astype(BF16), pw_ref[g]))
        b_out = jnp.concatenate(mixed, axis=1) * ps_ref[...] * _silu(gb_ref[rows, :].astype(F32))
        y = _dot(a_ref[rows, :], wo_ref[:A_WIDTH, :]) + _dot(b_out.astype(BF16), wo_ref[A_WIDTH:, :])
        o_ref[rows, :] = x_ref[rows, :] + y


def even_tail(z, a_out, x, pool_w, pool_scale, w_out, layer, *, batch, seq, tm, sub):
    vb_col = 4 * A_WIDTH // B_WIDTH
    halo_per_tile = tm // POOL_HALO
    return pl.pallas_call(
        functools.partial(_even_tail_kernel, tm=tm, sub=sub),
        out_shape=jax.ShapeDtypeStruct((batch, seq, D_MODEL), F32),
        grid=(batch, seq // tm),
        in_specs=[
            pl.BlockSpec((None, tm, B_WIDTH), lambda b, t: (b, t, vb_col)),
            pl.BlockSpec((None, POOL_HALO, B_WIDTH),
                         lambda b, t: (b, jnp.maximum(t * halo_per_tile - 1, 0), vb_col)),
            pl.BlockSpec((None, tm, B_WIDTH), lambda b, t: (b, t, vb_col + 1)),
            pl.BlockSpec((None, tm, A_WIDTH), lambda b, t: (b, t, 0)),
            pl.BlockSpec((None, tm, D_MODEL), lambda b, t: (b, t, 0)),
            _layer_block((len(POOL_WINDOWS), B_GROUP, B_GROUP), layer),
            pl.BlockSpec((1, B_WIDTH), lambda b, t: (0, 0)),
            _layer_block((A_WIDTH + B_WIDTH, D_MODEL), layer),
        ],
        out_specs=pl.BlockSpec((None, tm, D_MODEL), lambda b, t: (b, t, 0)),
        compiler_params=_params("parallel", "parallel"),
        name="even_tail",
    )(z, z, z, a_out, x, pool_w, pool_scale.reshape(1, B_WIDTH), w_out)


def _s5_prep_kernel(ar_ref, ai_ref, ldt_ref, br_ref, bi_ref, abr_ref, abi_ref, bbr_ref, bbi_ref):
    ar, ai = ar_ref[...], ai_ref[...]
    dt = jnp.exp(ldt_ref[...])
    mag = jnp.exp(dt * ar)
    abar_re = mag * jnp.cos(dt * ai)
    abar_im = mag * jnp.sin(dt * ai)
    nr, ni = abar_re - 1.0, abar_im
    inv = 1.0 / (ar * ar + ai * ai)
    coef_re = (nr * ar + ni * ai) * inv
    coef_im = (ni * ar - nr * ai) * inv
    br, bi = br_ref[...], bi_ref[...]
    abr_ref[...] = abar_re
    abi_ref[...] = abar_im
    bbr_ref[...] = coef_re * br - coef_im * bi
    bbi_ref[...] = coef_re * bi + coef_im * br


def _part_eye():
    return jnp.eye(S5_GROUPS // S5_PARTS, dtype=F32)


def s5_prep(a_re, a_im, log_dt, b_re, b_im):
    rep = lambda t: jnp.repeat(t, S5_GROUP_DIM, axis=0)
    to_rows = lambda t: t.transpose(0, 2, 1).reshape(D_WIDTH, S5_STATE)
    shp = jax.ShapeDtypeStruct((D_WIDTH, S5_STATE), F32)
    abr, abi, bbr, bbi = pl.pallas_call(
        _s5_prep_kernel, out_shape=(shp, shp, shp, shp), name="s5_prep",
    )(rep(a_re), rep(a_im), rep(jnp.broadcast_to(log_dt[:, None], (S5_GROUPS, S5_STATE))),
      to_rows(b_re), to_rows(b_im))
    gpp = S5_GROUPS // S5_PARTS

    def block_diag_in(t):
        t = t.reshape(S5_PARTS, gpp, S5_GROUP_DIM, 1, S5_STATE) * _part_eye()[None, :, None, :, None]
        return t.reshape(S5_PARTS, S5_PART_IN, S5_PART_STATE)

    w_b = jnp.concatenate([block_diag_in(bbr), block_diag_in(bbi)], axis=2).astype(BF16)
    abar_re = abr[::S5_GROUP_DIM].reshape(1, S5_LANES)
    abar_im = abi[::S5_GROUP_DIM].reshape(1, S5_LANES)
    return w_b, abar_re, abar_im


def s5_out_weights(c_re, c_im):
    gpp = S5_GROUPS // S5_PARTS

    def block_diag_out(t):
        t = t.reshape(S5_PARTS, gpp, S5_GROUP_DIM, S5_STATE).transpose(0, 1, 3, 2)
        t = t[:, :, :, None, :] * _part_eye()[None, :, None, :, None]
        return t.reshape(S5_PARTS, S5_PART_STATE, S5_PART_IN).astype(BF16)

    return block_diag_out(c_re), block_diag_out(c_im)


def _s5_kernel(xd_ref, wb_ref, are_ref, aim_ref, wcr_ref, wci_ref, dsk_ref, w1_ref, w2_ref,
               o_ref, u_ref, y_ref, bu_ref, h_ref, *, lc, nb, lane_chunk):
    @pl.when(pl.program_id(0) == 0)
    def _():
        h_ref[...] = jnp.zeros_like(h_ref)

    n_slab = D_WIDTH // LANES
    for b in range(nb):
        for s in range(n_slab):
            col = b * 2 * D_WIDTH + s * LANES
            u_ref[s, pl.ds(b, lc, stride=nb), :] = xd_ref[:, col:col + LANES].astype(F32)
    u = jnp.concatenate([u_ref[s] for s in range(n_slab)], axis=1)
    ub = u.astype(BF16)
    width = 2 * S5_PART_STATE
    def project(part):
        bu_ref[:, part * width:(part + 1) * width] = _dot(
            ub[:, part * S5_PART_IN:(part + 1) * S5_PART_IN], wb_ref[part])

    def scan(part):
        for c in range(S5_PART_STATE // lane_chunk):
            re = slice(part * width + c * lane_chunk, part * width + (c + 1) * lane_chunk)
            im = slice(re.start + S5_PART_STATE, re.stop + S5_PART_STATE)
            ab = slice(part * S5_PART_STATE + c * lane_chunk, part * S5_PART_STATE + (c + 1) * lane_chunk)
            a_r = jnp.broadcast_to(are_ref[:, ab], (nb, lane_chunk))
            a_i = jnp.broadcast_to(aim_ref[:, ab], (nb, lane_chunk))
            h_r, h_i = h_ref[:, re], h_ref[:, im]
            for t in range(lc):
                row = slice(t * nb, (t + 1) * nb)
                h_r, h_i = (a_r * h_r - a_i * h_i + bu_ref[row, re], a_r * h_i + a_i * h_r + bu_ref[row, im])
                bu_ref[row, re] = h_r
                bu_ref[row, im] = h_i
            h_ref[:, re] = h_r
            h_ref[:, im] = h_i

    def readout(part):
        re = slice(part * width, part * width + S5_PART_STATE)
        im = slice(re.stop, re.stop + S5_PART_STATE)
        return _dot(bu_ref[:, re].astype(BF16), wcr_ref[part]) - _dot(bu_ref[:, im].astype(BF16), wci_ref[part])

    ys = []
    project(0)
    for part in range(S5_PARTS):
        if part + 1 < S5_PARTS:
            project(part + 1)
        scan(part)
        ys.append(readout(part))
    y = jnp.concatenate(ys, axis=1) + dsk_ref[...] * u
    y = jax.nn.gelu(y).astype(BF16)
    out = _dot(y, w1_ref[...]) * jax.nn.sigmoid(_dot(y, w2_ref[...]))
    for s in range(n_slab):
        y_ref[s] = out[:, s * LANES:(s + 1) * LANES]
    o_ref[...] = jnp.concatenate(
        [y_ref[s, pl.ds(b, lc, stride=nb), :] for b in range(nb) for s in range(n_slab)], axis=1).astype(o_ref.dtype)


def s5_layer(zd, w_b, abar_re, abar_im, w_cr, w_ci, d_skip, w1, w2, layer, *, batch, seq, lc):
    rows = lc * batch
    n_slab = D_WIDTH // LANES
    full2 = lambda shape: pl.BlockSpec(shape, lambda c: (0, 0))
    full3 = lambda shape: pl.BlockSpec(shape, lambda c: (0, 0, 0))
    return pl.pallas_call(
        functools.partial(_s5_kernel, lc=lc, nb=batch, lane_chunk=512),
        out_shape=jax.ShapeDtypeStruct((seq, batch * D_WIDTH), BF16),
        grid=(seq // lc,),
        in_specs=[
            pl.BlockSpec((lc, batch * 2 * D_WIDTH), lambda c: (c, 0)),
            full3((S5_PARTS, S5_PART_IN, 2 * S5_PART_STATE)),
            full2((1, S5_LANES)),
            full2((1, S5_LANES)),
            full3((S5_PARTS, S5_PART_STATE, S5_PART_IN)),
            full3((S5_PARTS, S5_PART_STATE, S5_PART_IN)),
            full2((1, D_WIDTH)),
            _layer_block((D_WIDTH, D_WIDTH), layer),
            _layer_block((D_WIDTH, D_WIDTH), layer),
        ],
        out_specs=pl.BlockSpec((lc, batch * D_WIDTH), lambda c: (c, 0)),
        scratch_shapes=[
            pltpu.VMEM((n_slab, rows, LANES), F32),
            pltpu.VMEM((n_slab, rows, LANES), F32),
            pltpu.VMEM((rows, 2 * S5_LANES), F32),
            pltpu.VMEM((batch, 2 * S5_LANES), F32),
        ],
        compiler_params=_params("arbitrary"),
        name="s5_layer",
    )(zd, w_b, abar_re, abar_im, w_cr, w_ci, d_skip.reshape(1, D_WIDTH), w1, w2)


def _odd_tail_kernel(u_ref, v_ref, gc_ref, d_ref, gd_ref, x_ref, lng_ref, lnb_ref, ws_ref, bs_ref, wo_ref,
                     o_ref, c_ref, *, tm, sub):
    ri = lax.broadcasted_iota(jnp.int32, (C_CHUNK, C_CHUNK), 0)
    ci = lax.broadcasted_iota(jnp.int32, (C_CHUNK, C_CHUNK), 1)
    causal = ri >= ci
    w_s = [jnp.where(causal, ws_ref[g], 0.0).astype(BF16) for g in range(C_GROUPS)]
    for r in range(tm // sub):
        tile = slice(r * sub, (r + 1) * sub)
        v = v_ref[tile, :].astype(F32)
        mu = jnp.mean(v, axis=-1, keepdims=True)
        vc = v - mu
        var = jnp.mean(vc * vc, axis=-1, keepdims=True)
        vn = (vc * lax.rsqrt(var + EPS) * lng_ref[...] + lnb_ref[...]).astype(BF16)
        for g in range(C_GROUPS):
            cols = slice(g * C_GROUP_DIM, (g + 1) * C_GROUP_DIM)
            b_g = bs_ref[:, g:g + 1]
            for c in range(sub // C_CHUNK):
                rows = slice(r * sub + c * C_CHUNK, r * sub + (c + 1) * C_CHUNK)
                mixed = _dot(w_s[g], vn[c * C_CHUNK:(c + 1) * C_CHUNK, cols]) + b_g
                gate = _silu(gc_ref[rows, cols].astype(F32))
                c_ref[rows, cols] = (u_ref[rows, cols].astype(F32) * mixed * gate).astype(BF16)
        d = (d_ref[tile, :].astype(F32) * _silu(gd_ref[tile, :].astype(F32))).astype(BF16)
        y = _dot(c_ref[tile, :], wo_ref[:C_WIDTH, :]) + _dot(d, wo_ref[C_WIDTH:, :])
        o_ref[tile, :] = x_ref[tile, :] + y


def odd_tail(zc, zd, d_out, x, ln_g, ln_b, w_s, b_s, w_out, layer, *, batch, seq, tm, sub):
    return pl.pallas_call(
        functools.partial(_odd_tail_kernel, tm=tm, sub=sub),
        out_shape=jax.ShapeDtypeStruct((batch, seq, D_MODEL), F32),
        grid=(batch, seq // tm),
        in_specs=[
            pl.BlockSpec((None, tm, C_WIDTH), lambda b, t: (b, t, 0)),
            pl.BlockSpec((None, tm, C_WIDTH), lambda b, t: (b, t, 1)),
            pl.BlockSpec((None, tm, C_WIDTH), lambda b, t: (b, t, 2)),
            pl.BlockSpec((tm, D_WIDTH), lambda b, t: (t, b)),
            pl.BlockSpec((tm, D_WIDTH), lambda b, t: (t, 2 * b + 1)),
            pl.BlockSpec((None, tm, D_MODEL), lambda b, t: (b, t, 0)),
            pl.BlockSpec((1, C_WIDTH), lambda b, t: (0, 0)),
            pl.BlockSpec((1, C_WIDTH), lambda b, t: (0, 0)),
            _layer_block((C_GROUPS, C_CHUNK, C_CHUNK), layer),
            pl.BlockSpec((C_CHUNK, C_GROUPS), lambda b, t: (0, 0)),
            _layer_block((C_WIDTH + D_WIDTH, D_MODEL), layer),
        ],
        out_specs=pl.BlockSpec((None, tm, D_MODEL), lambda b, t: (b, t, 0)),
        scratch_shapes=[pltpu.VMEM((tm, C_WIDTH), BF16)],
        compiler_params=_params("parallel", "parallel"),
        name="odd_tail",
    )(zc, zc, zc, d_out, zd, x, ln_g.reshape(1, C_WIDTH), ln_b.reshape(1, C_WIDTH), w_s, b_s.T, w_out)


def _xattn_kernel(x_ref, g_ref, wq_ref, kv_ref, wo_ref, fg_ref, o_ref, *, final, sub):
    for r in range(x_ref.shape[0] // sub):
        rows = slice(r * sub, (r + 1) * sub)
        x = x_ref[rows, :]
        q = _dot(_rms(x, g_ref[...]).astype(BF16), wq_ref[...])
        heads = []
        for h in range(X_HEADS):
            ks = slice(h * X_HEAD_DIM, (h + 1) * X_HEAD_DIM)
            vs = slice(D_MODEL + h * X_HEAD_DIM, D_MODEL + (h + 1) * X_HEAD_DIM)
            s = _dot_nt(q[:, ks].astype(BF16), kv_ref[:, ks]) * (X_HEAD_DIM ** -0.5)
            m = jnp.max(s, axis=-1, keepdims=True)
            p = jnp.exp(s - m)
            den = jnp.sum(p, axis=-1, keepdims=True)
            heads.append((_dot(p.astype(BF16), kv_ref[:, vs]) / den).astype(BF16))
        y = x + _dot(jnp.concatenate(heads, axis=1), wo_ref[...])
        if final:
            y = _rms(y, fg_ref[...])
        o_ref[rows, :] = y


def cross_attention(x, g, w_q, kv, w_o, final_g, layer, *, batch, seq, mem_len, tq, sub, final):
    return pl.pallas_call(
        functools.partial(_xattn_kernel, final=final, sub=sub),
        out_shape=jax.ShapeDtypeStruct((batch, seq, D_MODEL), F32),
        grid=(batch, seq // tq),
        in_specs=[
            pl.BlockSpec((None, tq, D_MODEL), lambda b, t: (b, t, 0)),
            pl.BlockSpec((1, D_MODEL), lambda b, t: (0, 0)),
            _layer_block((D_MODEL, D_MODEL), layer),
            pl.BlockSpec((None, None, mem_len, 2 * D_MODEL), lambda b, t: (layer, b, 0, 0)),
            _layer_block((D_MODEL, D_MODEL), layer),
            pl.BlockSpec((1, D_MODEL), lambda b, t: (0, 0)),
        ],
        out_specs=pl.BlockSpec((None, tq, D_MODEL), lambda b, t: (b, t, 0)),
        compiler_params=_params("parallel", "parallel"),
        name="cross_attention",
    )(x, g.reshape(1, D_MODEL), w_q, kv, w_o, final_g.reshape(1, D_MODEL))


def kernel(x, mem, norm_ab, w_in_ab, pool_w, pool_scale, w_out_ab, norm_cd, w_in_cd, sgu_ln_g, sgu_ln_b,
           sgu_w, sgu_b, s5_a_re, s5_a_im, s5_log_dt, s5_b_re, s5_b_im, s5_c_re, s5_c_im, s5_d,
           glu_w1, glu_w2, w_out_cd, norm_x, w_xq, w_xkv, w_xo, mem_norm, final_norm):
    batch, seq, _ = x.shape
    mem_len = mem.shape[1]
    depth = norm_x.shape[0]
    tokens = batch * seq
    tm_proj = 1024
    tiles_per_row = seq // tm_proj
    bf = lambda t: t.astype(BF16)

    w_in_ab, pool_w, w_out_ab, w_in_cd, glu_w1, glu_w2, w_out_cd, w_xq, w_xkv, w_xo = (
        bf(t) for t in (w_in_ab, pool_w, w_out_ab, w_in_cd, glu_w1, glu_w2, w_out_cd, w_xq, w_xkv, w_xo))
    kv = mem_kv_proj(mem.reshape(batch * mem_len, D_MODEL), mem_norm, w_xkv, tn=1024)
    kv = kv.reshape(depth, batch, mem_len, 2 * D_MODEL)
    for layer in range(depth):
        i = layer // 2
        x2 = x.reshape(tokens, D_MODEL)
        if layer % 2 == 0:
            z = norm_matmul(x2, norm_ab[i], w_in_ab, i, tm=tm_proj, tn=2048)
            z = z.reshape(batch, seq, z.shape[1])
            a_out = dilated_attention(z, batch=batch, seq=seq)
            x = even_tail(z, a_out, x, pool_w, pool_scale[i], w_out_ab, i, batch=batch, seq=seq,
                          tm=1024, sub=512)
        else:
            zc, zd = odd_in_proj(x2, norm_cd[i], w_in_cd, i, batch=batch, seq=seq, tm=tm_proj, tn=2 * D_WIDTH)
            w_b, abar_re, abar_im = s5_prep(s5_a_re[i], s5_a_im[i], s5_log_dt[i], s5_b_re[i], s5_b_im[i])
            w_cr, w_ci = s5_out_weights(s5_c_re[i], s5_c_im[i])
            d_out = s5_layer(zd, w_b, abar_re, abar_im, w_cr, w_ci, s5_d[i], glu_w1, glu_w2, i,
                             batch=batch, seq=seq, lc=64)
            x = odd_tail(zc.reshape(batch, seq, zc.shape[1]), zd, d_out, x, sgu_ln_g[i], sgu_ln_b[i], sgu_w,
                         sgu_b[i], w_out_cd, i, batch=batch, seq=seq, tm=1024, sub=512)
        x = cross_attention(x, norm_x[layer], w_xq, kv, w_xo, final_norm,
                            layer, batch=batch, seq=seq, mem_len=mem_len, tq=1024, sub=512,
                            final=(layer == depth - 1))
    return x
```

```python
import functools

import jax
import jax.numpy as jnp
from jax import lax
from jax.experimental import pallas as pl
from jax.experimental.pallas import tpu as pltpu

F32 = jnp.float32
BF16 = jnp.bfloat16

LANES = 128
D_MODEL = 1024
A_WIDTH = 1024
A_HEAD_DIM = 64
A_BLOCK = 128
A_DILATIONS = (1, 4, 16)
DIL_STEP = 4
B_WIDTH = 1024
POOL_WINDOWS = (2, 4, 8, 16)
B_GROUP = 256
POOL_HALO = 16
C_WIDTH = 1024
C_CHUNK = 128
C_GROUPS = 4
C_GROUP_DIM = 256
D_WIDTH = 512
S5_GROUPS = 32
S5_GROUP_DIM = 16
S5_STATE = 64
S5_LANES = S5_GROUPS * S5_STATE
S5_PARTS = 2
S5_PART_IN = D_WIDTH // S5_PARTS
S5_PART_STATE = S5_LANES // S5_PARTS
X_HEADS = 4
X_HEAD_DIM = 256
EPS = 1e-6
NEG = -1e30
LOG2E = 1.4426950408889634

VMEM_LIMIT = 48 * 1024 * 1024


def _params(*sem):
    return pltpu.CompilerParams(dimension_semantics=sem, vmem_limit_bytes=VMEM_LIMIT)


def _rms(x, g):
    ms = jnp.mean(x * x, axis=-1, keepdims=True)
    return x * lax.rsqrt(ms + EPS) * g


def _silu(x):
    return x * jax.nn.sigmoid(x)


def _dot(a, b):
    return jnp.dot(a, b, preferred_element_type=F32)


def _layer_block(shape, layer):
    return pl.BlockSpec((None,) + tuple(shape), lambda *_: (layer,) + (0,) * len(shape),
                        pipeline_mode=pl.Buffered(1))


def _cast_weights_once(pairs, grid_rank):
    first = pl.program_id(0) == 0
    for axis in range(1, grid_rank):
        first = first & (pl.program_id(axis) == 0)

    @pl.when(first)
    def _():
        for src, dst in pairs:
            dst[...] = src[...].astype(dst.dtype)


def _dot_nt(a, b):
    return lax.dot_general(a, b, (((1,), (1,)), ((), ())), preferred_element_type=F32)


def _norm_matmul_kernel(x_ref, g_ref, w_ref, o_ref, xn_ref):
    @pl.when(pl.program_id(1) == 0)
    def _():
        xn_ref[...] = _rms(x_ref[...], g_ref[...]).astype(BF16)

    o_ref[...] = _dot(xn_ref[...], w_ref[...]).astype(o_ref.dtype)


def norm_matmul(x, g, w, layer, *, tm, tn):
    m, k = x.shape
    n = w.shape[2]
    return pl.pallas_call(
        _norm_matmul_kernel,
        out_shape=jax.ShapeDtypeStruct((m, n), BF16),
        grid=(m // tm, n // tn),
        in_specs=[
            pl.BlockSpec((tm, k), lambda i, j: (i, 0)),
            pl.BlockSpec((1, k), lambda i, j: (0, 0)),
            pl.BlockSpec((None, k, tn), lambda i, j: (layer, 0, j)),
        ],
        out_specs=pl.BlockSpec((tm, tn), lambda i, j: (i, j)),
        scratch_shapes=[pltpu.VMEM((tm, k), BF16)],
        compiler_params=_params("parallel", "arbitrary"),
        name="norm_matmul",
    )(x, g.reshape(1, k), w)


def _mem_proj_kernel(x_ref, g_ref, w_ref, o_ref, xn_ref):
    @pl.when((pl.program_id(0) == 0) & (pl.program_id(1) == 0))
    def _():
        xn_ref[...] = _rms(x_ref[...], g_ref[...]).astype(BF16)

    o_ref[...] = _dot(xn_ref[...], w_ref[...].astype(BF16)).astype(o_ref.dtype)


def mem_kv_proj(mem, g, w, *, tn):
    m, k = mem.shape
    n_layers, _, n = w.shape
    return pl.pallas_call(
        _mem_proj_kernel,
        out_shape=jax.ShapeDtypeStruct((n_layers, m, n), BF16),
        grid=(n_layers, n // tn),
        in_specs=[
            pl.BlockSpec((m, k), lambda l, j: (0, 0)),
            pl.BlockSpec((1, k), lambda l, j: (0, 0)),
            pl.BlockSpec((None, k, tn), lambda l, j: (l, 0, j)),
        ],
        out_specs=pl.BlockSpec((None, m, tn), lambda l, j: (l, 0, j)),
        scratch_shapes=[pltpu.VMEM((m, k), BF16)],
        compiler_params=_params("arbitrary", "arbitrary"),
        name="mem_kv_proj",
    )(mem, g.reshape(1, k), w)


def _odd_proj_kernel(x_ref, g_ref, w_ref, zc_ref, zd_ref, xn_ref, *, c_blocks):
    j = pl.program_id(1)

    @pl.when(j == 0)
    def _():
        xn_ref[...] = _rms(x_ref[...], g_ref[...]).astype(BF16)

    @pl.when(j < c_blocks)
    def _():
        zc_ref[...] = _dot(xn_ref[...], w_ref[...]).astype(BF16)

    @pl.when(j == c_blocks)
    def _():
        zd_ref[...] = _dot(xn_ref[...], w_ref[...]).astype(BF16)


def odd_in_proj(x, g, w, layer, *, batch, seq, tm, tn):
    m, k = x.shape
    c_blocks = 3 * C_WIDTH // tn
    assert tn == 2 * D_WIDTH and seq % tm == 0
    tiles_per_row = seq // tm
    return pl.pallas_call(
        functools.partial(_odd_proj_kernel, c_blocks=c_blocks),
        out_shape=(jax.ShapeDtypeStruct((m, 3 * C_WIDTH), BF16),
                   jax.ShapeDtypeStruct((seq, batch * tn), BF16)),
        grid=(m // tm, c_blocks + 1),
        in_specs=[
            pl.BlockSpec((tm, k), lambda i, j: (i, 0)),
            pl.BlockSpec((1, k), lambda i, j: (0, 0)),
            pl.BlockSpec((None, k, tn), lambda i, j: (layer, 0, j)),
        ],
        out_specs=(pl.BlockSpec((tm, tn), lambda i, j: (i, jnp.minimum(j, c_blocks - 1))),
                   pl.BlockSpec((tm, tn), lambda i, j: (i % tiles_per_row, i // tiles_per_row))),
        scratch_shapes=[pltpu.VMEM((tm, k), BF16)],
        compiler_params=_params("parallel", "arbitrary"),
        name="odd_in_proj",
    )(x, g.reshape(1, k), w)


def _dilated_kernel(q_ref, k_ref, v_ref, g_ref, o_ref, qs_ref, ks_ref, vs_ref, gs_ref, os_ref, num_ref, den_ref,
                    m_ref, bias_ref, *, seq, unroll):
    assert A_DILATIONS == (1, DIL_STEP, DIL_STEP * DIL_STEP)
    qi = lax.broadcasted_iota(jnp.int32, (A_BLOCK, 2 * A_BLOCK), 0)
    kj = lax.broadcasted_iota(jnp.int32, (A_BLOCK, 2 * A_BLOCK), 1)
    dist = qi + A_BLOCK - kj
    band = (dist >= 0) & (dist <= A_BLOCK)
    bias_ref[1] = jnp.where(band, 0.0, NEG).astype(BF16)
    bias_ref[0] = jnp.where(band & (kj >= A_BLOCK), 0.0, NEG).astype(BF16)

    len_s = seq // DIL_STEP
    len_w = len_s // DIL_STEP
    blocks_s = len_s // A_BLOCK
    q_scale = A_HEAD_DIM ** -0.5 * LOG2E
    for src, dst, scale in ((q_ref, qs_ref, q_scale), (k_ref, ks_ref, None), (v_ref, vs_ref, None)):
        nat = src[...].astype(F32)
        dst[0] = nat if scale is None else nat * scale
        for b in range(DIL_STEP):
            dst[1, b * len_s:(b + 1) * len_s, :] = dst[0, pl.ds(b, len_s, stride=DIL_STEP), :]
        for c in range(DIL_STEP * DIL_STEP):
            b, a = divmod(c, DIL_STEP)
            dst[2, c * len_w:(c + 1) * len_w, :] = dst[1, pl.ds(b * len_s + a, len_w, stride=DIL_STEP), :]
    gs_ref[0] = g_ref[...].astype(F32)

    head0 = lax.broadcasted_iota(jnp.int32, (A_BLOCK, LANES), 1) < A_HEAD_DIM

    def load(slab, cur, prev):
        q = qs_ref[slab, cur, :]
        q2 = jnp.concatenate([jnp.where(head0, q, 0.0), jnp.where(head0, 0.0, q)], axis=0).astype(BF16)
        if slab == 0:
            kk = jnp.concatenate([k_ref[prev, :], k_ref[cur, :]], axis=0)
            vv = jnp.concatenate([v_ref[prev, :], v_ref[cur, :]], axis=0)
            return q2, kk, vv
        kk = jnp.concatenate([ks_ref[slab, prev, :], ks_ref[slab, cur, :]], axis=0).astype(BF16)
        vv = jnp.concatenate([vs_ref[slab, prev, :], vs_ref[slab, cur, :]], axis=0).astype(BF16)
        return q2, kk, vv

    def attend(q2, kk, vv, bias):
        s = _dot_nt(q2, kk).astype(BF16) + jnp.concatenate([bias, bias], axis=0)
        m = jnp.max(s, axis=-1, keepdims=True)
        p = jnp.exp2(s - m)
        res = _dot(p, jnp.concatenate([vv, jnp.ones(vv.shape, BF16)], axis=1))
        top, bot = res[:A_BLOCK], res[A_BLOCK:]
        m_top = jnp.broadcast_to(m[:A_BLOCK], (A_BLOCK, LANES)).astype(F32)
        m_bot = jnp.broadcast_to(m[A_BLOCK:], (A_BLOCK, LANES)).astype(F32)
        return (jnp.where(head0, top[:, :LANES], bot[:, :LANES]), jnp.where(head0, top[:, LANES:], bot[:, LANES:]),
                jnp.where(head0, m_top, m_bot))

    head0_b = lax.broadcasted_iota(jnp.int32, (A_BLOCK, LANES), 1).astype(BF16) < A_HEAD_DIM
    one_b, zero_b = jnp.ones((), BF16), jnp.zeros((), BF16)

    def attend_single(cur):
        q = qs_ref[2, cur, :].astype(BF16)
        k = ks_ref[2, cur, :].astype(BF16)
        v = vs_ref[2, cur, :].astype(BF16)
        k_bd = jnp.concatenate([jnp.where(head0_b, k, zero_b), jnp.where(head0_b, zero_b, k)], axis=0)
        causal = bias_ref[0][:, A_BLOCK:]
        s = _dot_nt(q, k_bd).astype(BF16) + jnp.concatenate([causal, causal], axis=1)
        s0, s1 = s[:, :A_BLOCK], s[:, A_BLOCK:]
        m0 = jnp.max(s0, axis=-1, keepdims=True)
        m1 = jnp.max(s1, axis=-1, keepdims=True)
        p = jnp.concatenate([jnp.exp2(s0 - m0), jnp.exp2(s1 - m1)], axis=1)
        v_bd = jnp.concatenate([
            jnp.concatenate([jnp.where(head0_b, v, zero_b), jnp.where(head0_b, one_b, zero_b)], axis=1),
            jnp.concatenate([jnp.where(head0_b, zero_b, v), jnp.where(head0_b, zero_b, one_b)], axis=1)], axis=0)
        res = _dot(p, v_bd)
        m = jnp.where(head0, jnp.broadcast_to(m0, (A_BLOCK, LANES)).astype(F32),
                      jnp.broadcast_to(m1, (A_BLOCK, LANES)).astype(F32))
        return res[:, :LANES], res[:, LANES:], m

    def aligned(start):
        return pl.ds(pl.multiple_of(start, A_BLOCK), A_BLOCK)

    def banded_rows(idx, blocks_per_seq):
        start = idx * A_BLOCK
        first = idx % blocks_per_seq == 0
        return aligned(start), aligned(jnp.where(first, start, start - A_BLOCK)), bias_ref[jnp.where(first, 0, 1)]

    def merge(num_a, den_a, m_a, num_b, den_b, m_b):
        m = jnp.maximum(m_a, m_b)
        w_a = jnp.exp2(m_a - m)
        w_b = jnp.exp2(m_b - m)
        return w_a * num_a + w_b * num_b, w_a * den_a + w_b * den_b, m

    def narrow_body(idx, carry):
        cur, prev, bias = banded_rows(idx, n_blocks)
        num, den, m = attend(*load(0, cur, prev), bias)
        num_ref[0, cur, :] = num
        den_ref[0, cur, :] = den
        m_ref[0, cur, :] = m
        return carry

    def step_body(idx, carry):
        cur, prev, bias = banded_rows(idx, blocks_s)
        natural = pl.ds(idx // blocks_s + DIL_STEP * A_BLOCK * (idx % blocks_s), A_BLOCK, stride=DIL_STEP)
        num, den, m = merge(*attend(*load(1, cur, prev), bias),
                            num_ref[0, natural, :], den_ref[0, natural, :], m_ref[0, natural, :])
        num_ref[1, cur, :] = num
        den_ref[1, cur, :] = den
        m_ref[1, cur, :] = m
        gs_ref[1, cur, :] = gs_ref[0, natural, :]
        return carry

    def wide_body(idx, carry):
        cur = aligned(idx * A_BLOCK)
        by_step = pl.ds((idx // DIL_STEP) * len_s + idx % DIL_STEP, A_BLOCK, stride=DIL_STEP)
        natural = pl.ds(DIL_STEP * (idx % DIL_STEP) + idx // DIL_STEP, A_BLOCK, stride=DIL_STEP * DIL_STEP)
        num, den, _ = merge(*attend_single(cur),
                            num_ref[1, by_step, :], den_ref[1, by_step, :], m_ref[1, by_step, :])
        os_ref[natural, :] = num / den * _silu(gs_ref[1, by_step, :])
        return carry

    assert len_w == A_BLOCK
    n_blocks = seq // A_BLOCK
    for body in (narrow_body, step_body, wide_body):
        lax.fori_loop(0, n_blocks, body, 0, unroll=unroll)
    o_ref[...] = os_ref[...].astype(o_ref.dtype)


def dilated_attention(z, *, batch, seq, unroll=16):
    n_pair = A_WIDTH // LANES
    n_pat = len(A_DILATIONS)
    blk = (None, seq, LANES)
    slabs = pltpu.VMEM((n_pat, seq, LANES), F32)
    partial = pltpu.VMEM((2, seq, LANES), F32)
    return pl.pallas_call(
        functools.partial(_dilated_kernel, seq=seq, unroll=unroll),
        out_shape=jax.ShapeDtypeStruct((batch, seq, A_WIDTH), BF16),
        grid=(batch, n_pair),
        in_specs=[
            pl.BlockSpec(blk, lambda b, h: (b, 0, h)),
            pl.BlockSpec(blk, lambda b, h: (b, 0, n_pair + h)),
            pl.BlockSpec(blk, lambda b, h: (b, 0, 2 * n_pair + h)),
            pl.BlockSpec(blk, lambda b, h: (b, 0, 3 * n_pair + h)),
        ],
        out_specs=pl.BlockSpec(blk, lambda b, h: (b, 0, h)),
        scratch_shapes=[
            slabs, slabs, slabs,
            pltpu.VMEM((2, seq, LANES), F32),
            pltpu.VMEM((seq, LANES), F32),
            partial, partial, partial,
            pltpu.VMEM((2, A_BLOCK, 2 * A_BLOCK), BF16),
        ],
        compiler_params=_params("parallel", "parallel"),
        name="dilated_attention",
    )(z, z, z, z)


def _even_tail_kernel(vb_ref, halo_ref, gb_ref, a_ref, x_ref, pw32_ref, ps_ref, wo32_ref, o_ref, pw_ref, wo_ref,
                      *, tm, sub):
    _cast_weights_once([(pw32_ref, pw_ref), (wo32_ref, wo_ref)], grid_rank=2)
    ti = pl.program_id(1)
    for r in range(tm // sub):
        rows = slice(r * sub, (r + 1) * sub)
        v = vb_ref[rows, :].astype(F32)
        if r == 0:
            halo = jnp.where(ti > 0, halo_ref[...].astype(F32), 0.0)
        else:
            halo = vb_ref[r * sub - POOL_HALO:r * sub, :].astype(F32)
        xc = jnp.concatenate([halo, v], axis=0)
        s2 = xc + pltpu.roll(xc, 1, 0)
        t4 = s2[:, B_GROUP:]
        s4 = t4 + pltpu.roll(t4, 2, 0)
        t8 = s4[:, B_GROUP:]
        s8 = t8 + pltpu.roll(t8, 4, 0)
        t16 = s8[:, B_GROUP:]
        s16 = t16 + pltpu.roll(t16, 8, 0)
        sums = (s2[:, :B_GROUP], s4[:, :B_GROUP], s8[:, :B_GROUP], s16)
        pos = (ti * tm + r * sub + 1 + lax.broadcasted_iota(jnp.int32, (sub, 1), 0)).astype(F32)
        mixed = []
        for g, w in enumerate(POOL_WINDOWS):
            mean = sums[g][POOL_HALO:, :] / jnp.minimum(pos, float(w))
            pooled = mean - v[:, g * B_GROUP:(g + 1) * B_GROUP]
            mixed.append(_dot(pooled.astype(BF16), pw_ref[g]))
        b_out = jnp.concatenate(mixed, axis=1) * ps_ref[...] * _silu(gb_ref[rows, :].astype(F32))
        y = _dot(a_ref[rows, :], wo_ref[:A_WIDTH, :]) + _dot(b_out.astype(BF16), wo_ref[A_WIDTH:, :])
        o_ref[rows, :] = x_ref[rows, :] + y


def even_tail(z, a_out, x, pool_w, pool_scale, w_out, layer, *, batch, seq, tm, sub):
    vb_col = 4 * A_WIDTH // B_WIDTH
    halo_per_tile = tm // POOL_HALO
    return pl.pallas_call(
        functools.partial(_even_tail_kernel, tm=tm, sub=sub),
        out_shape=jax.ShapeDtypeStruct((batch, seq, D_MODEL), F32),
        grid=(batch, seq // tm),
        in_specs=[
            pl.BlockSpec((None, tm, B_WIDTH), lambda b, t: (b, t, vb_col)),
            pl.BlockSpec((None, POOL_HALO, B_WIDTH),
                         lambda b, t: (b, jnp.maximum(t * halo_per_tile - 1, 0), vb_col)),
            pl.BlockSpec((None, tm, B_WIDTH), lambda b, t: (b, t, vb_col + 1)),
            pl.BlockSpec((None, tm, A_WIDTH), lambda b, t: (b, t, 0)),
            pl.BlockSpec((None, tm, D_MODEL), lambda b, t: (b, t, 0)),
            _layer_block((len(POOL_WINDOWS), B_GROUP, B_GROUP), layer),
            pl.BlockSpec((1, B_WIDTH), lambda b, t: (0, 0)),
            _layer_block((A_WIDTH + B_WIDTH, D_MODEL), layer),
        ],
        out_specs=pl.BlockSpec((None, tm, D_MODEL), lambda b, t: (b, t, 0)),
        scratch_shapes=[pltpu.VMEM((len(POOL_WINDOWS), B_GROUP, B_GROUP), BF16),
                        pltpu.VMEM((A_WIDTH + B_WIDTH, D_MODEL), BF16)],
        compiler_params=_params("arbitrary", "arbitrary"),
        name="even_tail",
    )(z, z, z, a_out, x, pool_w, pool_scale.reshape(1, B_WIDTH), w_out)


def _s5_prep_kernel(ar_ref, ai_ref, ldt_ref, br_ref, bi_ref, abr_ref, abi_ref, bbr_ref, bbi_ref):
    ar, ai = ar_ref[...], ai_ref[...]
    dt = jnp.exp(ldt_ref[...])
    mag = jnp.exp(dt * ar)
    abar_re = mag * jnp.cos(dt * ai)
    abar_im = mag * jnp.sin(dt * ai)
    nr, ni = abar_re - 1.0, abar_im
    inv = 1.0 / (ar * ar + ai * ai)
    coef_re = (nr * ar + ni * ai) * inv
    coef_im = (ni * ar - nr * ai) * inv
    br, bi = br_ref[...], bi_ref[...]
    abr_ref[...] = abar_re
    abi_ref[...] = abar_im
    bbr_ref[...] = coef_re * br - coef_im * bi
    bbi_ref[...] = coef_re * bi + coef_im * br


def _part_eye():
    return jnp.eye(S5_GROUPS // S5_PARTS, dtype=F32)


def s5_prep(a_re, a_im, log_dt, b_re, b_im):
    rep = lambda t: jnp.repeat(t, S5_GROUP_DIM, axis=0)
    to_rows = lambda t: t.transpose(0, 2, 1).reshape(D_WIDTH, S5_STATE)
    shp = jax.ShapeDtypeStruct((D_WIDTH, S5_STATE), F32)
    abr, abi, bbr, bbi = pl.pallas_call(
        _s5_prep_kernel, out_shape=(shp, shp, shp, shp), name="s5_prep",
    )(rep(a_re), rep(a_im), rep(jnp.broadcast_to(log_dt[:, None], (S5_GROUPS, S5_STATE))),
      to_rows(b_re), to_rows(b_im))
    gpp = S5_GROUPS // S5_PARTS

    def block_diag_in(t):
        t = t.reshape(S5_PARTS, gpp, S5_GROUP_DIM, 1, S5_STATE) * _part_eye()[None, :, None, :, None]
        return t.reshape(S5_PARTS, S5_PART_IN, S5_PART_STATE)

    w_b = jnp.concatenate([block_diag_in(bbr), block_diag_in(bbi)], axis=2).astype(BF16)
    abar_re = abr[::S5_GROUP_DIM].reshape(1, S5_LANES)
    abar_im = abi[::S5_GROUP_DIM].reshape(1, S5_LANES)
    return w_b, abar_re, abar_im


def s5_out_weights(c_re, c_im):
    gpp = S5_GROUPS // S5_PARTS

    def block_diag_out(t):
        t = t.reshape(S5_PARTS, gpp, S5_GROUP_DIM, S5_STATE).transpose(0, 1, 3, 2)
        t = t[:, :, :, None, :] * _part_eye()[None, :, None, :, None]
        return t.reshape(S5_PARTS, S5_PART_STATE, S5_PART_IN).astype(BF16)

    return block_diag_out(c_re), block_diag_out(c_im)


def _s5_kernel(xd_ref, wb_ref, are_ref, aim_ref, wcr_ref, wci_ref, dsk_ref, w1_32_ref, w2_32_ref,
               o_ref, u_ref, y_ref, bu_ref, h_ref, w1_ref, w2_ref, *, lc, nb, lane_chunk):
    @pl.when(pl.program_id(0) == 0)
    def _():
        h_ref[...] = jnp.zeros_like(h_ref)

    _cast_weights_once([(w1_32_ref, w1_ref), (w2_32_ref, w2_ref)], grid_rank=1)
    n_slab = D_WIDTH // LANES
    for b in range(nb):
        for s in range(n_slab):
            col = b * 2 * D_WIDTH + s * LANES
            u_ref[s, pl.ds(b, lc, stride=nb), :] = xd_ref[:, col:col + LANES].astype(F32)
    u = jnp.concatenate([u_ref[s] for s in range(n_slab)], axis=1)
    ub = u.astype(BF16)
    width = 2 * S5_PART_STATE
    def project(part):
        bu_ref[:, part * width:(part + 1) * width] = _dot(
            ub[:, part * S5_PART_IN:(part + 1) * S5_PART_IN], wb_ref[part])

    def scan(part):
        for c in range(S5_PART_STATE // lane_chunk):
            re = slice(part * width + c * lane_chunk, part * width + (c + 1) * lane_chunk)
            im = slice(re.start + S5_PART_STATE, re.stop + S5_PART_STATE)
            ab = slice(part * S5_PART_STATE + c * lane_chunk, part * S5_PART_STATE + (c + 1) * lane_chunk)
            a_r = jnp.broadcast_to(are_ref[:, ab], (nb, lane_chunk))
            a_i = jnp.broadcast_to(aim_ref[:, ab], (nb, lane_chunk))
            h_r, h_i = h_ref[:, re], h_ref[:, im]
            for t in range(lc):
                row = slice(t * nb, (t + 1) * nb)
                h_r, h_i = (a_r * h_r - a_i * h_i + bu_ref[row, re], a_r * h_i + a_i * h_r + bu_ref[row, im])
                bu_ref[row, re] = h_r
                bu_ref[row, im] = h_i
            h_ref[:, re] = h_r
            h_ref[:, im] = h_i

    def readout(part):
        re = slice(part * width, part * width + S5_PART_STATE)
        im = slice(re.stop, re.stop + S5_PART_STATE)
        return _dot(bu_ref[:, re].astype(BF16), wcr_ref[part]) - _dot(bu_ref[:, im].astype(BF16), wci_ref[part])

    ys = []
    project(0)
    for part in range(S5_PARTS):
        if part + 1 < S5_PARTS:
            project(part + 1)
        scan(part)
        ys.append(readout(part))
    y = jnp.concatenate(ys, axis=1) + dsk_ref[...] * u
    y = jax.nn.gelu(y).astype(BF16)
    out = _dot(y, w1_ref[...]) * jax.nn.sigmoid(_dot(y, w2_ref[...]))
    for s in range(n_slab):
        y_ref[s] = out[:, s * LANES:(s + 1) * LANES]
    o_ref[...] = jnp.concatenate(
        [y_ref[s, pl.ds(b, lc, stride=nb), :] for b in range(nb) for s in range(n_slab)], axis=1).astype(o_ref.dtype)


def s5_layer(zd, w_b, abar_re, abar_im, w_cr, w_ci, d_skip, w1, w2, layer, *, batch, seq, lc):
    rows = lc * batch
    n_slab = D_WIDTH // LANES
    full2 = lambda shape: pl.BlockSpec(shape, lambda c: (0, 0))
    full3 = lambda shape: pl.BlockSpec(shape, lambda c: (0, 0, 0))
    return pl.pallas_call(
        functools.partial(_s5_kernel, lc=lc, nb=batch, lane_chunk=512),
        out_shape=jax.ShapeDtypeStruct((seq, batch * D_WIDTH), BF16),
        grid=(seq // lc,),
        in_specs=[
            pl.BlockSpec((lc, batch * 2 * D_WIDTH), lambda c: (c, 0)),
            full3((S5_PARTS, S5_PART_IN, 2 * S5_PART_STATE)),
            full2((1, S5_LANES)),
            full2((1, S5_LANES)),
            full3((S5_PARTS, S5_PART_STATE, S5_PART_IN)),
            full3((S5_PARTS, S5_PART_STATE, S5_PART_IN)),
            full2((1, D_WIDTH)),
            _layer_block((D_WIDTH, D_WIDTH), layer),
            _layer_block((D_WIDTH, D_WIDTH), layer),
        ],
        out_specs=pl.BlockSpec((lc, batch * D_WIDTH), lambda c: (c, 0)),
        scratch_shapes=[
            pltpu.VMEM((n_slab, rows, LANES), F32),
            pltpu.VMEM((n_slab, rows, LANES), F32),
            pltpu.VMEM((rows, 2 * S5_LANES), F32),
            pltpu.VMEM((batch, 2 * S5_LANES), F32),
            pltpu.VMEM((D_WIDTH, D_WIDTH), BF16),
            pltpu.VMEM((D_WIDTH, D_WIDTH), BF16),
        ],
        compiler_params=_params("arbitrary"),
        name="s5_layer",
    )(zd, w_b, abar_re, abar_im, w_cr, w_ci, d_skip.reshape(1, D_WIDTH), w1, w2)


def _odd_tail_kernel(u_ref, v_ref, gc_ref, d_ref, gd_ref, x_ref, lng_ref, lnb_ref, ws_ref, bs_ref, wo32_ref,
                     o_ref, c_ref, wo_ref, *, tm, sub):
    _cast_weights_once([(wo32_ref, wo_ref)], grid_rank=2)
    ri = lax.broadcasted_iota(jnp.int32, (C_CHUNK, C_CHUNK), 0)
    ci = lax.broadcasted_iota(jnp.int32, (C_CHUNK, C_CHUNK), 1)
    causal = ri >= ci
    w_s = [jnp.where(causal, ws_ref[g], 0.0).astype(BF16) for g in range(C_GROUPS)]
    for r in range(tm // sub):
        tile = slice(r * sub, (r + 1) * sub)
        v = v_ref[tile, :].astype(F32)
        mu = jnp.mean(v, axis=-1, keepdims=True)
        vc = v - mu
        var = jnp.mean(vc * vc, axis=-1, keepdims=True)
        vn = (vc * lax.rsqrt(var + EPS) * lng_ref[...] + lnb_ref[...]).astype(BF16)
        for g in range(C_GROUPS):
            cols = slice(g * C_GROUP_DIM, (g + 1) * C_GROUP_DIM)
            b_g = bs_ref[:, g:g + 1]
            for c in range(sub // C_CHUNK):
                rows = slice(r * sub + c * C_CHUNK, r * sub + (c + 1) * C_CHUNK)
                mixed = _dot(w_s[g], vn[c * C_CHUNK:(c + 1) * C_CHUNK, cols]) + b_g
                gate = _silu(gc_ref[rows, cols].astype(F32))
                c_ref[rows, cols] = (u_ref[rows, cols].astype(F32) * mixed * gate).astype(BF16)
        d = (d_ref[tile, :].astype(F32) * _silu(gd_ref[tile, :].astype(F32))).astype(BF16)
        y = _dot(c_ref[tile, :], wo_ref[:C_WIDTH, :]) + _dot(d, wo_ref[C_WIDTH:, :])
        o_ref[tile, :] = x_ref[tile, :] + y


def odd_tail(zc, zd, d_out, x, ln_g, ln_b, w_s, b_s, w_out, layer, *, batch, seq, tm, sub):
    return pl.pallas_call(
        functools.partial(_odd_tail_kernel, tm=tm, sub=sub),
        out_shape=jax.ShapeDtypeStruct((batch, seq, D_MODEL), F32),
        grid=(batch, seq // tm),
        in_specs=[
            pl.BlockSpec((None, tm, C_WIDTH), lambda b, t: (b, t, 0)),
            pl.BlockSpec((None, tm, C_WIDTH), lambda b, t: (b, t, 1)),
            pl.BlockSpec((None, tm, C_WIDTH), lambda b, t: (b, t, 2)),
            pl.BlockSpec((tm, D_WIDTH), lambda b, t: (t, b)),
            pl.BlockSpec((tm, D_WIDTH), lambda b, t: (t, 2 * b + 1)),
            pl.BlockSpec((None, tm, D_MODEL), lambda b, t: (b, t, 0)),
            pl.BlockSpec((1, C_WIDTH), lambda b, t: (0, 0)),
            pl.BlockSpec((1, C_WIDTH), lambda b, t: (0, 0)),
            _layer_block((C_GROUPS, C_CHUNK, C_CHUNK), layer),
            pl.BlockSpec((C_CHUNK, C_GROUPS), lambda b, t: (0, 0)),
            _layer_block((C_WIDTH + D_WIDTH, D_MODEL), layer),
        ],
        out_specs=pl.BlockSpec((None, tm, D_MODEL), lambda b, t: (b, t, 0)),
        scratch_shapes=[pltpu.VMEM((tm, C_WIDTH), BF16), pltpu.VMEM((C_WIDTH + D_WIDTH, D_MODEL), BF16)],
        compiler_params=_params("arbitrary", "arbitrary"),
        name="odd_tail",
    )(zc, zc, zc, d_out, zd, x, ln_g.reshape(1, C_WIDTH), ln_b.reshape(1, C_WIDTH), w_s, b_s.T, w_out)


def _xattn_kernel(x_ref, g_ref, wq32_ref, kv_ref, wo32_ref, fg_ref, o_ref, wq_ref, wo_ref, *, final, sub):
    _cast_weights_once([(wq32_ref, wq_ref), (wo32_ref, wo_ref)], grid_rank=2)
    for r in range(x_ref.shape[0] // sub):
        rows = slice(r * sub, (r + 1) * sub)
        x = x_ref[rows, :]
        q = _dot(_rms(x, g_ref[...]).astype(BF16), wq_ref[...])
        heads = []
        for h in range(X_HEADS):
            ks = slice(h * X_HEAD_DIM, (h + 1) * X_HEAD_DIM)
            vs = slice(D_MODEL + h * X_HEAD_DIM, D_MODEL + (h + 1) * X_HEAD_DIM)
            s = _dot_nt(q[:, ks].astype(BF16), kv_ref[:, ks]) * (X_HEAD_DIM ** -0.5)
            m = jnp.max(s, axis=-1, keepdims=True)
            p = jnp.exp(s - m)
            den = jnp.sum(p, axis=-1, keepdims=True)
            heads.append((_dot(p.astype(BF16), kv_ref[:, vs]) / den).astype(BF16))
        y = x + _dot(jnp.concatenate(heads, axis=1), wo_ref[...])
        if final:
            y = _rms(y, fg_ref[...])
        o_ref[rows, :] = y


def cross_attention(x, g, w_q, kv, w_o, final_g, layer, *, batch, seq, mem_len, tq, sub, final):
    return pl.pallas_call(
        functools.partial(_xattn_kernel, final=final, sub=sub),
        out_shape=jax.ShapeDtypeStruct((batch, seq, D_MODEL), F32),
        grid=(batch, seq // tq),
        in_specs=[
            pl.BlockSpec((None, tq, D_MODEL), lambda b, t: (b, t, 0)),
            pl.BlockSpec((1, D_MODEL), lambda b, t: (0, 0)),
            _layer_block((D_MODEL, D_MODEL), layer),
            pl.BlockSpec((None, None, mem_len, 2 * D_MODEL), lambda b, t: (layer, b, 0, 0)),
            _layer_block((D_MODEL, D_MODEL), layer),
            pl.BlockSpec((1, D_MODEL), lambda b, t: (0, 0)),
        ],
        out_specs=pl.BlockSpec((None, tq, D_MODEL), lambda b, t: (b, t, 0)),
        scratch_shapes=[pltpu.VMEM((D_MODEL, D_MODEL), BF16), pltpu.VMEM((D_MODEL, D_MODEL), BF16)],
        compiler_params=_params("arbitrary", "arbitrary"),
        name="cross_attention",
    )(x, g.reshape(1, D_MODEL), w_q, kv, w_o, final_g.reshape(1, D_MODEL))


def kernel(x, mem, norm_ab, w_in_ab, pool_w, pool_scale, w_out_ab, norm_cd, w_in_cd, sgu_ln_g, sgu_ln_b,
           sgu_w, sgu_b, s5_a_re, s5_a_im, s5_log_dt, s5_b_re, s5_b_im, s5_c_re, s5_c_im, s5_d,
           glu_w1, glu_w2, w_out_cd, norm_x, w_xq, w_xkv, w_xo, mem_norm, final_norm):
    batch, seq, _ = x.shape
    mem_len = mem.shape[1]
    depth = norm_x.shape[0]
    tokens = batch * seq
    tm_proj = 1024
    tiles_per_row = seq // tm_proj
    bf = lambda t: t.astype(BF16)

    w_in_ab, w_in_cd = bf(w_in_ab), bf(w_in_cd)
    kv = mem_kv_proj(mem.reshape(batch * mem_len, D_MODEL), mem_norm, w_xkv, tn=1024)
    kv = kv.reshape(depth, batch, mem_len, 2 * D_MODEL)
    for layer in range(depth):
        i = layer // 2
        x2 = x.reshape(tokens, D_MODEL)
        if layer % 2 == 0:
            z = norm_matmul(x2, norm_ab[i], w_in_ab, i, tm=tm_proj, tn=2048)
            z = z.reshape(batch, seq, z.shape[1])
            a_out = dilated_attention(z, batch=batch, seq=seq)
            x = even_tail(z, a_out, x, pool_w, pool_scale[i], w_out_ab, i, batch=batch, seq=seq,
                          tm=1024, sub=512)
        else:
            zc, zd = odd_in_proj(x2, norm_cd[i], w_in_cd, i, batch=batch, seq=seq, tm=tm_proj, tn=2 * D_WIDTH)
            w_b, abar_re, abar_im = s5_prep(s5_a_re[i], s5_a_im[i], s5_log_dt[i], s5_b_re[i], s5_b_im[i])
            w_cr, w_ci = s5_out_weights(s5_c_re[i], s5_c_im[i])
            d_out = s5_layer(zd, w_b, abar_re, abar_im, w_cr, w_ci, s5_d[i], glu_w1, glu_w2, i,
                             batch=batch, seq=seq, lc=64)
            x = odd_tail(zc.reshape(batch, seq, zc.shape[1]), zd, d_out, x, sgu_ln_g[i], sgu_ln_b[i], sgu_w,
                         sgu_b[i], w_out_cd, i, batch=batch, seq=seq, tm=1024, sub=512)
        x = cross_attention(x, norm_x[layer], w_xq, kv, w_xo, final_norm,
                            layer, batch=batch, seq=seq, mem_len=mem_len, tq=1024, sub=512,
                            final=(layer == depth - 1))
    return x
```

```python
import functools

import jax
import jax.numpy as jnp
from jax import lax
from jax.experimental import pallas as pl
from jax.experimental.pallas import tpu as pltpu

F32 = jnp.float32
BF16 = jnp.bfloat16

LANES = 128
D_MODEL = 1024
A_WIDTH = 1024
A_HEAD_DIM = 64
A_BLOCK = 128
A_DILATIONS = (1, 4, 16)
DIL_STEP = 4
B_WIDTH = 1024
POOL_WINDOWS = (2, 4, 8, 16)
B_GROUP = 256
POOL_HALO = 16
C_WIDTH = 1024
C_CHUNK = 128
C_GROUPS = 4
C_GROUP_DIM = 256
D_WIDTH = 512
S5_GROUPS = 32
S5_GROUP_DIM = 16
S5_STATE = 64
S5_LANES = S5_GROUPS * S5_STATE
S5_PARTS = 2
S5_PART_IN = D_WIDTH // S5_PARTS
S5_PART_STATE = S5_LANES // S5_PARTS
X_HEADS = 4
X_HEAD_DIM = 256
EPS = 1e-6
NEG = -1e30
LOG2E = 1.4426950408889634

VMEM_LIMIT = 48 * 1024 * 1024


def _params(*sem):
    return pltpu.CompilerParams(dimension_semantics=sem, vmem_limit_bytes=VMEM_LIMIT)


def _rms(x, g):
    ms = jnp.mean(x * x, axis=-1, keepdims=True)
    return x * lax.rsqrt(ms + EPS) * g


def _silu(x):
    return x * jax.nn.sigmoid(x)


def _dot(a, b):
    return jnp.dot(a, b, preferred_element_type=F32)


def _layer_block(shape, layer):
    return pl.BlockSpec((None,) + tuple(shape), lambda *_: (layer,) + (0,) * len(shape),
                        pipeline_mode=pl.Buffered(1))


def _cast_weights_once(pairs, grid_rank):
    first = pl.program_id(0) == 0
    for axis in range(1, grid_rank):
        first = first & (pl.program_id(axis) == 0)

    @pl.when(first)
    def _():
        for src, dst in pairs:
            dst[...] = src[...].astype(dst.dtype)


def _dot_nt(a, b):
    return lax.dot_general(a, b, (((1,), (1,)), ((), ())), preferred_element_type=F32)


def _rms_cast_kernel(x_ref, g_ref, o_ref):
    o_ref[...] = _rms(x_ref[...], g_ref[...]).astype(o_ref.dtype)


def rms_cast(x, g, *, tm):
    m, k = x.shape
    return pl.pallas_call(
        _rms_cast_kernel,
        out_shape=jax.ShapeDtypeStruct((m, k), BF16),
        grid=(m // tm,),
        in_specs=[pl.BlockSpec((tm, k), lambda i: (i, 0)), pl.BlockSpec((1, k), lambda i: (0, 0))],
        out_specs=pl.BlockSpec((tm, k), lambda i: (i, 0)),
        compiler_params=_params("parallel"),
        name="rms_cast",
    )(x, g.reshape(1, k))


def _in_proj_kernel(x_ref, w32_ref, o_ref, w_ref):
    @pl.when(pl.program_id(1) == 0)
    def _():
        w_ref[...] = w32_ref[...].astype(BF16)

    o_ref[...] = _dot(x_ref[...], w_ref[...]).astype(o_ref.dtype)


def in_proj(xn, w, layer, *, n, tm, tn, col0=0, out_shape=None, out_map=None):
    m, k = xn.shape
    col_block0 = col0 // tn
    if out_shape is None:
        out_shape, out_map = (m, n), (lambda i, j: (i, j))
    return pl.pallas_call(
        _in_proj_kernel,
        out_shape=jax.ShapeDtypeStruct(out_shape, BF16),
        grid=(n // tn, m // tm),
        in_specs=[
            pl.BlockSpec((tm, k), lambda j, i: (i, 0)),
            pl.BlockSpec((None, k, tn), lambda j, i: (layer, 0, col_block0 + j)),
        ],
        out_specs=pl.BlockSpec((tm, tn), lambda j, i: out_map(i, j)),
        scratch_shapes=[pltpu.VMEM((k, tn), BF16)],
        compiler_params=_params("arbitrary", "arbitrary"),
        name="in_proj",
    )(xn, w)


def _mem_proj_kernel(x_ref, g_ref, w_ref, o_ref, xn_ref):
    @pl.when((pl.program_id(0) == 0) & (pl.program_id(1) == 0))
    def _():
        xn_ref[...] = _rms(x_ref[...], g_ref[...]).astype(BF16)

    o_ref[...] = _dot(xn_ref[...], w_ref[...].astype(BF16)).astype(o_ref.dtype)


def mem_kv_proj(mem, g, w, *, tn):
    m, k = mem.shape
    n_layers, _, n = w.shape
    return pl.pallas_call(
        _mem_proj_kernel,
        out_shape=jax.ShapeDtypeStruct((n_layers, m, n), BF16),
        grid=(n_layers, n // tn),
        in_specs=[
            pl.BlockSpec((m, k), lambda l, j: (0, 0)),
            pl.BlockSpec((1, k), lambda l, j: (0, 0)),
            pl.BlockSpec((None, k, tn), lambda l, j: (l, 0, j)),
        ],
        out_specs=pl.BlockSpec((None, m, tn), lambda l, j: (l, 0, j)),
        scratch_shapes=[pltpu.VMEM((m, k), BF16)],
        compiler_params=_params("arbitrary", "arbitrary"),
        name="mem_kv_proj",
    )(mem, g.reshape(1, k), w)


def _dilated_kernel(q_ref, k_ref, v_ref, g_ref, o_ref, qs_ref, ks_ref, vs_ref, gs_ref, os_ref, num_ref, den_ref,
                    m_ref, bias_ref, *, seq, unroll):
    assert A_DILATIONS == (1, DIL_STEP, DIL_STEP * DIL_STEP)
    qi = lax.broadcasted_iota(jnp.int32, (A_BLOCK, 2 * A_BLOCK), 0)
    kj = lax.broadcasted_iota(jnp.int32, (A_BLOCK, 2 * A_BLOCK), 1)
    dist = qi + A_BLOCK - kj
    band = (dist >= 0) & (dist <= A_BLOCK)
    bias_ref[1] = jnp.where(band, 0.0, NEG).astype(BF16)
    bias_ref[0] = jnp.where(band & (kj >= A_BLOCK), 0.0, NEG).astype(BF16)

    len_s = seq // DIL_STEP
    len_w = len_s // DIL_STEP
    blocks_s = len_s // A_BLOCK
    q_scale = A_HEAD_DIM ** -0.5 * LOG2E
    for src, dst, scale in ((q_ref, qs_ref, q_scale), (k_ref, ks_ref, None), (v_ref, vs_ref, None)):
        nat = src[...].astype(F32)
        dst[0] = nat if scale is None else nat * scale
        for b in range(DIL_STEP):
            dst[1, b * len_s:(b + 1) * len_s, :] = dst[0, pl.ds(b, len_s, stride=DIL_STEP), :]
        for c in range(DIL_STEP * DIL_STEP):
            b, a = divmod(c, DIL_STEP)
            dst[2, c * len_w:(c + 1) * len_w, :] = dst[1, pl.ds(b * len_s + a, len_w, stride=DIL_STEP), :]
    gs_ref[0] = g_ref[...].astype(F32)

    head0 = lax.broadcasted_iota(jnp.int32, (A_BLOCK, LANES), 1) < A_HEAD_DIM

    def load(slab, cur, prev):
        q = qs_ref[slab, cur, :]
        q2 = jnp.concatenate([jnp.where(head0, q, 0.0), jnp.where(head0, 0.0, q)], axis=0).astype(BF16)
        if slab == 0:
            kk = jnp.concatenate([k_ref[prev, :], k_ref[cur, :]], axis=0)
            vv = jnp.concatenate([v_ref[prev, :], v_ref[cur, :]], axis=0)
            return q2, kk, vv
        kk = jnp.concatenate([ks_ref[slab, prev, :], ks_ref[slab, cur, :]], axis=0).astype(BF16)
        vv = jnp.concatenate([vs_ref[slab, prev, :], vs_ref[slab, cur, :]], axis=0).astype(BF16)
        return q2, kk, vv

    def attend(q2, kk, vv, bias):
        s = _dot_nt(q2, kk).astype(BF16) + jnp.concatenate([bias, bias], axis=0)
        m = jnp.max(s, axis=-1, keepdims=True)
        p = jnp.exp2(s - m)
        res = _dot(p, jnp.concatenate([vv, jnp.ones(vv.shape, BF16)], axis=1))
        top, bot = res[:A_BLOCK], res[A_BLOCK:]
        m_top = jnp.broadcast_to(m[:A_BLOCK], (A_BLOCK, LANES)).astype(F32)
        m_bot = jnp.broadcast_to(m[A_BLOCK:], (A_BLOCK, LANES)).astype(F32)
        return (jnp.where(head0, top[:, :LANES], bot[:, :LANES]), jnp.where(head0, top[:, LANES:], bot[:, LANES:]),
                jnp.where(head0, m_top, m_bot))

    head0_b = lax.broadcasted_iota(jnp.int32, (A_BLOCK, LANES), 1).astype(BF16) < A_HEAD_DIM
    one_b, zero_b = jnp.ones((), BF16), jnp.zeros((), BF16)

    def attend_single(cur):
        q = qs_ref[2, cur, :].astype(BF16)
        k = ks_ref[2, cur, :].astype(BF16)
        v = vs_ref[2, cur, :].astype(BF16)
        k_bd = jnp.concatenate([jnp.where(head0_b, k, zero_b), jnp.where(head0_b, zero_b, k)], axis=0)
        causal = bias_ref[0][:, A_BLOCK:]
        s = _dot_nt(q, k_bd).astype(BF16) + jnp.concatenate([causal, causal], axis=1)
        s0, s1 = s[:, :A_BLOCK], s[:, A_BLOCK:]
        m0 = jnp.max(s0, axis=-1, keepdims=True)
        m1 = jnp.max(s1, axis=-1, keepdims=True)
        p = jnp.concatenate([jnp.exp2(s0 - m0), jnp.exp2(s1 - m1)], axis=1)
        v_bd = jnp.concatenate([
            jnp.concatenate([jnp.where(head0_b, v, zero_b), jnp.where(head0_b, one_b, zero_b)], axis=1),
            jnp.concatenate([jnp.where(head0_b, zero_b, v), jnp.where(head0_b, zero_b, one_b)], axis=1)], axis=0)
        res = _dot(p, v_bd)
        m = jnp.where(head0, jnp.broadcast_to(m0, (A_BLOCK, LANES)).astype(F32),
                      jnp.broadcast_to(m1, (A_BLOCK, LANES)).astype(F32))
        return res[:, :LANES], res[:, LANES:], m

    def aligned(start):
        return pl.ds(pl.multiple_of(start, A_BLOCK), A_BLOCK)

    def banded_rows(idx, blocks_per_seq):
        start = idx * A_BLOCK
        first = idx % blocks_per_seq == 0
        return aligned(start), aligned(jnp.where(first, start, start - A_BLOCK)), bias_ref[jnp.where(first, 0, 1)]

    def merge(num_a, den_a, m_a, num_b, den_b, m_b):
        m = jnp.maximum(m_a, m_b)
        w_a = jnp.exp2(m_a - m)
        w_b = jnp.exp2(m_b - m)
        return w_a * num_a + w_b * num_b, w_a * den_a + w_b * den_b, m

    def narrow_body(idx, carry):
        cur, prev, bias = banded_rows(idx, n_blocks)
        num, den, m = attend(*load(0, cur, prev), bias)
        num_ref[0, cur, :] = num
        den_ref[0, cur, :] = den
        m_ref[0, cur, :] = m
        return carry

    def step_body(idx, carry):
        cur, prev, bias = banded_rows(idx, blocks_s)
        natural = pl.ds(idx // blocks_s + DIL_STEP * A_BLOCK * (idx % blocks_s), A_BLOCK, stride=DIL_STEP)
        num, den, m = merge(*attend(*load(1, cur, prev), bias),
                            num_ref[0, natural, :], den_ref[0, natural, :], m_ref[0, natural, :])
        num_ref[1, cur, :] = num
        den_ref[1, cur, :] = den
        m_ref[1, cur, :] = m
        gs_ref[1, cur, :] = gs_ref[0, natural, :]
        return carry

    def wide_body(idx, carry):
        cur = aligned(idx * A_BLOCK)
        by_step = pl.ds((idx // DIL_STEP) * len_s + idx % DIL_STEP, A_BLOCK, stride=DIL_STEP)
        natural = pl.ds(DIL_STEP * (idx % DIL_STEP) + idx // DIL_STEP, A_BLOCK, stride=DIL_STEP * DIL_STEP)
        num, den, _ = merge(*attend_single(cur),
                            num_ref[1, by_step, :], den_ref[1, by_step, :], m_ref[1, by_step, :])
        os_ref[natural, :] = num / den * _silu(gs_ref[1, by_step, :])
        return carry

    assert len_w == A_BLOCK
    n_blocks = seq // A_BLOCK
    for body in (narrow_body, step_body, wide_body):
        lax.fori_loop(0, n_blocks, body, 0, unroll=unroll)
    o_ref[...] = os_ref[...].astype(o_ref.dtype)


def dilated_attention(z, *, batch, seq, unroll=16):
    n_pair = A_WIDTH // LANES
    n_pat = len(A_DILATIONS)
    blk = (None, seq, LANES)
    slabs = pltpu.VMEM((n_pat, seq, LANES), F32)
    partial = pltpu.VMEM((2, seq, LANES), F32)
    return pl.pallas_call(
        functools.partial(_dilated_kernel, seq=seq, unroll=unroll),
        out_shape=jax.ShapeDtypeStruct((batch, seq, A_WIDTH), BF16),
        grid=(batch, n_pair),
        in_specs=[
            pl.BlockSpec(blk, lambda b, h: (b, 0, h)),
            pl.BlockSpec(blk, lambda b, h: (b, 0, n_pair + h)),
            pl.BlockSpec(blk, lambda b, h: (b, 0, 2 * n_pair + h)),
            pl.BlockSpec(blk, lambda b, h: (b, 0, 3 * n_pair + h)),
        ],
        out_specs=pl.BlockSpec(blk, lambda b, h: (b, 0, h)),
        scratch_shapes=[
            slabs, slabs, slabs,
            pltpu.VMEM((2, seq, LANES), F32),
            pltpu.VMEM((seq, LANES), F32),
            partial, partial, partial,
            pltpu.VMEM((2, A_BLOCK, 2 * A_BLOCK), BF16),
        ],
        compiler_params=_params("parallel", "parallel"),
        name="dilated_attention",
    )(z, z, z, z)


def _even_tail_kernel(vb_ref, halo_ref, gb_ref, a_ref, x_ref, pw32_ref, ps_ref, wo32_ref, o_ref, pw_ref, wo_ref,
                      *, tm, sub):
    _cast_weights_once([(pw32_ref, pw_ref), (wo32_ref, wo_ref)], grid_rank=2)
    ti = pl.program_id(1)
    for r in range(tm // sub):
        rows = slice(r * sub, (r + 1) * sub)
        v = vb_ref[rows, :].astype(F32)
        if r == 0:
            halo = jnp.where(ti > 0, halo_ref[...].astype(F32), 0.0)
        else:
            halo = vb_ref[r * sub - POOL_HALO:r * sub, :].astype(F32)
        xc = jnp.concatenate([halo, v], axis=0)
        s2 = xc + pltpu.roll(xc, 1, 0)
        t4 = s2[:, B_GROUP:]
        s4 = t4 + pltpu.roll(t4, 2, 0)
        t8 = s4[:, B_GROUP:]
        s8 = t8 + pltpu.roll(t8, 4, 0)
        t16 = s8[:, B_GROUP:]
        s16 = t16 + pltpu.roll(t16, 8, 0)
        sums = (s2[:, :B_GROUP], s4[:, :B_GROUP], s8[:, :B_GROUP], s16)
        pos = (ti * tm + r * sub + 1 + lax.broadcasted_iota(jnp.int32, (sub, 1), 0)).astype(F32)
        mixed = []
        for g, w in enumerate(POOL_WINDOWS):
            mean = sums[g][POOL_HALO:, :] / jnp.minimum(pos, float(w))
            pooled = mean - v[:, g * B_GROUP:(g + 1) * B_GROUP]
            mixed.append(_dot(pooled.astype(BF16), pw_ref[g]))
        b_out = jnp.concatenate(mixed, axis=1) * ps_ref[...] * _silu(gb_ref[rows, :].astype(F32))
        y = _dot(a_ref[rows, :], wo_ref[:A_WIDTH, :]) + _dot(b_out.astype(BF16), wo_ref[A_WIDTH:, :])
        o_ref[rows, :] = x_ref[rows, :] + y


def even_tail(z, a_out, x, pool_w, pool_scale, w_out, layer, *, batch, seq, tm, sub):
    vb_col = 4 * A_WIDTH // B_WIDTH
    halo_per_tile = tm // POOL_HALO
    return pl.pallas_call(
        functools.partial(_even_tail_kernel, tm=tm, sub=sub),
        out_shape=jax.ShapeDtypeStruct((batch, seq, D_MODEL), F32),
        grid=(batch, seq // tm),
        in_specs=[
            pl.BlockSpec((None, tm, B_WIDTH), lambda b, t: (b, t, vb_col)),
            pl.BlockSpec((None, POOL_HALO, B_WIDTH),
                         lambda b, t: (b, jnp.maximum(t * halo_per_tile - 1, 0), vb_col)),
            pl.BlockSpec((None, tm, B_WIDTH), lambda b, t: (b, t, vb_col + 1)),
            pl.BlockSpec((None, tm, A_WIDTH), lambda b, t: (b, t, 0)),
            pl.BlockSpec((None, tm, D_MODEL), lambda b, t: (b, t, 0)),
            _layer_block((len(POOL_WINDOWS), B_GROUP, B_GROUP), layer),
            pl.BlockSpec((1, B_WIDTH), lambda b, t: (0, 0)),
            _layer_block((A_WIDTH + B_WIDTH, D_MODEL), layer),
        ],
        out_specs=pl.BlockSpec((None, tm, D_MODEL), lambda b, t: (b, t, 0)),
        scratch_shapes=[pltpu.VMEM((len(POOL_WINDOWS), B_GROUP, B_GROUP), BF16),
                        pltpu.VMEM((A_WIDTH + B_WIDTH, D_MODEL), BF16)],
        compiler_params=_params("arbitrary", "arbitrary"),
        name="even_tail",
    )(z, z, z, a_out, x, pool_w, pool_scale.reshape(1, B_WIDTH), w_out)


def _s5_prep_kernel(ar_ref, ai_ref, ldt_ref, br_ref, bi_ref, abr_ref, abi_ref, bbr_ref, bbi_ref):
    ar, ai = ar_ref[...], ai_ref[...]
    dt = jnp.exp(ldt_ref[...])
    mag = jnp.exp(dt * ar)
    abar_re = mag * jnp.cos(dt * ai)
    abar_im = mag * jnp.sin(dt * ai)
    nr, ni = abar_re - 1.0, abar_im
    inv = 1.0 / (ar * ar + ai * ai)
    coef_re = (nr * ar + ni * ai) * inv
    coef_im = (ni * ar - nr * ai) * inv
    br, bi = br_ref[...], bi_ref[...]
    abr_ref[...] = abar_re
    abi_ref[...] = abar_im
    bbr_ref[...] = coef_re * br - coef_im * bi
    bbi_ref[...] = coef_re * bi + coef_im * br


def _part_eye():
    return jnp.eye(S5_GROUPS // S5_PARTS, dtype=F32)


def s5_prep(a_re, a_im, log_dt, b_re, b_im):
    rep = lambda t: jnp.repeat(t, S5_GROUP_DIM, axis=0)
    to_rows = lambda t: t.transpose(0, 2, 1).reshape(D_WIDTH, S5_STATE)
    shp = jax.ShapeDtypeStruct((D_WIDTH, S5_STATE), F32)
    abr, abi, bbr, bbi = pl.pallas_call(
        _s5_prep_kernel, out_shape=(shp, shp, shp, shp), name="s5_prep",
    )(rep(a_re), rep(a_im), rep(jnp.broadcast_to(log_dt[:, None], (S5_GROUPS, S5_STATE))),
      to_rows(b_re), to_rows(b_im))
    gpp = S5_GROUPS // S5_PARTS

    def block_diag_in(t):
        t = t.reshape(S5_PARTS, gpp, S5_GROUP_DIM, 1, S5_STATE) * _part_eye()[None, :, None, :, None]
        return t.reshape(S5_PARTS, S5_PART_IN, S5_PART_STATE)

    w_b = jnp.concatenate([block_diag_in(bbr), block_diag_in(bbi)], axis=2).astype(BF16)
    abar_re = abr[::S5_GROUP_DIM].reshape(1, S5_LANES)
    abar_im = abi[::S5_GROUP_DIM].reshape(1, S5_LANES)
    return w_b, abar_re, abar_im


def s5_out_weights(c_re, c_im):
    gpp = S5_GROUPS // S5_PARTS

    def block_diag_out(t):
        t = t.reshape(S5_PARTS, gpp, S5_GROUP_DIM, S5_STATE).transpose(0, 1, 3, 2)
        t = t[:, :, :, None, :] * _part_eye()[None, :, None, :, None]
        return t.reshape(S5_PARTS, S5_PART_STATE, S5_PART_IN).astype(BF16)

    return block_diag_out(c_re), block_diag_out(c_im)


def _s5_kernel(xd_ref, wb_ref, are_ref, aim_ref, wcr_ref, wci_ref, dsk_ref, w1_32_ref, w2_32_ref,
               o_ref, u_ref, y_ref, bu_ref, h_ref, w1_ref, w2_ref, *, lc, nb, lane_chunk):
    @pl.when(pl.program_id(0) == 0)
    def _():
        h_ref[...] = jnp.zeros_like(h_ref)

    _cast_weights_once([(w1_32_ref, w1_ref), (w2_32_ref, w2_ref)], grid_rank=1)
    n_slab = D_WIDTH // LANES
    for b in range(nb):
        for s in range(n_slab):
            col = b * 2 * D_WIDTH + s * LANES
            u_ref[s, pl.ds(b, lc, stride=nb), :] = xd_ref[:, col:col + LANES].astype(F32)
    u = jnp.concatenate([u_ref[s] for s in range(n_slab)], axis=1)
    ub = u.astype(BF16)
    width = 2 * S5_PART_STATE
    def project(part):
        bu_ref[:, part * width:(part + 1) * width] = _dot(
            ub[:, part * S5_PART_IN:(part + 1) * S5_PART_IN], wb_ref[part])

    def scan(part):
        for c in range(S5_PART_STATE // lane_chunk):
            re = slice(part * width + c * lane_chunk, part * width + (c + 1) * lane_chunk)
            im = slice(re.start + S5_PART_STATE, re.stop + S5_PART_STATE)
            ab = slice(part * S5_PART_STATE + c * lane_chunk, part * S5_PART_STATE + (c + 1) * lane_chunk)
            a_r = jnp.broadcast_to(are_ref[:, ab], (nb, lane_chunk))
            a_i = jnp.broadcast_to(aim_ref[:, ab], (nb, lane_chunk))
            h_r, h_i = h_ref[:, re], h_ref[:, im]
            for t in range(lc):
                row = slice(t * nb, (t + 1) * nb)
                h_r, h_i = (a_r * h_r - a_i * h_i + bu_ref[row, re], a_r * h_i + a_i * h_r + bu_ref[row, im])
                bu_ref[row, re] = h_r
                bu_ref[row, im] = h_i
            h_ref[:, re] = h_r
            h_ref[:, im] = h_i

    def readout(part):
        re = slice(part * width, part * width + S5_PART_STATE)
        im = slice(re.stop, re.stop + S5_PART_STATE)
        return _dot(bu_ref[:, re].astype(BF16), wcr_ref[part]) - _dot(bu_ref[:, im].astype(BF16), wci_ref[part])

    ys = []
    project(0)
    for part in range(S5_PARTS):
        if part + 1 < S5_PARTS:
            project(part + 1)
        scan(part)
        ys.append(readout(part))
    y = jnp.concatenate(ys, axis=1) + dsk_ref[...] * u
    y = jax.nn.gelu(y).astype(BF16)
    out = _dot(y, w1_ref[...]) * jax.nn.sigmoid(_dot(y, w2_ref[...]))
    for s in range(n_slab):
        y_ref[s] = out[:, s * LANES:(s + 1) * LANES]
    o_ref[...] = jnp.concatenate(
        [y_ref[s, pl.ds(b, lc, stride=nb), :] for b in range(nb) for s in range(n_slab)], axis=1).astype(o_ref.dtype)


def s5_layer(zd, w_b, abar_re, abar_im, w_cr, w_ci, d_skip, w1, w2, layer, *, batch, seq, lc):
    rows = lc * batch
    n_slab = D_WIDTH // LANES
    full2 = lambda shape: pl.BlockSpec(shape, lambda c: (0, 0))
    full3 = lambda shape: pl.BlockSpec(shape, lambda c: (0, 0, 0))
    return pl.pallas_call(
        functools.partial(_s5_kernel, lc=lc, nb=batch, lane_chunk=512),
        out_shape=jax.ShapeDtypeStruct((seq, batch * D_WIDTH), BF16),
        grid=(seq // lc,),
        in_specs=[
            pl.BlockSpec((lc, batch * 2 * D_WIDTH), lambda c: (c, 0)),
            full3((S5_PARTS, S5_PART_IN, 2 * S5_PART_STATE)),
            full2((1, S5_LANES)),
            full2((1, S5_LANES)),
            full3((S5_PARTS, S5_PART_STATE, S5_PART_IN)),
            full3((S5_PARTS, S5_PART_STATE, S5_PART_IN)),
            full2((1, D_WIDTH)),
            _layer_block((D_WIDTH, D_WIDTH), layer),
            _layer_block((D_WIDTH, D_WIDTH), layer),
        ],
        out_specs=pl.BlockSpec((lc, batch * D_WIDTH), lambda c: (c, 0)),
        scratch_shapes=[
            pltpu.VMEM((n_slab, rows, LANES), F32),
            pltpu.VMEM((n_slab, rows, LANES), F32),
            pltpu.VMEM((rows, 2 * S5_LANES), F32),
            pltpu.VMEM((batch, 2 * S5_LANES), F32),
            pltpu.VMEM((D_WIDTH, D_WIDTH), BF16),
            pltpu.VMEM((D_WIDTH, D_WIDTH), BF16),
        ],
        compiler_params=_params("arbitrary"),
        name="s5_layer",
    )(zd, w_b, abar_re, abar_im, w_cr, w_ci, d_skip.reshape(1, D_WIDTH), w1, w2)


def _odd_tail_kernel(u_ref, v_ref, gc_ref, d_ref, gd_ref, x_ref, lng_ref, lnb_ref, ws_ref, bs_ref, wo32_ref,
                     o_ref, c_ref, wo_ref, *, tm, sub):
    _cast_weights_once([(wo32_ref, wo_ref)], grid_rank=2)
    ri = lax.broadcasted_iota(jnp.int32, (C_CHUNK, C_CHUNK), 0)
    ci = lax.broadcasted_iota(jnp.int32, (C_CHUNK, C_CHUNK), 1)
    causal = ri >= ci
    w_s = [jnp.where(causal, ws_ref[g], 0.0).astype(BF16) for g in range(C_GROUPS)]
    for r in range(tm // sub):
        tile = slice(r * sub, (r + 1) * sub)
        v = v_ref[tile, :].astype(F32)
        mu = jnp.mean(v, axis=-1, keepdims=True)
        vc = v - mu
        var = jnp.mean(vc * vc, axis=-1, keepdims=True)
        vn = (vc * lax.rsqrt(var + EPS) * lng_ref[...] + lnb_ref[...]).astype(BF16)
        for g in range(C_GROUPS):
            cols = slice(g * C_GROUP_DIM, (g + 1) * C_GROUP_DIM)
            b_g = bs_ref[:, g:g + 1]
            for c in range(sub // C_CHUNK):
                rows = slice(r * sub + c * C_CHUNK, r * sub + (c + 1) * C_CHUNK)
                mixed = _dot(w_s[g], vn[c * C_CHUNK:(c + 1) * C_CHUNK, cols]) + b_g
                gate = _silu(gc_ref[rows, cols].astype(F32))
                c_ref[rows, cols] = (u_ref[rows, cols].astype(F32) * mixed * gate).astype(BF16)
        d = (d_ref[tile, :].astype(F32) * _silu(gd_ref[tile, :].astype(F32))).astype(BF16)
        y = _dot(c_ref[tile, :], wo_ref[:C_WIDTH, :]) + _dot(d, wo_ref[C_WIDTH:, :])
        o_ref[tile, :] = x_ref[tile, :] + y


def odd_tail(zc, zd, d_out, x, ln_g, ln_b, w_s, b_s, w_out, layer, *, batch, seq, tm, sub):
    return pl.pallas_call(
        functools.partial(_odd_tail_kernel, tm=tm, sub=sub),
        out_shape=jax.ShapeDtypeStruct((batch, seq, D_MODEL), F32),
        grid=(batch, seq // tm),
        in_specs=[
            pl.BlockSpec((None, tm, C_WIDTH), lambda b, t: (b, t, 0)),
            pl.BlockSpec((None, tm, C_WIDTH), lambda b, t: (b, t, 1)),
            pl.BlockSpec((None, tm, C_WIDTH), lambda b, t: (b, t, 2)),
            pl.BlockSpec((tm, D_WIDTH), lambda b, t: (t, b)),
            pl.BlockSpec((tm, D_WIDTH), lambda b, t: (t, 2 * b + 1)),
            pl.BlockSpec((None, tm, D_MODEL), lambda b, t: (b, t, 0)),
            pl.BlockSpec((1, C_WIDTH), lambda b, t: (0, 0)),
            pl.BlockSpec((1, C_WIDTH), lambda b, t: (0, 0)),
            _layer_block((C_GROUPS, C_CHUNK, C_CHUNK), layer),
            pl.BlockSpec((C_CHUNK, C_GROUPS), lambda b, t: (0, 0)),
            _layer_block((C_WIDTH + D_WIDTH, D_MODEL), layer),
        ],
        out_specs=pl.BlockSpec((None, tm, D_MODEL), lambda b, t: (b, t, 0)),
        scratch_shapes=[pltpu.VMEM((tm, C_WIDTH), BF16), pltpu.VMEM((C_WIDTH + D_WIDTH, D_MODEL), BF16)],
        compiler_params=_params("arbitrary", "arbitrary"),
        name="odd_tail",
    )(zc, zc, zc, d_out, zd, x, ln_g.reshape(1, C_WIDTH), ln_b.reshape(1, C_WIDTH), w_s, b_s.T, w_out)


def _xattn_kernel(x_ref, g_ref, wq32_ref, kv_ref, wo32_ref, pg_ref, *refs, final, sub):
    if final:
        o_ref, wq_ref, wo_ref = refs
    else:
        o_ref, xn_ref, wq_ref, wo_ref = refs
    _cast_weights_once([(wq32_ref, wq_ref), (wo32_ref, wo_ref)], grid_rank=2)
    for r in range(x_ref.shape[0] // sub):
        rows = slice(r * sub, (r + 1) * sub)
        x = x_ref[rows, :]
        q = _dot(_rms(x, g_ref[...]).astype(BF16), wq_ref[...])
        heads = []
        for h in range(X_HEADS):
            ks = slice(h * X_HEAD_DIM, (h + 1) * X_HEAD_DIM)
            vs = slice(D_MODEL + h * X_HEAD_DIM, D_MODEL + (h + 1) * X_HEAD_DIM)
            s = _dot_nt(q[:, ks].astype(BF16), kv_ref[:, ks]) * (X_HEAD_DIM ** -0.5)
            m = jnp.max(s, axis=-1, keepdims=True)
            p = jnp.exp(s - m)
            den = jnp.sum(p, axis=-1, keepdims=True)
            heads.append((_dot(p.astype(BF16), kv_ref[:, vs]) / den).astype(BF16))
        y = x + _dot(jnp.concatenate(heads, axis=1), wo_ref[...])
        if final:
            o_ref[rows, :] = _rms(y, pg_ref[...])
        else:
            o_ref[rows, :] = y
            xn_ref[rows, :] = _rms(y, pg_ref[...]).astype(xn_ref.dtype)


def cross_attention(x, g, w_q, kv, w_o, post_g, layer, *, batch, seq, mem_len, tq, sub, final):
    tile = pl.BlockSpec((None, tq, D_MODEL), lambda b, t: (b, t, 0))
    res = jax.ShapeDtypeStruct((batch, seq, D_MODEL), F32)
    nxt = jax.ShapeDtypeStruct((batch, seq, D_MODEL), BF16)
    return pl.pallas_call(
        functools.partial(_xattn_kernel, final=final, sub=sub),
        out_shape=res if final else (res, nxt),
        grid=(batch, seq // tq),
        in_specs=[
            tile,
            pl.BlockSpec((1, D_MODEL), lambda b, t: (0, 0)),
            _layer_block((D_MODEL, D_MODEL), layer),
            pl.BlockSpec((None, None, mem_len, 2 * D_MODEL), lambda b, t: (layer, b, 0, 0)),
            _layer_block((D_MODEL, D_MODEL), layer),
            pl.BlockSpec((1, D_MODEL), lambda b, t: (0, 0)),
        ],
        out_specs=tile if final else (tile, tile),
        scratch_shapes=[pltpu.VMEM((D_MODEL, D_MODEL), BF16), pltpu.VMEM((D_MODEL, D_MODEL), BF16)],
        compiler_params=_params("arbitrary", "arbitrary"),
        name="cross_attention",
    )(x, g.reshape(1, D_MODEL), w_q, kv, w_o, post_g.reshape(1, D_MODEL))


def kernel(x, mem, norm_ab, w_in_ab, pool_w, pool_scale, w_out_ab, norm_cd, w_in_cd, sgu_ln_g, sgu_ln_b,
           sgu_w, sgu_b, s5_a_re, s5_a_im, s5_log_dt, s5_b_re, s5_b_im, s5_c_re, s5_c_im, s5_d,
           glu_w1, glu_w2, w_out_cd, norm_x, w_xq, w_xkv, w_xo, mem_norm, final_norm):
    batch, seq, _ = x.shape
    mem_len = mem.shape[1]
    depth = norm_x.shape[0]
    tokens = batch * seq
    tm_proj = 1024
    tiles_per_row = seq // tm_proj
    mixer_norm = lambda layer: (norm_ab if layer % 2 == 0 else norm_cd)[layer // 2]

    kv = mem_kv_proj(mem.reshape(batch * mem_len, D_MODEL), mem_norm, w_xkv, tn=1024)
    kv = kv.reshape(depth, batch, mem_len, 2 * D_MODEL)
    xn = rms_cast(x.reshape(tokens, D_MODEL), mixer_norm(0), tm=tm_proj)
    for layer in range(depth):
        i = layer // 2
        if layer % 2 == 0:
            z = in_proj(xn, w_in_ab, i, n=w_in_ab.shape[2], tm=tm_proj, tn=2048)
            z = z.reshape(batch, seq, z.shape[1])
            a_out = dilated_attention(z, batch=batch, seq=seq)
            x = even_tail(z, a_out, x, pool_w, pool_scale[i], w_out_ab, i, batch=batch, seq=seq,
                          tm=1024, sub=512)
        else:
            n_c = 3 * C_WIDTH
            zc = in_proj(xn, w_in_cd, i, n=n_c, tm=tm_proj, tn=1024)
            zd = in_proj(xn, w_in_cd, i, n=2 * D_WIDTH, col0=n_c, tm=tm_proj, tn=2 * D_WIDTH,
                         out_shape=(seq, batch * 2 * D_WIDTH),
                         out_map=lambda r, j: (r % tiles_per_row, r // tiles_per_row))
            w_b, abar_re, abar_im = s5_prep(s5_a_re[i], s5_a_im[i], s5_log_dt[i], s5_b_re[i], s5_b_im[i])
            w_cr, w_ci = s5_out_weights(s5_c_re[i], s5_c_im[i])
            d_out = s5_layer(zd, w_b, abar_re, abar_im, w_cr, w_ci, s5_d[i], glu_w1, glu_w2, i,
                             batch=batch, seq=seq, lc=64)
            x = odd_tail(zc.reshape(batch, seq, n_c), zd, d_out, x, sgu_ln_g[i], sgu_ln_b[i], sgu_w,
                         sgu_b[i], w_out_cd, i, batch=batch, seq=seq, tm=1024, sub=512)
        final = layer == depth - 1
        out = cross_attention(x, norm_x[layer], w_xq, kv, w_xo, final_norm if final else mixer_norm(layer + 1),
                              layer, batch=batch, seq=seq, mem_len=mem_len, tq=1024, sub=512, final=final)
        if final:
            return out
        x, xn = out[0], out[1].reshape(tokens, D_MODEL)
```

```python
import functools

import jax
import jax.numpy as jnp
from jax import lax
from jax.experimental import pallas as pl
from jax.experimental.pallas import tpu as pltpu

F32 = jnp.float32
BF16 = jnp.bfloat16

LANES = 128
D_MODEL = 1024
A_WIDTH = 1024
A_HEAD_DIM = 64
A_BLOCK = 128
A_DILATIONS = (1, 4, 16)
DIL_STEP = 4
B_WIDTH = 1024
POOL_WINDOWS = (2, 4, 8, 16)
B_GROUP = 256
POOL_HALO = 16
C_WIDTH = 1024
C_CHUNK = 128
C_GROUPS = 4
C_GROUP_DIM = 256
D_WIDTH = 512
S5_GROUPS = 32
S5_GROUP_DIM = 16
S5_STATE = 64
S5_LANES = S5_GROUPS * S5_STATE
S5_PARTS = 2
S5_PART_IN = D_WIDTH // S5_PARTS
S5_PART_STATE = S5_LANES // S5_PARTS
X_HEADS = 4
X_HEAD_DIM = 256
EPS = 1e-6
NEG = -1e30
LOG2E = 1.4426950408889634

VMEM_LIMIT = 48 * 1024 * 1024


def _params(*sem):
    return pltpu.CompilerParams(dimension_semantics=sem, vmem_limit_bytes=VMEM_LIMIT)


def _rms(x, g):
    ms = jnp.mean(x * x, axis=-1, keepdims=True)
    return x * lax.rsqrt(ms + EPS) * g


def _silu(x):
    return x * jax.nn.sigmoid(x)


def _dot(a, b):
    return jnp.dot(a, b, preferred_element_type=F32)


def _layer_block(shape, layer):
    return pl.BlockSpec((None,) + tuple(shape), lambda *_: (layer,) + (0,) * len(shape),
                        pipeline_mode=pl.Buffered(1))


def _cast_weights_once(pairs, grid_rank):
    first = pl.program_id(0) == 0
    for axis in range(1, grid_rank):
        first = first & (pl.program_id(axis) == 0)

    @pl.when(first)
    def _():
        for src, dst in pairs:
            dst[...] = src[...].astype(dst.dtype)


def _dot_nt(a, b):
    return lax.dot_general(a, b, (((1,), (1,)), ((), ())), preferred_element_type=F32)


def _rms_cast_kernel(x_ref, g_ref, o_ref):
    o_ref[...] = _rms(x_ref[...], g_ref[...]).astype(o_ref.dtype)


def rms_cast(x, g, *, tm):
    m, k = x.shape
    return pl.pallas_call(
        _rms_cast_kernel,
        out_shape=jax.ShapeDtypeStruct((m, k), BF16),
        grid=(m // tm,),
        in_specs=[pl.BlockSpec((tm, k), lambda i: (i, 0)), pl.BlockSpec((1, k), lambda i: (0, 0))],
        out_specs=pl.BlockSpec((tm, k), lambda i: (i, 0)),
        compiler_params=_params("parallel"),
        name="rms_cast",
    )(x, g.reshape(1, k))


def _in_proj_kernel(x_ref, w32_ref, o_ref, w_ref):
    @pl.when(pl.program_id(1) == 0)
    def _():
        w_ref[...] = w32_ref[...].astype(BF16)

    o_ref[...] = _dot(x_ref[...], w_ref[...]).astype(o_ref.dtype)


def in_proj(xn, w, layer, *, n, tm, tn, col0=0, out_shape=None, out_map=None):
    m, k = xn.shape
    col_block0 = col0 // tn
    if out_shape is None:
        out_shape, out_map = (m, n), (lambda i, j: (i, j))
    return pl.pallas_call(
        _in_proj_kernel,
        out_shape=jax.ShapeDtypeStruct(out_shape, BF16),
        grid=(n // tn, m // tm),
        in_specs=[
            pl.BlockSpec((tm, k), lambda j, i: (i, 0)),
            pl.BlockSpec((None, k, tn), lambda j, i: (layer, 0, col_block0 + j)),
        ],
        out_specs=pl.BlockSpec((tm, tn), lambda j, i: out_map(i, j)),
        scratch_shapes=[pltpu.VMEM((k, tn), BF16)],
        compiler_params=_params("arbitrary", "arbitrary"),
        name="in_proj",
    )(xn, w)


def _mem_proj_kernel(x_ref, g_ref, w_ref, o_ref, xn_ref):
    @pl.when((pl.program_id(0) == 0) & (pl.program_id(1) == 0))
    def _():
        xn_ref[...] = _rms(x_ref[...], g_ref[...]).astype(BF16)

    o_ref[...] = _dot(xn_ref[...], w_ref[...].astype(BF16)).astype(o_ref.dtype)


def mem_kv_proj(mem, g, w, *, tn):
    m, k = mem.shape
    n_layers, _, n = w.shape
    return pl.pallas_call(
        _mem_proj_kernel,
        out_shape=jax.ShapeDtypeStruct((n_layers, m, n), BF16),
        grid=(n_layers, n // tn),
        in_specs=[
            pl.BlockSpec((m, k), lambda l, j: (0, 0)),
            pl.BlockSpec((1, k), lambda l, j: (0, 0)),
            pl.BlockSpec((None, k, tn), lambda l, j: (l, 0, j)),
        ],
        out_specs=pl.BlockSpec((None, m, tn), lambda l, j: (l, 0, j)),
        scratch_shapes=[pltpu.VMEM((m, k), BF16)],
        compiler_params=_params("arbitrary", "arbitrary"),
        name="mem_kv_proj",
    )(mem, g.reshape(1, k), w)


def _dilated_kernel(q_ref, k_ref, v_ref, g_ref, o_ref, qs_ref, ks_ref, vs_ref, gs_ref, os_ref, num_ref, den_ref,
                    m_ref, bias_ref, *, seq, unroll):
    assert A_DILATIONS == (1, DIL_STEP, DIL_STEP * DIL_STEP)
    qi = lax.broadcasted_iota(jnp.int32, (A_BLOCK, 2 * A_BLOCK), 0)
    kj = lax.broadcasted_iota(jnp.int32, (A_BLOCK, 2 * A_BLOCK), 1)
    dist = qi + A_BLOCK - kj
    band = (dist >= 0) & (dist <= A_BLOCK)
    bias_ref[1] = jnp.where(band, 0.0, NEG).astype(BF16)
    bias_ref[0] = jnp.where(band & (kj >= A_BLOCK), 0.0, NEG).astype(BF16)

    len_s = seq // DIL_STEP
    len_w = len_s // DIL_STEP
    blocks_s = len_s // A_BLOCK
    q_scale = A_HEAD_DIM ** -0.5 * LOG2E
    for src, dst, scale in ((q_ref, qs_ref, q_scale), (k_ref, ks_ref, None), (v_ref, vs_ref, None)):
        nat = src[...].astype(F32)
        dst[0] = nat if scale is None else nat * scale
        for b in range(DIL_STEP):
            dst[1, b * len_s:(b + 1) * len_s, :] = dst[0, pl.ds(b, len_s, stride=DIL_STEP), :]
        for c in range(DIL_STEP * DIL_STEP):
            b, a = divmod(c, DIL_STEP)
            dst[2, c * len_w:(c + 1) * len_w, :] = dst[1, pl.ds(b * len_s + a, len_w, stride=DIL_STEP), :]
    gs_ref[0] = g_ref[...].astype(F32)

    head0 = lax.broadcasted_iota(jnp.int32, (A_BLOCK, LANES), 1) < A_HEAD_DIM

    def load(slab, cur, prev):
        q = qs_ref[slab, cur, :]
        q2 = jnp.concatenate([jnp.where(head0, q, 0.0), jnp.where(head0, 0.0, q)], axis=0).astype(BF16)
        if slab == 0:
            kk = jnp.concatenate([k_ref[prev, :], k_ref[cur, :]], axis=0)
            vv = jnp.concatenate([v_ref[prev, :], v_ref[cur, :]], axis=0)
            return q2, kk, vv
        kk = jnp.concatenate([ks_ref[slab, prev, :], ks_ref[slab, cur, :]], axis=0).astype(BF16)
        vv = jnp.concatenate([vs_ref[slab, prev, :], vs_ref[slab, cur, :]], axis=0).astype(BF16)
        return q2, kk, vv

    def attend(q2, kk, vv, bias):
        s = _dot_nt(q2, kk).astype(BF16) + jnp.concatenate([bias, bias], axis=0)
        m = jnp.max(s, axis=-1, keepdims=True)
        p = jnp.exp2(s - m)
        res = _dot(p, jnp.concatenate([vv, jnp.ones(vv.shape, BF16)], axis=1))
        top, bot = res[:A_BLOCK], res[A_BLOCK:]
        m_top = jnp.broadcast_to(m[:A_BLOCK], (A_BLOCK, LANES)).astype(F32)
        m_bot = jnp.broadcast_to(m[A_BLOCK:], (A_BLOCK, LANES)).astype(F32)
        return (jnp.where(head0, top[:, :LANES], bot[:, :LANES]), jnp.where(head0, top[:, LANES:], bot[:, LANES:]),
                jnp.where(head0, m_top, m_bot))

    head0_b = lax.broadcasted_iota(jnp.int32, (A_BLOCK, LANES), 1).astype(BF16) < A_HEAD_DIM
    one_b, zero_b = jnp.ones((), BF16), jnp.zeros((), BF16)

    def attend_single(cur):
        q = qs_ref[2, cur, :].astype(BF16)
        k = ks_ref[2, cur, :].astype(BF16)
        v = vs_ref[2, cur, :].astype(BF16)
        k_bd = jnp.concatenate([jnp.where(head0_b, k, zero_b), jnp.where(head0_b, zero_b, k)], axis=0)
        causal = bias_ref[0][:, A_BLOCK:]
        s = _dot_nt(q, k_bd).astype(BF16) + jnp.concatenate([causal, causal], axis=1)
        s0, s1 = s[:, :A_BLOCK], s[:, A_BLOCK:]
        m0 = jnp.max(s0, axis=-1, keepdims=True)
        m1 = jnp.max(s1, axis=-1, keepdims=True)
        p = jnp.concatenate([jnp.exp2(s0 - m0), jnp.exp2(s1 - m1)], axis=1)
        v_bd = jnp.concatenate([
            jnp.concatenate([jnp.where(head0_b, v, zero_b), jnp.where(head0_b, one_b, zero_b)], axis=1),
            jnp.concatenate([jnp.where(head0_b, zero_b, v), jnp.where(head0_b, zero_b, one_b)], axis=1)], axis=0)
        res = _dot(p, v_bd)
        m = jnp.where(head0, jnp.broadcast_to(m0, (A_BLOCK, LANES)).astype(F32),
                      jnp.broadcast_to(m1, (A_BLOCK, LANES)).astype(F32))
        return res[:, :LANES], res[:, LANES:], m

    def aligned(start):
        return pl.ds(pl.multiple_of(start, A_BLOCK), A_BLOCK)

    def banded_rows(idx, blocks_per_seq):
        start = idx * A_BLOCK
        first = idx % blocks_per_seq == 0
        return aligned(start), aligned(jnp.where(first, start, start - A_BLOCK)), bias_ref[jnp.where(first, 0, 1)]

    def merge(num_a, den_a, m_a, num_b, den_b, m_b):
        m = jnp.maximum(m_a, m_b)
        w_a = jnp.exp2(m_a - m)
        w_b = jnp.exp2(m_b - m)
        return w_a * num_a + w_b * num_b, w_a * den_a + w_b * den_b, m

    def narrow_body(idx, carry):
        cur, prev, bias = banded_rows(idx, n_blocks)
        num, den, m = attend(*load(0, cur, prev), bias)
        num_ref[0, cur, :] = num
        den_ref[0, cur, :] = den
        m_ref[0, cur, :] = m
        return carry

    def step_body(idx, carry):
        cur, prev, bias = banded_rows(idx, blocks_s)
        natural = pl.ds(idx // blocks_s + DIL_STEP * A_BLOCK * (idx % blocks_s), A_BLOCK, stride=DIL_STEP)
        num, den, m = merge(*attend(*load(1, cur, prev), bias),
                            num_ref[0, natural, :], den_ref[0, natural, :], m_ref[0, natural, :])
        num_ref[1, cur, :] = num
        den_ref[1, cur, :] = den
        m_ref[1, cur, :] = m
        gs_ref[1, cur, :] = gs_ref[0, natural, :]
        return carry

    def wide_body(idx, carry):
        cur = aligned(idx * A_BLOCK)
        by_step = pl.ds((idx // DIL_STEP) * len_s + idx % DIL_STEP, A_BLOCK, stride=DIL_STEP)
        natural = pl.ds(DIL_STEP * (idx % DIL_STEP) + idx // DIL_STEP, A_BLOCK, stride=DIL_STEP * DIL_STEP)
        num, den, _ = merge(*attend_single(cur),
                            num_ref[1, by_step, :], den_ref[1, by_step, :], m_ref[1, by_step, :])
        os_ref[natural, :] = num / den * _silu(gs_ref[1, by_step, :])
        return carry

    assert len_w == A_BLOCK
    n_blocks = seq // A_BLOCK
    for body in (narrow_body, step_body, wide_body):
        lax.fori_loop(0, n_blocks, body, 0, unroll=unroll)
    o_ref[...] = os_ref[...].astype(o_ref.dtype)


def dilated_attention(z, *, batch, seq, unroll=16):
    n_pair = A_WIDTH // LANES
    n_pat = len(A_DILATIONS)
    blk = (None, seq, LANES)
    slabs = pltpu.VMEM((n_pat, seq, LANES), F32)
    partial = pltpu.VMEM((2, seq, LANES), F32)
    return pl.pallas_call(
        functools.partial(_dilated_kernel, seq=seq, unroll=unroll),
        out_shape=jax.ShapeDtypeStruct((batch, seq, A_WIDTH), BF16),
        grid=(batch, n_pair),
        in_specs=[
            pl.BlockSpec(blk, lambda b, h: (b, 0, h)),
            pl.BlockSpec(blk, lambda b, h: (b, 0, n_pair + h)),
            pl.BlockSpec(blk, lambda b, h: (b, 0, 2 * n_pair + h)),
            pl.BlockSpec(blk, lambda b, h: (b, 0, 3 * n_pair + h)),
        ],
        out_specs=pl.BlockSpec(blk, lambda b, h: (b, 0, h)),
        scratch_shapes=[
            slabs, slabs, slabs,
            pltpu.VMEM((2, seq, LANES), F32),
            pltpu.VMEM((seq, LANES), F32),
            partial, partial, partial,
            pltpu.VMEM((2, A_BLOCK, 2 * A_BLOCK), BF16),
        ],
        compiler_params=_params("parallel", "parallel"),
        name="dilated_attention",
    )(z, z, z, z)


def _even_tail_kernel(vb_ref, halo_ref, gb_ref, a_ref, x_ref, pw32_ref, ps_ref, wo32_ref, o_ref, pw_ref, wo_ref,
                      *, tm, sub):
    _cast_weights_once([(pw32_ref, pw_ref), (wo32_ref, wo_ref)], grid_rank=2)
    ti = pl.program_id(1)
    for r in range(tm // sub):
        rows = slice(r * sub, (r + 1) * sub)
        v = vb_ref[rows, :].astype(F32)
        if r == 0:
            halo = jnp.where(ti > 0, halo_ref[...].astype(F32), 0.0)
        else:
            halo = vb_ref[r * sub - POOL_HALO:r * sub, :].astype(F32)
        xc = jnp.concatenate([halo, v], axis=0)
        s2 = xc + pltpu.roll(xc, 1, 0)
        t4 = s2[:, B_GROUP:]
        s4 = t4 + pltpu.roll(t4, 2, 0)
        t8 = s4[:, B_GROUP:]
        s8 = t8 + pltpu.roll(t8, 4, 0)
        t16 = s8[:, B_GROUP:]
        s16 = t16 + pltpu.roll(t16, 8, 0)
        sums = (s2[:, :B_GROUP], s4[:, :B_GROUP], s8[:, :B_GROUP], s16)
        pos = (ti * tm + r * sub + 1 + lax.broadcasted_iota(jnp.int32, (sub, 1), 0)).astype(F32)
        mixed = []
        for g, w in enumerate(POOL_WINDOWS):
            mean = sums[g][POOL_HALO:, :] / jnp.minimum(pos, float(w))
            pooled = mean - v[:, g * B_GROUP:(g + 1) * B_GROUP]
            mixed.append(_dot(pooled.astype(BF16), pw_ref[g]))
        b_out = jnp.concatenate(mixed, axis=1) * ps_ref[...] * _silu(gb_ref[rows, :].astype(F32))
        y = _dot(a_ref[rows, :], wo_ref[:A_WIDTH, :]) + _dot(b_out.astype(BF16), wo_ref[A_WIDTH:, :])
        o_ref[rows, :] = x_ref[rows, :] + y


def even_tail(z, a_out, x, pool_w, pool_scale, w_out, layer, *, batch, seq, tm, sub):
    vb_col = 4 * A_WIDTH // B_WIDTH
    halo_per_tile = tm // POOL_HALO
    return pl.pallas_call(
        functools.partial(_even_tail_kernel, tm=tm, sub=sub),
        out_shape=jax.ShapeDtypeStruct((batch, seq, D_MODEL), F32),
        grid=(batch, seq // tm),
        in_specs=[
            pl.BlockSpec((None, tm, B_WIDTH), lambda b, t: (b, t, vb_col)),
            pl.BlockSpec((None, POOL_HALO, B_WIDTH),
                         lambda b, t: (b, jnp.maximum(t * halo_per_tile - 1, 0), vb_col)),
            pl.BlockSpec((None, tm, B_WIDTH), lambda b, t: (b, t, vb_col + 1)),
            pl.BlockSpec((None, tm, A_WIDTH), lambda b, t: (b, t, 0)),
            pl.BlockSpec((None, tm, D_MODEL), lambda b, t: (b, t, 0)),
            _layer_block((len(POOL_WINDOWS), B_GROUP, B_GROUP), layer),
            _layer_block((1, B_WIDTH), layer),
            _layer_block((A_WIDTH + B_WIDTH, D_MODEL), layer),
        ],
        out_specs=pl.BlockSpec((None, tm, D_MODEL), lambda b, t: (b, t, 0)),
        scratch_shapes=[pltpu.VMEM((len(POOL_WINDOWS), B_GROUP, B_GROUP), BF16),
                        pltpu.VMEM((A_WIDTH + B_WIDTH, D_MODEL), BF16)],
        compiler_params=_params("arbitrary", "arbitrary"),
        name="even_tail",
    )(z, z, z, a_out, x, pool_w, pool_scale.reshape(-1, 1, B_WIDTH), w_out)


def _s5_prep_kernel(ar_ref, ai_ref, ldt_ref, br_ref, bi_ref, abr_ref, abi_ref, bbr_ref, bbi_ref):
    ar, ai = ar_ref[...], ai_ref[...]
    dt = jnp.exp(ldt_ref[...])
    mag = jnp.exp(dt * ar)
    abar_re = mag * jnp.cos(dt * ai)
    abar_im = mag * jnp.sin(dt * ai)
    nr, ni = abar_re - 1.0, abar_im
    inv = 1.0 / (ar * ar + ai * ai)
    coef_re = (nr * ar + ni * ai) * inv
    coef_im = (ni * ar - nr * ai) * inv
    br, bi = br_ref[...], bi_ref[...]
    abr_ref[...] = abar_re
    abi_ref[...] = abar_im
    bbr_ref[...] = coef_re * br - coef_im * bi
    bbi_ref[...] = coef_re * bi + coef_im * br


def _part_eye():
    return jnp.eye(S5_GROUPS // S5_PARTS, dtype=F32)


def s5_prep(a_re, a_im, log_dt, b_re, b_im):
    n_l = a_re.shape[0]
    rows = n_l * D_WIDTH
    rep = lambda t: jnp.repeat(t.reshape(n_l * S5_GROUPS, S5_STATE), S5_GROUP_DIM, axis=0)
    to_rows = lambda t: t.transpose(0, 1, 3, 2).reshape(rows, S5_STATE)
    shp = jax.ShapeDtypeStruct((rows, S5_STATE), F32)
    abr, abi, bbr, bbi = pl.pallas_call(
        _s5_prep_kernel, out_shape=(shp, shp, shp, shp), name="s5_prep",
    )(rep(a_re), rep(a_im), rep(jnp.broadcast_to(log_dt[:, :, None], (n_l, S5_GROUPS, S5_STATE))),
      to_rows(b_re), to_rows(b_im))
    gpp = S5_GROUPS // S5_PARTS

    def block_diag_in(t):
        t = t.reshape(n_l, S5_PARTS, gpp, S5_GROUP_DIM, 1, S5_STATE) * _part_eye()[None, None, :, None, :, None]
        return t.reshape(n_l, S5_PARTS, S5_PART_IN, S5_PART_STATE)

    w_b = jnp.concatenate([block_diag_in(bbr), block_diag_in(bbi)], axis=3).astype(BF16)
    abar_re = abr[::S5_GROUP_DIM].reshape(n_l, 1, S5_LANES)
    abar_im = abi[::S5_GROUP_DIM].reshape(n_l, 1, S5_LANES)
    return w_b, abar_re, abar_im


def s5_out_weights(c_re, c_im):
    n_l = c_re.shape[0]
    gpp = S5_GROUPS // S5_PARTS

    def block_diag_out(t):
        t = t.reshape(n_l, S5_PARTS, gpp, S5_GROUP_DIM, S5_STATE).transpose(0, 1, 2, 4, 3)
        t = t[:, :, :, :, None, :] * _part_eye()[None, None, :, None, :, None]
        return t.reshape(n_l, S5_PARTS, S5_PART_STATE, S5_PART_IN).astype(BF16)

    return block_diag_out(c_re), block_diag_out(c_im)


def _s5_kernel(xd_ref, wb_ref, are_ref, aim_ref, wcr_ref, wci_ref, dsk_ref, w1_32_ref, w2_32_ref,
               o_ref, u_ref, y_ref, bu_ref, h_ref, w1_ref, w2_ref, *, lc, nb, lane_chunk):
    @pl.when(pl.program_id(0) == 0)
    def _():
        h_ref[...] = jnp.zeros_like(h_ref)

    _cast_weights_once([(w1_32_ref, w1_ref), (w2_32_ref, w2_ref)], grid_rank=1)
    n_slab = D_WIDTH // LANES
    for b in range(nb):
        for s in range(n_slab):
            col = b * 2 * D_WIDTH + s * LANES
            u_ref[s, pl.ds(b, lc, stride=nb), :] = xd_ref[:, col:col + LANES].astype(F32)
    u = jnp.concatenate([u_ref[s] for s in range(n_slab)], axis=1)
    ub = u.astype(BF16)
    width = 2 * S5_PART_STATE
    def project(part):
        bu_ref[:, part * width:(part + 1) * width] = _dot(
            ub[:, part * S5_PART_IN:(part + 1) * S5_PART_IN], wb_ref[part])

    def scan(part):
        for c in range(S5_PART_STATE // lane_chunk):
            re = slice(part * width + c * lane_chunk, part * width + (c + 1) * lane_chunk)
            im = slice(re.start + S5_PART_STATE, re.stop + S5_PART_STATE)
            ab = slice(part * S5_PART_STATE + c * lane_chunk, part * S5_PART_STATE + (c + 1) * lane_chunk)
            a_r = jnp.broadcast_to(are_ref[:, ab], (nb, lane_chunk))
            a_i = jnp.broadcast_to(aim_ref[:, ab], (nb, lane_chunk))
            h_r, h_i = h_ref[:, re], h_ref[:, im]
            for t in range(lc):
                row = slice(t * nb, (t + 1) * nb)
                h_r, h_i = (a_r * h_r - a_i * h_i + bu_ref[row, re], a_r * h_i + a_i * h_r + bu_ref[row, im])
                bu_ref[row, re] = h_r
                bu_ref[row, im] = h_i
            h_ref[:, re] = h_r
            h_ref[:, im] = h_i

    def readout(part):
        re = slice(part * width, part * width + S5_PART_STATE)
        im = slice(re.stop, re.stop + S5_PART_STATE)
        return _dot(bu_ref[:, re].astype(BF16), wcr_ref[part]) - _dot(bu_ref[:, im].astype(BF16), wci_ref[part])

    ys = []
    project(0)
    for part in range(S5_PARTS):
        if part + 1 < S5_PARTS:
            project(part + 1)
        scan(part)
        ys.append(readout(part))
    y = jnp.concatenate(ys, axis=1) + dsk_ref[...] * u
    y = jax.nn.gelu(y).astype(BF16)
    out = _dot(y, w1_ref[...]) * jax.nn.sigmoid(_dot(y, w2_ref[...]))
    for s in range(n_slab):
        y_ref[s] = out[:, s * LANES:(s + 1) * LANES]
    o_ref[...] = jnp.concatenate(
        [y_ref[s, pl.ds(b, lc, stride=nb), :] for b in range(nb) for s in range(n_slab)], axis=1).astype(o_ref.dtype)


def s5_layer(zd, w_b, abar_re, abar_im, w_cr, w_ci, d_skip, w1, w2, layer, *, batch, seq, lc):
    rows = lc * batch
    n_slab = D_WIDTH // LANES
    return pl.pallas_call(
        functools.partial(_s5_kernel, lc=lc, nb=batch, lane_chunk=512),
        out_shape=jax.ShapeDtypeStruct((seq, batch * D_WIDTH), BF16),
        grid=(seq // lc,),
        in_specs=[
            pl.BlockSpec((lc, batch * 2 * D_WIDTH), lambda c: (c, 0)),
            _layer_block((S5_PARTS, S5_PART_IN, 2 * S5_PART_STATE), layer),
            _layer_block((1, S5_LANES), layer),
            _layer_block((1, S5_LANES), layer),
            _layer_block((S5_PARTS, S5_PART_STATE, S5_PART_IN), layer),
            _layer_block((S5_PARTS, S5_PART_STATE, S5_PART_IN), layer),
            _layer_block((1, D_WIDTH), layer),
            _layer_block((D_WIDTH, D_WIDTH), layer),
            _layer_block((D_WIDTH, D_WIDTH), layer),
        ],
        out_specs=pl.BlockSpec((lc, batch * D_WIDTH), lambda c: (c, 0)),
        scratch_shapes=[
            pltpu.VMEM((n_slab, rows, LANES), F32),
            pltpu.VMEM((n_slab, rows, LANES), F32),
            pltpu.VMEM((rows, 2 * S5_LANES), F32),
            pltpu.VMEM((batch, 2 * S5_LANES), F32),
            pltpu.VMEM((D_WIDTH, D_WIDTH), BF16),
            pltpu.VMEM((D_WIDTH, D_WIDTH), BF16),
        ],
        compiler_params=_params("arbitrary"),
        name="s5_layer",
    )(zd, w_b, abar_re, abar_im, w_cr, w_ci, d_skip.reshape(-1, 1, D_WIDTH), w1, w2)


def _odd_tail_kernel(u_ref, v_ref, gc_ref, d_ref, gd_ref, x_ref, lng_ref, lnb_ref, ws_ref, bs_ref, wo32_ref,
                     o_ref, c_ref, wo_ref, *, tm, sub):
    _cast_weights_once([(wo32_ref, wo_ref)], grid_rank=2)
    ri = lax.broadcasted_iota(jnp.int32, (C_CHUNK, C_CHUNK), 0)
    ci = lax.broadcasted_iota(jnp.int32, (C_CHUNK, C_CHUNK), 1)
    causal = ri >= ci
    w_s = [jnp.where(causal, ws_ref[g], 0.0).astype(BF16) for g in range(C_GROUPS)]
    for r in range(tm // sub):
        tile = slice(r * sub, (r + 1) * sub)
        v = v_ref[tile, :].astype(F32)
        mu = jnp.mean(v, axis=-1, keepdims=True)
        vc = v - mu
        var = jnp.mean(vc * vc, axis=-1, keepdims=True)
        vn = (vc * lax.rsqrt(var + EPS) * lng_ref[...] + lnb_ref[...]).astype(BF16)
        for g in range(C_GROUPS):
            cols = slice(g * C_GROUP_DIM, (g + 1) * C_GROUP_DIM)
            b_g = bs_ref[:, g:g + 1]
            for c in range(sub // C_CHUNK):
                rows = slice(r * sub + c * C_CHUNK, r * sub + (c + 1) * C_CHUNK)
                mixed = _dot(w_s[g], vn[c * C_CHUNK:(c + 1) * C_CHUNK, cols]) + b_g
                gate = _silu(gc_ref[rows, cols].astype(F32))
                c_ref[rows, cols] = (u_ref[rows, cols].astype(F32) * mixed * gate).astype(BF16)
        d = (d_ref[tile, :].astype(F32) * _silu(gd_ref[tile, :].astype(F32))).astype(BF16)
        y = _dot(c_ref[tile, :], wo_ref[:C_WIDTH, :]) + _dot(d, wo_ref[C_WIDTH:, :])
        o_ref[tile, :] = x_ref[tile, :] + y


def odd_tail(zc, zd, d_out, x, ln_g, ln_b, w_s, b_s, w_out, layer, *, batch, seq, tm, sub):
    return pl.pallas_call(
        functools.partial(_odd_tail_kernel, tm=tm, sub=sub),
        out_shape=jax.ShapeDtypeStruct((batch, seq, D_MODEL), F32),
        grid=(batch, seq // tm),
        in_specs=[
            pl.BlockSpec((None, tm, C_WIDTH), lambda b, t: (b, t, 0)),
            pl.BlockSpec((None, tm, C_WIDTH), lambda b, t: (b, t, 1)),
            pl.BlockSpec((None, tm, C_WIDTH), lambda b, t: (b, t, 2)),
            pl.BlockSpec((tm, D_WIDTH), lambda b, t: (t, b)),
            pl.BlockSpec((tm, D_WIDTH), lambda b, t: (t, 2 * b + 1)),
            pl.BlockSpec((None, tm, D_MODEL), lambda b, t: (b, t, 0)),
            _layer_block((1, C_WIDTH), layer),
            _layer_block((1, C_WIDTH), layer),
            _layer_block((C_GROUPS, C_CHUNK, C_CHUNK), layer),
            _layer_block((C_CHUNK, C_GROUPS), layer),
            _layer_block((C_WIDTH + D_WIDTH, D_MODEL), layer),
        ],
        out_specs=pl.BlockSpec((None, tm, D_MODEL), lambda b, t: (b, t, 0)),
        scratch_shapes=[pltpu.VMEM((tm, C_WIDTH), BF16), pltpu.VMEM((C_WIDTH + D_WIDTH, D_MODEL), BF16)],
        compiler_params=_params("arbitrary", "arbitrary"),
        name="odd_tail",
    )(zc, zc, zc, d_out, zd, x, ln_g.reshape(-1, 1, C_WIDTH), ln_b.reshape(-1, 1, C_WIDTH), w_s,
      b_s.transpose(0, 2, 1), w_out)


def _xattn_kernel(x_ref, g_ref, wq32_ref, kv_ref, wo32_ref, pg_ref, *refs, final, sub):
    if final:
        o_ref, wq_ref, wo_ref = refs
    else:
        o_ref, xn_ref, wq_ref, wo_ref = refs
    _cast_weights_once([(wq32_ref, wq_ref), (wo32_ref, wo_ref)], grid_rank=2)
    for r in range(x_ref.shape[0] // sub):
        rows = slice(r * sub, (r + 1) * sub)
        x = x_ref[rows, :]
        q = _dot(_rms(x, g_ref[...]).astype(BF16), wq_ref[...])
        heads = []
        for h in range(X_HEADS):
            ks = slice(h * X_HEAD_DIM, (h + 1) * X_HEAD_DIM)
            vs = slice(D_MODEL + h * X_HEAD_DIM, D_MODEL + (h + 1) * X_HEAD_DIM)
            s = _dot_nt(q[:, ks].astype(BF16), kv_ref[:, ks]) * (X_HEAD_DIM ** -0.5)
            m = jnp.max(s, axis=-1, keepdims=True)
            p = jnp.exp(s - m)
            den = jnp.sum(p, axis=-1, keepdims=True)
            heads.append((_dot(p.astype(BF16), kv_ref[:, vs]) / den).astype(BF16))
        y = x + _dot(jnp.concatenate(heads, axis=1), wo_ref[...])
        if final:
            o_ref[rows, :] = _rms(y, pg_ref[...])
        else:
            o_ref[rows, :] = y
            xn_ref[rows, :] = _rms(y, pg_ref[...]).astype(xn_ref.dtype)


def cross_attention(x, g, w_q, kv, w_o, post_g, layer, *, batch, seq, mem_len, tq, sub, final):
    tile = pl.BlockSpec((None, tq, D_MODEL), lambda b, t: (b, t, 0))
    res = jax.ShapeDtypeStruct((batch, seq, D_MODEL), F32)
    nxt = jax.ShapeDtypeStruct((batch, seq, D_MODEL), BF16)
    return pl.pallas_call(
        functools.partial(_xattn_kernel, final=final, sub=sub),
        out_shape=res if final else (res, nxt),
        grid=(batch, seq // tq),
        in_specs=[
            tile,
            _layer_block((1, D_MODEL), layer),
            _layer_block((D_MODEL, D_MODEL), layer),
            pl.BlockSpec((None, None, mem_len, 2 * D_MODEL), lambda b, t: (layer, b, 0, 0)),
            _layer_block((D_MODEL, D_MODEL), layer),
            pl.BlockSpec((1, D_MODEL), lambda b, t: (0, 0)),
        ],
        out_specs=tile if final else (tile, tile),
        scratch_shapes=[pltpu.VMEM((D_MODEL, D_MODEL), BF16), pltpu.VMEM((D_MODEL, D_MODEL), BF16)],
        compiler_params=_params("arbitrary", "arbitrary"),
        name="cross_attention",
    )(x, g.reshape(-1, 1, D_MODEL), w_q, kv, w_o, post_g.reshape(1, D_MODEL))


def kernel(x, mem, norm_ab, w_in_ab, pool_w, pool_scale, w_out_ab, norm_cd, w_in_cd, sgu_ln_g, sgu_ln_b,
           sgu_w, sgu_b, s5_a_re, s5_a_im, s5_log_dt, s5_b_re, s5_b_im, s5_c_re, s5_c_im, s5_d,
           glu_w1, glu_w2, w_out_cd, norm_x, w_xq, w_xkv, w_xo, mem_norm, final_norm):
    batch, seq, _ = x.shape
    mem_len = mem.shape[1]
    depth = norm_x.shape[0]
    tokens = batch * seq
    tm_proj = 1024
    tiles_per_row = seq // tm_proj
    mixer_norm = lambda layer: (norm_ab if layer % 2 == 0 else norm_cd)[layer // 2]

    kv = mem_kv_proj(mem.reshape(batch * mem_len, D_MODEL), mem_norm, w_xkv, tn=1024)
    kv = kv.reshape(depth, batch, mem_len, 2 * D_MODEL)
    w_b, abar_re, abar_im = s5_prep(s5_a_re, s5_a_im, s5_log_dt, s5_b_re, s5_b_im)
    w_cr, w_ci = s5_out_weights(s5_c_re, s5_c_im)
    xn = rms_cast(x.reshape(tokens, D_MODEL), mixer_norm(0), tm=tm_proj)
    for layer in range(depth):
        i = layer // 2
        if layer % 2 == 0:
            z = in_proj(xn, w_in_ab, i, n=w_in_ab.shape[2], tm=tm_proj, tn=2048)
            z = z.reshape(batch, seq, z.shape[1])
            a_out = dilated_attention(z, batch=batch, seq=seq)
            x = even_tail(z, a_out, x, pool_w, pool_scale, w_out_ab, i, batch=batch, seq=seq,
                          tm=1024, sub=512)
        else:
            n_c = 3 * C_WIDTH
            zc = in_proj(xn, w_in_cd, i, n=n_c, tm=tm_proj, tn=1024)
            zd = in_proj(xn, w_in_cd, i, n=2 * D_WIDTH, col0=n_c, tm=tm_proj, tn=2 * D_WIDTH,
                         out_shape=(seq, batch * 2 * D_WIDTH),
                         out_map=lambda r, j: (r % tiles_per_row, r // tiles_per_row))
            d_out = s5_layer(zd, w_b, abar_re, abar_im, w_cr, w_ci, s5_d, glu_w1, glu_w2, i,
                             batch=batch, seq=seq, lc=64)
            x = odd_tail(zc.reshape(batch, seq, n_c), zd, d_out, x, sgu_ln_g, sgu_ln_b, sgu_w,
                         sgu_b, w_out_cd, i, batch=batch, seq=seq, tm=1024, sub=512)
        final = layer == depth - 1
        out = cross_attention(x, norm_x, w_xq, kv, w_xo, final_norm if final else mixer_norm(layer + 1),
                              layer, batch=batch, seq=seq, mem_len=mem_len, tq=1024, sub=512, final=final)
        if final:
            return out
        x, xn = out[0], out[1].reshape(tokens, D_MODEL)
```

```python
import functools

import jax
import jax.numpy as jnp
from jax import lax
from jax.experimental import pallas as pl
from jax.experimental.pallas import tpu as pltpu

F32 = jnp.float32
BF16 = jnp.bfloat16

LANES = 128
D_MODEL = 1024
A_WIDTH = 1024
A_HEAD_DIM = 64
A_BLOCK = 128
A_DILATIONS = (1, 4, 16)
DIL_STEP = 4
B_WIDTH = 1024
POOL_WINDOWS = (2, 4, 8, 16)
B_GROUP = 256
POOL_HALO = 16
C_WIDTH = 1024
C_CHUNK = 128
C_GROUPS = 4
C_GROUP_DIM = 256
D_WIDTH = 512
S5_GROUPS = 32
S5_GROUP_DIM = 16
S5_STATE = 64
S5_LANES = S5_GROUPS * S5_STATE
S5_PARTS = 2
S5_PART_IN = D_WIDTH // S5_PARTS
S5_PART_STATE = S5_LANES // S5_PARTS
X_HEADS = 4
X_HEAD_DIM = 256
EPS = 1e-6
NEG = -1e30
LOG2E = 1.4426950408889634

VMEM_LIMIT = 48 * 1024 * 1024


def _params(*sem):
    return pltpu.CompilerParams(dimension_semantics=sem, vmem_limit_bytes=VMEM_LIMIT)


def _rms(x, g):
    ms = jnp.mean(x * x, axis=-1, keepdims=True)
    return x * lax.rsqrt(ms + EPS) * g


def _silu(x):
    return x * jax.nn.sigmoid(x)


def _dot(a, b):
    return jnp.dot(a, b, preferred_element_type=F32)


def _layer_block(shape, layer):
    return pl.BlockSpec((None,) + tuple(shape), lambda *_: (layer,) + (0,) * len(shape),
                        pipeline_mode=pl.Buffered(1))


def _cast_weights_once(pairs, grid_rank):
    first = pl.program_id(0) == 0
    for axis in range(1, grid_rank):
        first = first & (pl.program_id(axis) == 0)

    @pl.when(first)
    def _():
        for src, dst in pairs:
            dst[...] = src[...].astype(dst.dtype)


def _dot_nt(a, b):
    return lax.dot_general(a, b, (((1,), (1,)), ((), ())), preferred_element_type=F32)


def _rms_cast_kernel(x_ref, g_ref, o_ref):
    o_ref[...] = _rms(x_ref[...], g_ref[...]).astype(o_ref.dtype)


def rms_cast(x, g, *, tm):
    m, k = x.shape
    return pl.pallas_call(
        _rms_cast_kernel,
        out_shape=jax.ShapeDtypeStruct((m, k), BF16),
        grid=(m // tm,),
        in_specs=[pl.BlockSpec((tm, k), lambda i: (i, 0)), pl.BlockSpec((1, k), lambda i: (0, 0))],
        out_specs=pl.BlockSpec((tm, k), lambda i: (i, 0)),
        compiler_params=_params("parallel"),
        name="rms_cast",
    )(x, g.reshape(1, k))


def _in_proj_kernel(x_ref, w32_ref, o_ref, w_ref):
    @pl.when(pl.program_id(1) == 0)
    def _():
        w_ref[...] = w32_ref[...].astype(BF16)

    o_ref[...] = _dot(x_ref[...], w_ref[...]).astype(o_ref.dtype)


def in_proj(xn, w, layer, *, n, tm, tn, col0=0, out_shape=None, out_map=None):
    m, k = xn.shape
    col_block0 = col0 // tn
    if out_shape is None:
        out_shape, out_map = (m, n), (lambda i, j: (i, j))
    return pl.pallas_call(
        _in_proj_kernel,
        out_shape=jax.ShapeDtypeStruct(out_shape, BF16),
        grid=(n // tn, m // tm),
        in_specs=[
            pl.BlockSpec((tm, k), lambda j, i: (i, 0)),
            pl.BlockSpec((None, k, tn), lambda j, i: (layer, 0, col_block0 + j)),
        ],
        out_specs=pl.BlockSpec((tm, tn), lambda j, i: out_map(i, j)),
        scratch_shapes=[pltpu.VMEM((k, tn), BF16)],
        compiler_params=_params("arbitrary", "arbitrary"),
        name="in_proj",
    )(xn, w)


def _mem_proj_kernel(x_ref, g_ref, w_ref, o_ref, xn_ref):
    @pl.when((pl.program_id(0) == 0) & (pl.program_id(1) == 0))
    def _():
        xn_ref[...] = _rms(x_ref[...], g_ref[...]).astype(BF16)

    o_ref[...] = _dot(xn_ref[...], w_ref[...].astype(BF16)).astype(o_ref.dtype)


def mem_kv_proj(mem, g, w, *, tn):
    m, k = mem.shape
    n_layers, _, n = w.shape
    return pl.pallas_call(
        _mem_proj_kernel,
        out_shape=jax.ShapeDtypeStruct((n_layers, m, n), BF16),
        grid=(n_layers, n // tn),
        in_specs=[
            pl.BlockSpec((m, k), lambda l, j: (0, 0)),
            pl.BlockSpec((1, k), lambda l, j: (0, 0)),
            pl.BlockSpec((None, k, tn), lambda l, j: (l, 0, j)),
        ],
        out_specs=pl.BlockSpec((None, m, tn), lambda l, j: (l, 0, j)),
        scratch_shapes=[pltpu.VMEM((m, k), BF16)],
        compiler_params=_params("arbitrary", "arbitrary"),
        name="mem_kv_proj",
    )(mem, g.reshape(1, k), w)


def _dilated_kernel(q_ref, k_ref, v_ref, g_ref, o_ref, qs_ref, ks_ref, vs_ref, gs_ref, os_ref, num_ref, den_ref,
                    m_ref, bias_ref, *, seq, unroll):
    assert A_DILATIONS == (1, DIL_STEP, DIL_STEP * DIL_STEP)
    qi = lax.broadcasted_iota(jnp.int32, (A_BLOCK, 2 * A_BLOCK), 0)
    kj = lax.broadcasted_iota(jnp.int32, (A_BLOCK, 2 * A_BLOCK), 1)
    dist = qi + A_BLOCK - kj
    band = (dist >= 0) & (dist <= A_BLOCK)
    bias_ref[1] = jnp.where(band, 0.0, NEG).astype(BF16)
    bias_ref[0] = jnp.where(band & (kj >= A_BLOCK), 0.0, NEG).astype(BF16)

    len_s = seq // DIL_STEP
    len_w = len_s // DIL_STEP
    blocks_s = len_s // A_BLOCK
    q_scale = A_HEAD_DIM ** -0.5 * LOG2E
    for src, dst, scale in ((q_ref, qs_ref, q_scale), (k_ref, ks_ref, None), (v_ref, vs_ref, None)):
        nat = src[...].astype(F32)
        dst[0] = nat if scale is None else nat * scale
        for b in range(DIL_STEP):
            dst[1, b * len_s:(b + 1) * len_s, :] = dst[0, pl.ds(b, len_s, stride=DIL_STEP), :]
        for c in range(DIL_STEP * DIL_STEP):
            b, a = divmod(c, DIL_STEP)
            dst[2, c * len_w:(c + 1) * len_w, :] = dst[1, pl.ds(b * len_s + a, len_w, stride=DIL_STEP), :]
    gs_ref[0] = g_ref[...].astype(F32)

    head0 = lax.broadcasted_iota(jnp.int32, (A_BLOCK, LANES), 1) < A_HEAD_DIM

    def load(slab, cur, prev):
        q = qs_ref[slab, cur, :].astype(BF16)
        if slab == 0:
            kk = jnp.concatenate([k_ref[prev, :], k_ref[cur, :]], axis=0)
            vv = jnp.concatenate([v_ref[prev, :], v_ref[cur, :]], axis=0)
            return q, kk, vv
        kk = jnp.concatenate([ks_ref[slab, prev, :], ks_ref[slab, cur, :]], axis=0).astype(BF16)
        vv = jnp.concatenate([vs_ref[slab, prev, :], vs_ref[slab, cur, :]], axis=0).astype(BF16)
        return q, kk, vv

    one_b, zero_b = jnp.ones((), BF16), jnp.zeros((), BF16)

    def attend(q, k, v, bias):
        own0 = lax.broadcasted_iota(jnp.int32, k.shape, 1).astype(BF16) < A_HEAD_DIM
        k_bd = jnp.concatenate([jnp.where(own0, k, zero_b), jnp.where(own0, zero_b, k)], axis=0)
        s = _dot_nt(q, k_bd).astype(BF16) + jnp.concatenate([bias, bias], axis=1)
        s0, s1 = s[:, :k.shape[0]], s[:, k.shape[0]:]
        m0 = jnp.max(s0, axis=-1, keepdims=True)
        m1 = jnp.max(s1, axis=-1, keepdims=True)
        p = jnp.concatenate([jnp.exp2(s0 - m0), jnp.exp2(s1 - m1)], axis=1)
        v_bd = jnp.concatenate([
            jnp.concatenate([jnp.where(own0, v, zero_b), jnp.where(own0, one_b, zero_b)], axis=1),
            jnp.concatenate([jnp.where(own0, zero_b, v), jnp.where(own0, zero_b, one_b)], axis=1)], axis=0)
        res = _dot(p, v_bd)
        m = jnp.where(head0, jnp.broadcast_to(m0, (A_BLOCK, LANES)).astype(F32),
                      jnp.broadcast_to(m1, (A_BLOCK, LANES)).astype(F32))
        return res[:, :LANES], res[:, LANES:], m

    def attend_single(cur):
        return attend(qs_ref[2, cur, :].astype(BF16), ks_ref[2, cur, :].astype(BF16),
                      vs_ref[2, cur, :].astype(BF16), bias_ref[0][:, A_BLOCK:])

    def aligned(start):
        return pl.ds(pl.multiple_of(start, A_BLOCK), A_BLOCK)

    def banded_rows(idx, blocks_per_seq):
        start = idx * A_BLOCK
        first = idx % blocks_per_seq == 0
        return aligned(start), aligned(jnp.where(first, start, start - A_BLOCK)), bias_ref[jnp.where(first, 0, 1)]

    def merge(num_a, den_a, m_a, num_b, den_b, m_b):
        m = jnp.maximum(m_a, m_b)
        w_a = jnp.exp2(m_a - m)
        w_b = jnp.exp2(m_b - m)
        return w_a * num_a + w_b * num_b, w_a * den_a + w_b * den_b, m

    def narrow_body(idx, carry):
        cur, prev, bias = banded_rows(idx, n_blocks)
        num, den, m = attend(*load(0, cur, prev), bias)
        num_ref[0, cur, :] = num
        den_ref[0, cur, :] = den
        m_ref[0, cur, :] = m
        return carry

    def step_body(idx, carry):
        cur, prev, bias = banded_rows(idx, blocks_s)
        natural = pl.ds(idx // blocks_s + DIL_STEP * A_BLOCK * (idx % blocks_s), A_BLOCK, stride=DIL_STEP)
        num, den, m = merge(*attend(*load(1, cur, prev), bias),
                            num_ref[0, natural, :], den_ref[0, natural, :], m_ref[0, natural, :])
        num_ref[1, cur, :] = num
        den_ref[1, cur, :] = den
        m_ref[1, cur, :] = m
        gs_ref[1, cur, :] = gs_ref[0, natural, :]
        return carry

    def wide_body(idx, carry):
        cur = aligned(idx * A_BLOCK)
        by_step = pl.ds((idx // DIL_STEP) * len_s + idx % DIL_STEP, A_BLOCK, stride=DIL_STEP)
        natural = pl.ds(DIL_STEP * (idx % DIL_STEP) + idx // DIL_STEP, A_BLOCK, stride=DIL_STEP * DIL_STEP)
        num, den, _ = merge(*attend_single(cur),
                            num_ref[1, by_step, :], den_ref[1, by_step, :], m_ref[1, by_step, :])
        os_ref[natural, :] = num / den * _silu(gs_ref[1, by_step, :])
        return carry

    assert len_w == A_BLOCK
    n_blocks = seq // A_BLOCK
    for body in (narrow_body, step_body, wide_body):
        lax.fori_loop(0, n_blocks, body, 0, unroll=unroll)
    o_ref[...] = os_ref[...].astype(o_ref.dtype)


def dilated_attention(z, *, batch, seq, unroll=16):
    n_pair = A_WIDTH // LANES
    n_pat = len(A_DILATIONS)
    blk = (None, seq, LANES)
    slabs = pltpu.VMEM((n_pat, seq, LANES), F32)
    partial = pltpu.VMEM((2, seq, LANES), F32)
    return pl.pallas_call(
        functools.partial(_dilated_kernel, seq=seq, unroll=unroll),
        out_shape=jax.ShapeDtypeStruct((batch, seq, A_WIDTH), BF16),
        grid=(batch, n_pair),
        in_specs=[
            pl.BlockSpec(blk, lambda b, h: (b, 0, h)),
            pl.BlockSpec(blk, lambda b, h: (b, 0, n_pair + h)),
            pl.BlockSpec(blk, lambda b, h: (b, 0, 2 * n_pair + h)),
            pl.BlockSpec(blk, lambda b, h: (b, 0, 3 * n_pair + h)),
        ],
        out_specs=pl.BlockSpec(blk, lambda b, h: (b, 0, h)),
        scratch_shapes=[
            slabs, slabs, slabs,
            pltpu.VMEM((2, seq, LANES), F32),
            pltpu.VMEM((seq, LANES), F32),
            partial, partial, partial,
            pltpu.VMEM((2, A_BLOCK, 2 * A_BLOCK), BF16),
        ],
        compiler_params=_params("parallel", "parallel"),
        name="dilated_attention",
    )(z, z, z, z)


def _even_tail_kernel(vb_ref, halo_ref, gb_ref, a_ref, x_ref, pw32_ref, ps_ref, wo32_ref, o_ref, pw_ref, wo_ref,
                      *, tm, sub):
    _cast_weights_once([(pw32_ref, pw_ref), (wo32_ref, wo_ref)], grid_rank=2)
    ti = pl.program_id(1)
    for r in range(tm // sub):
        rows = slice(r * sub, (r + 1) * sub)
        v = vb_ref[rows, :].astype(F32)
        if r == 0:
            halo = jnp.where(ti > 0, halo_ref[...].astype(F32), 0.0)
        else:
            halo = vb_ref[r * sub - POOL_HALO:r * sub, :].astype(F32)
        xc = jnp.concatenate([halo, v], axis=0)
        s2 = xc + pltpu.roll(xc, 1, 0)
        t4 = s2[:, B_GROUP:]
        s4 = t4 + pltpu.roll(t4, 2, 0)
        t8 = s4[:, B_GROUP:]
        s8 = t8 + pltpu.roll(t8, 4, 0)
        t16 = s8[:, B_GROUP:]
        s16 = t16 + pltpu.roll(t16, 8, 0)
        sums = (s2[:, :B_GROUP], s4[:, :B_GROUP], s8[:, :B_GROUP], s16)
        pos = (ti * tm + r * sub + 1 + lax.broadcasted_iota(jnp.int32, (sub, 1), 0)).astype(F32)
        mixed = []
        for g, w in enumerate(POOL_WINDOWS):
            mean = sums[g][POOL_HALO:, :] / jnp.minimum(pos, float(w))
            pooled = mean - v[:, g * B_GROUP:(g + 1) * B_GROUP]
            mixed.append(_dot(pooled.astype(BF16), pw_ref[g]))
        b_out = jnp.concatenate(mixed, axis=1) * ps_ref[...] * _silu(gb_ref[rows, :].astype(F32))
        y = _dot(a_ref[rows, :], wo_ref[:A_WIDTH, :]) + _dot(b_out.astype(BF16), wo_ref[A_WIDTH:, :])
        o_ref[rows, :] = x_ref[rows, :] + y


def even_tail(z, a_out, x, pool_w, pool_scale, w_out, layer, *, batch, seq, tm, sub):
    vb_col = 4 * A_WIDTH // B_WIDTH
    halo_per_tile = tm // POOL_HALO
    return pl.pallas_call(
        functools.partial(_even_tail_kernel, tm=tm, sub=sub),
        out_shape=jax.ShapeDtypeStruct((batch, seq, D_MODEL), F32),
        grid=(batch, seq // tm),
        in_specs=[
            pl.BlockSpec((None, tm, B_WIDTH), lambda b, t: (b, t, vb_col)),
            pl.BlockSpec((None, POOL_HALO, B_WIDTH),
                         lambda b, t: (b, jnp.maximum(t * halo_per_tile - 1, 0), vb_col)),
            pl.BlockSpec((None, tm, B_WIDTH), lambda b, t: (b, t, vb_col + 1)),
            pl.BlockSpec((None, tm, A_WIDTH), lambda b, t: (b, t, 0)),
            pl.BlockSpec((None, tm, D_MODEL), lambda b, t: (b, t, 0)),
            _layer_block((len(POOL_WINDOWS), B_GROUP, B_GROUP), layer),
            pl.BlockSpec((1, B_WIDTH), lambda b, t: (0, 0)),
            _layer_block((A_WIDTH + B_WIDTH, D_MODEL), layer),
        ],
        out_specs=pl.BlockSpec((None, tm, D_MODEL), lambda b, t: (b, t, 0)),
        scratch_shapes=[pltpu.VMEM((len(POOL_WINDOWS), B_GROUP, B_GROUP), BF16),
                        pltpu.VMEM((A_WIDTH + B_WIDTH, D_MODEL), BF16)],
        compiler_params=_params("arbitrary", "arbitrary"),
        name="even_tail",
    )(z, z, z, a_out, x, pool_w, pool_scale.reshape(1, B_WIDTH), w_out)


def _s5_prep_kernel(ar_ref, ai_ref, ldt_ref, br_ref, bi_ref, abr_ref, abi_ref, bbr_ref, bbi_ref):
    ar, ai = ar_ref[...], ai_ref[...]
    dt = jnp.exp(ldt_ref[...])
    mag = jnp.exp(dt * ar)
    abar_re = mag * jnp.cos(dt * ai)
    abar_im = mag * jnp.sin(dt * ai)
    nr, ni = abar_re - 1.0, abar_im
    inv = 1.0 / (ar * ar + ai * ai)
    coef_re = (nr * ar + ni * ai) * inv
    coef_im = (ni * ar - nr * ai) * inv
    br, bi = br_ref[...], bi_ref[...]
    abr_ref[...] = abar_re
    abi_ref[...] = abar_im
    bbr_ref[...] = coef_re * br - coef_im * bi
    bbi_ref[...] = coef_re * bi + coef_im * br


def _part_eye():
    return jnp.eye(S5_GROUPS // S5_PARTS, dtype=F32)


def s5_prep(a_re, a_im, log_dt, b_re, b_im):
    n_l = a_re.shape[0]
    rows = n_l * D_WIDTH
    rep = lambda t: jnp.repeat(t.reshape(n_l * S5_GROUPS, S5_STATE), S5_GROUP_DIM, axis=0)
    to_rows = lambda t: t.transpose(0, 1, 3, 2).reshape(rows, S5_STATE)
    shp = jax.ShapeDtypeStruct((rows, S5_STATE), F32)
    abr, abi, bbr, bbi = pl.pallas_call(
        _s5_prep_kernel, out_shape=(shp, shp, shp, shp), name="s5_prep",
    )(rep(a_re), rep(a_im), rep(jnp.broadcast_to(log_dt[:, :, None], (n_l, S5_GROUPS, S5_STATE))),
      to_rows(b_re), to_rows(b_im))
    gpp = S5_GROUPS // S5_PARTS

    def block_diag_in(t):
        t = t.reshape(n_l, S5_PARTS, gpp, S5_GROUP_DIM, 1, S5_STATE) * _part_eye()[None, None, :, None, :, None]
        return t.reshape(n_l, S5_PARTS, S5_PART_IN, S5_PART_STATE)

    w_b = jnp.concatenate([block_diag_in(bbr), block_diag_in(bbi)], axis=3).astype(BF16)
    abar_re = abr[::S5_GROUP_DIM].reshape(n_l, 1, S5_LANES)
    abar_im = abi[::S5_GROUP_DIM].reshape(n_l, 1, S5_LANES)
    return w_b, abar_re, abar_im


def s5_out_weights(c_re, c_im):
    n_l = c_re.shape[0]
    gpp = S5_GROUPS // S5_PARTS

    def block_diag_out(t):
        t = t.reshape(n_l, S5_PARTS, gpp, S5_GROUP_DIM, S5_STATE).transpose(0, 1, 2, 4, 3)
        t = t[:, :, :, :, None, :] * _part_eye()[None, None, :, None, :, None]
        return t.reshape(n_l, S5_PARTS, S5_PART_STATE, S5_PART_IN).astype(BF16)

    return block_diag_out(c_re), block_diag_out(c_im)


def _s5_kernel(xd_ref, wb_ref, are_ref, aim_ref, wcr_ref, wci_ref, dsk_ref, w1_32_ref, w2_32_ref,
               o_ref, u_ref, y_ref, bu_ref, h_ref, w1_ref, w2_ref, *, lc, nb, lane_chunk):
    @pl.when(pl.program_id(0) == 0)
    def _():
        h_ref[...] = jnp.zeros_like(h_ref)

    _cast_weights_once([(w1_32_ref, w1_ref), (w2_32_ref, w2_ref)], grid_rank=1)
    n_slab = D_WIDTH // LANES
    for b in range(nb):
        for s in range(n_slab):
            col = b * 2 * D_WIDTH + s * LANES
            u_ref[s, pl.ds(b, lc, stride=nb), :] = xd_ref[:, col:col + LANES].astype(F32)
    u = jnp.concatenate([u_ref[s] for s in range(n_slab)], axis=1)
    ub = u.astype(BF16)
    width = 2 * S5_PART_STATE
    def project(part):
        bu_ref[:, part * width:(part + 1) * width] = _dot(
            ub[:, part * S5_PART_IN:(part + 1) * S5_PART_IN], wb_ref[part])

    def scan(part):
        for c in range(S5_PART_STATE // lane_chunk):
            re = slice(part * width + c * lane_chunk, part * width + (c + 1) * lane_chunk)
            im = slice(re.start + S5_PART_STATE, re.stop + S5_PART_STATE)
            ab = slice(part * S5_PART_STATE + c * lane_chunk, part * S5_PART_STATE + (c + 1) * lane_chunk)
            a_r = jnp.broadcast_to(are_ref[:, ab], (nb, lane_chunk))
            a_i = jnp.broadcast_to(aim_ref[:, ab], (nb, lane_chunk))
            h_r, h_i = h_ref[:, re], h_ref[:, im]
            for t in range(lc):
                row = slice(t * nb, (t + 1) * nb)
                h_r, h_i = (a_r * h_r - a_i * h_i + bu_ref[row, re], a_r * h_i + a_i * h_r + bu_ref[row, im])
                bu_ref[row, re] = h_r
                bu_ref[row, im] = h_i
            h_ref[:, re] = h_r
            h_ref[:, im] = h_i

    def readout(part):
        re = slice(part * width, part * width + S5_PART_STATE)
        im = slice(re.stop, re.stop + S5_PART_STATE)
        return _dot(bu_ref[:, re].astype(BF16), wcr_ref[part]) - _dot(bu_ref[:, im].astype(BF16), wci_ref[part])

    ys = []
    project(0)
    for part in range(S5_PARTS):
        if part + 1 < S5_PARTS:
            project(part + 1)
        scan(part)
        ys.append(readout(part))
    y = jnp.concatenate(ys, axis=1) + dsk_ref[...] * u
    y = jax.nn.gelu(y).astype(BF16)
    out = _dot(y, w1_ref[...]) * jax.nn.sigmoid(_dot(y, w2_ref[...]))
    for s in range(n_slab):
        y_ref[s] = out[:, s * LANES:(s + 1) * LANES]
    o_ref[...] = jnp.concatenate(
        [y_ref[s, pl.ds(b, lc, stride=nb), :] for b in range(nb) for s in range(n_slab)], axis=1).astype(o_ref.dtype)


def s5_layer(zd, w_b, abar_re, abar_im, w_cr, w_ci, d_skip, w1, w2, layer, *, batch, seq, lc):
    rows = lc * batch
    n_slab = D_WIDTH // LANES
    return pl.pallas_call(
        functools.partial(_s5_kernel, lc=lc, nb=batch, lane_chunk=512),
        out_shape=jax.ShapeDtypeStruct((seq, batch * D_WIDTH), BF16),
        grid=(seq // lc,),
        in_specs=[
            pl.BlockSpec((lc, batch * 2 * D_WIDTH), lambda c: (c, 0)),
            _layer_block((S5_PARTS, S5_PART_IN, 2 * S5_PART_STATE), layer),
            _layer_block((1, S5_LANES), layer),
            _layer_block((1, S5_LANES), layer),
            _layer_block((S5_PARTS, S5_PART_STATE, S5_PART_IN), layer),
            _layer_block((S5_PARTS, S5_PART_STATE, S5_PART_IN), layer),
            pl.BlockSpec((1, D_WIDTH), lambda c: (0, 0)),
            _layer_block((D_WIDTH, D_WIDTH), layer),
            _layer_block((D_WIDTH, D_WIDTH), layer),
        ],
        out_specs=pl.BlockSpec((lc, batch * D_WIDTH), lambda c: (c, 0)),
        scratch_shapes=[
            pltpu.VMEM((n_slab, rows, LANES), F32),
            pltpu.VMEM((n_slab, rows, LANES), F32),
            pltpu.VMEM((rows, 2 * S5_LANES), F32),
            pltpu.VMEM((batch, 2 * S5_LANES), F32),
            pltpu.VMEM((D_WIDTH, D_WIDTH), BF16),
            pltpu.VMEM((D_WIDTH, D_WIDTH), BF16),
        ],
        compiler_params=_params("arbitrary"),
        name="s5_layer",
    )(zd, w_b, abar_re, abar_im, w_cr, w_ci, d_skip.reshape(1, D_WIDTH), w1, w2)


def _odd_tail_kernel(u_ref, v_ref, gc_ref, d_ref, gd_ref, x_ref, lng_ref, lnb_ref, ws_ref, bs_ref, wo32_ref,
                     o_ref, c_ref, wo_ref, *, tm, sub):
    _cast_weights_once([(wo32_ref, wo_ref)], grid_rank=2)
    ri = lax.broadcasted_iota(jnp.int32, (C_CHUNK, C_CHUNK), 0)
    ci = lax.broadcasted_iota(jnp.int32, (C_CHUNK, C_CHUNK), 1)
    causal = ri >= ci
    w_s = [jnp.where(causal, ws_ref[g], 0.0).astype(BF16) for g in range(C_GROUPS)]
    for r in range(tm // sub):
        tile = slice(r * sub, (r + 1) * sub)
        v = v_ref[tile, :].astype(F32)
        mu = jnp.mean(v, axis=-1, keepdims=True)
        vc = v - mu
        var = jnp.mean(vc * vc, axis=-1, keepdims=True)
        vn = (vc * lax.rsqrt(var + EPS) * lng_ref[...] + lnb_ref[...]).astype(BF16)
        for g in range(C_GROUPS):
            cols = slice(g * C_GROUP_DIM, (g + 1) * C_GROUP_DIM)
            b_g = bs_ref[:, g:g + 1]
            for c in range(sub // C_CHUNK):
                rows = slice(r * sub + c * C_CHUNK, r * sub + (c + 1) * C_CHUNK)
                mixed = _dot(w_s[g], vn[c * C_CHUNK:(c + 1) * C_CHUNK, cols]) + b_g
                gate = _silu(gc_ref[rows, cols].astype(F32))
                c_ref[rows, cols] = (u_ref[rows, cols].astype(F32) * mixed * gate).astype(BF16)
        d = (d_ref[tile, :].astype(F32) * _silu(gd_ref[tile, :].astype(F32))).astype(BF16)
        y = _dot(c_ref[tile, :], wo_ref[:C_WIDTH, :]) + _dot(d, wo_ref[C_WIDTH:, :])
        o_ref[tile, :] = x_ref[tile, :] + y


def odd_tail(zc, zd, d_out, x, ln_g, ln_b, w_s, b_s, w_out, layer, *, batch, seq, tm, sub):
    return pl.pallas_call(
        functools.partial(_odd_tail_kernel, tm=tm, sub=sub),
        out_shape=jax.ShapeDtypeStruct((batch, seq, D_MODEL), F32),
        grid=(batch, seq // tm),
        in_specs=[
            pl.BlockSpec((None, tm, C_WIDTH), lambda b, t: (b, t, 0)),
            pl.BlockSpec((None, tm, C_WIDTH), lambda b, t: (b, t, 1)),
            pl.BlockSpec((None, tm, C_WIDTH), lambda b, t: (b, t, 2)),
            pl.BlockSpec((tm, D_WIDTH), lambda b, t: (t, b)),
            pl.BlockSpec((tm, D_WIDTH), lambda b, t: (t, 2 * b + 1)),
            pl.BlockSpec((None, tm, D_MODEL), lambda b, t: (b, t, 0)),
            pl.BlockSpec((1, C_WIDTH), lambda b, t: (0, 0)),
            pl.BlockSpec((1, C_WIDTH), lambda b, t: (0, 0)),
            _layer_block((C_GROUPS, C_CHUNK, C_CHUNK), layer),
            pl.BlockSpec((C_CHUNK, C_GROUPS), lambda b, t: (0, 0)),
            _layer_block((C_WIDTH + D_WIDTH, D_MODEL), layer),
        ],
        out_specs=pl.BlockSpec((None, tm, D_MODEL), lambda b, t: (b, t, 0)),
        scratch_shapes=[pltpu.VMEM((tm, C_WIDTH), BF16), pltpu.VMEM((C_WIDTH + D_WIDTH, D_MODEL), BF16)],
        compiler_params=_params("arbitrary", "arbitrary"),
        name="odd_tail",
    )(zc, zc, zc, d_out, zd, x, ln_g.reshape(1, C_WIDTH), ln_b.reshape(1, C_WIDTH), w_s, b_s.T, w_out)


def _xattn_kernel(x_ref, g_ref, wq32_ref, kv_ref, wo32_ref, pg_ref, *refs, final, sub):
    if final:
        o_ref, wq_ref, wo_ref = refs
    else:
        o_ref, xn_ref, wq_ref, wo_ref = refs
    _cast_weights_once([(wq32_ref, wq_ref), (wo32_ref, wo_ref)], grid_rank=2)
    for r in range(x_ref.shape[0] // sub):
        rows = slice(r * sub, (r + 1) * sub)
        x = x_ref[rows, :]
        q = _dot(_rms(x, g_ref[...]).astype(BF16), wq_ref[...])
        heads = []
        for h in range(X_HEADS):
            ks = slice(h * X_HEAD_DIM, (h + 1) * X_HEAD_DIM)
            vs = slice(D_MODEL + h * X_HEAD_DIM, D_MODEL + (h + 1) * X_HEAD_DIM)
            s = _dot_nt(q[:, ks].astype(BF16), kv_ref[:, ks]) * (X_HEAD_DIM ** -0.5)
            m = jnp.max(s, axis=-1, keepdims=True)
            p = jnp.exp(s - m)
            den = jnp.sum(p, axis=-1, keepdims=True)
            heads.append((_dot(p.astype(BF16), kv_ref[:, vs]) / den).astype(BF16))
        y = x + _dot(jnp.concatenate(heads, axis=1), wo_ref[...])
        if final:
            o_ref[rows, :] = _rms(y, pg_ref[...])
        else:
            o_ref[rows, :] = y
            xn_ref[rows, :] = _rms(y, pg_ref[...]).astype(xn_ref.dtype)


def cross_attention(x, g, w_q, kv, w_o, post_g, layer, *, batch, seq, mem_len, tq, sub, final):
    tile = pl.BlockSpec((None, tq, D_MODEL), lambda b, t: (b, t, 0))
    res = jax.ShapeDtypeStruct((batch, seq, D_MODEL), F32)
    nxt = jax.ShapeDtypeStruct((batch, seq, D_MODEL), BF16)
    return pl.pallas_call(
        functools.partial(_xattn_kernel, final=final, sub=sub),
        out_shape=res if final else (res, nxt),
        grid=(batch, seq // tq),
        in_specs=[
            tile,
            pl.BlockSpec((1, D_MODEL), lambda b, t: (0, 0)),
            _layer_block((D_MODEL, D_MODEL), layer),
            pl.BlockSpec((None, None, mem_len, 2 * D_MODEL), lambda b, t: (layer, b, 0, 0)),
            _layer_block((D_MODEL, D_MODEL), layer),
            pl.BlockSpec((1, D_MODEL), lambda b, t: (0, 0)),
        ],
        out_specs=tile if final else (tile, tile),
        scratch_shapes=[pltpu.VMEM((D_MODEL, D_MODEL), BF16), pltpu.VMEM((D_MODEL, D_MODEL), BF16)],
        compiler_params=_params("arbitrary", "arbitrary"),
        name="cross_attention",
    )(x, g.reshape(1, D_MODEL), w_q, kv, w_o, post_g.reshape(1, D_MODEL))


def kernel(x, mem, norm_ab, w_in_ab, pool_w, pool_scale, w_out_ab, norm_cd, w_in_cd, sgu_ln_g, sgu_ln_b,
           sgu_w, sgu_b, s5_a_re, s5_a_im, s5_log_dt, s5_b_re, s5_b_im, s5_c_re, s5_c_im, s5_d,
           glu_w1, glu_w2, w_out_cd, norm_x, w_xq, w_xkv, w_xo, mem_norm, final_norm):
    batch, seq, _ = x.shape
    mem_len = mem.shape[1]
    depth = norm_x.shape[0]
    tokens = batch * seq
    tm_proj = 1024
    tiles_per_row = seq // tm_proj
    mixer_norm = lambda layer: (norm_ab if layer % 2 == 0 else norm_cd)[layer // 2]

    kv = mem_kv_proj(mem.reshape(batch * mem_len, D_MODEL), mem_norm, w_xkv, tn=1024)
    kv = kv.reshape(depth, batch, mem_len, 2 * D_MODEL)
    w_b, abar_re, abar_im = s5_prep(s5_a_re, s5_a_im, s5_log_dt, s5_b_re, s5_b_im)
    w_cr, w_ci = s5_out_weights(s5_c_re, s5_c_im)
    xn = rms_cast(x.reshape(tokens, D_MODEL), mixer_norm(0), tm=tm_proj)
    for layer in range(depth):
        i = layer // 2
        if layer % 2 == 0:
            z = in_proj(xn, w_in_ab, i, n=w_in_ab.shape[2], tm=tm_proj, tn=2048)
            z = z.reshape(batch, seq, z.shape[1])
            a_out = dilated_attention(z, batch=batch, seq=seq)
            x = even_tail(z, a_out, x, pool_w, pool_scale[i], w_out_ab, i, batch=batch, seq=seq,
                          tm=1024, sub=512)
        else:
            n_c = 3 * C_WIDTH
            zc = in_proj(xn, w_in_cd, i, n=n_c, tm=tm_proj, tn=1024)
            zd = in_proj(xn, w_in_cd, i, n=2 * D_WIDTH, col0=n_c, tm=tm_proj, tn=2 * D_WIDTH,
                         out_shape=(seq, batch * 2 * D_WIDTH),
                         out_map=lambda r, j: (r % tiles_per_row, r // tiles_per_row))
            d_out = s5_layer(zd, w_b, abar_re, abar_im, w_cr, w_ci, s5_d[i], glu_w1, glu_w2, i,
                             batch=batch, seq=seq, lc=64)
            x = odd_tail(zc.reshape(batch, seq, n_c), zd, d_out, x, sgu_ln_g[i], sgu_ln_b[i], sgu_w,
                         sgu_b[i], w_out_cd, i, batch=batch, seq=seq, tm=1024, sub=512)
        final = layer == depth - 1
        out = cross_attention(x, norm_x[layer], w_xq, kv, w_xo, final_norm if final else mixer_norm(layer + 1),
                              layer, batch=batch, seq=seq, mem_len=mem_len, tq=1024, sub=512, final=final)
        if final:
            return out
        x, xn = out[0], out[1].reshape(tokens, D_MODEL)
```

```python
import functools

import jax
import jax.numpy as jnp
from jax import lax
from jax.experimental import pallas as pl
from jax.experimental.pallas import tpu as pltpu

F32 = jnp.float32
BF16 = jnp.bfloat16

LANES = 128
D_MODEL = 1024
A_WIDTH = 1024
A_HEAD_DIM = 64
A_BLOCK = 128
A_DILATIONS = (1, 4, 16)
DIL_STEP = 4
B_WIDTH = 1024
POOL_WINDOWS = (2, 4, 8, 16)
B_GROUP = 256
POOL_HALO = 16
C_WIDTH = 1024
C_CHUNK = 128
C_GROUPS = 4
C_GROUP_DIM = 256
D_WIDTH = 512
S5_GROUPS = 32
S5_GROUP_DIM = 16
S5_STATE = 64
S5_LANES = S5_GROUPS * S5_STATE
S5_PARTS = 2
S5_PART_IN = D_WIDTH // S5_PARTS
S5_PART_STATE = S5_LANES // S5_PARTS
X_HEADS = 4
X_HEAD_DIM = 256
EPS = 1e-6
NEG = -1e30
LOG2E = 1.4426950408889634

VMEM_LIMIT = 48 * 1024 * 1024


def _params(*sem):
    return pltpu.CompilerParams(dimension_semantics=sem, vmem_limit_bytes=VMEM_LIMIT)


def _rms(x, g):
    ms = jnp.mean(x * x, axis=-1, keepdims=True)
    return x * lax.rsqrt(ms + EPS) * g


def _silu(x):
    return x * jax.nn.sigmoid(x)


def _dot(a, b):
    return jnp.dot(a, b, preferred_element_type=F32)


def _layer_block(shape, layer):
    return pl.BlockSpec((None,) + tuple(shape), lambda *_: (layer,) + (0,) * len(shape),
                        pipeline_mode=pl.Buffered(1))


def _cast_weights_once(pairs, grid_rank):
    first = pl.program_id(0) == 0
    for axis in range(1, grid_rank):
        first = first & (pl.program_id(axis) == 0)

    @pl.when(first)
    def _():
        for src, dst in pairs:
            dst[...] = src[...].astype(dst.dtype)


def _dot_nt(a, b):
    return lax.dot_general(a, b, (((1,), (1,)), ((), ())), preferred_element_type=F32)


def _rms_cast_kernel(x_ref, g_ref, o_ref):
    o_ref[...] = _rms(x_ref[...], g_ref[...]).astype(o_ref.dtype)


def rms_cast(x, g, *, tm):
    m, k = x.shape
    return pl.pallas_call(
        _rms_cast_kernel,
        out_shape=jax.ShapeDtypeStruct((m, k), BF16),
        grid=(m // tm,),
        in_specs=[pl.BlockSpec((tm, k), lambda i: (i, 0)), pl.BlockSpec((1, k), lambda i: (0, 0))],
        out_specs=pl.BlockSpec((tm, k), lambda i: (i, 0)),
        compiler_params=_params("parallel"),
        name="rms_cast",
    )(x, g.reshape(1, k))


def _in_proj_kernel(x_ref, w32_ref, o_ref, w_ref):
    @pl.when(pl.program_id(1) == 0)
    def _():
        w_ref[...] = w32_ref[...].astype(BF16)

    o_ref[...] = _dot(x_ref[...], w_ref[...]).astype(o_ref.dtype)


def in_proj(xn, w, layer, *, n, tm, tn, col0=0, out_shape=None, out_map=None):
    m, k = xn.shape
    col_block0 = col0 // tn
    if out_shape is None:
        out_shape, out_map = (m, n), (lambda i, j: (i, j))
    return pl.pallas_call(
        _in_proj_kernel,
        out_shape=jax.ShapeDtypeStruct(out_shape, BF16),
        grid=(n // tn, m // tm),
        in_specs=[
            pl.BlockSpec((tm, k), lambda j, i: (i, 0)),
            pl.BlockSpec((None, k, tn), lambda j, i: (layer, 0, col_block0 + j)),
        ],
        out_specs=pl.BlockSpec((tm, tn), lambda j, i: out_map(i, j)),
        scratch_shapes=[pltpu.VMEM((k, tn), BF16)],
        compiler_params=_params("arbitrary", "arbitrary"),
        name="in_proj",
    )(xn, w)


def _mem_proj_kernel(x_ref, g_ref, w_ref, o_ref, xn_ref):
    @pl.when((pl.program_id(0) == 0) & (pl.program_id(1) == 0))
    def _():
        xn_ref[...] = _rms(x_ref[...], g_ref[...]).astype(BF16)

    o_ref[...] = _dot(xn_ref[...], w_ref[...].astype(BF16)).astype(o_ref.dtype)


def mem_kv_proj(mem, g, w, *, tn):
    m, k = mem.shape
    n_layers, _, n = w.shape
    return pl.pallas_call(
        _mem_proj_kernel,
        out_shape=jax.ShapeDtypeStruct((n_layers, m, n), BF16),
        grid=(n_layers, n // tn),
        in_specs=[
            pl.BlockSpec((m, k), lambda l, j: (0, 0)),
            pl.BlockSpec((1, k), lambda l, j: (0, 0)),
            pl.BlockSpec((None, k, tn), lambda l, j: (l, 0, j)),
        ],
        out_specs=pl.BlockSpec((None, m, tn), lambda l, j: (l, 0, j)),
        scratch_shapes=[pltpu.VMEM((m, k), BF16)],
        compiler_params=_params("arbitrary", "arbitrary"),
        name="mem_kv_proj",
    )(mem, g.reshape(1, k), w)


def _dilated_kernel(q_ref, k_ref, v_ref, g_ref, o_ref, qs_ref, ks_ref, vs_ref, gs_ref, os_ref, num_ref, den_ref,
                    m_ref, bias_ref, *, seq, unroll):
    assert A_DILATIONS == (1, DIL_STEP, DIL_STEP * DIL_STEP)
    qi = lax.broadcasted_iota(jnp.int32, (A_BLOCK, 2 * A_BLOCK), 0)
    kj = lax.broadcasted_iota(jnp.int32, (A_BLOCK, 2 * A_BLOCK), 1)
    dist = qi + A_BLOCK - kj
    band = (dist >= 0) & (dist <= A_BLOCK)
    bias_ref[1] = jnp.where(band, 0.0, NEG).astype(BF16)
    bias_ref[0] = jnp.where(band & (kj >= A_BLOCK), 0.0, NEG).astype(BF16)

    len_s = seq // DIL_STEP
    len_w = len_s // DIL_STEP
    blocks_s = len_s // A_BLOCK
    q_scale = A_HEAD_DIM ** -0.5 * LOG2E
    for src, dst, scale in ((q_ref, qs_ref, q_scale), (k_ref, ks_ref, None), (v_ref, vs_ref, None)):
        nat = src[...].astype(F32)
        dst[0] = nat if scale is None else nat * scale
        for b in range(DIL_STEP):
            dst[1, b * len_s:(b + 1) * len_s, :] = dst[0, pl.ds(b, len_s, stride=DIL_STEP), :]
        for c in range(DIL_STEP * DIL_STEP):
            b, a = divmod(c, DIL_STEP)
            dst[2, c * len_w:(c + 1) * len_w, :] = dst[1, pl.ds(b * len_s + a, len_w, stride=DIL_STEP), :]
    gs_ref[0] = g_ref[...].astype(F32)

    head0 = lax.broadcasted_iota(jnp.int32, (A_BLOCK, LANES), 1) < A_HEAD_DIM

    def load(slab, cur, prev):
        q = qs_ref[slab, cur, :].astype(BF16)
        if slab == 0:
            kk = jnp.concatenate([k_ref[prev, :], k_ref[cur, :]], axis=0)
            vv = jnp.concatenate([v_ref[prev, :], v_ref[cur, :]], axis=0)
            return q, kk, vv
        kk = jnp.concatenate([ks_ref[slab, prev, :], ks_ref[slab, cur, :]], axis=0).astype(BF16)
        vv = jnp.concatenate([vs_ref[slab, prev, :], vs_ref[slab, cur, :]], axis=0).astype(BF16)
        return q, kk, vv

    one_b, zero_b = jnp.ones((), BF16), jnp.zeros((), BF16)

    def attend(q, k, v, bias):
        own0 = lax.broadcasted_iota(jnp.int32, k.shape, 1).astype(BF16) < A_HEAD_DIM
        k_bd = jnp.concatenate([jnp.where(own0, k, zero_b), jnp.where(own0, zero_b, k)], axis=0)
        s = _dot_nt(q, k_bd).astype(BF16) + jnp.concatenate([bias, bias], axis=1)
        s0, s1 = s[:, :k.shape[0]], s[:, k.shape[0]:]
        m0 = jnp.max(s0, axis=-1, keepdims=True)
        m1 = jnp.max(s1, axis=-1, keepdims=True)
        p = jnp.concatenate([jnp.exp2(s0 - m0), jnp.exp2(s1 - m1)], axis=1)
        v_bd = jnp.concatenate([
            jnp.concatenate([jnp.where(own0, v, zero_b), jnp.where(own0, one_b, zero_b)], axis=1),
            jnp.concatenate([jnp.where(own0, zero_b, v), jnp.where(own0, zero_b, one_b)], axis=1)], axis=0)
        res = _dot(p, v_bd)
        m = jnp.where(head0, jnp.broadcast_to(m0, (A_BLOCK, LANES)).astype(F32),
                      jnp.broadcast_to(m1, (A_BLOCK, LANES)).astype(F32))
        return res[:, :LANES], res[:, LANES:], m

    def attend_single(cur):
        return attend(qs_ref[2, cur, :].astype(BF16), ks_ref[2, cur, :].astype(BF16),
                      vs_ref[2, cur, :].astype(BF16), bias_ref[0][:, A_BLOCK:])

    def aligned(start):
        return pl.ds(pl.multiple_of(start, A_BLOCK), A_BLOCK)

    def banded_rows(idx, blocks_per_seq):
        start = idx * A_BLOCK
        first = idx % blocks_per_seq == 0
        return aligned(start), aligned(jnp.where(first, start, start - A_BLOCK)), bias_ref[jnp.where(first, 0, 1)]

    def merge(num_a, den_a, m_a, num_b, den_b, m_b):
        m = jnp.maximum(m_a, m_b)
        w_a = jnp.exp2(m_a - m)
        w_b = jnp.exp2(m_b - m)
        return w_a * num_a + w_b * num_b, w_a * den_a + w_b * den_b, m

    def narrow_body(idx, carry):
        cur, prev, bias = banded_rows(idx, n_blocks)
        num, den, m = attend(*load(0, cur, prev), bias)
        num_ref[0, cur, :] = num
        den_ref[0, cur, :] = den
        m_ref[0, cur, :] = m
        return carry

    def step_body(idx, carry):
        cur, prev, bias = banded_rows(idx, blocks_s)
        natural = pl.ds(idx // blocks_s + DIL_STEP * A_BLOCK * (idx % blocks_s), A_BLOCK, stride=DIL_STEP)
        num, den, m = merge(*attend(*load(1, cur, prev), bias),
                            num_ref[0, natural, :], den_ref[0, natural, :], m_ref[0, natural, :])
        num_ref[1, cur, :] = num
        den_ref[1, cur, :] = den
        m_ref[1, cur, :] = m
        gs_ref[1, cur, :] = gs_ref[0, natural, :]
        return carry

    def wide_body(idx, carry):
        cur = aligned(idx * A_BLOCK)
        by_step = pl.ds((idx // DIL_STEP) * len_s + idx % DIL_STEP, A_BLOCK, stride=DIL_STEP)
        natural = pl.ds(DIL_STEP * (idx % DIL_STEP) + idx // DIL_STEP, A_BLOCK, stride=DIL_STEP * DIL_STEP)
        num, den, _ = merge(*attend_single(cur),
                            num_ref[1, by_step, :], den_ref[1, by_step, :], m_ref[1, by_step, :])
        os_ref[natural, :] = num / den * _silu(gs_ref[1, by_step, :])
        return carry

    assert len_w == A_BLOCK
    n_blocks = seq // A_BLOCK
    for body in (narrow_body, step_body, wide_body):
        lax.fori_loop(0, n_blocks, body, 0, unroll=unroll)
    o_ref[...] = os_ref[...].astype(o_ref.dtype)


def dilated_attention(z, *, batch, seq, unroll=16):
    n_pair = A_WIDTH // LANES
    n_pat = len(A_DILATIONS)
    blk = (None, seq, LANES)
    slabs = pltpu.VMEM((n_pat, seq, LANES), F32)
    partial = pltpu.VMEM((2, seq, LANES), F32)
    return pl.pallas_call(
        functools.partial(_dilated_kernel, seq=seq, unroll=unroll),
        out_shape=jax.ShapeDtypeStruct((batch, seq, A_WIDTH), BF16),
        grid=(batch, n_pair),
        in_specs=[
            pl.BlockSpec(blk, lambda b, h: (b, 0, h)),
            pl.BlockSpec(blk, lambda b, h: (b, 0, n_pair + h)),
            pl.BlockSpec(blk, lambda b, h: (b, 0, 2 * n_pair + h)),
            pl.BlockSpec(blk, lambda b, h: (b, 0, 3 * n_pair + h)),
        ],
        out_specs=pl.BlockSpec(blk, lambda b, h: (b, 0, h)),
        scratch_shapes=[
            slabs, slabs, slabs,
            pltpu.VMEM((2, seq, LANES), F32),
            pltpu.VMEM((seq, LANES), F32),
            partial, partial, partial,
            pltpu.VMEM((2, A_BLOCK, 2 * A_BLOCK), BF16),
        ],
        compiler_params=_params("parallel", "parallel"),
        name="dilated_attention",
    )(z, z, z, z)


def _even_tail_kernel(vb_ref, halo_ref, gb_ref, a_ref, x_ref, pw32_ref, ps_ref, wo32_ref, o_ref, pw_ref, wo_ref,
                      *, tm, sub):
    _cast_weights_once([(pw32_ref, pw_ref), (wo32_ref, wo_ref)], grid_rank=2)
    ti = pl.program_id(1)
    for r in range(tm // sub):
        rows = slice(r * sub, (r + 1) * sub)
        v = vb_ref[rows, :].astype(F32)
        if r == 0:
            halo = jnp.where(ti > 0, halo_ref[...].astype(F32), 0.0)
        else:
            halo = vb_ref[r * sub - POOL_HALO:r * sub, :].astype(F32)
        xc = jnp.concatenate([halo, v], axis=0)
        s2 = xc + pltpu.roll(xc, 1, 0)
        t4 = s2[:, B_GROUP:]
        s4 = t4 + pltpu.roll(t4, 2, 0)
        t8 = s4[:, B_GROUP:]
        s8 = t8 + pltpu.roll(t8, 4, 0)
        t16 = s8[:, B_GROUP:]
        s16 = t16 + pltpu.roll(t16, 8, 0)
        sums = (s2[:, :B_GROUP], s4[:, :B_GROUP], s8[:, :B_GROUP], s16)
        pos = (ti * tm + r * sub + 1 + lax.broadcasted_iota(jnp.int32, (sub, 1), 0)).astype(F32)
        mixed = []
        for g, w in enumerate(POOL_WINDOWS):
            mean = sums[g][POOL_HALO:, :] / jnp.minimum(pos, float(w))
            pooled = mean - v[:, g * B_GROUP:(g + 1) * B_GROUP]
            mixed.append(_dot(pooled.astype(BF16), pw_ref[g]))
        b_out = jnp.concatenate(mixed, axis=1) * ps_ref[...] * _silu(gb_ref[rows, :].astype(F32))
        y = _dot(a_ref[rows, :], wo_ref[:A_WIDTH, :]) + _dot(b_out.astype(BF16), wo_ref[A_WIDTH:, :])
        o_ref[rows, :] = x_ref[rows, :] + y


def even_tail(z, a_out, x, pool_w, pool_scale, w_out, layer, *, batch, seq, tm, sub):
    vb_col = 4 * A_WIDTH // B_WIDTH
    halo_per_tile = tm // POOL_HALO
    return pl.pallas_call(
        functools.partial(_even_tail_kernel, tm=tm, sub=sub),
        out_shape=jax.ShapeDtypeStruct((batch, seq, D_MODEL), F32),
        grid=(batch, seq // tm),
        in_specs=[
            pl.BlockSpec((None, tm, B_WIDTH), lambda b, t: (b, t, vb_col)),
            pl.BlockSpec((None, POOL_HALO, B_WIDTH),
                         lambda b, t: (b, jnp.maximum(t * halo_per_tile - 1, 0), vb_col)),
            pl.BlockSpec((None, tm, B_WIDTH), lambda b, t: (b, t, vb_col + 1)),
            pl.BlockSpec((None, tm, A_WIDTH), lambda b, t: (b, t, 0)),
            pl.BlockSpec((None, tm, D_MODEL), lambda b, t: (b, t, 0)),
            _layer_block((len(POOL_WINDOWS), B_GROUP, B_GROUP), layer),
            pl.BlockSpec((1, B_WIDTH), lambda b, t: (0, 0)),
            _layer_block((A_WIDTH + B_WIDTH, D_MODEL), layer),
        ],
        out_specs=pl.BlockSpec((None, tm, D_MODEL), lambda b, t: (b, t, 0)),
        scratch_shapes=[pltpu.VMEM((len(POOL_WINDOWS), B_GROUP, B_GROUP), BF16),
                        pltpu.VMEM((A_WIDTH + B_WIDTH, D_MODEL), BF16)],
        compiler_params=_params("arbitrary", "arbitrary"),
        name="even_tail",
    )(z, z, z, a_out, x, pool_w, pool_scale.reshape(1, B_WIDTH), w_out)


def _s5_prep_kernel(ar_ref, ai_ref, ldt_ref, br_ref, bi_ref, abr_ref, abi_ref, bbr_ref, bbi_ref):
    ar, ai = ar_ref[...], ai_ref[...]
    dt = jnp.exp(ldt_ref[...])
    mag = jnp.exp(dt * ar)
    abar_re = mag * jnp.cos(dt * ai)
    abar_im = mag * jnp.sin(dt * ai)
    nr, ni = abar_re - 1.0, abar_im
    inv = 1.0 / (ar * ar + ai * ai)
    coef_re = (nr * ar + ni * ai) * inv
    coef_im = (ni * ar - nr * ai) * inv
    br, bi = br_ref[...], bi_ref[...]
    abr_ref[...] = abar_re
    abi_ref[...] = abar_im
    bbr_ref[...] = coef_re * br - coef_im * bi
    bbi_ref[...] = coef_re * bi + coef_im * br


def _block_diag(rows, reps, row_group, col_group):
    tiled = jnp.concatenate([rows] * reps, axis=-1)
    r = lax.broadcasted_iota(jnp.int32, tiled.shape[1:], 0) // row_group
    c = lax.broadcasted_iota(jnp.int32, tiled.shape[1:], 1) // col_group
    return jnp.where(r == c, tiled, 0.0)


def s5_prep(a_re, a_im, log_dt, b_re, b_im):
    n_l = a_re.shape[0]
    rows = n_l * D_WIDTH
    rep = lambda t: jnp.repeat(t.reshape(n_l * S5_GROUPS, S5_STATE), S5_GROUP_DIM, axis=0)
    to_rows = lambda t: t.transpose(0, 1, 3, 2).reshape(rows, S5_STATE)
    shp = jax.ShapeDtypeStruct((rows, S5_STATE), F32)
    abr, abi, bbr, bbi = pl.pallas_call(
        _s5_prep_kernel, out_shape=(shp, shp, shp, shp), name="s5_prep",
    )(rep(a_re), rep(a_im), rep(jnp.broadcast_to(log_dt[:, :, None], (n_l, S5_GROUPS, S5_STATE))),
      to_rows(b_re), to_rows(b_im))
    gpp = S5_GROUPS // S5_PARTS

    def block_diag_in(t):
        return _block_diag(t.reshape(n_l * S5_PARTS, S5_PART_IN, S5_STATE), gpp, S5_GROUP_DIM, S5_STATE)

    w_b = jnp.concatenate([block_diag_in(bbr), block_diag_in(bbi)], axis=2).astype(BF16)
    w_b = w_b.reshape(n_l, S5_PARTS, S5_PART_IN, 2 * S5_PART_STATE)
    abar_re = abr[::S5_GROUP_DIM].reshape(n_l, 1, S5_LANES)
    abar_im = abi[::S5_GROUP_DIM].reshape(n_l, 1, S5_LANES)
    return w_b, abar_re, abar_im


def s5_out_weights(c_re, c_im):
    n_l = c_re.shape[0]
    gpp = S5_GROUPS // S5_PARTS

    def block_diag_out(t):
        t = t.transpose(0, 1, 3, 2).reshape(n_l * S5_PARTS, S5_PART_STATE, S5_GROUP_DIM)
        t = _block_diag(t, gpp, S5_STATE, S5_GROUP_DIM).astype(BF16)
        return t.reshape(n_l, S5_PARTS, S5_PART_STATE, S5_PART_IN)

    return block_diag_out(c_re), block_diag_out(c_im)


def _s5_kernel(xd_ref, wb_ref, are_ref, aim_ref, wcr_ref, wci_ref, dsk_ref, w1_32_ref, w2_32_ref,
               o_ref, u_ref, y_ref, bu_ref, h_ref, w1_ref, w2_ref, *, lc, nb, lane_chunk):
    @pl.when(pl.program_id(0) == 0)
    def _():
        h_ref[...] = jnp.zeros_like(h_ref)

    _cast_weights_once([(w1_32_ref, w1_ref), (w2_32_ref, w2_ref)], grid_rank=1)
    n_slab = D_WIDTH // LANES
    for b in range(nb):
        for s in range(n_slab):
            col = b * 2 * D_WIDTH + s * LANES
            u_ref[s, pl.ds(b, lc, stride=nb), :] = xd_ref[:, col:col + LANES].astype(F32)
    u = jnp.concatenate([u_ref[s] for s in range(n_slab)], axis=1)
    ub = u.astype(BF16)
    width = 2 * S5_PART_STATE
    def project(part):
        bu_ref[:, part * width:(part + 1) * width] = _dot(
            ub[:, part * S5_PART_IN:(part + 1) * S5_PART_IN], wb_ref[part])

    def scan(part):
        for c in range(S5_PART_STATE // lane_chunk):
            re = slice(part * width + c * lane_chunk, part * width + (c + 1) * lane_chunk)
            im = slice(re.start + S5_PART_STATE, re.stop + S5_PART_STATE)
            ab = slice(part * S5_PART_STATE + c * lane_chunk, part * S5_PART_STATE + (c + 1) * lane_chunk)
            a_r = jnp.broadcast_to(are_ref[:, ab], (nb, lane_chunk))
            a_i = jnp.broadcast_to(aim_ref[:, ab], (nb, lane_chunk))
            h_r, h_i = h_ref[:, re], h_ref[:, im]
            for t in range(lc):
                row = slice(t * nb, (t + 1) * nb)
                h_r, h_i = (a_r * h_r - a_i * h_i + bu_ref[row, re], a_r * h_i + a_i * h_r + bu_ref[row, im])
                bu_ref[row, re] = h_r
                bu_ref[row, im] = h_i
            h_ref[:, re] = h_r
            h_ref[:, im] = h_i

    def readout(part):
        re = slice(part * width, part * width + S5_PART_STATE)
        im = slice(re.stop, re.stop + S5_PART_STATE)
        return _dot(bu_ref[:, re].astype(BF16), wcr_ref[part]) - _dot(bu_ref[:, im].astype(BF16), wci_ref[part])

    ys = []
    project(0)
    for part in range(S5_PARTS):
        if part + 1 < S5_PARTS:
            project(part + 1)
        scan(part)
        ys.append(readout(part))
    y = jnp.concatenate(ys, axis=1) + dsk_ref[...] * u
    y = jax.nn.gelu(y).astype(BF16)
    out = _dot(y, w1_ref[...]) * jax.nn.sigmoid(_dot(y, w2_ref[...]))
    for s in range(n_slab):
        y_ref[s] = out[:, s * LANES:(s + 1) * LANES]
    o_ref[...] = jnp.concatenate(
        [y_ref[s, pl.ds(b, lc, stride=nb), :] for b in range(nb) for s in range(n_slab)], axis=1).astype(o_ref.dtype)


def s5_layer(zd, w_b, abar_re, abar_im, w_cr, w_ci, d_skip, w1, w2, layer, *, batch, seq, lc):
    rows = lc * batch
    n_slab = D_WIDTH // LANES
    return pl.pallas_call(
        functools.partial(_s5_kernel, lc=lc, nb=batch, lane_chunk=512),
        out_shape=jax.ShapeDtypeStruct((seq, batch * D_WIDTH), BF16),
        grid=(seq // lc,),
        in_specs=[
            pl.BlockSpec((lc, batch * 2 * D_WIDTH), lambda c: (c, 0)),
            _layer_block((S5_PARTS, S5_PART_IN, 2 * S5_PART_STATE), layer),
            _layer_block((1, S5_LANES), layer),
            _layer_block((1, S5_LANES), layer),
            _layer_block((S5_PARTS, S5_PART_STATE, S5_PART_IN), layer),
            _layer_block((S5_PARTS, S5_PART_STATE, S5_PART_IN), layer),
            pl.BlockSpec((1, D_WIDTH), lambda c: (0, 0)),
            _layer_block((D_WIDTH, D_WIDTH), layer),
            _layer_block((D_WIDTH, D_WIDTH), layer),
        ],
        out_specs=pl.BlockSpec((lc, batch * D_WIDTH), lambda c: (c, 0)),
        scratch_shapes=[
            pltpu.VMEM((n_slab, rows, LANES), F32),
            pltpu.VMEM((n_slab, rows, LANES), F32),
            pltpu.VMEM((rows, 2 * S5_LANES), F32),
            pltpu.VMEM((batch, 2 * S5_LANES), F32),
            pltpu.VMEM((D_WIDTH, D_WIDTH), BF16),
            pltpu.VMEM((D_WIDTH, D_WIDTH), BF16),
        ],
        compiler_params=_params("arbitrary"),
        name="s5_layer",
    )(zd, w_b, abar_re, abar_im, w_cr, w_ci, d_skip.reshape(1, D_WIDTH), w1, w2)


def _odd_tail_kernel(u_ref, v_ref, gc_ref, d_ref, gd_ref, x_ref, lng_ref, lnb_ref, ws_ref, bs_ref, wo32_ref,
                     o_ref, c_ref, wo_ref, *, tm, sub):
    _cast_weights_once([(wo32_ref, wo_ref)], grid_rank=2)
    ri = lax.broadcasted_iota(jnp.int32, (C_CHUNK, C_CHUNK), 0)
    ci = lax.broadcasted_iota(jnp.int32, (C_CHUNK, C_CHUNK), 1)
    causal = ri >= ci
    w_s = [jnp.where(causal, ws_ref[g], 0.0).astype(BF16) for g in range(C_GROUPS)]
    for r in range(tm // sub):
        tile = slice(r * sub, (r + 1) * sub)
        v = v_ref[tile, :].astype(F32)
        mu = jnp.mean(v, axis=-1, keepdims=True)
        vc = v - mu
        var = jnp.mean(vc * vc, axis=-1, keepdims=True)
        vn = (vc * lax.rsqrt(var + EPS) * lng_ref[...] + lnb_ref[...]).astype(BF16)
        for g in range(C_GROUPS):
            cols = slice(g * C_GROUP_DIM, (g + 1) * C_GROUP_DIM)
            b_g = bs_ref[:, g:g + 1]
            for c in range(sub // C_CHUNK):
                rows = slice(r * sub + c * C_CHUNK, r * sub + (c + 1) * C_CHUNK)
                mixed = _dot(w_s[g], vn[c * C_CHUNK:(c + 1) * C_CHUNK, cols]) + b_g
                gate = _silu(gc_ref[rows, cols].astype(F32))
                c_ref[rows, cols] = (u_ref[rows, cols].astype(F32) * mixed * gate).astype(BF16)
        d = (d_ref[tile, :].astype(F32) * _silu(gd_ref[tile, :].astype(F32))).astype(BF16)
        y = _dot(c_ref[tile, :], wo_ref[:C_WIDTH, :]) + _dot(d, wo_ref[C_WIDTH:, :])
        o_ref[tile, :] = x_ref[tile, :] + y


def odd_tail(zc, zd, d_out, x, ln_g, ln_b, w_s, b_s, w_out, layer, *, batch, seq, tm, sub):
    return pl.pallas_call(
        functools.partial(_odd_tail_kernel, tm=tm, sub=sub),
        out_shape=jax.ShapeDtypeStruct((batch, seq, D_MODEL), F32),
        grid=(batch, seq // tm),
        in_specs=[
            pl.BlockSpec((None, tm, C_WIDTH), lambda b, t: (b, t, 0)),
            pl.BlockSpec((None, tm, C_WIDTH), lambda b, t: (b, t, 1)),
            pl.BlockSpec((None, tm, C_WIDTH), lambda b, t: (b, t, 2)),
            pl.BlockSpec((tm, D_WIDTH), lambda b, t: (t, b)),
            pl.BlockSpec((tm, D_WIDTH), lambda b, t: (t, 2 * b + 1)),
            pl.BlockSpec((None, tm, D_MODEL), lambda b, t: (b, t, 0)),
            pl.BlockSpec((1, C_WIDTH), lambda b, t: (0, 0)),
            pl.BlockSpec((1, C_WIDTH), lambda b, t: (0, 0)),
            _layer_block((C_GROUPS, C_CHUNK, C_CHUNK), layer),
            pl.BlockSpec((C_CHUNK, C_GROUPS), lambda b, t: (0, 0)),
            _layer_block((C_WIDTH + D_WIDTH, D_MODEL), layer),
        ],
        out_specs=pl.BlockSpec((None, tm, D_MODEL), lambda b, t: (b, t, 0)),
        scratch_shapes=[pltpu.VMEM((tm, C_WIDTH), BF16), pltpu.VMEM((C_WIDTH + D_WIDTH, D_MODEL), BF16)],
        compiler_params=_params("arbitrary", "arbitrary"),
        name="odd_tail",
    )(zc, zc, zc, d_out, zd, x, ln_g.reshape(1, C_WIDTH), ln_b.reshape(1, C_WIDTH), w_s, b_s.T, w_out)


def _xattn_kernel(x_ref, g_ref, wq32_ref, kv_ref, wo32_ref, pg_ref, *refs, final, sub):
    if final:
        o_ref, wq_ref, wo_ref = refs
    else:
        o_ref, xn_ref, wq_ref, wo_ref = refs
    _cast_weights_once([(wq32_ref, wq_ref), (wo32_ref, wo_ref)], grid_rank=2)
    for r in range(x_ref.shape[0] // sub):
        rows = slice(r * sub, (r + 1) * sub)
        x = x_ref[rows, :]
        q = _dot(_rms(x, g_ref[...]).astype(BF16), wq_ref[...])
        heads = []
        for h in range(X_HEADS):
            ks = slice(h * X_HEAD_DIM, (h + 1) * X_HEAD_DIM)
            vs = slice(D_MODEL + h * X_HEAD_DIM, D_MODEL + (h + 1) * X_HEAD_DIM)
            s = _dot_nt(q[:, ks].astype(BF16), kv_ref[:, ks]) * (X_HEAD_DIM ** -0.5)
            m = jnp.max(s, axis=-1, keepdims=True)
            p = jnp.exp(s - m)
            den = jnp.sum(p, axis=-1, keepdims=True)
            heads.append((_dot(p.astype(BF16), kv_ref[:, vs]) / den).astype(BF16))
        y = x + _dot(jnp.concatenate(heads, axis=1), wo_ref[...])
        if final:
            o_ref[rows, :] = _rms(y, pg_ref[...])
        else:
            o_ref[rows, :] = y
            xn_ref[rows, :] = _rms(y, pg_ref[...]).astype(xn_ref.dtype)


def cross_attention(x, g, w_q, kv, w_o, post_g, layer, *, batch, seq, mem_len, tq, sub, final):
    tile = pl.BlockSpec((None, tq, D_MODEL), lambda b, t: (b, t, 0))
    res = jax.ShapeDtypeStruct((batch, seq, D_MODEL), F32)
    nxt = jax.ShapeDtypeStruct((batch, seq, D_MODEL), BF16)
    return pl.pallas_call(
        functools.partial(_xattn_kernel, final=final, sub=sub),
        out_shape=res if final else (res, nxt),
        grid=(batch, seq // tq),
        in_specs=[
            tile,
            pl.BlockSpec((1, D_MODEL), lambda b, t: (0, 0)),
            _layer_block((D_MODEL, D_MODEL), layer),
            pl.BlockSpec((None, None, mem_len, 2 * D_MODEL), lambda b, t: (layer, b, 0, 0)),
            _layer_block((D_MODEL, D_MODEL), layer),
            pl.BlockSpec((1, D_MODEL), lambda b, t: (0, 0)),
        ],
        out_specs=tile if final else (tile, tile),
        scratch_shapes=[pltpu.VMEM((D_MODEL, D_MODEL), BF16), pltpu.VMEM((D_MODEL, D_MODEL), BF16)],
        compiler_params=_params("arbitrary", "arbitrary"),
        name="cross_attention",
    )(x, g.reshape(1, D_MODEL), w_q, kv, w_o, post_g.reshape(1, D_MODEL))


def kernel(x, mem, norm_ab, w_in_ab, pool_w, pool_scale, w_out_ab, norm_cd, w_in_cd, sgu_ln_g, sgu_ln_b,
           sgu_w, sgu_b, s5_a_re, s5_a_im, s5_log_dt, s5_b_re, s5_b_im, s5_c_re, s5_c_im, s5_d,
           glu_w1, glu_w2, w_out_cd, norm_x, w_xq, w_xkv, w_xo, mem_norm, final_norm):
    batch, seq, _ = x.shape
    mem_len = mem.shape[1]
    depth = norm_x.shape[0]
    tokens = batch * seq
    tm_proj = 1024
    tiles_per_row = seq // tm_proj
    mixer_norm = lambda layer: (norm_ab if layer % 2 == 0 else norm_cd)[layer // 2]

    kv = mem_kv_proj(mem.reshape(batch * mem_len, D_MODEL), mem_norm, w_xkv, tn=1024)
    kv = kv.reshape(depth, batch, mem_len, 2 * D_MODEL)
    w_b, abar_re, abar_im = s5_prep(s5_a_re, s5_a_im, s5_log_dt, s5_b_re, s5_b_im)
    w_cr, w_ci = s5_out_weights(s5_c_re, s5_c_im)
    xn = rms_cast(x.reshape(tokens, D_MODEL), mixer_norm(0), tm=tm_proj)
    for layer in range(depth):
        i = layer // 2
        if layer % 2 == 0:
            z = in_proj(xn, w_in_ab, i, n=w_in_ab.shape[2], tm=tm_proj, tn=2048)
            z = z.reshape(batch, seq, z.shape[1])
            a_out = dilated_attention(z, batch=batch, seq=seq)
            x = even_tail(z, a_out, x, pool_w, pool_scale[i], w_out_ab, i, batch=batch, seq=seq,
                          tm=1024, sub=512)
        else:
            n_c = 3 * C_WIDTH
            zc = in_proj(xn, w_in_cd, i, n=n_c, tm=tm_proj, tn=1024)
            zd = in_proj(xn, w_in_cd, i, n=2 * D_WIDTH, col0=n_c, tm=tm_proj, tn=2 * D_WIDTH,
                         out_shape=(seq, batch * 2 * D_WIDTH),
                         out_map=lambda r, j: (r % tiles_per_row, r // tiles_per_row))
            d_out = s5_layer(zd, w_b, abar_re, abar_im, w_cr, w_ci, s5_d[i], glu_w1, glu_w2, i,
                             batch=batch, seq=seq, lc=64)
            x = odd_tail(zc.reshape(batch, seq, n_c), zd, d_out, x, sgu_ln_g[i], sgu_ln_b[i], sgu_w,
                         sgu_b[i], w_out_cd, i, batch=batch, seq=seq, tm=1024, sub=512)
        final = layer == depth - 1
        out = cross_attention(x, norm_x[layer], w_xq, kv, w_xo, final_norm if final else mixer_norm(layer + 1),
                              layer, batch=batch, seq=seq, mem_len=mem_len, tq=1024, sub=512, final=final)
        if final:
            return out
        x, xn = out[0], out[1].reshape(tokens, D_MODEL)
```

```python
import functools

import jax
import jax.numpy as jnp
from jax import lax
from jax.experimental import pallas as pl
from jax.experimental.pallas import tpu as pltpu

F32 = jnp.float32
BF16 = jnp.bfloat16

LANES = 128
D_MODEL = 1024
A_WIDTH = 1024
A_HEAD_DIM = 64
A_BLOCK = 128
A_DILATIONS = (1, 4, 16)
DIL_STEP = 4
B_WIDTH = 1024
POOL_WINDOWS = (2, 4, 8, 16)
B_GROUP = 256
POOL_HALO = 16
C_WIDTH = 1024
C_CHUNK = 128
C_GROUPS = 4
C_GROUP_DIM = 256
D_WIDTH = 512
S5_GROUPS = 32
S5_GROUP_DIM = 16
S5_STATE = 64
S5_LANES = S5_GROUPS * S5_STATE
S5_PARTS = 2
S5_PART_IN = D_WIDTH // S5_PARTS
S5_PART_STATE = S5_LANES // S5_PARTS
X_HEADS = 4
X_HEAD_DIM = 256
EPS = 1e-6
NEG = -1e30
LOG2E = 1.4426950408889634

VMEM_LIMIT = 48 * 1024 * 1024


def _params(*sem):
    return pltpu.CompilerParams(dimension_semantics=sem, vmem_limit_bytes=VMEM_LIMIT)


def _rms(x, g):
    ms = jnp.mean(x * x, axis=-1, keepdims=True)
    return x * lax.rsqrt(ms + EPS) * g


def _silu(x):
    return x * jax.nn.sigmoid(x)


def _dot(a, b):
    return jnp.dot(a, b, preferred_element_type=F32)


def _layer_block(shape, layer):
    return pl.BlockSpec((None,) + tuple(shape), lambda *_: (layer,) + (0,) * len(shape),
                        pipeline_mode=pl.Buffered(1))


def _cast_weights_once(pairs, grid_rank):
    first = pl.program_id(0) == 0
    for axis in range(1, grid_rank):
        first = first & (pl.program_id(axis) == 0)

    @pl.when(first)
    def _():
        for src, dst in pairs:
            dst[...] = src[...].astype(dst.dtype)


def _dot_nt(a, b):
    return lax.dot_general(a, b, (((1,), (1,)), ((), ())), preferred_element_type=F32)


def _rms_cast_kernel(x_ref, g_ref, o_ref):
    o_ref[...] = _rms(x_ref[...], g_ref[...]).astype(o_ref.dtype)


def rms_cast(x, g, *, tm):
    m, k = x.shape
    return pl.pallas_call(
        _rms_cast_kernel,
        out_shape=jax.ShapeDtypeStruct((m, k), BF16),
        grid=(m // tm,),
        in_specs=[pl.BlockSpec((tm, k), lambda i: (i, 0)), pl.BlockSpec((1, k), lambda i: (0, 0))],
        out_specs=pl.BlockSpec((tm, k), lambda i: (i, 0)),
        compiler_params=_params("parallel"),
        name="rms_cast",
    )(x, g.reshape(1, k))


def _in_proj_kernel(x_ref, w32_ref, o_ref, w_ref):
    @pl.when(pl.program_id(1) == 0)
    def _():
        w_ref[...] = w32_ref[...].astype(BF16)

    o_ref[...] = _dot(x_ref[...], w_ref[...]).astype(o_ref.dtype)


def in_proj(xn, w, layer, *, n, tm, tn, col0=0, out_shape=None, out_map=None):
    m, k = xn.shape
    col_block0 = col0 // tn
    if out_shape is None:
        out_shape, out_map = (m, n), (lambda i, j: (i, j))
    return pl.pallas_call(
        _in_proj_kernel,
        out_shape=jax.ShapeDtypeStruct(out_shape, BF16),
        grid=(n // tn, m // tm),
        in_specs=[
            pl.BlockSpec((tm, k), lambda j, i: (i, 0)),
            pl.BlockSpec((None, k, tn), lambda j, i: (layer, 0, col_block0 + j)),
        ],
        out_specs=pl.BlockSpec((tm, tn), lambda j, i: out_map(i, j)),
        scratch_shapes=[pltpu.VMEM((k, tn), BF16)],
        compiler_params=_params("arbitrary", "arbitrary"),
        name="in_proj",
    )(xn, w)


def _mem_proj_kernel(x_ref, g_ref, w_ref, o_ref, xn_ref):
    @pl.when((pl.program_id(0) == 0) & (pl.program_id(1) == 0))
    def _():
        xn_ref[...] = _rms(x_ref[...], g_ref[...]).astype(BF16)

    o_ref[...] = _dot(xn_ref[...], w_ref[...].astype(BF16)).astype(o_ref.dtype)


def mem_kv_proj(mem, g, w, *, tn):
    m, k = mem.shape
    n_layers, _, n = w.shape
    return pl.pallas_call(
        _mem_proj_kernel,
        out_shape=jax.ShapeDtypeStruct((n_layers, m, n), BF16),
        grid=(n_layers, n // tn),
        in_specs=[
            pl.BlockSpec((m, k), lambda l, j: (0, 0)),
            pl.BlockSpec((1, k), lambda l, j: (0, 0)),
            pl.BlockSpec((None, k, tn), lambda l, j: (l, 0, j)),
        ],
        out_specs=pl.BlockSpec((None, m, tn), lambda l, j: (l, 0, j)),
        scratch_shapes=[pltpu.VMEM((m, k), BF16)],
        compiler_params=_params("arbitrary", "arbitrary"),
        name="mem_kv_proj",
    )(mem, g.reshape(1, k), w)


def _dilated_kernel(q_ref, k_ref, v_ref, g_ref, o_ref, qs_ref, ks_ref, vs_ref, gs_ref, os_ref, num_ref, den_ref,
                    m_ref, bias_ref, *, seq, unroll):
    assert A_DILATIONS == (1, DIL_STEP, DIL_STEP * DIL_STEP)
    qi = lax.broadcasted_iota(jnp.int32, (A_BLOCK, 2 * A_BLOCK), 0)
    kj = lax.broadcasted_iota(jnp.int32, (A_BLOCK, 2 * A_BLOCK), 1)
    dist = qi + A_BLOCK - kj
    band = (dist >= 0) & (dist <= A_BLOCK)
    bias_ref[1] = jnp.where(band, 0.0, NEG).astype(BF16)
    bias_ref[0] = jnp.where(band & (kj >= A_BLOCK), 0.0, NEG).astype(BF16)

    len_s = seq // DIL_STEP
    len_w = len_s // DIL_STEP
    blocks_s = len_s // A_BLOCK
    q_scale = A_HEAD_DIM ** -0.5 * LOG2E
    for src, dst, scale in ((q_ref, qs_ref, q_scale), (k_ref, ks_ref, None), (v_ref, vs_ref, None)):
        nat = src[...].astype(F32)
        dst[0] = nat if scale is None else nat * scale
        for b in range(DIL_STEP):
            dst[1, b * len_s:(b + 1) * len_s, :] = dst[0, pl.ds(b, len_s, stride=DIL_STEP), :]
        for c in range(DIL_STEP * DIL_STEP):
            b, a = divmod(c, DIL_STEP)
            dst[2, c * len_w:(c + 1) * len_w, :] = dst[1, pl.ds(b * len_s + a, len_w, stride=DIL_STEP), :]
    gs_ref[0] = g_ref[...].astype(F32)

    head0 = lax.broadcasted_iota(jnp.int32, (A_BLOCK, LANES), 1) < A_HEAD_DIM

    def load(slab, cur, prev):
        q = qs_ref[slab, cur, :].astype(BF16)
        if slab == 0:
            kk = jnp.concatenate([k_ref[prev, :], k_ref[cur, :]], axis=0)
            vv = jnp.concatenate([v_ref[prev, :], v_ref[cur, :]], axis=0)
            return q, kk, vv
        kk = jnp.concatenate([ks_ref[slab, prev, :], ks_ref[slab, cur, :]], axis=0).astype(BF16)
        vv = jnp.concatenate([vs_ref[slab, prev, :], vs_ref[slab, cur, :]], axis=0).astype(BF16)
        return q, kk, vv

    one_b, zero_b = jnp.ones((), BF16), jnp.zeros((), BF16)

    def attend(q, k, v, bias):
        own0 = lax.broadcasted_iota(jnp.int32, k.shape, 1).astype(BF16) < A_HEAD_DIM
        k_bd = jnp.concatenate([jnp.where(own0, k, zero_b), jnp.where(own0, zero_b, k)], axis=0)
        s = _dot_nt(q, k_bd).astype(BF16) + jnp.concatenate([bias, bias], axis=1)
        s0, s1 = s[:, :k.shape[0]], s[:, k.shape[0]:]
        m0 = jnp.max(s0, axis=-1, keepdims=True)
        m1 = jnp.max(s1, axis=-1, keepdims=True)
        p = jnp.concatenate([jnp.exp2(s0 - m0), jnp.exp2(s1 - m1)], axis=1)
        v_bd = jnp.concatenate([
            jnp.concatenate([jnp.where(own0, v, zero_b), jnp.where(own0, one_b, zero_b)], axis=1),
            jnp.concatenate([jnp.where(own0, zero_b, v), jnp.where(own0, zero_b, one_b)], axis=1)], axis=0)
        res = _dot(p, v_bd)
        m = jnp.where(head0, jnp.broadcast_to(m0, (A_BLOCK, LANES)).astype(F32),
                      jnp.broadcast_to(m1, (A_BLOCK, LANES)).astype(F32))
        return res[:, :LANES], res[:, LANES:], m

    def attend_single(cur):
        return attend(qs_ref[2, cur, :].astype(BF16), ks_ref[2, cur, :].astype(BF16),
                      vs_ref[2, cur, :].astype(BF16), bias_ref[0][:, A_BLOCK:])

    def aligned(start):
        return pl.ds(pl.multiple_of(start, A_BLOCK), A_BLOCK)

    def banded_rows(idx, blocks_per_seq):
        start = idx * A_BLOCK
        first = idx % blocks_per_seq == 0
        return aligned(start), aligned(jnp.where(first, start, start - A_BLOCK)), bias_ref[jnp.where(first, 0, 1)]

    def merge(num_a, den_a, m_a, num_b, den_b, m_b):
        m = jnp.maximum(m_a, m_b)
        w_a = jnp.exp2(m_a - m)
        w_b = jnp.exp2(m_b - m)
        return w_a * num_a + w_b * num_b, w_a * den_a + w_b * den_b, m

    def narrow_body(idx, carry):
        cur, prev, bias = banded_rows(idx, n_blocks)
        num, den, m = attend(*load(0, cur, prev), bias)
        num_ref[0, cur, :] = num
        den_ref[0, cur, :] = den
        m_ref[0, cur, :] = m
        return carry

    def step_body(idx, carry):
        cur, prev, bias = banded_rows(idx, blocks_s)
        natural = pl.ds(idx // blocks_s + DIL_STEP * A_BLOCK * (idx % blocks_s), A_BLOCK, stride=DIL_STEP)
        num, den, m = merge(*attend(*load(1, cur, prev), bias),
                            num_ref[0, natural, :], den_ref[0, natural, :], m_ref[0, natural, :])
        num_ref[1, cur, :] = num
        den_ref[1, cur, :] = den
        m_ref[1, cur, :] = m
        gs_ref[1, cur, :] = gs_ref[0, natural, :]
        return carry

    def wide_body(idx, carry):
        cur = aligned(idx * A_BLOCK)
        by_step = pl.ds((idx // DIL_STEP) * len_s + idx % DIL_STEP, A_BLOCK, stride=DIL_STEP)
        natural = pl.ds(DIL_STEP * (idx % DIL_STEP) + idx // DIL_STEP, A_BLOCK, stride=DIL_STEP * DIL_STEP)
        num, den, _ = merge(*attend_single(cur),
                            num_ref[1, by_step, :], den_ref[1, by_step, :], m_ref[1, by_step, :])
        os_ref[natural, :] = num / den * _silu(gs_ref[1, by_step, :])
        return carry

    assert len_w == A_BLOCK
    n_blocks = seq // A_BLOCK
    for body in (narrow_body, step_body, wide_body):
        lax.fori_loop(0, n_blocks, body, 0, unroll=unroll)
    o_ref[...] = os_ref[...].astype(o_ref.dtype)


def dilated_attention(z, *, batch, seq, unroll=16):
    n_pair = A_WIDTH // LANES
    n_pat = len(A_DILATIONS)
    blk = (None, seq, LANES)
    slabs = pltpu.VMEM((n_pat, seq, LANES), F32)
    partial = pltpu.VMEM((2, seq, LANES), F32)
    return pl.pallas_call(
        functools.partial(_dilated_kernel, seq=seq, unroll=unroll),
        out_shape=jax.ShapeDtypeStruct((batch, seq, A_WIDTH), BF16),
        grid=(batch, n_pair),
        in_specs=[
            pl.BlockSpec(blk, lambda b, h: (b, 0, h)),
            pl.BlockSpec(blk, lambda b, h: (b, 0, n_pair + h)),
            pl.BlockSpec(blk, lambda b, h: (b, 0, 2 * n_pair + h)),
            pl.BlockSpec(blk, lambda b, h: (b, 0, 3 * n_pair + h)),
        ],
        out_specs=pl.BlockSpec(blk, lambda b, h: (b, 0, h)),
        scratch_shapes=[
            slabs, slabs, slabs,
            pltpu.VMEM((2, seq, LANES), F32),
            pltpu.VMEM((seq, LANES), F32),
            partial, partial, partial,
            pltpu.VMEM((2, A_BLOCK, 2 * A_BLOCK), BF16),
        ],
        compiler_params=_params("parallel", "parallel"),
        name="dilated_attention",
    )(z, z, z, z)


def _even_tail_kernel(vb_ref, halo_ref, gb_ref, a_ref, x_ref, pw32_ref, ps_ref, wo32_ref, o_ref, pw_ref, wo_ref,
                      *, tm, sub):
    _cast_weights_once([(pw32_ref, pw_ref), (wo32_ref, wo_ref)], grid_rank=2)
    ti = pl.program_id(1)
    for r in range(tm // sub):
        rows = slice(r * sub, (r + 1) * sub)
        v = vb_ref[rows, :].astype(F32)
        if r == 0:
            halo = jnp.where(ti > 0, halo_ref[...].astype(F32), 0.0)
        else:
            halo = vb_ref[r * sub - POOL_HALO:r * sub, :].astype(F32)
        xc = jnp.concatenate([halo, v], axis=0)
        s2 = xc + pltpu.roll(xc, 1, 0)
        t4 = s2[:, B_GROUP:]
        s4 = t4 + pltpu.roll(t4, 2, 0)
        t8 = s4[:, B_GROUP:]
        s8 = t8 + pltpu.roll(t8, 4, 0)
        t16 = s8[:, B_GROUP:]
        s16 = t16 + pltpu.roll(t16, 8, 0)
        sums = (s2[:, :B_GROUP], s4[:, :B_GROUP], s8[:, :B_GROUP], s16)
        pos = (ti * tm + r * sub + 1 + lax.broadcasted_iota(jnp.int32, (sub, 1), 0)).astype(F32)
        mixed = []
        for g, w in enumerate(POOL_WINDOWS):
            mean = sums[g][POOL_HALO:, :] / jnp.minimum(pos, float(w))
            pooled = mean - v[:, g * B_GROUP:(g + 1) * B_GROUP]
            mixed.append(_dot(pooled.astype(BF16), pw_ref[g]))
        b_out = jnp.concatenate(mixed, axis=1) * ps_ref[...] * _silu(gb_ref[rows, :].astype(F32))
        y = _dot(a_ref[rows, :], wo_ref[:A_WIDTH, :]) + _dot(b_out.astype(BF16), wo_ref[A_WIDTH:, :])
        o_ref[rows, :] = x_ref[rows, :] + y


def even_tail(z, a_out, x, pool_w, pool_scale, w_out, layer, *, batch, seq, tm, sub):
    vb_col = 4 * A_WIDTH // B_WIDTH
    halo_per_tile = tm // POOL_HALO
    return pl.pallas_call(
        functools.partial(_even_tail_kernel, tm=tm, sub=sub),
        out_shape=jax.ShapeDtypeStruct((batch, seq, D_MODEL), F32),
        grid=(batch, seq // tm),
        in_specs=[
            pl.BlockSpec((None, tm, B_WIDTH), lambda b, t: (b, t, vb_col)),
            pl.BlockSpec((None, POOL_HALO, B_WIDTH),
                         lambda b, t: (b, jnp.maximum(t * halo_per_tile - 1, 0), vb_col)),
            pl.BlockSpec((None, tm, B_WIDTH), lambda b, t: (b, t, vb_col + 1)),
            pl.BlockSpec((None, tm, A_WIDTH), lambda b, t: (b, t, 0)),
            pl.BlockSpec((None, tm, D_MODEL), lambda b, t: (b, t, 0)),
            _layer_block((len(POOL_WINDOWS), B_GROUP, B_GROUP), layer),
            pl.BlockSpec((1, B_WIDTH), lambda b, t: (0, 0)),
            _layer_block((A_WIDTH + B_WIDTH, D_MODEL), layer),
        ],
        out_specs=pl.BlockSpec((None, tm, D_MODEL), lambda b, t: (b, t, 0)),
        scratch_shapes=[pltpu.VMEM((len(POOL_WINDOWS), B_GROUP, B_GROUP), BF16),
                        pltpu.VMEM((A_WIDTH + B_WIDTH, D_MODEL), BF16)],
        compiler_params=_params("arbitrary", "arbitrary"),
        name="even_tail",
    )(z, z, z, a_out, x, pool_w, pool_scale.reshape(1, B_WIDTH), w_out)


def _s5_prep_kernel(ar_ref, ai_ref, ldt_ref, br_ref, bi_ref, abr_ref, abi_ref, bbr_ref, bbi_ref):
    ar, ai = ar_ref[...], ai_ref[...]
    dt = jnp.exp(ldt_ref[...])
    mag = jnp.exp(dt * ar)
    abar_re = mag * jnp.cos(dt * ai)
    abar_im = mag * jnp.sin(dt * ai)
    nr, ni = abar_re - 1.0, abar_im
    inv = 1.0 / (ar * ar + ai * ai)
    coef_re = (nr * ar + ni * ai) * inv
    coef_im = (ni * ar - nr * ai) * inv
    br, bi = br_ref[...], bi_ref[...]
    abr_ref[...] = abar_re
    abi_ref[...] = abar_im
    bbr_ref[...] = coef_re * br - coef_im * bi
    bbi_ref[...] = coef_re * bi + coef_im * br


def _expand_block_diag(t, reps, row_group):
    rows, cols = t.shape
    src = lax.broadcasted_iota(jnp.int32, (cols, reps * cols), 0)
    dst = lax.broadcasted_iota(jnp.int32, (cols, reps * cols), 1)
    tiled = _dot(t, jnp.where(dst % cols == src, 1.0, 0.0).astype(BF16))
    r = lax.broadcasted_iota(jnp.int32, tiled.shape, 0) // row_group
    c = lax.broadcasted_iota(jnp.int32, tiled.shape, 1) // cols
    return jnp.where(r == c, tiled, 0.0).astype(BF16)


def _s5_expand_kernel(bbr_ref, bbi_ref, cr_ref, ci_ref, wb_ref, wcr_ref, wci_ref):
    gpp = S5_GROUPS // S5_PARTS
    wb_ref[...] = jnp.concatenate(
        [_expand_block_diag(ref[...].astype(BF16), gpp, S5_GROUP_DIM) for ref in (bbr_ref, bbi_ref)], axis=1)
    wcr_ref[...] = _expand_block_diag(cr_ref[...].astype(BF16), gpp, S5_STATE)
    wci_ref[...] = _expand_block_diag(ci_ref[...].astype(BF16), gpp, S5_STATE)


def s5_prep(a_re, a_im, log_dt, b_re, b_im, c_re, c_im):
    n_l = a_re.shape[0]
    rows = n_l * D_WIDTH
    rep = lambda t: jnp.repeat(t.reshape(n_l * S5_GROUPS, S5_STATE), S5_GROUP_DIM, axis=0)
    to_rows = lambda t: t.transpose(0, 1, 3, 2).reshape(rows, S5_STATE)
    shp = jax.ShapeDtypeStruct((rows, S5_STATE), F32)
    abr, abi, bbr, bbi = pl.pallas_call(
        _s5_prep_kernel, out_shape=(shp, shp, shp, shp), name="s5_prep",
    )(rep(a_re), rep(a_im), rep(jnp.broadcast_to(log_dt[:, :, None], (n_l, S5_GROUPS, S5_STATE))),
      to_rows(b_re), to_rows(b_im))

    n_blk = n_l * S5_PARTS
    in_rows = lambda t: t.reshape(n_blk, S5_PART_IN, S5_STATE)
    out_rows = lambda t: t.transpose(0, 1, 3, 2).reshape(n_blk, S5_PART_STATE, S5_GROUP_DIM)
    blk = lambda r, c: pl.BlockSpec((None, r, c), lambda i: (i, 0, 0))
    w_b, w_cr, w_ci = pl.pallas_call(
        _s5_expand_kernel,
        out_shape=(jax.ShapeDtypeStruct((n_blk, S5_PART_IN, 2 * S5_PART_STATE), BF16),
                   jax.ShapeDtypeStruct((n_blk, S5_PART_STATE, S5_PART_IN), BF16),
                   jax.ShapeDtypeStruct((n_blk, S5_PART_STATE, S5_PART_IN), BF16)),
        grid=(n_blk,),
        in_specs=[blk(S5_PART_IN, S5_STATE), blk(S5_PART_IN, S5_STATE),
                  blk(S5_PART_STATE, S5_GROUP_DIM), blk(S5_PART_STATE, S5_GROUP_DIM)],
        out_specs=(blk(S5_PART_IN, 2 * S5_PART_STATE), blk(S5_PART_STATE, S5_PART_IN),
                   blk(S5_PART_STATE, S5_PART_IN)),
        compiler_params=_params("parallel"),
        name="s5_expand",
    )(in_rows(bbr), in_rows(bbi), out_rows(c_re), out_rows(c_im))
    per_layer = lambda t: t.reshape((n_l, S5_PARTS) + t.shape[1:])
    abar_re = abr[::S5_GROUP_DIM].reshape(n_l, 1, S5_LANES)
    abar_im = abi[::S5_GROUP_DIM].reshape(n_l, 1, S5_LANES)
    return per_layer(w_b), per_layer(w_cr), per_layer(w_ci), abar_re, abar_im


def _s5_kernel(xd_ref, wb_ref, are_ref, aim_ref, wcr_ref, wci_ref, dsk_ref, w1_32_ref, w2_32_ref,
               o_ref, u_ref, y_ref, bu_ref, h_ref, w1_ref, w2_ref, *, lc, nb, lane_chunk):
    @pl.when(pl.program_id(0) == 0)
    def _():
        h_ref[...] = jnp.zeros_like(h_ref)

    _cast_weights_once([(w1_32_ref, w1_ref), (w2_32_ref, w2_ref)], grid_rank=1)
    n_slab = D_WIDTH // LANES
    for b in range(nb):
        for s in range(n_slab):
            col = b * 2 * D_WIDTH + s * LANES
            u_ref[s, pl.ds(b, lc, stride=nb), :] = xd_ref[:, col:col + LANES].astype(F32)
    u = jnp.concatenate([u_ref[s] for s in range(n_slab)], axis=1)
    ub = u.astype(BF16)
    width = 2 * S5_PART_STATE
    def project(part):
        bu_ref[:, part * width:(part + 1) * width] = _dot(
            ub[:, part * S5_PART_IN:(part + 1) * S5_PART_IN], wb_ref[part])

    def scan(part):
        for c in range(S5_PART_STATE // lane_chunk):
            re = slice(part * width + c * lane_chunk, part * width + (c + 1) * lane_chunk)
            im = slice(re.start + S5_PART_STATE, re.stop + S5_PART_STATE)
            ab = slice(part * S5_PART_STATE + c * lane_chunk, part * S5_PART_STATE + (c + 1) * lane_chunk)
            a_r = jnp.broadcast_to(are_ref[:, ab], (nb, lane_chunk))
            a_i = jnp.broadcast_to(aim_ref[:, ab], (nb, lane_chunk))
            h_r, h_i = h_ref[:, re], h_ref[:, im]
            for t in range(lc):
                row = slice(t * nb, (t + 1) * nb)
                h_r, h_i = (a_r * h_r - a_i * h_i + bu_ref[row, re], a_r * h_i + a_i * h_r + bu_ref[row, im])
                bu_ref[row, re] = h_r
                bu_ref[row, im] = h_i
            h_ref[:, re] = h_r
            h_ref[:, im] = h_i

    def readout(part):
        re = slice(part * width, part * width + S5_PART_STATE)
        im = slice(re.stop, re.stop + S5_PART_STATE)
        return _dot(bu_ref[:, re].astype(BF16), wcr_ref[part]) - _dot(bu_ref[:, im].astype(BF16), wci_ref[part])

    ys = []
    project(0)
    for part in range(S5_PARTS):
        if part + 1 < S5_PARTS:
            project(part + 1)
        scan(part)
        ys.append(readout(part))
    y = jnp.concatenate(ys, axis=1) + dsk_ref[...] * u
    y = jax.nn.gelu(y).astype(BF16)
    out = _dot(y, w1_ref[...]) * jax.nn.sigmoid(_dot(y, w2_ref[...]))
    for s in range(n_slab):
        y_ref[s] = out[:, s * LANES:(s + 1) * LANES]
    o_ref[...] = jnp.concatenate(
        [y_ref[s, pl.ds(b, lc, stride=nb), :] for b in range(nb) for s in range(n_slab)], axis=1).astype(o_ref.dtype)


def s5_layer(zd, w_b, abar_re, abar_im, w_cr, w_ci, d_skip, w1, w2, layer, *, batch, seq, lc):
    rows = lc * batch
    n_slab = D_WIDTH // LANES
    return pl.pallas_call(
        functools.partial(_s5_kernel, lc=lc, nb=batch, lane_chunk=512),
        out_shape=jax.ShapeDtypeStruct((seq, batch * D_WIDTH), BF16),
        grid=(seq // lc,),
        in_specs=[
            pl.BlockSpec((lc, batch * 2 * D_WIDTH), lambda c: (c, 0)),
            _layer_block((S5_PARTS, S5_PART_IN, 2 * S5_PART_STATE), layer),
            _layer_block((1, S5_LANES), layer),
            _layer_block((1, S5_LANES), layer),
            _layer_block((S5_PARTS, S5_PART_STATE, S5_PART_IN), layer),
            _layer_block((S5_PARTS, S5_PART_STATE, S5_PART_IN), layer),
            pl.BlockSpec((1, D_WIDTH), lambda c: (0, 0)),
            _layer_block((D_WIDTH, D_WIDTH), layer),
            _layer_block((D_WIDTH, D_WIDTH), layer),
        ],
        out_specs=pl.BlockSpec((lc, batch * D_WIDTH), lambda c: (c, 0)),
        scratch_shapes=[
            pltpu.VMEM((n_slab, rows, LANES), F32),
            pltpu.VMEM((n_slab, rows, LANES), F32),
            pltpu.VMEM((rows, 2 * S5_LANES), F32),
            pltpu.VMEM((batch, 2 * S5_LANES), F32),
            pltpu.VMEM((D_WIDTH, D_WIDTH), BF16),
            pltpu.VMEM((D_WIDTH, D_WIDTH), BF16),
        ],
        compiler_params=_params("arbitrary"),
        name="s5_layer",
    )(zd, w_b, abar_re, abar_im, w_cr, w_ci, d_skip.reshape(1, D_WIDTH), w1, w2)


def _odd_tail_kernel(u_ref, v_ref, gc_ref, d_ref, gd_ref, x_ref, lng_ref, lnb_ref, ws_ref, bs_ref, wo32_ref,
                     o_ref, c_ref, wo_ref, *, tm, sub):
    _cast_weights_once([(wo32_ref, wo_ref)], grid_rank=2)
    ri = lax.broadcasted_iota(jnp.int32, (C_CHUNK, C_CHUNK), 0)
    ci = lax.broadcasted_iota(jnp.int32, (C_CHUNK, C_CHUNK), 1)
    causal = ri >= ci
    w_s = [jnp.where(causal, ws_ref[g], 0.0).astype(BF16) for g in range(C_GROUPS)]
    for r in range(tm // sub):
        tile = slice(r * sub, (r + 1) * sub)
        v = v_ref[tile, :].astype(F32)
        mu = jnp.mean(v, axis=-1, keepdims=True)
        vc = v - mu
        var = jnp.mean(vc * vc, axis=-1, keepdims=True)
        vn = (vc * lax.rsqrt(var + EPS) * lng_ref[...] + lnb_ref[...]).astype(BF16)
        for g in range(C_GROUPS):
            cols = slice(g * C_GROUP_DIM, (g + 1) * C_GROUP_DIM)
            b_g = bs_ref[:, g:g + 1]
            for c in range(sub // C_CHUNK):
                rows = slice(r * sub + c * C_CHUNK, r * sub + (c + 1) * C_CHUNK)
                mixed = _dot(w_s[g], vn[c * C_CHUNK:(c + 1) * C_CHUNK, cols]) + b_g
                gate = _silu(gc_ref[rows, cols].astype(F32))
                c_ref[rows, cols] = (u_ref[rows, cols].astype(F32) * mixed * gate).astype(BF16)
        d = (d_ref[tile, :].astype(F32) * _silu(gd_ref[tile, :].astype(F32))).astype(BF16)
        y = _dot(c_ref[tile, :], wo_ref[:C_WIDTH, :]) + _dot(d, wo_ref[C_WIDTH:, :])
        o_ref[tile, :] = x_ref[tile, :] + y


def odd_tail(zc, zd, d_out, x, ln_g, ln_b, w_s, b_s, w_out, layer, *, batch, seq, tm, sub):
    return pl.pallas_call(
        functools.partial(_odd_tail_kernel, tm=tm, sub=sub),
        out_shape=jax.ShapeDtypeStruct((batch, seq, D_MODEL), F32),
        grid=(batch, seq // tm),
        in_specs=[
            pl.BlockSpec((None, tm, C_WIDTH), lambda b, t: (b, t, 0)),
            pl.BlockSpec((None, tm, C_WIDTH), lambda b, t: (b, t, 1)),
            pl.BlockSpec((None, tm, C_WIDTH), lambda b, t: (b, t, 2)),
            pl.BlockSpec((tm, D_WIDTH), lambda b, t: (t, b)),
            pl.BlockSpec((tm, D_WIDTH), lambda b, t: (t, 2 * b + 1)),
            pl.BlockSpec((None, tm, D_MODEL), lambda b, t: (b, t, 0)),
            pl.BlockSpec((1, C_WIDTH), lambda b, t: (0, 0)),
            pl.BlockSpec((1, C_WIDTH), lambda b, t: (0, 0)),
            _layer_block((C_GROUPS, C_CHUNK, C_CHUNK), layer),
            pl.BlockSpec((C_CHUNK, C_GROUPS), lambda b, t: (0, 0)),
            _layer_block((C_WIDTH + D_WIDTH, D_MODEL), layer),
        ],
        out_specs=pl.BlockSpec((None, tm, D_MODEL), lambda b, t: (b, t, 0)),
        scratch_shapes=[pltpu.VMEM((tm, C_WIDTH), BF16), pltpu.VMEM((C_WIDTH + D_WIDTH, D_MODEL), BF16)],
        compiler_params=_params("arbitrary", "arbitrary"),
        name="odd_tail",
    )(zc, zc, zc, d_out, zd, x, ln_g.reshape(1, C_WIDTH), ln_b.reshape(1, C_WIDTH), w_s, b_s.T, w_out)


def _xattn_kernel(x_ref, g_ref, wq32_ref, kv_ref, wo32_ref, pg_ref, *refs, final, sub):
    if final:
        o_ref, wq_ref, wo_ref = refs
    else:
        o_ref, xn_ref, wq_ref, wo_ref = refs
    _cast_weights_once([(wq32_ref, wq_ref), (wo32_ref, wo_ref)], grid_rank=2)
    for r in range(x_ref.shape[0] // sub):
        rows = slice(r * sub, (r + 1) * sub)
        x = x_ref[rows, :]
        q = _dot(_rms(x, g_ref[...]).astype(BF16), wq_ref[...])
        heads = []
        for h in range(X_HEADS):
            ks = slice(h * X_HEAD_DIM, (h + 1) * X_HEAD_DIM)
            vs = slice(D_MODEL + h * X_HEAD_DIM, D_MODEL + (h + 1) * X_HEAD_DIM)
            s = _dot_nt(q[:, ks].astype(BF16), kv_ref[:, ks]) * (X_HEAD_DIM ** -0.5)
            m = jnp.max(s, axis=-1, keepdims=True)
            p = jnp.exp(s - m)
            den = jnp.sum(p, axis=-1, keepdims=True)
            heads.append((_dot(p.astype(BF16), kv_ref[:, vs]) / den).astype(BF16))
        y = x + _dot(jnp.concatenate(heads, axis=1), wo_ref[...])
        if final:
            o_ref[rows, :] = _rms(y, pg_ref[...])
        else:
            o_ref[rows, :] = y
            xn_ref[rows, :] = _rms(y, pg_ref[...]).astype(xn_ref.dtype)


def cross_attention(x, g, w_q, kv, w_o, post_g, layer, *, batch, seq, mem_len, tq, sub, final):
    tile = pl.BlockSpec((None, tq, D_MODEL), lambda b, t: (b, t, 0))
    res = jax.ShapeDtypeStruct((batch, seq, D_MODEL), F32)
    nxt = jax.ShapeDtypeStruct((batch, seq, D_MODEL), BF16)
    return pl.pallas_call(
        functools.partial(_xattn_kernel, final=final, sub=sub),
        out_shape=res if final else (res, nxt),
        grid=(batch, seq // tq),
        in_specs=[
            tile,
            pl.BlockSpec((1, D_MODEL), lambda b, t: (0, 0)),
            _layer_block((D_MODEL, D_MODEL), layer),
            pl.BlockSpec((None, None, mem_len, 2 * D_MODEL), lambda b, t: (layer, b, 0, 0)),
            _layer_block((D_MODEL, D_MODEL), layer),
            pl.BlockSpec((1, D_MODEL), lambda b, t: (0, 0)),
        ],
        out_specs=tile if final else (tile, tile),
        scratch_shapes=[pltpu.VMEM((D_MODEL, D_MODEL), BF16), pltpu.VMEM((D_MODEL, D_MODEL), BF16)],
        compiler_params=_params("arbitrary", "arbitrary"),
        name="cross_attention",
    )(x, g.reshape(1, D_MODEL), w_q, kv, w_o, post_g.reshape(1, D_MODEL))


def kernel(x, mem, norm_ab, w_in_ab, pool_w, pool_scale, w_out_ab, norm_cd, w_in_cd, sgu_ln_g, sgu_ln_b,
           sgu_w, sgu_b, s5_a_re, s5_a_im, s5_log_dt, s5_b_re, s5_b_im, s5_c_re, s5_c_im, s5_d,
           glu_w1, glu_w2, w_out_cd, norm_x, w_xq, w_xkv, w_xo, mem_norm, final_norm):
    batch, seq, _ = x.shape
    mem_len = mem.shape[1]
    depth = norm_x.shape[0]
    tokens = batch * seq
    tm_proj = 1024
    tiles_per_row = seq // tm_proj
    mixer_norm = lambda layer: (norm_ab if layer % 2 == 0 else norm_cd)[layer // 2]

    kv = mem_kv_proj(mem.reshape(batch * mem_len, D_MODEL), mem_norm, w_xkv, tn=1024)
    kv = kv.reshape(depth, batch, mem_len, 2 * D_MODEL)
    w_b, w_cr, w_ci, abar_re, abar_im = s5_prep(s5_a_re, s5_a_im, s5_log_dt, s5_b_re, s5_b_im, s5_c_re, s5_c_im)
    xn = rms_cast(x.reshape(tokens, D_MODEL), mixer_norm(0), tm=tm_proj)
    for layer in range(depth):
        i = layer // 2
        if layer % 2 == 0:
            z = in_proj(xn, w_in_ab, i, n=w_in_ab.shape[2], tm=tm_proj, tn=2048)
            z = z.reshape(batch, seq, z.shape[1])
            a_out = dilated_attention(z, batch=batch, seq=seq)
            x = even_tail(z, a_out, x, pool_w, pool_scale[i], w_out_ab, i, batch=batch, seq=seq,
                          tm=1024, sub=512)
        else:
            n_c = 3 * C_WIDTH
            zc = in_proj(xn, w_in_cd, i, n=n_c, tm=tm_proj, tn=1024)
            zd = in_proj(xn, w_in_cd, i, n=2 * D_WIDTH, col0=n_c, tm=tm_proj, tn=2 * D_WIDTH,
                         out_shape=(seq, batch * 2 * D_WIDTH),
                         out_map=lambda r, j: (r % tiles_per_row, r // tiles_per_row))
            d_out = s5_layer(zd, w_b, abar_re, abar_im, w_cr, w_ci, s5_d[i], glu_w1, glu_w2, i,
                             batch=batch, seq=seq, lc=64)
            x = odd_tail(zc.reshape(batch, seq, n_c), zd, d_out, x, sgu_ln_g[i], sgu_ln_b[i], sgu_w,
                         sgu_b[i], w_out_cd, i, batch=batch, seq=seq, tm=1024, sub=512)
        final = layer == depth - 1
        out = cross_attention(x, norm_x[layer], w_xq, kv, w_xo, final_norm if final else mixer_norm(layer + 1),
                              layer, batch=batch, seq=seq, mem_len=mem_len, tq=1024, sub=512, final=final)
        if final:
            return out
        x, xn = out[0], out[1].reshape(tokens, D_MODEL)
```

```python
import functools

import jax
import jax.numpy as jnp
from jax import lax
from jax.experimental import pallas as pl
from jax.experimental.pallas import tpu as pltpu

F32 = jnp.float32
BF16 = jnp.bfloat16

LANES = 128
D_MODEL = 1024
A_WIDTH = 1024
A_HEAD_DIM = 64
A_BLOCK = 128
A_DILATIONS = (1, 4, 16)
DIL_STEP = 4
B_WIDTH = 1024
POOL_WINDOWS = (2, 4, 8, 16)
B_GROUP = 256
POOL_HALO = 16
C_WIDTH = 1024
C_CHUNK = 128
C_GROUPS = 4
C_GROUP_DIM = 256
D_WIDTH = 512
S5_GROUPS = 32
S5_GROUP_DIM = 16
S5_STATE = 64
S5_LANES = S5_GROUPS * S5_STATE
S5_PARTS = 2
S5_PART_IN = D_WIDTH // S5_PARTS
S5_PART_STATE = S5_LANES // S5_PARTS
X_HEADS = 4
X_HEAD_DIM = 256
EPS = 1e-6
NEG = -1e30
LOG2E = 1.4426950408889634

VMEM_LIMIT = 48 * 1024 * 1024


def _params(*sem):
    return pltpu.CompilerParams(dimension_semantics=sem, vmem_limit_bytes=VMEM_LIMIT)


def _rms(x, g):
    ms = jnp.mean(x * x, axis=-1, keepdims=True)
    return x * lax.rsqrt(ms + EPS) * g


def _silu(x):
    return x * jax.nn.sigmoid(x)


def _dot(a, b):
    return jnp.dot(a, b, preferred_element_type=F32)


def _layer_block(shape, layer):
    return pl.BlockSpec((None,) + tuple(shape), lambda *_: (layer,) + (0,) * len(shape),
                        pipeline_mode=pl.Buffered(1))


def _cast_weights_once(pairs, grid_rank):
    first = pl.program_id(0) == 0
    for axis in range(1, grid_rank):
        first = first & (pl.program_id(axis) == 0)

    @pl.when(first)
    def _():
        for src, dst in pairs:
            dst[...] = src[...].astype(dst.dtype)


def _dot_nt(a, b):
    return lax.dot_general(a, b, (((1,), (1,)), ((), ())), preferred_element_type=F32)


def _rms_cast_kernel(x_ref, g_ref, o_ref):
    o_ref[...] = _rms(x_ref[...], g_ref[...]).astype(o_ref.dtype)


def rms_cast(x, g, *, tm):
    m, k = x.shape
    return pl.pallas_call(
        _rms_cast_kernel,
        out_shape=jax.ShapeDtypeStruct((m, k), BF16),
        grid=(m // tm,),
        in_specs=[pl.BlockSpec((tm, k), lambda i: (i, 0)), pl.BlockSpec((1, k), lambda i: (0, 0))],
        out_specs=pl.BlockSpec((tm, k), lambda i: (i, 0)),
        compiler_params=_params("parallel"),
        name="rms_cast",
    )(x, g.reshape(1, k))


def _in_proj_kernel(x_ref, w32_ref, o_ref, w_ref):
    @pl.when(pl.program_id(1) == 0)
    def _():
        w_ref[...] = w32_ref[...].astype(BF16)

    o_ref[...] = _dot(x_ref[...], w_ref[...]).astype(o_ref.dtype)


def in_proj(xn, w, layer, *, n, tm, tn, col0=0, out_shape=None, out_map=None):
    m, k = xn.shape
    col_block0 = col0 // tn
    if out_shape is None:
        out_shape, out_map = (m, n), (lambda i, j: (i, j))
    return pl.pallas_call(
        _in_proj_kernel,
        out_shape=jax.ShapeDtypeStruct(out_shape, BF16),
        grid=(n // tn, m // tm),
        in_specs=[
            pl.BlockSpec((tm, k), lambda j, i: (i, 0)),
            pl.BlockSpec((None, k, tn), lambda j, i: (layer, 0, col_block0 + j)),
        ],
        out_specs=pl.BlockSpec((tm, tn), lambda j, i: out_map(i, j)),
        scratch_shapes=[pltpu.VMEM((k, tn), BF16)],
        compiler_params=_params("arbitrary", "arbitrary"),
        name="in_proj",
    )(xn, w)


def _mem_proj_kernel(x_ref, g_ref, w_ref, o_ref, xn_ref):
    @pl.when((pl.program_id(0) == 0) & (pl.program_id(1) == 0))
    def _():
        xn_ref[...] = _rms(x_ref[...], g_ref[...]).astype(BF16)

    o_ref[...] = _dot(xn_ref[...], w_ref[...].astype(BF16)).astype(o_ref.dtype)


def mem_kv_proj(mem, g, w, *, tn):
    m, k = mem.shape
    n_layers, _, n = w.shape
    return pl.pallas_call(
        _mem_proj_kernel,
        out_shape=jax.ShapeDtypeStruct((n_layers, m, n), BF16),
        grid=(n_layers, n // tn),
        in_specs=[
            pl.BlockSpec((m, k), lambda l, j: (0, 0)),
            pl.BlockSpec((1, k), lambda l, j: (0, 0)),
            pl.BlockSpec((None, k, tn), lambda l, j: (l, 0, j)),
        ],
        out_specs=pl.BlockSpec((None, m, tn), lambda l, j: (l, 0, j)),
        scratch_shapes=[pltpu.VMEM((m, k), BF16)],
        compiler_params=_params("arbitrary", "arbitrary"),
        name="mem_kv_proj",
    )(mem, g.reshape(1, k), w)


def _dilated_kernel(q_ref, k_ref, v_ref, g_ref, o_ref, qs_ref, ks_ref, vs_ref, gs_ref, os_ref, num_ref, den_ref,
                    m_ref, bias_ref, *, seq, unroll):
    assert A_DILATIONS == (1, DIL_STEP, DIL_STEP * DIL_STEP)
    qi = lax.broadcasted_iota(jnp.int32, (A_BLOCK, 2 * A_BLOCK), 0)
    kj = lax.broadcasted_iota(jnp.int32, (A_BLOCK, 2 * A_BLOCK), 1)
    dist = qi + A_BLOCK - kj
    band = (dist >= 0) & (dist <= A_BLOCK)
    bias_ref[1] = jnp.where(band, 0.0, NEG).astype(BF16)
    bias_ref[0] = jnp.where(band & (kj >= A_BLOCK), 0.0, NEG).astype(BF16)

    len_s = seq // DIL_STEP
    len_w = len_s // DIL_STEP
    blocks_s = len_s // A_BLOCK
    q_scale = A_HEAD_DIM ** -0.5 * LOG2E
    for src, dst, scale in ((q_ref, qs_ref, q_scale), (k_ref, ks_ref, None), (v_ref, vs_ref, None)):
        nat = src[...].astype(F32)
        dst[0] = nat if scale is None else nat * scale
        for b in range(DIL_STEP):
            dst[1, b * len_s:(b + 1) * len_s, :] = dst[0, pl.ds(b, len_s, stride=DIL_STEP), :]
        for c in range(DIL_STEP * DIL_STEP):
            b, a = divmod(c, DIL_STEP)
            dst[2, c * len_w:(c + 1) * len_w, :] = dst[1, pl.ds(b * len_s + a, len_w, stride=DIL_STEP), :]
    gs_ref[0] = g_ref[...].astype(F32)

    head0 = lax.broadcasted_iota(jnp.int32, (A_BLOCK, LANES), 1) < A_HEAD_DIM

    def load(slab, cur, prev):
        q = qs_ref[slab, cur, :].astype(BF16)
        if slab == 0:
            kk = jnp.concatenate([k_ref[prev, :], k_ref[cur, :]], axis=0)
            vv = jnp.concatenate([v_ref[prev, :], v_ref[cur, :]], axis=0)
            return q, kk, vv
        kk = jnp.concatenate([ks_ref[slab, prev, :], ks_ref[slab, cur, :]], axis=0).astype(BF16)
        vv = jnp.concatenate([vs_ref[slab, prev, :], vs_ref[slab, cur, :]], axis=0).astype(BF16)
        return q, kk, vv

    one_b, zero_b = jnp.ones((), BF16), jnp.zeros((), BF16)

    def attend(q, k, v, bias):
        own0 = lax.broadcasted_iota(jnp.int32, k.shape, 1).astype(BF16) < A_HEAD_DIM
        k_bd = jnp.concatenate([jnp.where(own0, k, zero_b), jnp.where(own0, zero_b, k)], axis=0)
        s = _dot_nt(q, k_bd).astype(BF16) + jnp.concatenate([bias, bias], axis=1)
        s0, s1 = s[:, :k.shape[0]], s[:, k.shape[0]:]
        m0 = jnp.max(s0, axis=-1, keepdims=True)
        m1 = jnp.max(s1, axis=-1, keepdims=True)
        p = jnp.concatenate([jnp.exp2(s0 - m0), jnp.exp2(s1 - m1)], axis=1)
        v_bd = jnp.concatenate([
            jnp.concatenate([jnp.where(own0, v, zero_b), jnp.where(own0, one_b, zero_b)], axis=1),
            jnp.concatenate([jnp.where(own0, zero_b, v), jnp.where(own0, zero_b, one_b)], axis=1)], axis=0)
        res = _dot(p, v_bd)
        m = jnp.where(head0, jnp.broadcast_to(m0, (A_BLOCK, LANES)).astype(F32),
                      jnp.broadcast_to(m1, (A_BLOCK, LANES)).astype(F32))
        return res[:, :LANES], res[:, LANES:], m

    def attend_single(cur):
        return attend(qs_ref[2, cur, :].astype(BF16), ks_ref[2, cur, :].astype(BF16),
                      vs_ref[2, cur, :].astype(BF16), bias_ref[0][:, A_BLOCK:])

    def aligned(start):
        return pl.ds(pl.multiple_of(start, A_BLOCK), A_BLOCK)

    def banded_rows(idx, blocks_per_seq):
        start = idx * A_BLOCK
        first = idx % blocks_per_seq == 0
        return aligned(start), aligned(jnp.where(first, start, start - A_BLOCK)), bias_ref[jnp.where(first, 0, 1)]

    def merge(num_a, den_a, m_a, num_b, den_b, m_b):
        m = jnp.maximum(m_a, m_b)
        w_a = jnp.exp2(m_a - m)
        w_b = jnp.exp2(m_b - m)
        return w_a * num_a + w_b * num_b, w_a * den_a + w_b * den_b, m

    def narrow_body(idx, carry):
        cur, prev, bias = banded_rows(idx, n_blocks)
        num, den, m = attend(*load(0, cur, prev), bias)
        num_ref[0, cur, :] = num
        den_ref[0, cur, :] = den
        m_ref[0, cur, :] = m
        return carry

    def step_body(idx, carry):
        cur, prev, bias = banded_rows(idx, blocks_s)
        natural = pl.ds(idx // blocks_s + DIL_STEP * A_BLOCK * (idx % blocks_s), A_BLOCK, stride=DIL_STEP)
        num, den, m = merge(*attend(*load(1, cur, prev), bias),
                            num_ref[0, natural, :], den_ref[0, natural, :], m_ref[0, natural, :])
        num_ref[1, cur, :] = num
        den_ref[1, cur, :] = den
        m_ref[1, cur, :] = m
        gs_ref[1, cur, :] = gs_ref[0, natural, :]
        return carry

    def wide_body(idx, carry):
        cur = aligned(idx * A_BLOCK)
        by_step = pl.ds((idx // DIL_STEP) * len_s + idx % DIL_STEP, A_BLOCK, stride=DIL_STEP)
        natural = pl.ds(DIL_STEP * (idx % DIL_STEP) + idx // DIL_STEP, A_BLOCK, stride=DIL_STEP * DIL_STEP)
        num, den, _ = merge(*attend_single(cur),
                            num_ref[1, by_step, :], den_ref[1, by_step, :], m_ref[1, by_step, :])
        os_ref[natural, :] = num / den * _silu(gs_ref[1, by_step, :])
        return carry

    assert len_w == A_BLOCK
    n_blocks = seq // A_BLOCK
    for body in (narrow_body, step_body, wide_body):
        lax.fori_loop(0, n_blocks, body, 0, unroll=unroll)
    o_ref[...] = os_ref[...].astype(o_ref.dtype)


def dilated_attention(z, *, batch, seq, unroll=16):
    n_pair = A_WIDTH // LANES
    n_pat = len(A_DILATIONS)
    blk = (None, seq, LANES)
    slabs = pltpu.VMEM((n_pat, seq, LANES), F32)
    partial = pltpu.VMEM((2, seq, LANES), F32)
    return pl.pallas_call(
        functools.partial(_dilated_kernel, seq=seq, unroll=unroll),
        out_shape=jax.ShapeDtypeStruct((batch, seq, A_WIDTH), BF16),
        grid=(batch, n_pair),
        in_specs=[
            pl.BlockSpec(blk, lambda b, h: (b, 0, h)),
            pl.BlockSpec(blk, lambda b, h: (b, 0, n_pair + h)),
            pl.BlockSpec(blk, lambda b, h: (b, 0, 2 * n_pair + h)),
            pl.BlockSpec(blk, lambda b, h: (b, 0, 3 * n_pair + h)),
        ],
        out_specs=pl.BlockSpec(blk, lambda b, h: (b, 0, h)),
        scratch_shapes=[
            slabs, slabs, slabs,
            pltpu.VMEM((2, seq, LANES), F32),
            pltpu.VMEM((seq, LANES), F32),
            partial, partial, partial,
            pltpu.VMEM((2, A_BLOCK, 2 * A_BLOCK), BF16),
        ],
        compiler_params=_params("parallel", "parallel"),
        name="dilated_attention",
    )(z, z, z, z)


def _even_tail_kernel(vb_ref, halo_ref, gb_ref, a_ref, x_ref, pw32_ref, ps_ref, wo32_ref, o_ref, pw_ref, wo_ref,
                      *, tm, sub):
    _cast_weights_once([(pw32_ref, pw_ref), (wo32_ref, wo_ref)], grid_rank=2)
    ti = pl.program_id(1)
    for r in range(tm // sub):
        rows = slice(r * sub, (r + 1) * sub)
        v = vb_ref[rows, :].astype(F32)
        if r == 0:
            halo = jnp.where(ti > 0, halo_ref[...].astype(F32), 0.0)
        else:
            halo = vb_ref[r * sub - POOL_HALO:r * sub, :].astype(F32)
        xc = jnp.concatenate([halo, v], axis=0)
        s2 = xc + pltpu.roll(xc, 1, 0)
        t4 = s2[:, B_GROUP:]
        s4 = t4 + pltpu.roll(t4, 2, 0)
        t8 = s4[:, B_GROUP:]
        s8 = t8 + pltpu.roll(t8, 4, 0)
        t16 = s8[:, B_GROUP:]
        s16 = t16 + pltpu.roll(t16, 8, 0)
        sums = (s2[:, :B_GROUP], s4[:, :B_GROUP], s8[:, :B_GROUP], s16)
        pos = (ti * tm + r * sub + 1 + lax.broadcasted_iota(jnp.int32, (sub, 1), 0)).astype(F32)
        mixed = []
        for g, w in enumerate(POOL_WINDOWS):
            mean = sums[g][POOL_HALO:, :] / jnp.minimum(pos, float(w))
            pooled = mean - v[:, g * B_GROUP:(g + 1) * B_GROUP]
            mixed.append(_dot(pooled.astype(BF16), pw_ref[g]))
        b_out = (jnp.concatenate(mixed, axis=1) * ps_ref[...]).astype(BF16) * _silu(gb_ref[rows, :])
        y = _dot(a_ref[rows, :], wo_ref[:A_WIDTH, :]) + _dot(b_out, wo_ref[A_WIDTH:, :])
        o_ref[rows, :] = x_ref[rows, :] + y


def even_tail(z, a_out, x, pool_w, pool_scale, w_out, layer, *, batch, seq, tm, sub):
    vb_col = 4 * A_WIDTH // B_WIDTH
    halo_per_tile = tm // POOL_HALO
    return pl.pallas_call(
        functools.partial(_even_tail_kernel, tm=tm, sub=sub),
        out_shape=jax.ShapeDtypeStruct((batch, seq, D_MODEL), F32),
        grid=(batch, seq // tm),
        in_specs=[
            pl.BlockSpec((None, tm, B_WIDTH), lambda b, t: (b, t, vb_col)),
            pl.BlockSpec((None, POOL_HALO, B_WIDTH),
                         lambda b, t: (b, jnp.maximum(t * halo_per_tile - 1, 0), vb_col)),
            pl.BlockSpec((None, tm, B_WIDTH), lambda b, t: (b, t, vb_col + 1)),
            pl.BlockSpec((None, tm, A_WIDTH), lambda b, t: (b, t, 0)),
            pl.BlockSpec((None, tm, D_MODEL), lambda b, t: (b, t, 0)),
            _layer_block((len(POOL_WINDOWS), B_GROUP, B_GROUP), layer),
            pl.BlockSpec((1, B_WIDTH), lambda b, t: (0, 0)),
            _layer_block((A_WIDTH + B_WIDTH, D_MODEL), layer),
        ],
        out_specs=pl.BlockSpec((None, tm, D_MODEL), lambda b, t: (b, t, 0)),
        scratch_shapes=[pltpu.VMEM((len(POOL_WINDOWS), B_GROUP, B_GROUP), BF16),
                        pltpu.VMEM((A_WIDTH + B_WIDTH, D_MODEL), BF16)],
        compiler_params=_params("arbitrary", "arbitrary"),
        name="even_tail",
    )(z, z, z, a_out, x, pool_w, pool_scale.reshape(1, B_WIDTH), w_out)


def _s5_prep_kernel(ar_ref, ai_ref, ldt_ref, br_ref, bi_ref, abr_ref, abi_ref, bbr_ref, bbi_ref):
    ar, ai = ar_ref[...], ai_ref[...]
    dt = jnp.exp(ldt_ref[...])
    mag = jnp.exp(dt * ar)
    abar_re = mag * jnp.cos(dt * ai)
    abar_im = mag * jnp.sin(dt * ai)
    nr, ni = abar_re - 1.0, abar_im
    inv = 1.0 / (ar * ar + ai * ai)
    coef_re = (nr * ar + ni * ai) * inv
    coef_im = (ni * ar - nr * ai) * inv
    br, bi = br_ref[...], bi_ref[...]
    abr_ref[...] = abar_re
    abi_ref[...] = abar_im
    bbr_ref[...] = coef_re * br - coef_im * bi
    bbi_ref[...] = coef_re * bi + coef_im * br


def _expand_block_diag(t, reps, row_group):
    rows, cols = t.shape
    src = lax.broadcasted_iota(jnp.int32, (cols, reps * cols), 0)
    dst = lax.broadcasted_iota(jnp.int32, (cols, reps * cols), 1)
    tiled = _dot(t, jnp.where(dst % cols == src, 1.0, 0.0).astype(BF16))
    r = lax.broadcasted_iota(jnp.int32, tiled.shape, 0) // row_group
    c = lax.broadcasted_iota(jnp.int32, tiled.shape, 1) // cols
    return jnp.where(r == c, tiled, 0.0).astype(BF16)


def _s5_expand_kernel(bbr_ref, bbi_ref, cr_ref, ci_ref, wb_ref, wcr_ref, wci_ref):
    gpp = S5_GROUPS // S5_PARTS
    wb_ref[...] = jnp.concatenate(
        [_expand_block_diag(ref[...].astype(BF16), gpp, S5_GROUP_DIM) for ref in (bbr_ref, bbi_ref)], axis=1)
    wcr_ref[...] = _expand_block_diag(cr_ref[...].astype(BF16), gpp, S5_STATE)
    wci_ref[...] = _expand_block_diag(ci_ref[...].astype(BF16), gpp, S5_STATE)


def s5_prep(a_re, a_im, log_dt, b_re, b_im, c_re, c_im):
    n_l = a_re.shape[0]
    rows = n_l * D_WIDTH
    rep = lambda t: jnp.repeat(t.reshape(n_l * S5_GROUPS, S5_STATE), S5_GROUP_DIM, axis=0)
    to_rows = lambda t: t.transpose(0, 1, 3, 2).reshape(rows, S5_STATE)
    shp = jax.ShapeDtypeStruct((rows, S5_STATE), F32)
    abr, abi, bbr, bbi = pl.pallas_call(
        _s5_prep_kernel, out_shape=(shp, shp, shp, shp), name="s5_prep",
    )(rep(a_re), rep(a_im), rep(jnp.broadcast_to(log_dt[:, :, None], (n_l, S5_GROUPS, S5_STATE))),
      to_rows(b_re), to_rows(b_im))

    n_blk = n_l * S5_PARTS
    in_rows = lambda t: t.reshape(n_blk, S5_PART_IN, S5_STATE)
    out_rows = lambda t: t.transpose(0, 1, 3, 2).reshape(n_blk, S5_PART_STATE, S5_GROUP_DIM)
    blk = lambda r, c: pl.BlockSpec((None, r, c), lambda i: (i, 0, 0))
    w_b, w_cr, w_ci = pl.pallas_call(
        _s5_expand_kernel,
        out_shape=(jax.ShapeDtypeStruct((n_blk, S5_PART_IN, 2 * S5_PART_STATE), BF16),
                   jax.ShapeDtypeStruct((n_blk, S5_PART_STATE, S5_PART_IN), BF16),
                   jax.ShapeDtypeStruct((n_blk, S5_PART_STATE, S5_PART_IN), BF16)),
        grid=(n_blk,),
        in_specs=[blk(S5_PART_IN, S5_STATE), blk(S5_PART_IN, S5_STATE),
                  blk(S5_PART_STATE, S5_GROUP_DIM), blk(S5_PART_STATE, S5_GROUP_DIM)],
        out_specs=(blk(S5_PART_IN, 2 * S5_PART_STATE), blk(S5_PART_STATE, S5_PART_IN),
                   blk(S5_PART_STATE, S5_PART_IN)),
        compiler_params=_params("parallel"),
        name="s5_expand",
    )(in_rows(bbr), in_rows(bbi), out_rows(c_re), out_rows(c_im))
    per_layer = lambda t: t.reshape((n_l, S5_PARTS) + t.shape[1:])
    abar_re = abr[::S5_GROUP_DIM].reshape(n_l, 1, S5_LANES)
    abar_im = abi[::S5_GROUP_DIM].reshape(n_l, 1, S5_LANES)
    return per_layer(w_b), per_layer(w_cr), per_layer(w_ci), abar_re, abar_im


def _s5_kernel(xd_ref, wb_ref, are_ref, aim_ref, wcr_ref, wci_ref, dsk_ref, w1_32_ref, w2_32_ref,
               o_ref, u_ref, y_ref, bu_ref, h_ref, w1_ref, w2_ref, *, lc, nb, lane_chunk):
    @pl.when(pl.program_id(0) == 0)
    def _():
        h_ref[...] = jnp.zeros_like(h_ref)

    _cast_weights_once([(w1_32_ref, w1_ref), (w2_32_ref, w2_ref)], grid_rank=1)
    n_slab = D_WIDTH // LANES
    for b in range(nb):
        for s in range(n_slab):
            col = b * 2 * D_WIDTH + s * LANES
            u_ref[s, pl.ds(b, lc, stride=nb), :] = xd_ref[:, col:col + LANES].astype(F32)
    u = jnp.concatenate([u_ref[s] for s in range(n_slab)], axis=1)
    ub = u.astype(BF16)
    width = 2 * S5_PART_STATE
    def project(part):
        bu_ref[:, part * width:(part + 1) * width] = _dot(
            ub[:, part * S5_PART_IN:(part + 1) * S5_PART_IN], wb_ref[part])

    def scan(part):
        for c in range(S5_PART_STATE // lane_chunk):
            re = slice(part * width + c * lane_chunk, part * width + (c + 1) * lane_chunk)
            im = slice(re.start + S5_PART_STATE, re.stop + S5_PART_STATE)
            ab = slice(part * S5_PART_STATE + c * lane_chunk, part * S5_PART_STATE + (c + 1) * lane_chunk)
            a_r = jnp.broadcast_to(are_ref[:, ab], (nb, lane_chunk))
            a_i = jnp.broadcast_to(aim_ref[:, ab], (nb, lane_chunk))
            h_r, h_i = h_ref[:, re], h_ref[:, im]
            for t in range(lc):
                row = slice(t * nb, (t + 1) * nb)
                h_r, h_i = (a_r * h_r - a_i * h_i + bu_ref[row, re], a_r * h_i + a_i * h_r + bu_ref[row, im])
                bu_ref[row, re] = h_r
                bu_ref[row, im] = h_i
            h_ref[:, re] = h_r
            h_ref[:, im] = h_i

    def readout(part):
        re = slice(part * width, part * width + S5_PART_STATE)
        im = slice(re.stop, re.stop + S5_PART_STATE)
        return _dot(bu_ref[:, re].astype(BF16), wcr_ref[part]) - _dot(bu_ref[:, im].astype(BF16), wci_ref[part])

    ys = []
    project(0)
    for part in range(S5_PARTS):
        if part + 1 < S5_PARTS:
            project(part + 1)
        scan(part)
        ys.append(readout(part))
    y = jnp.concatenate(ys, axis=1) + dsk_ref[...] * u
    y = jax.nn.gelu(y).astype(BF16)
    out = _dot(y, w1_ref[...]) * jax.nn.sigmoid(_dot(y, w2_ref[...]))
    for s in range(n_slab):
        y_ref[s] = out[:, s * LANES:(s + 1) * LANES]
    o_ref[...] = jnp.concatenate(
        [y_ref[s, pl.ds(b, lc, stride=nb), :] for b in range(nb) for s in range(n_slab)], axis=1).astype(o_ref.dtype)


def s5_layer(zd, w_b, abar_re, abar_im, w_cr, w_ci, d_skip, w1, w2, layer, *, batch, seq, lc):
    rows = lc * batch
    n_slab = D_WIDTH // LANES
    return pl.pallas_call(
        functools.partial(_s5_kernel, lc=lc, nb=batch, lane_chunk=512),
        out_shape=jax.ShapeDtypeStruct((seq, batch * D_WIDTH), BF16),
        grid=(seq // lc,),
        in_specs=[
            pl.BlockSpec((lc, batch * 2 * D_WIDTH), lambda c: (c, 0)),
            _layer_block((S5_PARTS, S5_PART_IN, 2 * S5_PART_STATE), layer),
            _layer_block((1, S5_LANES), layer),
            _layer_block((1, S5_LANES), layer),
            _layer_block((S5_PARTS, S5_PART_STATE, S5_PART_IN), layer),
            _layer_block((S5_PARTS, S5_PART_STATE, S5_PART_IN), layer),
            pl.BlockSpec((1, D_WIDTH), lambda c: (0, 0)),
            _layer_block((D_WIDTH, D_WIDTH), layer),
            _layer_block((D_WIDTH, D_WIDTH), layer),
        ],
        out_specs=pl.BlockSpec((lc, batch * D_WIDTH), lambda c: (c, 0)),
        scratch_shapes=[
            pltpu.VMEM((n_slab, rows, LANES), F32),
            pltpu.VMEM((n_slab, rows, LANES), F32),
            pltpu.VMEM((rows, 2 * S5_LANES), F32),
            pltpu.VMEM((batch, 2 * S5_LANES), F32),
            pltpu.VMEM((D_WIDTH, D_WIDTH), BF16),
            pltpu.VMEM((D_WIDTH, D_WIDTH), BF16),
        ],
        compiler_params=_params("arbitrary"),
        name="s5_layer",
    )(zd, w_b, abar_re, abar_im, w_cr, w_ci, d_skip.reshape(1, D_WIDTH), w1, w2)


def _odd_tail_kernel(u_ref, v_ref, gc_ref, d_ref, gd_ref, x_ref, lng_ref, lnb_ref, ws_ref, bs_ref, wo32_ref,
                     o_ref, c_ref, wo_ref, *, tm, sub):
    _cast_weights_once([(wo32_ref, wo_ref)], grid_rank=2)
    ri = lax.broadcasted_iota(jnp.int32, (C_CHUNK, C_CHUNK), 0)
    ci = lax.broadcasted_iota(jnp.int32, (C_CHUNK, C_CHUNK), 1)
    causal = ri >= ci
    w_s = [jnp.where(causal, ws_ref[g], 0.0).astype(BF16) for g in range(C_GROUPS)]
    for r in range(tm // sub):
        tile = slice(r * sub, (r + 1) * sub)
        v = v_ref[tile, :].astype(F32)
        mu = jnp.mean(v, axis=-1, keepdims=True)
        vc = v - mu
        var = jnp.mean(vc * vc, axis=-1, keepdims=True)
        vn = (vc * lax.rsqrt(var + EPS) * lng_ref[...] + lnb_ref[...]).astype(BF16)
        for g in range(C_GROUPS):
            cols = slice(g * C_GROUP_DIM, (g + 1) * C_GROUP_DIM)
            b_g = bs_ref[:, g:g + 1]
            for c in range(sub // C_CHUNK):
                rows = slice(r * sub + c * C_CHUNK, r * sub + (c + 1) * C_CHUNK)
                mixed = _dot(w_s[g], vn[c * C_CHUNK:(c + 1) * C_CHUNK, cols]) + b_g
                gate = _silu(gc_ref[rows, cols]) * u_ref[rows, cols]
                c_ref[rows, cols] = mixed.astype(BF16) * gate
        d = d_ref[tile, :] * _silu(gd_ref[tile, :])
        y = _dot(c_ref[tile, :], wo_ref[:C_WIDTH, :]) + _dot(d, wo_ref[C_WIDTH:, :])
        o_ref[tile, :] = x_ref[tile, :] + y


def odd_tail(zc, zd, d_out, x, ln_g, ln_b, w_s, b_s, w_out, layer, *, batch, seq, tm, sub):
    return pl.pallas_call(
        functools.partial(_odd_tail_kernel, tm=tm, sub=sub),
        out_shape=jax.ShapeDtypeStruct((batch, seq, D_MODEL), F32),
        grid=(batch, seq // tm),
        in_specs=[
            pl.BlockSpec((None, tm, C_WIDTH), lambda b, t: (b, t, 0)),
            pl.BlockSpec((None, tm, C_WIDTH), lambda b, t: (b, t, 1)),
            pl.BlockSpec((None, tm, C_WIDTH), lambda b, t: (b, t, 2)),
            pl.BlockSpec((tm, D_WIDTH), lambda b, t: (t, b)),
            pl.BlockSpec((tm, D_WIDTH), lambda b, t: (t, 2 * b + 1)),
            pl.BlockSpec((None, tm, D_MODEL), lambda b, t: (b, t, 0)),
            pl.BlockSpec((1, C_WIDTH), lambda b, t: (0, 0)),
            pl.BlockSpec((1, C_WIDTH), lambda b, t: (0, 0)),
            _layer_block((C_GROUPS, C_CHUNK, C_CHUNK), layer),
            pl.BlockSpec((C_CHUNK, C_GROUPS), lambda b, t: (0, 0)),
            _layer_block((C_WIDTH + D_WIDTH, D_MODEL), layer),
        ],
        out_specs=pl.BlockSpec((None, tm, D_MODEL), lambda b, t: (b, t, 0)),
        scratch_shapes=[pltpu.VMEM((tm, C_WIDTH), BF16), pltpu.VMEM((C_WIDTH + D_WIDTH, D_MODEL), BF16)],
        compiler_params=_params("arbitrary", "arbitrary"),
        name="odd_tail",
    )(zc, zc, zc, d_out, zd, x, ln_g.reshape(1, C_WIDTH), ln_b.reshape(1, C_WIDTH), w_s, b_s.T, w_out)


def _xattn_kernel(x_ref, g_ref, wq32_ref, kv_ref, wo32_ref, pg_ref, *refs, final, sub):
    if final:
        o_ref, wq_ref, wo_ref = refs
    else:
        o_ref, xn_ref, wq_ref, wo_ref = refs
    _cast_weights_once([(wq32_ref, wq_ref), (wo32_ref, wo_ref)], grid_rank=2)
    for r in range(x_ref.shape[0] // sub):
        rows = slice(r * sub, (r + 1) * sub)
        x = x_ref[rows, :]
        q = _dot(_rms(x, g_ref[...]).astype(BF16), wq_ref[...])
        heads = []
        for h in range(X_HEADS):
            ks = slice(h * X_HEAD_DIM, (h + 1) * X_HEAD_DIM)
            vs = slice(D_MODEL + h * X_HEAD_DIM, D_MODEL + (h + 1) * X_HEAD_DIM)
            s = _dot_nt(q[:, ks].astype(BF16), kv_ref[:, ks]) * (X_HEAD_DIM ** -0.5)
            m = jnp.max(s, axis=-1, keepdims=True)
            p = jnp.exp(s - m)
            den = jnp.sum(p, axis=-1, keepdims=True)
            heads.append((_dot(p.astype(BF16), kv_ref[:, vs]) / den).astype(BF16))
        y = x + _dot(jnp.concatenate(heads, axis=1), wo_ref[...])
        if final:
            o_ref[rows, :] = _rms(y, pg_ref[...])
        else:
            o_ref[rows, :] = y
            xn_ref[rows, :] = _rms(y, pg_ref[...]).astype(xn_ref.dtype)


def cross_attention(x, g, w_q, kv, w_o, post_g, layer, *, batch, seq, mem_len, tq, sub, final):
    tile = pl.BlockSpec((None, tq, D_MODEL), lambda b, t: (b, t, 0))
    res = jax.ShapeDtypeStruct((batch, seq, D_MODEL), F32)
    nxt = jax.ShapeDtypeStruct((batch, seq, D_MODEL), BF16)
    return pl.pallas_call(
        functools.partial(_xattn_kernel, final=final, sub=sub),
        out_shape=res if final else (res, nxt),
        grid=(batch, seq // tq),
        in_specs=[
            tile,
            pl.BlockSpec((1, D_MODEL), lambda b, t: (0, 0)),
            _layer_block((D_MODEL, D_MODEL), layer),
            pl.BlockSpec((None, None, mem_len, 2 * D_MODEL), lambda b, t: (layer, b, 0, 0)),
            _layer_block((D_MODEL, D_MODEL), layer),
            pl.BlockSpec((1, D_MODEL), lambda b, t: (0, 0)),
        ],
        out_specs=tile if final else (tile, tile),
        scratch_shapes=[pltpu.VMEM((D_MODEL, D_MODEL), BF16), pltpu.VMEM((D_MODEL, D_MODEL), BF16)],
        compiler_params=_params("arbitrary", "arbitrary"),
        name="cross_attention",
    )(x, g.reshape(1, D_MODEL), w_q, kv, w_o, post_g.reshape(1, D_MODEL))


def kernel(x, mem, norm_ab, w_in_ab, pool_w, pool_scale, w_out_ab, norm_cd, w_in_cd, sgu_ln_g, sgu_ln_b,
           sgu_w, sgu_b, s5_a_re, s5_a_im, s5_log_dt, s5_b_re, s5_b_im, s5_c_re, s5_c_im, s5_d,
           glu_w1, glu_w2, w_out_cd, norm_x, w_xq, w_xkv, w_xo, mem_norm, final_norm):
    batch, seq, _ = x.shape
    mem_len = mem.shape[1]
    depth = norm_x.shape[0]
    tokens = batch * seq
    tm_proj = 1024
    tiles_per_row = seq // tm_proj
    mixer_norm = lambda layer: (norm_ab if layer % 2 == 0 else norm_cd)[layer // 2]

    kv = mem_kv_proj(mem.reshape(batch * mem_len, D_MODEL), mem_norm, w_xkv, tn=1024)
    kv = kv.reshape(depth, batch, mem_len, 2 * D_MODEL)
    w_b, w_cr, w_ci, abar_re, abar_im = s5_prep(s5_a_re, s5_a_im, s5_log_dt, s5_b_re, s5_b_im, s5_c_re, s5_c_im)
    xn = rms_cast(x.reshape(tokens, D_MODEL), mixer_norm(0), tm=tm_proj)
    for layer in range(depth):
        i = layer // 2
        if layer % 2 == 0:
            z = in_proj(xn, w_in_ab, i, n=w_in_ab.shape[2], tm=tm_proj, tn=2048)
            z = z.reshape(batch, seq, z.shape[1])
            a_out = dilated_attention(z, batch=batch, seq=seq)
            x = even_tail(z, a_out, x, pool_w, pool_scale[i], w_out_ab, i, batch=batch, seq=seq,
                          tm=1024, sub=512)
        else:
            n_c = 3 * C_WIDTH
            zc = in_proj(xn, w_in_cd, i, n=n_c, tm=tm_proj, tn=1024)
            zd = in_proj(xn, w_in_cd, i, n=2 * D_WIDTH, col0=n_c, tm=tm_proj, tn=2 * D_WIDTH,
                         out_shape=(seq, batch * 2 * D_WIDTH),
                         out_map=lambda r, j: (r % tiles_per_row, r // tiles_per_row))
            d_out = s5_layer(zd, w_b, abar_re, abar_im, w_cr, w_ci, s5_d[i], glu_w1, glu_w2, i,
                             batch=batch, seq=seq, lc=64)
            x = odd_tail(zc.reshape(batch, seq, n_c), zd, d_out, x, sgu_ln_g[i], sgu_ln_b[i], sgu_w,
                         sgu_b[i], w_out_cd, i, batch=batch, seq=seq, tm=1024, sub=512)
        final = layer == depth - 1
        out = cross_attention(x, norm_x[layer], w_xq, kv, w_xo, final_norm if final else mixer_norm(layer + 1),
                              layer, batch=batch, seq=seq, mem_len=mem_len, tq=1024, sub=512, final=final)
        if final:
            return out
        x, xn = out[0], out[1].reshape(tokens, D_MODEL)
```

```python
import functools

import jax
import jax.numpy as jnp
from jax import lax
from jax.experimental import pallas as pl
from jax.experimental.pallas import tpu as pltpu

F32 = jnp.float32
BF16 = jnp.bfloat16

LANES = 128
D_MODEL = 1024
A_WIDTH = 1024
A_HEAD_DIM = 64
A_BLOCK = 128
A_DILATIONS = (1, 4, 16)
DIL_STEP = 4
B_WIDTH = 1024
POOL_WINDOWS = (2, 4, 8, 16)
B_GROUP = 256
POOL_HALO = 16
C_WIDTH = 1024
C_CHUNK = 128
C_GROUPS = 4
C_GROUP_DIM = 256
D_WIDTH = 512
S5_GROUPS = 32
S5_GROUP_DIM = 16
S5_STATE = 64
S5_LANES = S5_GROUPS * S5_STATE
S5_PARTS = 2
S5_PART_IN = D_WIDTH // S5_PARTS
S5_PART_STATE = S5_LANES // S5_PARTS
X_HEADS = 4
X_HEAD_DIM = 256
EPS = 1e-6
NEG = -1e30
LOG2E = 1.4426950408889634

VMEM_LIMIT = 48 * 1024 * 1024


def _params(*sem):
    return pltpu.CompilerParams(dimension_semantics=sem, vmem_limit_bytes=VMEM_LIMIT)


def _rms(x, g):
    ms = jnp.mean(x * x, axis=-1, keepdims=True)
    return x * lax.rsqrt(ms + EPS) * g


def _silu(x):
    return x * jax.nn.sigmoid(x)


def _dot(a, b):
    return jnp.dot(a, b, preferred_element_type=F32)


def _layer_block(shape, layer):
    return pl.BlockSpec((None,) + tuple(shape), lambda *_: (layer,) + (0,) * len(shape),
                        pipeline_mode=pl.Buffered(1))


def _cast_weights_once(pairs, grid_rank):
    first = pl.program_id(0) == 0
    for axis in range(1, grid_rank):
        first = first & (pl.program_id(axis) == 0)

    @pl.when(first)
    def _():
        for src, dst in pairs:
            dst[...] = src[...].astype(dst.dtype)


def _dot_nt(a, b):
    return lax.dot_general(a, b, (((1,), (1,)), ((), ())), preferred_element_type=F32)


def _rms_cast_kernel(x_ref, g_ref, o_ref):
    o_ref[...] = _rms(x_ref[...], g_ref[...]).astype(o_ref.dtype)


def rms_cast(x, g, *, tm):
    m, k = x.shape
    return pl.pallas_call(
        _rms_cast_kernel,
        out_shape=jax.ShapeDtypeStruct((m, k), BF16),
        grid=(m // tm,),
        in_specs=[pl.BlockSpec((tm, k), lambda i: (i, 0)), pl.BlockSpec((1, k), lambda i: (0, 0))],
        out_specs=pl.BlockSpec((tm, k), lambda i: (i, 0)),
        compiler_params=_params("parallel"),
        name="rms_cast",
    )(x, g.reshape(1, k))


def _in_proj_kernel(x_ref, w32_ref, o_ref, w_ref):
    @pl.when(pl.program_id(1) == 0)
    def _():
        w_ref[...] = w32_ref[...].astype(BF16)

    o_ref[...] = _dot(x_ref[...], w_ref[...]).astype(o_ref.dtype)


def in_proj(xn, w, layer, *, n, tm, tn, col0=0, out_shape=None, out_map=None):
    m, k = xn.shape
    col_block0 = col0 // tn
    if out_shape is None:
        out_shape, out_map = (m, n), (lambda i, j: (i, j))
    return pl.pallas_call(
        _in_proj_kernel,
        out_shape=jax.ShapeDtypeStruct(out_shape, BF16),
        grid=(n // tn, m // tm),
        in_specs=[
            pl.BlockSpec((tm, k), lambda j, i: (i, 0)),
            pl.BlockSpec((None, k, tn), lambda j, i: (layer, 0, col_block0 + j)),
        ],
        out_specs=pl.BlockSpec((tm, tn), lambda j, i: out_map(i, j)),
        scratch_shapes=[pltpu.VMEM((k, tn), BF16)],
        compiler_params=_params("arbitrary", "arbitrary"),
        name="in_proj",
    )(xn, w)


def _mem_proj_kernel(x_ref, g_ref, w_ref, o_ref, xn_ref):
    @pl.when((pl.program_id(0) == 0) & (pl.program_id(1) == 0))
    def _():
        xn_ref[...] = _rms(x_ref[...], g_ref[...]).astype(BF16)

    o_ref[...] = _dot(xn_ref[...], w_ref[...].astype(BF16)).astype(o_ref.dtype)


def mem_kv_proj(mem, g, w, *, tn):
    m, k = mem.shape
    n_layers, _, n = w.shape
    return pl.pallas_call(
        _mem_proj_kernel,
        out_shape=jax.ShapeDtypeStruct((n_layers, m, n), BF16),
        grid=(n_layers, n // tn),
        in_specs=[
            pl.BlockSpec((m, k), lambda l, j: (0, 0)),
            pl.BlockSpec((1, k), lambda l, j: (0, 0)),
            pl.BlockSpec((None, k, tn), lambda l, j: (l, 0, j)),
        ],
        out_specs=pl.BlockSpec((None, m, tn), lambda l, j: (l, 0, j)),
        scratch_shapes=[pltpu.VMEM((m, k), BF16)],
        compiler_params=_params("arbitrary", "arbitrary"),
        name="mem_kv_proj",
    )(mem, g.reshape(1, k), w)


def _dilated_kernel(q_ref, k_ref, v_ref, g_ref, o_ref, qs_ref, ks_ref, vs_ref, gs_ref, os_ref, num_ref, den_ref,
                    m_ref, bias_ref, *, seq, unroll):
    assert A_DILATIONS == (1, DIL_STEP, DIL_STEP * DIL_STEP)
    qi = lax.broadcasted_iota(jnp.int32, (A_BLOCK, 2 * A_BLOCK), 0)
    kj = lax.broadcasted_iota(jnp.int32, (A_BLOCK, 2 * A_BLOCK), 1)
    dist = qi + A_BLOCK - kj
    band = (dist >= 0) & (dist <= A_BLOCK)
    bias_ref[1] = jnp.where(band, 0.0, NEG).astype(BF16)
    bias_ref[0] = jnp.where(band & (kj >= A_BLOCK), 0.0, NEG).astype(BF16)

    len_s = seq // DIL_STEP
    len_w = len_s // DIL_STEP
    blocks_s = len_s // A_BLOCK
    q_scale = A_HEAD_DIM ** -0.5 * LOG2E
    for src, dst, scale in ((q_ref, qs_ref, q_scale), (k_ref, ks_ref, None), (v_ref, vs_ref, None)):
        nat = src[...].astype(F32)
        dst[0] = nat if scale is None else nat * scale
        for b in range(DIL_STEP):
            dst[1, b * len_s:(b + 1) * len_s, :] = dst[0, pl.ds(b, len_s, stride=DIL_STEP), :]
        for c in range(DIL_STEP * DIL_STEP):
            b, a = divmod(c, DIL_STEP)
            dst[2, c * len_w:(c + 1) * len_w, :] = dst[1, pl.ds(b * len_s + a, len_w, stride=DIL_STEP), :]
    gs_ref[0] = g_ref[...].astype(F32)

    head0 = lax.broadcasted_iota(jnp.int32, (A_BLOCK, LANES), 1) < A_HEAD_DIM

    def load(slab, cur, prev):
        q = qs_ref[slab, cur, :].astype(BF16)
        if slab == 0:
            kk = jnp.concatenate([k_ref[prev, :], k_ref[cur, :]], axis=0)
            vv = jnp.concatenate([v_ref[prev, :], v_ref[cur, :]], axis=0)
            return q, kk, vv
        kk = jnp.concatenate([ks_ref[slab, prev, :], ks_ref[slab, cur, :]], axis=0).astype(BF16)
        vv = jnp.concatenate([vs_ref[slab, prev, :], vs_ref[slab, cur, :]], axis=0).astype(BF16)
        return q, kk, vv

    one_b, zero_b = jnp.ones((), BF16), jnp.zeros((), BF16)

    def attend(q, k, v, bias):
        own0 = lax.broadcasted_iota(jnp.int32, k.shape, 1).astype(BF16) < A_HEAD_DIM
        k_bd = jnp.concatenate([jnp.where(own0, k, zero_b), jnp.where(own0, zero_b, k)], axis=0)
        s = _dot_nt(q, k_bd).astype(BF16) + jnp.concatenate([bias, bias], axis=1)
        s0, s1 = s[:, :k.shape[0]], s[:, k.shape[0]:]
        m0 = jnp.max(s0, axis=-1, keepdims=True)
        m1 = jnp.max(s1, axis=-1, keepdims=True)
        p = jnp.concatenate([jnp.exp2(s0 - m0), jnp.exp2(s1 - m1)], axis=1)
        v_bd = jnp.concatenate([
            jnp.concatenate([jnp.where(own0, v, zero_b), jnp.where(own0, one_b, zero_b)], axis=1),
            jnp.concatenate([jnp.where(own0, zero_b, v), jnp.where(own0, zero_b, one_b)], axis=1)], axis=0)
        res = _dot(p, v_bd)
        m = jnp.where(head0, jnp.broadcast_to(m0, (A_BLOCK, LANES)).astype(F32),
                      jnp.broadcast_to(m1, (A_BLOCK, LANES)).astype(F32))
        return res[:, :LANES], res[:, LANES:], m

    def attend_single(cur):
        return attend(qs_ref[2, cur, :].astype(BF16), ks_ref[2, cur, :].astype(BF16),
                      vs_ref[2, cur, :].astype(BF16), bias_ref[0][:, A_BLOCK:])

    def aligned(start):
        return pl.ds(pl.multiple_of(start, A_BLOCK), A_BLOCK)

    def banded_rows(idx, blocks_per_seq):
        start = idx * A_BLOCK
        first = idx % blocks_per_seq == 0
        return aligned(start), aligned(jnp.where(first, start, start - A_BLOCK)), bias_ref[jnp.where(first, 0, 1)]

    def merge(num_a, den_a, m_a, num_b, den_b, m_b):
        m = jnp.maximum(m_a, m_b)
        w_a = jnp.exp2(m_a - m)
        w_b = jnp.exp2(m_b - m)
        return w_a * num_a + w_b * num_b, w_a * den_a + w_b * den_b, m

    def narrow_body(idx, carry):
        cur, prev, bias = banded_rows(idx, n_blocks)
        num, den, m = attend(*load(0, cur, prev), bias)
        num_ref[0, cur, :] = num
        den_ref[0, cur, :] = den
        m_ref[0, cur, :] = m
        return carry

    def step_body(idx, carry):
        cur, prev, bias = banded_rows(idx, blocks_s)
        natural = pl.ds(idx // blocks_s + DIL_STEP * A_BLOCK * (idx % blocks_s), A_BLOCK, stride=DIL_STEP)
        num, den, m = merge(*attend(*load(1, cur, prev), bias),
                            num_ref[0, natural, :], den_ref[0, natural, :], m_ref[0, natural, :])
        num_ref[1, cur, :] = num
        den_ref[1, cur, :] = den
        m_ref[1, cur, :] = m
        gs_ref[1, cur, :] = gs_ref[0, natural, :]
        return carry

    def wide_body(idx, carry):
        cur = aligned(idx * A_BLOCK)
        by_step = pl.ds((idx // DIL_STEP) * len_s + idx % DIL_STEP, A_BLOCK, stride=DIL_STEP)
        natural = pl.ds(DIL_STEP * (idx % DIL_STEP) + idx // DIL_STEP, A_BLOCK, stride=DIL_STEP * DIL_STEP)
        num, den, _ = merge(*attend_single(cur),
                            num_ref[1, by_step, :], den_ref[1, by_step, :], m_ref[1, by_step, :])
        os_ref[natural, :] = num / den * _silu(gs_ref[1, by_step, :])
        return carry

    assert len_w == A_BLOCK
    n_blocks = seq // A_BLOCK
    for body in (narrow_body, step_body, wide_body):
        lax.fori_loop(0, n_blocks, body, 0, unroll=unroll)
    o_ref[...] = os_ref[...].astype(o_ref.dtype)


def dilated_attention(z, *, batch, seq, unroll=16):
    n_pair = A_WIDTH // LANES
    n_pat = len(A_DILATIONS)
    blk = (None, seq, LANES)
    slabs = pltpu.VMEM((n_pat, seq, LANES), F32)
    partial = pltpu.VMEM((2, seq, LANES), F32)
    return pl.pallas_call(
        functools.partial(_dilated_kernel, seq=seq, unroll=unroll),
        out_shape=jax.ShapeDtypeStruct((batch, seq, A_WIDTH), BF16),
        grid=(batch, n_pair),
        in_specs=[
            pl.BlockSpec(blk, lambda b, h: (b, 0, h)),
            pl.BlockSpec(blk, lambda b, h: (b, 0, n_pair + h)),
            pl.BlockSpec(blk, lambda b, h: (b, 0, 2 * n_pair + h)),
            pl.BlockSpec(blk, lambda b, h: (b, 0, 3 * n_pair + h)),
        ],
        out_specs=pl.BlockSpec(blk, lambda b, h: (b, 0, h)),
        scratch_shapes=[
            slabs, slabs, slabs,
            pltpu.VMEM((2, seq, LANES), F32),
            pltpu.VMEM((seq, LANES), F32),
            partial, partial, partial,
            pltpu.VMEM((2, A_BLOCK, 2 * A_BLOCK), BF16),
        ],
        compiler_params=_params("parallel", "parallel"),
        name="dilated_attention",
    )(z, z, z, z)


def _even_tail_kernel(vb_ref, halo_ref, gb_ref, a_ref, x_ref, pw32_ref, ps_ref, wo32_ref, o_ref, pw_ref, wo_ref,
                      *, tm, sub):
    _cast_weights_once([(pw32_ref, pw_ref), (wo32_ref, wo_ref)], grid_rank=2)
    ti = pl.program_id(1)
    for r in range(tm // sub):
        rows = slice(r * sub, (r + 1) * sub)
        v = vb_ref[rows, :].astype(F32)
        if r == 0:
            halo = jnp.where(ti > 0, halo_ref[...].astype(F32), 0.0)
        else:
            halo = vb_ref[r * sub - POOL_HALO:r * sub, :].astype(F32)
        xc = jnp.concatenate([halo, v], axis=0)
        s2 = xc + pltpu.roll(xc, 1, 0)
        t4 = s2[:, B_GROUP:]
        s4 = t4 + pltpu.roll(t4, 2, 0)
        t8 = s4[:, B_GROUP:]
        s8 = t8 + pltpu.roll(t8, 4, 0)
        t16 = s8[:, B_GROUP:]
        s16 = t16 + pltpu.roll(t16, 8, 0)
        sums = (s2[:, :B_GROUP], s4[:, :B_GROUP], s8[:, :B_GROUP], s16)
        pos = (ti * tm + r * sub + 1 + lax.broadcasted_iota(jnp.int32, (sub, 1), 0)).astype(F32)
        mixed = []
        for g, w in enumerate(POOL_WINDOWS):
            mean = sums[g][POOL_HALO:, :] / jnp.minimum(pos, float(w))
            pooled = mean - v[:, g * B_GROUP:(g + 1) * B_GROUP]
            mixed.append(_dot(pooled.astype(BF16), pw_ref[g]))
        b_out = jnp.concatenate(mixed, axis=1) * ps_ref[...] * _silu(gb_ref[rows, :].astype(F32))
        y = _dot(jnp.concatenate([a_ref[rows, :], b_out.astype(BF16)], axis=1), wo_ref[...])
        o_ref[rows, :] = x_ref[rows, :] + y


def even_tail(z, a_out, x, pool_w, pool_scale, w_out, layer, *, batch, seq, tm, sub):
    vb_col = 4 * A_WIDTH // B_WIDTH
    halo_per_tile = tm // POOL_HALO
    return pl.pallas_call(
        functools.partial(_even_tail_kernel, tm=tm, sub=sub),
        out_shape=jax.ShapeDtypeStruct((batch, seq, D_MODEL), F32),
        grid=(batch, seq // tm),
        in_specs=[
            pl.BlockSpec((None, tm, B_WIDTH), lambda b, t: (b, t, vb_col)),
            pl.BlockSpec((None, POOL_HALO, B_WIDTH),
                         lambda b, t: (b, jnp.maximum(t * halo_per_tile - 1, 0), vb_col)),
            pl.BlockSpec((None, tm, B_WIDTH), lambda b, t: (b, t, vb_col + 1)),
            pl.BlockSpec((None, tm, A_WIDTH), lambda b, t: (b, t, 0)),
            pl.BlockSpec((None, tm, D_MODEL), lambda b, t: (b, t, 0)),
            _layer_block((len(POOL_WINDOWS), B_GROUP, B_GROUP), layer),
            pl.BlockSpec((1, B_WIDTH), lambda b, t: (0, 0)),
            _layer_block((A_WIDTH + B_WIDTH, D_MODEL), layer),
        ],
        out_specs=pl.BlockSpec((None, tm, D_MODEL), lambda b, t: (b, t, 0)),
        scratch_shapes=[pltpu.VMEM((len(POOL_WINDOWS), B_GROUP, B_GROUP), BF16),
                        pltpu.VMEM((A_WIDTH + B_WIDTH, D_MODEL), BF16)],
        compiler_params=_params("arbitrary", "arbitrary"),
        name="even_tail",
    )(z, z, z, a_out, x, pool_w, pool_scale.reshape(1, B_WIDTH), w_out)


def _s5_prep_kernel(ar_ref, ai_ref, ldt_ref, br_ref, bi_ref, abr_ref, abi_ref, bbr_ref, bbi_ref):
    ar, ai = ar_ref[...], ai_ref[...]
    dt = jnp.exp(ldt_ref[...])
    mag = jnp.exp(dt * ar)
    abar_re = mag * jnp.cos(dt * ai)
    abar_im = mag * jnp.sin(dt * ai)
    nr, ni = abar_re - 1.0, abar_im
    inv = 1.0 / (ar * ar + ai * ai)
    coef_re = (nr * ar + ni * ai) * inv
    coef_im = (ni * ar - nr * ai) * inv
    br, bi = br_ref[...], bi_ref[...]
    abr_ref[...] = abar_re
    abi_ref[...] = abar_im
    bbr_ref[...] = coef_re * br - coef_im * bi
    bbi_ref[...] = coef_re * bi + coef_im * br


def _expand_block_diag(t, reps, row_group):
    rows, cols = t.shape
    src = lax.broadcasted_iota(jnp.int32, (cols, reps * cols), 0)
    dst = lax.broadcasted_iota(jnp.int32, (cols, reps * cols), 1)
    tiled = _dot(t, jnp.where(dst % cols == src, 1.0, 0.0).astype(BF16))
    r = lax.broadcasted_iota(jnp.int32, tiled.shape, 0) // row_group
    c = lax.broadcasted_iota(jnp.int32, tiled.shape, 1) // cols
    return jnp.where(r == c, tiled, 0.0).astype(BF16)


def _s5_expand_kernel(bbr_ref, bbi_ref, cr_ref, ci_ref, wb_ref, wcr_ref, wci_ref):
    gpp = S5_GROUPS // S5_PARTS
    wb_ref[...] = jnp.concatenate(
        [_expand_block_diag(ref[...].astype(BF16), gpp, S5_GROUP_DIM) for ref in (bbr_ref, bbi_ref)], axis=1)
    wcr_ref[...] = _expand_block_diag(cr_ref[...].astype(BF16), gpp, S5_STATE)
    wci_ref[...] = _expand_block_diag(ci_ref[...].astype(BF16), gpp, S5_STATE)


def s5_prep(a_re, a_im, log_dt, b_re, b_im, c_re, c_im):
    n_l = a_re.shape[0]
    rows = n_l * D_WIDTH
    rep = lambda t: jnp.repeat(t.reshape(n_l * S5_GROUPS, S5_STATE), S5_GROUP_DIM, axis=0)
    to_rows = lambda t: t.transpose(0, 1, 3, 2).reshape(rows, S5_STATE)
    shp = jax.ShapeDtypeStruct((rows, S5_STATE), F32)
    abr, abi, bbr, bbi = pl.pallas_call(
        _s5_prep_kernel, out_shape=(shp, shp, shp, shp), name="s5_prep",
    )(rep(a_re), rep(a_im), rep(jnp.broadcast_to(log_dt[:, :, None], (n_l, S5_GROUPS, S5_STATE))),
      to_rows(b_re), to_rows(b_im))

    n_blk = n_l * S5_PARTS
    in_rows = lambda t: t.reshape(n_blk, S5_PART_IN, S5_STATE)
    out_rows = lambda t: t.transpose(0, 1, 3, 2).reshape(n_blk, S5_PART_STATE, S5_GROUP_DIM)
    blk = lambda r, c: pl.BlockSpec((None, r, c), lambda i: (i, 0, 0))
    w_b, w_cr, w_ci = pl.pallas_call(
        _s5_expand_kernel,
        out_shape=(jax.ShapeDtypeStruct((n_blk, S5_PART_IN, 2 * S5_PART_STATE), BF16),
                   jax.ShapeDtypeStruct((n_blk, S5_PART_STATE, S5_PART_IN), BF16),
                   jax.ShapeDtypeStruct((n_blk, S5_PART_STATE, S5_PART_IN), BF16)),
        grid=(n_blk,),
        in_specs=[blk(S5_PART_IN, S5_STATE), blk(S5_PART_IN, S5_STATE),
                  blk(S5_PART_STATE, S5_GROUP_DIM), blk(S5_PART_STATE, S5_GROUP_DIM)],
        out_specs=(blk(S5_PART_IN, 2 * S5_PART_STATE), blk(S5_PART_STATE, S5_PART_IN),
                   blk(S5_PART_STATE, S5_PART_IN)),
        compiler_params=_params("parallel"),
        name="s5_expand",
    )(in_rows(bbr), in_rows(bbi), out_rows(c_re), out_rows(c_im))
    per_layer = lambda t: t.reshape((n_l, S5_PARTS) + t.shape[1:])
    abar_re = abr[::S5_GROUP_DIM].reshape(n_l, 1, S5_LANES)
    abar_im = abi[::S5_GROUP_DIM].reshape(n_l, 1, S5_LANES)
    return per_layer(w_b), per_layer(w_cr), per_layer(w_ci), abar_re, abar_im


def _s5_kernel(xd_ref, wb_ref, are_ref, aim_ref, wcr_ref, wci_ref, dsk_ref, w1_32_ref, w2_32_ref,
               o_ref, u_ref, y_ref, bu_ref, h_ref, w1_ref, w2_ref, *, lc, nb, lane_chunk):
    @pl.when(pl.program_id(0) == 0)
    def _():
        h_ref[...] = jnp.zeros_like(h_ref)

    _cast_weights_once([(w1_32_ref, w1_ref), (w2_32_ref, w2_ref)], grid_rank=1)
    n_slab = D_WIDTH // LANES
    for b in range(nb):
        for s in range(n_slab):
            col = b * 2 * D_WIDTH + s * LANES
            u_ref[s, pl.ds(b, lc, stride=nb), :] = xd_ref[:, col:col + LANES].astype(F32)
    u = jnp.concatenate([u_ref[s] for s in range(n_slab)], axis=1)
    ub = u.astype(BF16)
    width = 2 * S5_PART_STATE
    def project(part):
        bu_ref[:, part * width:(part + 1) * width] = _dot(
            ub[:, part * S5_PART_IN:(part + 1) * S5_PART_IN], wb_ref[part])

    def scan(part):
        for c in range(S5_PART_STATE // lane_chunk):
            re = slice(part * width + c * lane_chunk, part * width + (c + 1) * lane_chunk)
            im = slice(re.start + S5_PART_STATE, re.stop + S5_PART_STATE)
            ab = slice(part * S5_PART_STATE + c * lane_chunk, part * S5_PART_STATE + (c + 1) * lane_chunk)
            a_r = jnp.broadcast_to(are_ref[:, ab], (nb, lane_chunk))
            a_i = jnp.broadcast_to(aim_ref[:, ab], (nb, lane_chunk))
            h_r, h_i = h_ref[:, re], h_ref[:, im]
            for t in range(lc):
                row = slice(t * nb, (t + 1) * nb)
                h_r, h_i = (a_r * h_r - a_i * h_i + bu_ref[row, re], a_r * h_i + a_i * h_r + bu_ref[row, im])
                bu_ref[row, re] = h_r
                bu_ref[row, im] = h_i
            h_ref[:, re] = h_r
            h_ref[:, im] = h_i

    def readout(part):
        re = slice(part * width, part * width + S5_PART_STATE)
        im = slice(re.stop, re.stop + S5_PART_STATE)
        return _dot(bu_ref[:, re].astype(BF16), wcr_ref[part]) - _dot(bu_ref[:, im].astype(BF16), wci_ref[part])

    ys = []
    project(0)
    for part in range(S5_PARTS):
        if part + 1 < S5_PARTS:
            project(part + 1)
        scan(part)
        ys.append(readout(part))
    y = jnp.concatenate(ys, axis=1) + dsk_ref[...] * u
    y = jax.nn.gelu(y).astype(BF16)
    out = _dot(y, w1_ref[...]) * jax.nn.sigmoid(_dot(y, w2_ref[...]))
    for s in range(n_slab):
        y_ref[s] = out[:, s * LANES:(s + 1) * LANES]
    o_ref[...] = jnp.concatenate(
        [y_ref[s, pl.ds(b, lc, stride=nb), :] for b in range(nb) for s in range(n_slab)], axis=1).astype(o_ref.dtype)


def s5_layer(zd, w_b, abar_re, abar_im, w_cr, w_ci, d_skip, w1, w2, layer, *, batch, seq, lc):
    rows = lc * batch
    n_slab = D_WIDTH // LANES
    return pl.pallas_call(
        functools.partial(_s5_kernel, lc=lc, nb=batch, lane_chunk=512),
        out_shape=jax.ShapeDtypeStruct((seq, batch * D_WIDTH), BF16),
        grid=(seq // lc,),
        in_specs=[
            pl.BlockSpec((lc, batch * 2 * D_WIDTH), lambda c: (c, 0)),
            _layer_block((S5_PARTS, S5_PART_IN, 2 * S5_PART_STATE), layer),
            _layer_block((1, S5_LANES), layer),
            _layer_block((1, S5_LANES), layer),
            _layer_block((S5_PARTS, S5_PART_STATE, S5_PART_IN), layer),
            _layer_block((S5_PARTS, S5_PART_STATE, S5_PART_IN), layer),
            pl.BlockSpec((1, D_WIDTH), lambda c: (0, 0)),
            _layer_block((D_WIDTH, D_WIDTH), layer),
            _layer_block((D_WIDTH, D_WIDTH), layer),
        ],
        out_specs=pl.BlockSpec((lc, batch * D_WIDTH), lambda c: (c, 0)),
        scratch_shapes=[
            pltpu.VMEM((n_slab, rows, LANES), F32),
            pltpu.VMEM((n_slab, rows, LANES), F32),
            pltpu.VMEM((rows, 2 * S5_LANES), F32),
            pltpu.VMEM((batch, 2 * S5_LANES), F32),
            pltpu.VMEM((D_WIDTH, D_WIDTH), BF16),
            pltpu.VMEM((D_WIDTH, D_WIDTH), BF16),
        ],
        compiler_params=_params("arbitrary"),
        name="s5_layer",
    )(zd, w_b, abar_re, abar_im, w_cr, w_ci, d_skip.reshape(1, D_WIDTH), w1, w2)


def _odd_tail_kernel(u_ref, v_ref, gc_ref, d_ref, gd_ref, x_ref, lng_ref, lnb_ref, ws_ref, bs_ref, wo32_ref,
                     o_ref, c_ref, wo_ref, *, tm, sub):
    _cast_weights_once([(wo32_ref, wo_ref)], grid_rank=2)
    ri = lax.broadcasted_iota(jnp.int32, (C_CHUNK, C_CHUNK), 0)
    ci = lax.broadcasted_iota(jnp.int32, (C_CHUNK, C_CHUNK), 1)
    causal = ri >= ci
    w_s = [jnp.where(causal, ws_ref[g], 0.0).astype(BF16) for g in range(C_GROUPS)]
    for r in range(tm // sub):
        tile = slice(r * sub, (r + 1) * sub)
        v = v_ref[tile, :].astype(F32)
        mu = jnp.mean(v, axis=-1, keepdims=True)
        vc = v - mu
        var = jnp.mean(vc * vc, axis=-1, keepdims=True)
        vn = (vc * lax.rsqrt(var + EPS) * lng_ref[...] + lnb_ref[...]).astype(BF16)
        for g in range(C_GROUPS):
            cols = slice(g * C_GROUP_DIM, (g + 1) * C_GROUP_DIM)
            b_g = bs_ref[:, g:g + 1]
            for c in range(sub // C_CHUNK):
                rows = slice(r * sub + c * C_CHUNK, r * sub + (c + 1) * C_CHUNK)
                mixed = _dot(w_s[g], vn[c * C_CHUNK:(c + 1) * C_CHUNK, cols]) + b_g
                gate = _silu(gc_ref[rows, cols].astype(F32))
                c_ref[rows, cols] = (u_ref[rows, cols].astype(F32) * mixed * gate).astype(BF16)
        d = (d_ref[tile, :].astype(F32) * _silu(gd_ref[tile, :].astype(F32))).astype(BF16)
        y = _dot(c_ref[tile, :], wo_ref[:C_WIDTH, :]) + _dot(d, wo_ref[C_WIDTH:, :])
        o_ref[tile, :] = x_ref[tile, :] + y


def odd_tail(zc, zd, d_out, x, ln_g, ln_b, w_s, b_s, w_out, layer, *, batch, seq, tm, sub):
    return pl.pallas_call(
        functools.partial(_odd_tail_kernel, tm=tm, sub=sub),
        out_shape=jax.ShapeDtypeStruct((batch, seq, D_MODEL), F32),
        grid=(batch, seq // tm),
        in_specs=[
            pl.BlockSpec((None, tm, C_WIDTH), lambda b, t: (b, t, 0)),
            pl.BlockSpec((None, tm, C_WIDTH), lambda b, t: (b, t, 1)),
            pl.BlockSpec((None, tm, C_WIDTH), lambda b, t: (b, t, 2)),
            pl.BlockSpec((tm, D_WIDTH), lambda b, t: (t, b)),
            pl.BlockSpec((tm, D_WIDTH), lambda b, t: (t, 2 * b + 1)),
            pl.BlockSpec((None, tm, D_MODEL), lambda b, t: (b, t, 0)),
            pl.BlockSpec((1, C_WIDTH), lambda b, t: (0, 0)),
            pl.BlockSpec((1, C_WIDTH), lambda b, t: (0, 0)),
            _layer_block((C_GROUPS, C_CHUNK, C_CHUNK), layer),
            pl.BlockSpec((C_CHUNK, C_GROUPS), lambda b, t: (0, 0)),
            _layer_block((C_WIDTH + D_WIDTH, D_MODEL), layer),
        ],
        out_specs=pl.BlockSpec((None, tm, D_MODEL), lambda b, t: (b, t, 0)),
        scratch_shapes=[pltpu.VMEM((tm, C_WIDTH), BF16), pltpu.VMEM((C_WIDTH + D_WIDTH, D_MODEL), BF16)],
        compiler_params=_params("arbitrary", "arbitrary"),
        name="odd_tail",
    )(zc, zc, zc, d_out, zd, x, ln_g.reshape(1, C_WIDTH), ln_b.reshape(1, C_WIDTH), w_s, b_s.T, w_out)


def _xattn_kernel(x_ref, g_ref, wq32_ref, kv_ref, wo32_ref, pg_ref, *refs, final, sub):
    if final:
        o_ref, wq_ref, wo_ref = refs
    else:
        o_ref, xn_ref, wq_ref, wo_ref = refs
    _cast_weights_once([(wq32_ref, wq_ref), (wo32_ref, wo_ref)], grid_rank=2)
    for r in range(x_ref.shape[0] // sub):
        rows = slice(r * sub, (r + 1) * sub)
        x = x_ref[rows, :]
        q = _dot(_rms(x, g_ref[...]).astype(BF16), wq_ref[...]) * (X_HEAD_DIM ** -0.5)
        heads = []
        for h in range(X_HEADS):
            ks = slice(h * X_HEAD_DIM, (h + 1) * X_HEAD_DIM)
            vs = slice(D_MODEL + h * X_HEAD_DIM, D_MODEL + (h + 1) * X_HEAD_DIM)
            s = _dot_nt(q[:, ks].astype(BF16), kv_ref[:, ks])
            m = jnp.max(s, axis=-1, keepdims=True)
            p = jnp.exp(s - m)
            den = jnp.sum(p, axis=-1, keepdims=True)
            heads.append((_dot(p.astype(BF16), kv_ref[:, vs]) / den).astype(BF16))
        y = x + _dot(jnp.concatenate(heads, axis=1), wo_ref[...])
        if final:
            o_ref[rows, :] = _rms(y, pg_ref[...])
        else:
            o_ref[rows, :] = y
            xn_ref[rows, :] = _rms(y, pg_ref[...]).astype(xn_ref.dtype)


def cross_attention(x, g, w_q, kv, w_o, post_g, layer, *, batch, seq, mem_len, tq, sub, final):
    tile = pl.BlockSpec((None, tq, D_MODEL), lambda b, t: (b, t, 0))
    res = jax.ShapeDtypeStruct((batch, seq, D_MODEL), F32)
    nxt = jax.ShapeDtypeStruct((batch, seq, D_MODEL), BF16)
    return pl.pallas_call(
        functools.partial(_xattn_kernel, final=final, sub=sub),
        out_shape=res if final else (res, nxt),
        grid=(batch, seq // tq),
        in_specs=[
            tile,
            pl.BlockSpec((1, D_MODEL), lambda b, t: (0, 0)),
            _layer_block((D_MODEL, D_MODEL), layer),
            pl.BlockSpec((None, None, mem_len, 2 * D_MODEL), lambda b, t: (layer, b, 0, 0)),
            _layer_block((D_MODEL, D_MODEL), layer),
            pl.BlockSpec((1, D_MODEL), lambda b, t: (0, 0)),
        ],
        out_specs=tile if final else (tile, tile),
        scratch_shapes=[pltpu.VMEM((D_MODEL, D_MODEL), BF16), pltpu.VMEM((D_MODEL, D_MODEL), BF16)],
        compiler_params=_params("arbitrary", "arbitrary"),
        name="cross_attention",
    )(x, g.reshape(1, D_MODEL), w_q, kv, w_o, post_g.reshape(1, D_MODEL))


def kernel(x, mem, norm_ab, w_in_ab, pool_w, pool_scale, w_out_ab, norm_cd, w_in_cd, sgu_ln_g, sgu_ln_b,
           sgu_w, sgu_b, s5_a_re, s5_a_im, s5_log_dt, s5_b_re, s5_b_im, s5_c_re, s5_c_im, s5_d,
           glu_w1, glu_w2, w_out_cd, norm_x, w_xq, w_xkv, w_xo, mem_norm, final_norm):
    batch, seq, _ = x.shape
    mem_len = mem.shape[1]
    depth = norm_x.shape[0]
    tokens = batch * seq
    tm_proj = 1024
    tiles_per_row = seq // tm_proj
    mixer_norm = lambda layer: (norm_ab if layer % 2 == 0 else norm_cd)[layer // 2]

    kv = mem_kv_proj(mem.reshape(batch * mem_len, D_MODEL), mem_norm, w_xkv, tn=1024)
    kv = kv.reshape(depth, batch, mem_len, 2 * D_MODEL)
    w_b, w_cr, w_ci, abar_re, abar_im = s5_prep(s5_a_re, s5_a_im, s5_log_dt, s5_b_re, s5_b_im, s5_c_re, s5_c_im)
    xn = rms_cast(x.reshape(tokens, D_MODEL), mixer_norm(0), tm=tm_proj)
    for layer in range(depth):
        i = layer // 2
        if layer % 2 == 0:
            z = in_proj(xn, w_in_ab, i, n=w_in_ab.shape[2], tm=tm_proj, tn=2048)
            z = z.reshape(batch, seq, z.shape[1])
            a_out = dilated_attention(z, batch=batch, seq=seq)
            x = even_tail(z, a_out, x, pool_w, pool_scale[i], w_out_ab, i, batch=batch, seq=seq,
                          tm=1024, sub=512)
        else:
            n_c = 3 * C_WIDTH
            zc = in_proj(xn, w_in_cd, i, n=n_c, tm=tm_proj, tn=1536)
            zd = in_proj(xn, w_in_cd, i, n=2 * D_WIDTH, col0=n_c, tm=tm_proj, tn=2 * D_WIDTH,
                         out_shape=(seq, batch * 2 * D_WIDTH),
                         out_map=lambda r, j: (r % tiles_per_row, r // tiles_per_row))
            d_out = s5_layer(zd, w_b, abar_re, abar_im, w_cr, w_ci, s5_d[i], glu_w1, glu_w2, i,
                             batch=batch, seq=seq, lc=64)
            x = odd_tail(zc.reshape(batch, seq, n_c), zd, d_out, x, sgu_ln_g[i], sgu_ln_b[i], sgu_w,
                         sgu_b[i], w_out_cd, i, batch=batch, seq=seq, tm=1024, sub=512)
        final = layer == depth - 1
        out = cross_attention(x, norm_x[layer], w_xq, kv, w_xo, final_norm if final else mixer_norm(layer + 1),
                              layer, batch=batch, seq=seq, mem_len=mem_len, tq=1024, sub=512, final=final)
        if final:
            return out
        x, xn = out[0], out[1].reshape(tokens, D_MODEL)
```

```python
import functools

import jax
import jax.numpy as jnp
from jax import lax
from jax.experimental import pallas as pl
from jax.experimental.pallas import tpu as pltpu

F32 = jnp.float32
BF16 = jnp.bfloat16

LANES = 128
D_MODEL = 1024
A_WIDTH = 1024
A_HEAD_DIM = 64
A_BLOCK = 128
A_DILATIONS = (1, 4, 16)
DIL_STEP = 4
B_WIDTH = 1024
POOL_WINDOWS = (2, 4, 8, 16)
B_GROUP = 256
POOL_HALO = 16
C_WIDTH = 1024
C_CHUNK = 128
C_GROUPS = 4
C_GROUP_DIM = 256
D_WIDTH = 512
S5_GROUPS = 32
S5_GROUP_DIM = 16
S5_STATE = 64
S5_LANES = S5_GROUPS * S5_STATE
S5_PARTS = 2
S5_PART_IN = D_WIDTH // S5_PARTS
S5_PART_STATE = S5_LANES // S5_PARTS
X_HEADS = 4
X_HEAD_DIM = 256
EPS = 1e-6
NEG = -1e30
LOG2E = 1.4426950408889634

VMEM_LIMIT = 48 * 1024 * 1024


def _params(*sem):
    return pltpu.CompilerParams(dimension_semantics=sem, vmem_limit_bytes=VMEM_LIMIT)


def _rms(x, g):
    ms = jnp.mean(x * x, axis=-1, keepdims=True)
    return x * lax.rsqrt(ms + EPS) * g


def _silu(x):
    return x * jax.nn.sigmoid(x)


def _dot(a, b):
    return jnp.dot(a, b, preferred_element_type=F32)


def _layer_block(shape, layer):
    return pl.BlockSpec((None,) + tuple(shape), lambda *_: (layer,) + (0,) * len(shape),
                        pipeline_mode=pl.Buffered(1))


def _cast_weights_once(pairs, grid_rank):
    first = pl.program_id(0) == 0
    for axis in range(1, grid_rank):
        first = first & (pl.program_id(axis) == 0)

    @pl.when(first)
    def _():
        for src, dst in pairs:
            dst[...] = src[...].astype(dst.dtype)


def _dot_nt(a, b):
    return lax.dot_general(a, b, (((1,), (1,)), ((), ())), preferred_element_type=F32)


def _rms_cast_kernel(x_ref, g_ref, o_ref):
    o_ref[...] = _rms(x_ref[...], g_ref[...]).astype(o_ref.dtype)


def rms_cast(x, g, *, tm):
    m, k = x.shape
    return pl.pallas_call(
        _rms_cast_kernel,
        out_shape=jax.ShapeDtypeStruct((m, k), BF16),
        grid=(m // tm,),
        in_specs=[pl.BlockSpec((tm, k), lambda i: (i, 0)), pl.BlockSpec((1, k), lambda i: (0, 0))],
        out_specs=pl.BlockSpec((tm, k), lambda i: (i, 0)),
        compiler_params=_params("parallel"),
        name="rms_cast",
    )(x, g.reshape(1, k))


def _in_proj_kernel(x_ref, w32_ref, o_ref, w_ref):
    @pl.when(pl.program_id(1) == 0)
    def _():
        w_ref[...] = w32_ref[...].astype(BF16)

    o_ref[...] = _dot(x_ref[...], w_ref[...]).astype(o_ref.dtype)


def in_proj(xn, w, layer, *, n, tm, tn, col0=0, out_shape=None, out_map=None):
    m, k = xn.shape
    col_block0 = col0 // tn
    if out_shape is None:
        out_shape, out_map = (m, n), (lambda i, j: (i, j))
    return pl.pallas_call(
        _in_proj_kernel,
        out_shape=jax.ShapeDtypeStruct(out_shape, BF16),
        grid=(n // tn, m // tm),
        in_specs=[
            pl.BlockSpec((tm, k), lambda j, i: (i, 0)),
            pl.BlockSpec((None, k, tn), lambda j, i: (layer, 0, col_block0 + j)),
        ],
        out_specs=pl.BlockSpec((tm, tn), lambda j, i: out_map(i, j)),
        scratch_shapes=[pltpu.VMEM((k, tn), BF16)],
        compiler_params=_params("arbitrary", "arbitrary"),
        name="in_proj",
    )(xn, w)


def _mem_proj_kernel(x_ref, g_ref, w_ref, o_ref, xn_ref):
    @pl.when((pl.program_id(0) == 0) & (pl.program_id(1) == 0))
    def _():
        xn_ref[...] = _rms(x_ref[...], g_ref[...]).astype(BF16)

    o_ref[...] = _dot(xn_ref[...], w_ref[...].astype(BF16)).astype(o_ref.dtype)


def mem_kv_proj(mem, g, w, *, tn):
    m, k = mem.shape
    n_layers, _, n = w.shape
    return pl.pallas_call(
        _mem_proj_kernel,
        out_shape=jax.ShapeDtypeStruct((n_layers, m, n), BF16),
        grid=(n_layers, n // tn),
        in_specs=[
            pl.BlockSpec((m, k), lambda l, j: (0, 0)),
            pl.BlockSpec((1, k), lambda l, j: (0, 0)),
            pl.BlockSpec((None, k, tn), lambda l, j: (l, 0, j)),
        ],
        out_specs=pl.BlockSpec((None, m, tn), lambda l, j: (l, 0, j)),
        scratch_shapes=[pltpu.VMEM((m, k), BF16)],
        compiler_params=_params("arbitrary", "arbitrary"),
        name="mem_kv_proj",
    )(mem, g.reshape(1, k), w)


def _dilated_kernel(q_ref, k_ref, v_ref, g_ref, o_ref, qs_ref, ks_ref, vs_ref, gs_ref, os_ref, num_ref, den_ref,
                    m_ref, bias_ref, *, seq, unroll):
    assert A_DILATIONS == (1, DIL_STEP, DIL_STEP * DIL_STEP)
    qi = lax.broadcasted_iota(jnp.int32, (A_BLOCK, 2 * A_BLOCK), 0)
    kj = lax.broadcasted_iota(jnp.int32, (A_BLOCK, 2 * A_BLOCK), 1)
    dist = qi + A_BLOCK - kj
    band = (dist >= 0) & (dist <= A_BLOCK)
    bias_ref[1] = jnp.where(band, 0.0, NEG).astype(BF16)
    bias_ref[0] = jnp.where(band & (kj >= A_BLOCK), 0.0, NEG).astype(BF16)

    len_s = seq // DIL_STEP
    len_w = len_s // DIL_STEP
    blocks_s = len_s // A_BLOCK
    q_scale = A_HEAD_DIM ** -0.5 * LOG2E
    for src, dst, scale in ((q_ref, qs_ref, q_scale), (k_ref, ks_ref, None), (v_ref, vs_ref, None)):
        nat = src[...].astype(F32)
        dst[0] = nat if scale is None else nat * scale
        for b in range(DIL_STEP):
            dst[1, b * len_s:(b + 1) * len_s, :] = dst[0, pl.ds(b, len_s, stride=DIL_STEP), :]
        for c in range(DIL_STEP * DIL_STEP):
            b, a = divmod(c, DIL_STEP)
            dst[2, c * len_w:(c + 1) * len_w, :] = dst[1, pl.ds(b * len_s + a, len_w, stride=DIL_STEP), :]
    gs_ref[0] = g_ref[...].astype(F32)

    head0 = lax.broadcasted_iota(jnp.int32, (A_BLOCK, LANES), 1) < A_HEAD_DIM

    def load(slab, cur, prev):
        q = qs_ref[slab, cur, :].astype(BF16)
        if slab == 0:
            kk = jnp.concatenate([k_ref[prev, :], k_ref[cur, :]], axis=0)
            vv = jnp.concatenate([v_ref[prev, :], v_ref[cur, :]], axis=0)
            return q, kk, vv
        kk = jnp.concatenate([ks_ref[slab, prev, :], ks_ref[slab, cur, :]], axis=0).astype(BF16)
        vv = jnp.concatenate([vs_ref[slab, prev, :], vs_ref[slab, cur, :]], axis=0).astype(BF16)
        return q, kk, vv

    one_b, zero_b = jnp.ones((), BF16), jnp.zeros((), BF16)

    def attend(q, k, v, bias):
        own0 = lax.broadcasted_iota(jnp.int32, k.shape, 1).astype(BF16) < A_HEAD_DIM
        k_bd = jnp.concatenate([jnp.where(own0, k, zero_b), jnp.where(own0, zero_b, k)], axis=0)
        s = _dot_nt(q, k_bd).astype(BF16) + jnp.concatenate([bias, bias], axis=1)
        s0, s1 = s[:, :k.shape[0]], s[:, k.shape[0]:]
        m0 = jnp.max(s0, axis=-1, keepdims=True)
        m1 = jnp.max(s1, axis=-1, keepdims=True)
        p = jnp.concatenate([jnp.exp2(s0 - m0), jnp.exp2(s1 - m1)], axis=1)
        v_bd = jnp.concatenate([
            jnp.concatenate([jnp.where(own0, v, zero_b), jnp.where(own0, one_b, zero_b)], axis=1),
            jnp.concatenate([jnp.where(own0, zero_b, v), jnp.where(own0, zero_b, one_b)], axis=1)], axis=0)
        res = _dot(p, v_bd)
        m = jnp.where(head0, jnp.broadcast_to(m0, (A_BLOCK, LANES)).astype(F32),
                      jnp.broadcast_to(m1, (A_BLOCK, LANES)).astype(F32))
        return res[:, :LANES], res[:, LANES:], m

    def attend_single(cur):
        return attend(qs_ref[2, cur, :].astype(BF16), ks_ref[2, cur, :].astype(BF16),
                      vs_ref[2, cur, :].astype(BF16), bias_ref[0][:, A_BLOCK:])

    def aligned(start):
        return pl.ds(pl.multiple_of(start, A_BLOCK), A_BLOCK)

    def banded_rows(idx, blocks_per_seq):
        start = idx * A_BLOCK
        first = idx % blocks_per_seq == 0
        return aligned(start), aligned(jnp.where(first, start, start - A_BLOCK)), bias_ref[jnp.where(first, 0, 1)]

    def merge(num_a, den_a, m_a, num_b, den_b, m_b):
        m = jnp.maximum(m_a, m_b)
        w_a = jnp.exp2(m_a - m)
        w_b = jnp.exp2(m_b - m)
        return w_a * num_a + w_b * num_b, w_a * den_a + w_b * den_b, m

    def narrow_body(idx, carry):
        cur, prev, bias = banded_rows(idx, n_blocks)
        num, den, m = attend(*load(0, cur, prev), bias)
        num_ref[0, cur, :] = num
        den_ref[0, cur, :] = den
        m_ref[0, cur, :] = m
        return carry

    def step_body(idx, carry):
        cur, prev, bias = banded_rows(idx, blocks_s)
        natural = pl.ds(idx // blocks_s + DIL_STEP * A_BLOCK * (idx % blocks_s), A_BLOCK, stride=DIL_STEP)
        num, den, m = merge(*attend(*load(1, cur, prev), bias),
                            num_ref[0, natural, :], den_ref[0, natural, :], m_ref[0, natural, :])
        num_ref[1, cur, :] = num
        den_ref[1, cur, :] = den
        m_ref[1, cur, :] = m
        gs_ref[1, cur, :] = gs_ref[0, natural, :]
        return carry

    def wide_body(idx, carry):
        cur = aligned(idx * A_BLOCK)
        by_step = pl.ds((idx // DIL_STEP) * len_s + idx % DIL_STEP, A_BLOCK, stride=DIL_STEP)
        natural = pl.ds(DIL_STEP * (idx % DIL_STEP) + idx // DIL_STEP, A_BLOCK, stride=DIL_STEP * DIL_STEP)
        num, den, _ = merge(*attend_single(cur),
                            num_ref[1, by_step, :], den_ref[1, by_step, :], m_ref[1, by_step, :])
        os_ref[natural, :] = num / den * _silu(gs_ref[1, by_step, :])
        return carry

    assert len_w == A_BLOCK
    n_blocks = seq // A_BLOCK
    for body in (narrow_body, step_body, wide_body):
        lax.fori_loop(0, n_blocks, body, 0, unroll=unroll)
    o_ref[...] = os_ref[...].astype(o_ref.dtype)


def dilated_attention(z, *, batch, seq, unroll=16):
    n_pair = A_WIDTH // LANES
    n_pat = len(A_DILATIONS)
    blk = (None, seq, LANES)
    slabs = pltpu.VMEM((n_pat, seq, LANES), F32)
    partial = pltpu.VMEM((2, seq, LANES), F32)
    return pl.pallas_call(
        functools.partial(_dilated_kernel, seq=seq, unroll=unroll),
        out_shape=jax.ShapeDtypeStruct((batch, seq, A_WIDTH), BF16),
        grid=(batch, n_pair),
        in_specs=[
            pl.BlockSpec(blk, lambda b, h: (b, 0, h)),
            pl.BlockSpec(blk, lambda b, h: (b, 0, n_pair + h)),
            pl.BlockSpec(blk, lambda b, h: (b, 0, 2 * n_pair + h)),
            pl.BlockSpec(blk, lambda b, h: (b, 0, 3 * n_pair + h)),
        ],
        out_specs=pl.BlockSpec(blk, lambda b, h: (b, 0, h)),
        scratch_shapes=[
            slabs, slabs, slabs,
            pltpu.VMEM((2, seq, LANES), F32),
            pltpu.VMEM((seq, LANES), F32),
            partial, partial, partial,
            pltpu.VMEM((2, A_BLOCK, 2 * A_BLOCK), BF16),
        ],
        compiler_params=_params("parallel", "parallel"),
        name="dilated_attention",
    )(z, z, z, z)


def _even_tail_kernel(vb_ref, halo_ref, gb_ref, a_ref, x_ref, pw32_ref, ps_ref, wo32_ref, o_ref, pw_ref, wo_ref,
                      *, tm, sub):
    _cast_weights_once([(pw32_ref, pw_ref), (wo32_ref, wo_ref)], grid_rank=2)
    ti = pl.program_id(1)
    for r in range(tm // sub):
        rows = slice(r * sub, (r + 1) * sub)
        v = vb_ref[rows, :].astype(F32)
        if r == 0:
            halo = jnp.where(ti > 0, halo_ref[...].astype(F32), 0.0)
        else:
            halo = vb_ref[r * sub - POOL_HALO:r * sub, :].astype(F32)
        xc = jnp.concatenate([halo, v], axis=0)
        s2 = xc + pltpu.roll(xc, 1, 0)
        t4 = s2[:, B_GROUP:]
        s4 = t4 + pltpu.roll(t4, 2, 0)
        t8 = s4[:, B_GROUP:]
        s8 = t8 + pltpu.roll(t8, 4, 0)
        t16 = s8[:, B_GROUP:]
        s16 = t16 + pltpu.roll(t16, 8, 0)
        sums = (s2[:, :B_GROUP], s4[:, :B_GROUP], s8[:, :B_GROUP], s16)
        pos = (ti * tm + r * sub + 1 + lax.broadcasted_iota(jnp.int32, (sub, 1), 0)).astype(F32)
        mixed = []
        for g, w in enumerate(POOL_WINDOWS):
            mean = sums[g][POOL_HALO:, :] / jnp.minimum(pos, float(w))
            pooled = mean - v[:, g * B_GROUP:(g + 1) * B_GROUP]
            mixed.append(_dot(pooled.astype(BF16), pw_ref[g]))
        b_out = jnp.concatenate(mixed, axis=1) * ps_ref[...] * _silu(gb_ref[rows, :].astype(F32))
        y = _dot(jnp.concatenate([a_ref[rows, :], b_out.astype(BF16)], axis=1), wo_ref[...])
        o_ref[rows, :] = x_ref[rows, :] + y


def even_tail(z, a_out, x, pool_w, pool_scale, w_out, layer, *, batch, seq, tm, sub):
    vb_col = 4 * A_WIDTH // B_WIDTH
    halo_per_tile = tm // POOL_HALO
    return pl.pallas_call(
        functools.partial(_even_tail_kernel, tm=tm, sub=sub),
        out_shape=jax.ShapeDtypeStruct((batch, seq, D_MODEL), F32),
        grid=(batch, seq // tm),
        in_specs=[
            pl.BlockSpec((None, tm, B_WIDTH), lambda b, t: (b, t, vb_col)),
            pl.BlockSpec((None, POOL_HALO, B_WIDTH),
                         lambda b, t: (b, jnp.maximum(t * halo_per_tile - 1, 0), vb_col)),
            pl.BlockSpec((None, tm, B_WIDTH), lambda b, t: (b, t, vb_col + 1)),
            pl.BlockSpec((None, tm, A_WIDTH), lambda b, t: (b, t, 0)),
            pl.BlockSpec((None, tm, D_MODEL), lambda b, t: (b, t, 0)),
            _layer_block((len(POOL_WINDOWS), B_GROUP, B_GROUP), layer),
            pl.BlockSpec((1, B_WIDTH), lambda b, t: (0, 0)),
            _layer_block((A_WIDTH + B_WIDTH, D_MODEL), layer),
        ],
        out_specs=pl.BlockSpec((None, tm, D_MODEL), lambda b, t: (b, t, 0)),
        scratch_shapes=[pltpu.VMEM((len(POOL_WINDOWS), B_GROUP, B_GROUP), BF16),
                        pltpu.VMEM((A_WIDTH + B_WIDTH, D_MODEL), BF16)],
        compiler_params=_params("arbitrary", "arbitrary"),
        name="even_tail",
    )(z, z, z, a_out, x, pool_w, pool_scale.reshape(1, B_WIDTH), w_out)


def _s5_prep_kernel(ar_ref, ai_ref, ldt_ref, br_ref, bi_ref, abr_ref, abi_ref, bbr_ref, bbi_ref):
    ar, ai = ar_ref[...], ai_ref[...]
    dt = jnp.exp(ldt_ref[...])
    mag = jnp.exp(dt * ar)
    abar_re = mag * jnp.cos(dt * ai)
    abar_im = mag * jnp.sin(dt * ai)
    nr, ni = abar_re - 1.0, abar_im
    inv = 1.0 / (ar * ar + ai * ai)
    coef_re = (nr * ar + ni * ai) * inv
    coef_im = (ni * ar - nr * ai) * inv
    br, bi = br_ref[...], bi_ref[...]
    abr_ref[...] = abar_re
    abi_ref[...] = abar_im
    bbr_ref[...] = coef_re * br - coef_im * bi
    bbi_ref[...] = coef_re * bi + coef_im * br


def _expand_block_diag(t, reps, row_group):
    rows, cols = t.shape
    src = lax.broadcasted_iota(jnp.int32, (cols, reps * cols), 0)
    dst = lax.broadcasted_iota(jnp.int32, (cols, reps * cols), 1)
    tiled = _dot(t, jnp.where(dst % cols == src, 1.0, 0.0).astype(BF16))
    r = lax.broadcasted_iota(jnp.int32, tiled.shape, 0) // row_group
    c = lax.broadcasted_iota(jnp.int32, tiled.shape, 1) // cols
    return jnp.where(r == c, tiled, 0.0).astype(BF16)


def _s5_expand_kernel(bbr_ref, bbi_ref, cr_ref, ci_ref, wb_ref, wcr_ref, wci_ref):
    gpp = S5_GROUPS // S5_PARTS
    wb_ref[...] = jnp.concatenate(
        [_expand_block_diag(ref[...].astype(BF16), gpp, S5_GROUP_DIM) for ref in (bbr_ref, bbi_ref)], axis=1)
    wcr_ref[...] = _expand_block_diag(cr_ref[...].astype(BF16), gpp, S5_STATE)
    wci_ref[...] = _expand_block_diag(ci_ref[...].astype(BF16), gpp, S5_STATE)


def s5_prep(a_re, a_im, log_dt, b_re, b_im, c_re, c_im):
    n_l = a_re.shape[0]
    rows = n_l * D_WIDTH
    rep = lambda t: jnp.repeat(t.reshape(n_l * S5_GROUPS, S5_STATE), S5_GROUP_DIM, axis=0)
    to_rows = lambda t: t.transpose(0, 1, 3, 2).reshape(rows, S5_STATE)
    shp = jax.ShapeDtypeStruct((rows, S5_STATE), F32)
    abr, abi, bbr, bbi = pl.pallas_call(
        _s5_prep_kernel, out_shape=(shp, shp, shp, shp), name="s5_prep",
    )(rep(a_re), rep(a_im), rep(jnp.broadcast_to(log_dt[:, :, None], (n_l, S5_GROUPS, S5_STATE))),
      to_rows(b_re), to_rows(b_im))

    n_blk = n_l * S5_PARTS
    in_rows = lambda t: t.reshape(n_blk, S5_PART_IN, S5_STATE)
    out_rows = lambda t: t.transpose(0, 1, 3, 2).reshape(n_blk, S5_PART_STATE, S5_GROUP_DIM)
    blk = lambda r, c: pl.BlockSpec((None, r, c), lambda i: (i, 0, 0))
    w_b, w_cr, w_ci = pl.pallas_call(
        _s5_expand_kernel,
        out_shape=(jax.ShapeDtypeStruct((n_blk, S5_PART_IN, 2 * S5_PART_STATE), BF16),
                   jax.ShapeDtypeStruct((n_blk, S5_PART_STATE, S5_PART_IN), BF16),
                   jax.ShapeDtypeStruct((n_blk, S5_PART_STATE, S5_PART_IN), BF16)),
        grid=(n_blk,),
        in_specs=[blk(S5_PART_IN, S5_STATE), blk(S5_PART_IN, S5_STATE),
                  blk(S5_PART_STATE, S5_GROUP_DIM), blk(S5_PART_STATE, S5_GROUP_DIM)],
        out_specs=(blk(S5_PART_IN, 2 * S5_PART_STATE), blk(S5_PART_STATE, S5_PART_IN),
                   blk(S5_PART_STATE, S5_PART_IN)),
        compiler_params=_params("parallel"),
        name="s5_expand",
    )(in_rows(bbr), in_rows(bbi), out_rows(c_re), out_rows(c_im))
    per_layer = lambda t: t.reshape((n_l, S5_PARTS) + t.shape[1:])
    abar_re = abr[::S5_GROUP_DIM].reshape(n_l, 1, S5_LANES)
    abar_im = abi[::S5_GROUP_DIM].reshape(n_l, 1, S5_LANES)
    return per_layer(w_b), per_layer(w_cr), per_layer(w_ci), abar_re, abar_im


def _s5_kernel(xd_ref, wb_ref, are_ref, aim_ref, wcr_ref, wci_ref, dsk_ref, w1_32_ref, w2_32_ref,
               o_ref, u_ref, y_ref, bu_ref, h_ref, w1_ref, w2_ref, *, lc, nb, lane_chunk):
    @pl.when(pl.program_id(0) == 0)
    def _():
        h_ref[...] = jnp.zeros_like(h_ref)

    _cast_weights_once([(w1_32_ref, w1_ref), (w2_32_ref, w2_ref)], grid_rank=1)
    n_slab = D_WIDTH // LANES
    for b in range(nb):
        for s in range(n_slab):
            col = b * 2 * D_WIDTH + s * LANES
            u_ref[s, pl.ds(b, lc, stride=nb), :] = xd_ref[:, col:col + LANES].astype(F32)
    u = jnp.concatenate([u_ref[s] for s in range(n_slab)], axis=1)
    ub = u.astype(BF16)
    width = 2 * S5_PART_STATE
    def project(part):
        bu_ref[:, part * width:(part + 1) * width] = _dot(
            ub[:, part * S5_PART_IN:(part + 1) * S5_PART_IN], wb_ref[part])

    def scan(part):
        for c in range(S5_PART_STATE // lane_chunk):
            re = slice(part * width + c * lane_chunk, part * width + (c + 1) * lane_chunk)
            im = slice(re.start + S5_PART_STATE, re.stop + S5_PART_STATE)
            ab = slice(part * S5_PART_STATE + c * lane_chunk, part * S5_PART_STATE + (c + 1) * lane_chunk)
            a_r = jnp.broadcast_to(are_ref[:, ab], (nb, lane_chunk))
            a_i = jnp.broadcast_to(aim_ref[:, ab], (nb, lane_chunk))
            h_r, h_i = h_ref[:, re], h_ref[:, im]
            for t in range(lc):
                row = slice(t * nb, (t + 1) * nb)
                h_r, h_i = (a_r * h_r - a_i * h_i + bu_ref[row, re], a_r * h_i + a_i * h_r + bu_ref[row, im])
                bu_ref[row, re] = h_r
                bu_ref[row, im] = h_i
            h_ref[:, re] = h_r
            h_ref[:, im] = h_i

    def readout(part):
        re = slice(part * width, part * width + S5_PART_STATE)
        im = slice(re.stop, re.stop + S5_PART_STATE)
        return _dot(bu_ref[:, re].astype(BF16), wcr_ref[part]) - _dot(bu_ref[:, im].astype(BF16), wci_ref[part])

    ys = []
    project(0)
    for part in range(S5_PARTS):
        if part + 1 < S5_PARTS:
            project(part + 1)
        scan(part)
        ys.append(readout(part))
    y = jnp.concatenate(ys, axis=1) + dsk_ref[...] * u
    y = jax.nn.gelu(y).astype(BF16)
    out = _dot(y, w1_ref[...]) * jax.nn.sigmoid(_dot(y, w2_ref[...]))
    for s in range(n_slab):
        y_ref[s] = out[:, s * LANES:(s + 1) * LANES]
    o_ref[...] = jnp.concatenate(
        [y_ref[s, pl.ds(b, lc, stride=nb), :] for b in range(nb) for s in range(n_slab)], axis=1).astype(o_ref.dtype)


def s5_layer(zd, w_b, abar_re, abar_im, w_cr, w_ci, d_skip, w1, w2, layer, *, batch, seq, lc):
    rows = lc * batch
    n_slab = D_WIDTH // LANES
    return pl.pallas_call(
        functools.partial(_s5_kernel, lc=lc, nb=batch, lane_chunk=512),
        out_shape=jax.ShapeDtypeStruct((seq, batch * D_WIDTH), BF16),
        grid=(seq // lc,),
        in_specs=[
            pl.BlockSpec((lc, batch * 2 * D_WIDTH), lambda c: (c, 0)),
            _layer_block((S5_PARTS, S5_PART_IN, 2 * S5_PART_STATE), layer),
            _layer_block((1, S5_LANES), layer),
            _layer_block((1, S5_LANES), layer),
            _layer_block((S5_PARTS, S5_PART_STATE, S5_PART_IN), layer),
            _layer_block((S5_PARTS, S5_PART_STATE, S5_PART_IN), layer),
            pl.BlockSpec((1, D_WIDTH), lambda c: (0, 0)),
            _layer_block((D_WIDTH, D_WIDTH), layer),
            _layer_block((D_WIDTH, D_WIDTH), layer),
        ],
        out_specs=pl.BlockSpec((lc, batch * D_WIDTH), lambda c: (c, 0)),
        scratch_shapes=[
            pltpu.VMEM((n_slab, rows, LANES), F32),
            pltpu.VMEM((n_slab, rows, LANES), F32),
            pltpu.VMEM((rows, 2 * S5_LANES), F32),
            pltpu.VMEM((batch, 2 * S5_LANES), F32),
            pltpu.VMEM((D_WIDTH, D_WIDTH), BF16),
            pltpu.VMEM((D_WIDTH, D_WIDTH), BF16),
        ],
        compiler_params=_params("arbitrary"),
        name="s5_layer",
    )(zd, w_b, abar_re, abar_im, w_cr, w_ci, d_skip.reshape(1, D_WIDTH), w1, w2)


def _odd_tail_kernel(u_ref, v_ref, gc_ref, d_ref, gd_ref, x_ref, lng_ref, lnb_ref, ws_ref, bs_ref, wo32_ref,
                     o_ref, c_ref, wo_ref, *, tm, sub):
    _cast_weights_once([(wo32_ref, wo_ref)], grid_rank=2)
    ri = lax.broadcasted_iota(jnp.int32, (C_CHUNK, C_CHUNK), 0)
    ci = lax.broadcasted_iota(jnp.int32, (C_CHUNK, C_CHUNK), 1)
    causal = ri >= ci
    w_s = [jnp.where(causal, ws_ref[g], 0.0).astype(BF16) for g in range(C_GROUPS)]
    for r in range(tm // sub):
        tile = slice(r * sub, (r + 1) * sub)
        v = v_ref[tile, :].astype(F32)
        mu = jnp.mean(v, axis=-1, keepdims=True)
        vc = v - mu
        var = jnp.mean(vc * vc, axis=-1, keepdims=True)
        vn = (vc * lax.rsqrt(var + EPS) * lng_ref[...] + lnb_ref[...]).astype(BF16)
        for g in range(C_GROUPS):
            cols = slice(g * C_GROUP_DIM, (g + 1) * C_GROUP_DIM)
            b_g = bs_ref[:, g:g + 1]
            for c in range(sub // C_CHUNK):
                rows = slice(r * sub + c * C_CHUNK, r * sub + (c + 1) * C_CHUNK)
                mixed = _dot(w_s[g], vn[c * C_CHUNK:(c + 1) * C_CHUNK, cols]) + b_g
                gate = _silu(gc_ref[rows, cols].astype(F32))
                c_ref[rows, cols] = (u_ref[rows, cols].astype(F32) * mixed * gate).astype(BF16)
        d = (d_ref[tile, :].astype(F32) * _silu(gd_ref[tile, :].astype(F32))).astype(BF16)
        y = _dot(c_ref[tile, :], wo_ref[:C_WIDTH, :]) + _dot(d, wo_ref[C_WIDTH:, :])
        o_ref[tile, :] = x_ref[tile, :] + y


def odd_tail(zc, zd, d_out, x, ln_g, ln_b, w_s, b_s, w_out, layer, *, batch, seq, tm, sub):
    return pl.pallas_call(
        functools.partial(_odd_tail_kernel, tm=tm, sub=sub),
        out_shape=jax.ShapeDtypeStruct((batch, seq, D_MODEL), F32),
        grid=(batch, seq // tm),
        in_specs=[
            pl.BlockSpec((None, tm, C_WIDTH), lambda b, t: (b, t, 0)),
            pl.BlockSpec((None, tm, C_WIDTH), lambda b, t: (b, t, 1)),
            pl.BlockSpec((None, tm, C_WIDTH), lambda b, t: (b, t, 2)),
            pl.BlockSpec((tm, D_WIDTH), lambda b, t: (t, b)),
            pl.BlockSpec((tm, D_WIDTH), lambda b, t: (t, 2 * b + 1)),
            pl.BlockSpec((None, tm, D_MODEL), lambda b, t: (b, t, 0)),
            pl.BlockSpec((1, C_WIDTH), lambda b, t: (0, 0)),
            pl.BlockSpec((1, C_WIDTH), lambda b, t: (0, 0)),
            _layer_block((C_GROUPS, C_CHUNK, C_CHUNK), layer),
            pl.BlockSpec((C_CHUNK, C_GROUPS), lambda b, t: (0, 0)),
            _layer_block((C_WIDTH + D_WIDTH, D_MODEL), layer),
        ],
        out_specs=pl.BlockSpec((None, tm, D_MODEL), lambda b, t: (b, t, 0)),
        scratch_shapes=[pltpu.VMEM((tm, C_WIDTH), BF16), pltpu.VMEM((C_WIDTH + D_WIDTH, D_MODEL), BF16)],
        compiler_params=_params("arbitrary", "arbitrary"),
        name="odd_tail",
    )(zc, zc, zc, d_out, zd, x, ln_g.reshape(1, C_WIDTH), ln_b.reshape(1, C_WIDTH), w_s, b_s.T, w_out)


def _xattn_kernel(x_ref, g_ref, wq32_ref, kv_ref, wo32_ref, pg_ref, *refs, final, sub):
    if final:
        o_ref, wq_ref, wo_ref, qk_ref, vo_ref = refs
    else:
        o_ref, xn_ref, wq_ref, wo_ref, qk_ref, vo_ref = refs
    _cast_weights_once([(wq32_ref, wq_ref), (wo32_ref, wo_ref)], grid_rank=2)

    mem = kv_ref.shape[0]

    @pl.when(pl.program_id(1) == 0)
    def _():
        for h in range(X_HEADS):
            ks = slice(h * X_HEAD_DIM, (h + 1) * X_HEAD_DIM)
            vs = slice(D_MODEL + h * X_HEAD_DIM, D_MODEL + (h + 1) * X_HEAD_DIM)
            ms = slice(h * mem, (h + 1) * mem)
            qk_ref[:, ms] = (_dot_nt(wq_ref[:, ks], kv_ref[:, ks]) * (X_HEAD_DIM ** -0.5)).astype(BF16)
            vo_ref[ms, :] = _dot(kv_ref[:, vs], wo_ref[ks, :]).astype(BF16)

    for r in range(x_ref.shape[0] // sub):
        rows = slice(r * sub, (r + 1) * sub)
        x = x_ref[rows, :]
        scores = _dot(_rms(x, g_ref[...]).astype(BF16), qk_ref[...])
        heads = []
        for h in range(X_HEADS):
            s = scores[:, h * mem:(h + 1) * mem]
            m = jnp.max(s, axis=-1, keepdims=True)
            p = jnp.exp(s - m)
            den = jnp.sum(p, axis=-1, keepdims=True)
            heads.append((p / den).astype(BF16))
        y = x + _dot(jnp.concatenate(heads, axis=1), vo_ref[...])
        if final:
            o_ref[rows, :] = _rms(y, pg_ref[...])
        else:
            o_ref[rows, :] = y
            xn_ref[rows, :] = _rms(y, pg_ref[...]).astype(xn_ref.dtype)


def cross_attention(x, g, w_q, kv, w_o, post_g, layer, *, batch, seq, mem_len, tq, sub, final):
    tile = pl.BlockSpec((None, tq, D_MODEL), lambda b, t: (b, t, 0))
    res = jax.ShapeDtypeStruct((batch, seq, D_MODEL), F32)
    nxt = jax.ShapeDtypeStruct((batch, seq, D_MODEL), BF16)
    return pl.pallas_call(
        functools.partial(_xattn_kernel, final=final, sub=sub),
        out_shape=res if final else (res, nxt),
        grid=(batch, seq // tq),
        in_specs=[
            tile,
            pl.BlockSpec((1, D_MODEL), lambda b, t: (0, 0)),
            _layer_block((D_MODEL, D_MODEL), layer),
            pl.BlockSpec((None, None, mem_len, 2 * D_MODEL), lambda b, t: (layer, b, 0, 0)),
            _layer_block((D_MODEL, D_MODEL), layer),
            pl.BlockSpec((1, D_MODEL), lambda b, t: (0, 0)),
        ],
        out_specs=tile if final else (tile, tile),
        scratch_shapes=[pltpu.VMEM((D_MODEL, D_MODEL), BF16), pltpu.VMEM((D_MODEL, D_MODEL), BF16),
                        pltpu.VMEM((D_MODEL, X_HEADS * mem_len), BF16),
                        pltpu.VMEM((X_HEADS * mem_len, D_MODEL), BF16)],
        compiler_params=_params("arbitrary", "arbitrary"),
        name="cross_attention",
    )(x, g.reshape(1, D_MODEL), w_q, kv, w_o, post_g.reshape(1, D_MODEL))


def kernel(x, mem, norm_ab, w_in_ab, pool_w, pool_scale, w_out_ab, norm_cd, w_in_cd, sgu_ln_g, sgu_ln_b,
           sgu_w, sgu_b, s5_a_re, s5_a_im, s5_log_dt, s5_b_re, s5_b_im, s5_c_re, s5_c_im, s5_d,
           glu_w1, glu_w2, w_out_cd, norm_x, w_xq, w_xkv, w_xo, mem_norm, final_norm):
    batch, seq, _ = x.shape
    mem_len = mem.shape[1]
    depth = norm_x.shape[0]
    tokens = batch * seq
    tm_proj = 1024
    tiles_per_row = seq // tm_proj
    mixer_norm = lambda layer: (norm_ab if layer % 2 == 0 else norm_cd)[layer // 2]

    kv = mem_kv_proj(mem.reshape(batch * mem_len, D_MODEL), mem_norm, w_xkv, tn=1024)
    kv = kv.reshape(depth, batch, mem_len, 2 * D_MODEL)
    w_b, w_cr, w_ci, abar_re, abar_im = s5_prep(s5_a_re, s5_a_im, s5_log_dt, s5_b_re, s5_b_im, s5_c_re, s5_c_im)
    xn = rms_cast(x.reshape(tokens, D_MODEL), mixer_norm(0), tm=tm_proj)
    for layer in range(depth):
        i = layer // 2
        if layer % 2 == 0:
            z = in_proj(xn, w_in_ab, i, n=w_in_ab.shape[2], tm=tm_proj, tn=2048)
            z = z.reshape(batch, seq, z.shape[1])
            a_out = dilated_attention(z, batch=batch, seq=seq)
            x = even_tail(z, a_out, x, pool_w, pool_scale[i], w_out_ab, i, batch=batch, seq=seq,
                          tm=1024, sub=512)
        else:
            n_c = 3 * C_WIDTH
            zc = in_proj(xn, w_in_cd, i, n=n_c, tm=tm_proj, tn=1536)
            zd = in_proj(xn, w_in_cd, i, n=2 * D_WIDTH, col0=n_c, tm=tm_proj, tn=2 * D_WIDTH,
                         out_shape=(seq, batch * 2 * D_WIDTH),
                         out_map=lambda r, j: (r % tiles_per_row, r // tiles_per_row))
            d_out = s5_layer(zd, w_b, abar_re, abar_im, w_cr, w_ci, s5_d[i], glu_w1, glu_w2, i,
                             batch=batch, seq=seq, lc=64)
            x = odd_tail(zc.reshape(batch, seq, n_c), zd, d_out, x, sgu_ln_g[i], sgu_ln_b[i], sgu_w,
                         sgu_b[i], w_out_cd, i, batch=batch, seq=seq, tm=1024, sub=512)
        final = layer == depth - 1
        out = cross_attention(x, norm_x[layer], w_xq, kv, w_xo, final_norm if final else mixer_norm(layer + 1),
                              layer, batch=batch, seq=seq, mem_len=mem_len, tq=1024, sub=512, final=final)
        if final:
            return out
        x, xn = out[0], out[1].reshape(tokens, D_MODEL)
```
